```python
import jax, jax.numpy as jnp
from jax import lax
import numpy as np

D_MODEL = 1024
BATCH = 8
SEQ = 4096
DEPTH = 4

CHUNK = 64
N_MIXERS = 2
EXPAND = 2
BRANCH = EXPAND * D_MODEL
GMLP_BLOCK = 128
A_GROUPS = 8
A_GROUP_DIM = BRANCH // A_GROUPS
POOL_WINDOWS = (2, 4, 8, 16)
B_GROUPS = len(POOL_WINDOWS)
B_GROUP_DIM = BRANCH // B_GROUPS
N_A = (DEPTH + 1) // 2
N_B = DEPTH // 2
EPS = 1e-6

kernel_name = "hybrid_gmlp_pool_sandwich_trunk"


def rms_norm(x, g):
    xf = x.astype(jnp.float32)
    y = xf * lax.rsqrt(jnp.mean(xf * xf, axis=-1, keepdims=True) + EPS)
    return (y * g.astype(jnp.float32)).astype(x.dtype)


def layer_norm(x, g, b):
    xf = x.astype(jnp.float32)
    mu = jnp.mean(xf, axis=-1, keepdims=True)
    xc = xf - mu
    y = xc * lax.rsqrt(jnp.mean(xc * xc, axis=-1, keepdims=True) + EPS)
    return (y * g.astype(jnp.float32) + b.astype(jnp.float32)).astype(x.dtype)


def spatial_mask():
    p = jnp.arange(GMLP_BLOCK)
    return (p[None, :] // CHUNK) <= (p[:, None] // CHUNK)


def gmlp_mixer(h, w_in, ln_g, ln_b, w_s, b_s, w_out):
    B, S, _ = h.shape
    proj = h @ w_in
    u, v, z = jnp.split(proj, 3, axis=-1)
    u = jax.nn.gelu(u)
    v = layer_norm(jax.nn.gelu(v), ln_g, ln_b)
    vb = v.reshape(B, S // GMLP_BLOCK, GMLP_BLOCK, A_GROUPS, A_GROUP_DIM)
    w = jnp.where(spatial_mask()[None], w_s, jnp.zeros_like(w_s))
    mixed = jnp.einsum('gpq,bnqgc->bnpgc', w, vb)
    mixed = mixed + jnp.transpose(b_s)[None, None, :, :, None]
    mixed = mixed.reshape(B, S, BRANCH)
    y = u * mixed * jax.nn.silu(z)
    return y @ w_out


def pool_mixer(h, w_in, w_grp, scale, w_out):
    B, S, _ = h.shape
    proj = h @ w_in
    xb, z = jnp.split(proj, 2, axis=-1)
    xf = xb.astype(jnp.float32)
    cs = jnp.concatenate([jnp.zeros((B, 1, BRANCH), jnp.float32),
                          jnp.cumsum(xf, axis=1)], axis=1)
    upper = cs[:, 1:]
    t1 = jnp.arange(1, S + 1, dtype=jnp.int32)
    outs = []
    for gi, win in enumerate(POOL_WINDOWS):
        sl = slice(gi * B_GROUP_DIM, (gi + 1) * B_GROUP_DIM)
        lower = jnp.pad(cs[:, :S + 1 - win, sl], ((0, 0), (win - 1, 0), (0, 0)))
        count = jnp.minimum(t1, win).astype(jnp.float32)[None, :, None]
        pooled = (upper[:, :, sl] - lower) / count - xf[:, :, sl]
        outs.append(jnp.einsum('bsc,cd->bsd', pooled.astype(xb.dtype), w_grp[gi]))
    mixed = jnp.concatenate(outs, axis=-1) * scale
    y = mixed * jax.nn.silu(z)
    return y @ w_out


def _fwd_setup_inputs(seed: int = 0) -> dict:
    key = jax.random.key(seed)
    ks = jax.random.split(key, 16)
    f32 = jnp.float32
    nrm = lambda k, shape, s: jax.random.normal(k, shape, f32) * s
    return {
        "x": nrm(ks[0], (BATCH, SEQ, D_MODEL), 1.0),
        "norm_pre": 1.0 + nrm(ks[1], (DEPTH, D_MODEL), 0.05),
        "norm_post": 1.0 + nrm(ks[2], (DEPTH, D_MODEL), 0.05),
        "a_w_in": nrm(ks[3], (N_A, D_MODEL, 3 * BRANCH), D_MODEL ** -0.5),
        "a_ln_g": 1.0 + nrm(ks[4], (N_A, BRANCH), 0.05),
        "a_ln_b": nrm(ks[5], (N_A, BRANCH), 0.02),
        "a_w_s": nrm(ks[6], (N_A, A_GROUPS, GMLP_BLOCK, GMLP_BLOCK), GMLP_BLOCK ** -0.5),
        "a_b_s": 1.0 + nrm(ks[7], (N_A, A_GROUPS, GMLP_BLOCK), 0.05),
        "a_w_out": nrm(ks[8], (N_A, BRANCH, D_MODEL), BRANCH ** -0.5),
        "b_w_in": nrm(ks[9], (N_B, D_MODEL, 2 * BRANCH), D_MODEL ** -0.5),
        "b_w_grp": nrm(ks[10], (N_B, B_GROUPS, B_GROUP_DIM, B_GROUP_DIM), B_GROUP_DIM ** -0.5),
        "b_scale": 1.0 + nrm(ks[11], (N_B, BRANCH), 0.1),
        "b_w_out": nrm(ks[12], (N_B, BRANCH, D_MODEL), BRANCH ** -0.5),
    }


def _fwd_reference(x, norm_pre, norm_post, a_w_in, a_ln_g, a_ln_b, a_w_s, a_b_s, a_w_out,
              b_w_in, b_w_grp, b_scale, b_w_out):
    for i in range(DEPTH):
        h = rms_norm(x, norm_pre[i])
        j = i // N_MIXERS
        if i % N_MIXERS == 0:
            out = gmlp_mixer(h, a_w_in[j], a_ln_g[j], a_ln_b[j], a_w_s[j], a_b_s[j], a_w_out[j])
        else:
            out = pool_mixer(h, b_w_in[j], b_w_grp[j], b_scale[j], b_w_out[j])
        x = x + rms_norm(out, norm_post[i])
    return x


import jax as _jax
import jax.numpy as _jnp

TWIN_FORMAT = 'train_step'
FWD_PARAMS = ['x', 'norm_pre', 'norm_post', 'a_w_in', 'a_ln_g', 'a_ln_b', 'a_w_s', 'a_b_s', 'a_w_out', 'b_w_in', 'b_w_grp', 'b_scale', 'b_w_out']
TWIN_WEIGHTS = ['norm_pre', 'norm_post', 'a_w_in', 'a_ln_g', 'a_ln_b', 'a_w_s', 'a_b_s', 'a_w_out', 'b_w_in', 'b_w_grp', 'b_scale', 'b_w_out']
TWIN_DIFF_INPUT = 'x'
TWIN_INPUTS = ['x', 'norm_pre', 'norm_post', 'a_w_in', 'a_ln_g', 'a_ln_b', 'a_w_s', 'a_b_s', 'a_w_out', 'b_w_in', 'b_w_grp', 'b_scale', 'b_w_out', 'loss_target', 'm_norm_pre', 'm_norm_post', 'm_a_w_in', 'm_a_ln_g', 'm_a_ln_b', 'm_a_w_s', 'm_a_b_s', 'm_a_w_out', 'm_b_w_in', 'm_b_w_grp', 'm_b_scale', 'm_b_w_out', 'v_norm_pre', 'v_norm_post', 'v_a_w_in', 'v_a_ln_g', 'v_a_ln_b', 'v_a_w_s', 'v_a_b_s', 'v_a_w_out', 'v_b_w_in', 'v_b_w_grp', 'v_b_scale', 'v_b_w_out']
TWIN_OUTPUTS = ['loss', 'grad_x', 'grad_norm_pre', 'grad_norm_post', 'grad_a_w_in', 'grad_a_ln_g', 'grad_a_ln_b', 'grad_a_w_s', 'grad_a_b_s', 'grad_a_w_out', 'grad_b_w_in', 'grad_b_w_grp', 'grad_b_scale', 'grad_b_w_out', 'delta_norm_pre', 'delta_norm_post', 'delta_a_w_in', 'delta_a_ln_g', 'delta_a_ln_b', 'delta_a_w_s', 'delta_a_b_s', 'delta_a_w_out', 'delta_b_w_in', 'delta_b_w_grp', 'delta_b_scale', 'delta_b_w_out', 'new_m_norm_pre', 'new_m_norm_post', 'new_m_a_w_in', 'new_m_a_ln_g', 'new_m_a_ln_b', 'new_m_a_w_s', 'new_m_a_b_s', 'new_m_a_w_out', 'new_m_b_w_in', 'new_m_b_w_grp', 'new_m_b_scale', 'new_m_b_w_out', 'new_v_norm_pre', 'new_v_norm_post', 'new_v_a_w_in', 'new_v_a_ln_g', 'new_v_a_ln_b', 'new_v_a_w_s', 'new_v_a_b_s', 'new_v_a_w_out', 'new_v_b_w_in', 'new_v_b_w_grp', 'new_v_b_scale', 'new_v_b_w_out']
TWIN_LEAF_KINDS = {'loss': 'loss', 'grad_x': 'grad_x', 'grad_norm_pre': 'grad_w', 'grad_norm_post': 'grad_w', 'grad_a_w_in': 'grad_w', 'grad_a_ln_g': 'grad_w', 'grad_a_ln_b': 'grad_w', 'grad_a_w_s': 'grad_w', 'grad_a_b_s': 'grad_w', 'grad_a_w_out': 'grad_w', 'grad_b_w_in': 'grad_w', 'grad_b_w_grp': 'grad_w', 'grad_b_scale': 'grad_w', 'grad_b_w_out': 'grad_w', 'delta_norm_pre': 'delta_w', 'delta_norm_post': 'delta_w', 'delta_a_w_in': 'delta_w', 'delta_a_ln_g': 'delta_w', 'delta_a_ln_b': 'delta_w', 'delta_a_w_s': 'delta_w', 'delta_a_b_s': 'delta_w', 'delta_a_w_out': 'delta_w', 'delta_b_w_in': 'delta_w', 'delta_b_w_grp': 'delta_w', 'delta_b_scale': 'delta_w', 'delta_b_w_out': 'delta_w', 'new_m_norm_pre': 'new_m', 'new_m_norm_post': 'new_m', 'new_m_a_w_in': 'new_m', 'new_m_a_ln_g': 'new_m', 'new_m_a_ln_b': 'new_m', 'new_m_a_w_s': 'new_m', 'new_m_a_b_s': 'new_m', 'new_m_a_w_out': 'new_m', 'new_m_b_w_in': 'new_m', 'new_m_b_w_grp': 'new_m', 'new_m_b_scale': 'new_m', 'new_m_b_w_out': 'new_m', 'new_v_norm_pre': 'new_v', 'new_v_norm_post': 'new_v', 'new_v_a_w_in': 'new_v', 'new_v_a_ln_g': 'new_v', 'new_v_a_ln_b': 'new_v', 'new_v_a_w_s': 'new_v', 'new_v_a_b_s': 'new_v', 'new_v_a_w_out': 'new_v', 'new_v_b_w_in': 'new_v', 'new_v_b_w_grp': 'new_v', 'new_v_b_scale': 'new_v', 'new_v_b_w_out': 'new_v'}


def _forward(args):
    return _fwd_reference(*[args[k] for k in FWD_PARAMS])


def _output_shape():
    out = _jax.eval_shape(lambda: _forward(_fwd_setup_inputs(0)))
    return out.shape, out.dtype

N_MICROBATCH = 1
ADAM_LR = 0.001
ADAM_B1 = 0.9
ADAM_B2 = 0.999
ADAM_EPS = 1e-08
ADAM_WD = 0.01
ADAM_STEP = 10
PER_EXAMPLE_BATCH_AXIS = {'x': 0, 'loss_target': 0}
SHARED_INPUTS = []
_WEIGHT_DTYPES = {'norm_pre': _jnp.float32, 'norm_post': _jnp.float32, 'a_w_in': _jnp.float32, 'a_ln_g': _jnp.float32, 'a_ln_b': _jnp.float32, 'a_w_s': _jnp.float32, 'a_b_s': _jnp.float32, 'a_w_out': _jnp.float32, 'b_w_in': _jnp.float32, 'b_w_grp': _jnp.float32, 'b_scale': _jnp.float32, 'b_w_out': _jnp.float32}
MOMENT_SCALE = {'norm_pre': 1.111159e+00, 'norm_post': 3.196737e+01, 'a_w_in': 5.302090e-01, 'a_ln_g': 3.468705e-01, 'a_ln_b': 3.685630e-01, 'a_w_s': 5.013886e-01, 'a_b_s': 5.750019e-01, 'a_w_out': 8.799122e-01, 'b_w_in': 4.377586e-01, 'b_w_grp': 4.608999e-01, 'b_scale': 4.679965e-01, 'b_w_out': 6.628797e-01}


def _to_microbatches(a, axis):
    t = _jnp.moveaxis(a, axis, 0)
    t = t.reshape((N_MICROBATCH, t.shape[0] // N_MICROBATCH) + t.shape[1:])
    return _jnp.moveaxis(t, 1, axis + 1)


def setup_inputs(seed: int = 0) -> dict:
    inp = _fwd_setup_inputs(seed)
    key = _jax.random.fold_in(_jax.random.key(seed), 7919)
    shape, _ = _output_shape()
    out = dict(inp)
    out["loss_target"] = _jax.random.normal(_jax.random.fold_in(key, 0), shape, _jnp.float32)
    for i, name in enumerate(TWIN_WEIGHTS):
        w = inp[name].astype(_jnp.float32)
        if MOMENT_SCALE is None:
            s = _jnp.sqrt(_jnp.mean(_jnp.square(w)) + 1e-30)
        else:
            s = MOMENT_SCALE[name]
        km, kv = _jax.random.split(_jax.random.fold_in(key, i + 1))
        out[name] = w
        out["m_" + name] = s * _jax.random.normal(km, w.shape, _jnp.float32)
        out["v_" + name] = (s * s) * _jax.random.uniform(kv, w.shape, _jnp.float32, 0.5, 1.5)
    if N_MICROBATCH > 1:
        for name, axis in PER_EXAMPLE_BATCH_AXIS.items():
            out[name] = _to_microbatches(out[name], axis)
    return {'x': out['x'], 'norm_pre': out['norm_pre'], 'norm_post': out['norm_post'], 'a_w_in': out['a_w_in'], 'a_ln_g': out['a_ln_g'], 'a_ln_b': out['a_ln_b'], 'a_w_s': out['a_w_s'], 'a_b_s': out['a_b_s'], 'a_w_out': out['a_w_out'], 'b_w_in': out['b_w_in'], 'b_w_grp': out['b_w_grp'], 'b_scale': out['b_scale'], 'b_w_out': out['b_w_out'], 'loss_target': out['loss_target'], 'm_norm_pre': out['m_norm_pre'], 'm_norm_post': out['m_norm_post'], 'm_a_w_in': out['m_a_w_in'], 'm_a_ln_g': out['m_a_ln_g'], 'm_a_ln_b': out['m_a_ln_b'], 'm_a_w_s': out['m_a_w_s'], 'm_a_b_s': out['m_a_b_s'], 'm_a_w_out': out['m_a_w_out'], 'm_b_w_in': out['m_b_w_in'], 'm_b_w_grp': out['m_b_w_grp'], 'm_b_scale': out['m_b_scale'], 'm_b_w_out': out['m_b_w_out'], 'v_norm_pre': out['v_norm_pre'], 'v_norm_post': out['v_norm_post'], 'v_a_w_in': out['v_a_w_in'], 'v_a_ln_g': out['v_a_ln_g'], 'v_a_ln_b': out['v_a_ln_b'], 'v_a_w_s': out['v_a_w_s'], 'v_a_b_s': out['v_a_b_s'], 'v_a_w_out': out['v_a_w_out'], 'v_b_w_in': out['v_b_w_in'], 'v_b_w_grp': out['v_b_w_grp'], 'v_b_scale': out['v_b_scale'], 'v_b_w_out': out['v_b_w_out']}


def _loss(weights, diff, rest, loss_target):
    with _jax.named_scope("forward"):
        args = {**rest, TWIN_DIFF_INPUT: diff, **{k: w.astype(_WEIGHT_DTYPES[k]) for k, w in weights.items()}}
        y = _forward(args)
    with _jax.named_scope("loss_head"):
        err = _jnp.square(y.astype(_jnp.float32) - loss_target)
        return 0.5 * _jnp.sum(_jnp.mean(err, axis=-1)) if err.ndim else 0.5 * err


def _adamw(w, g, m, v):
    m = ADAM_B1 * m + (1.0 - ADAM_B1) * g
    v = ADAM_B2 * v + (1.0 - ADAM_B2) * _jnp.square(g)
    m_hat = m / (1.0 - ADAM_B1 ** ADAM_STEP)
    v_hat = v / (1.0 - ADAM_B2 ** ADAM_STEP)
    delta = -ADAM_LR * (m_hat / (_jnp.sqrt(v_hat) + ADAM_EPS) + ADAM_WD * w)
    return delta, m, v


def reference(x, norm_pre, norm_post, a_w_in, a_ln_g, a_ln_b, a_w_s, a_b_s, a_w_out, b_w_in, b_w_grp, b_scale, b_w_out, loss_target, m_norm_pre, m_norm_post, m_a_w_in, m_a_ln_g, m_a_ln_b, m_a_w_s, m_a_b_s, m_a_w_out, m_b_w_in, m_b_w_grp, m_b_scale, m_b_w_out, v_norm_pre, v_norm_post, v_a_w_in, v_a_ln_g, v_a_ln_b, v_a_w_s, v_a_b_s, v_a_w_out, v_b_w_in, v_b_w_grp, v_b_scale, v_b_w_out):
    given = dict(x=x, norm_pre=norm_pre, norm_post=norm_post, a_w_in=a_w_in, a_ln_g=a_ln_g, a_ln_b=a_ln_b, a_w_s=a_w_s, a_b_s=a_b_s, a_w_out=a_w_out, b_w_in=b_w_in, b_w_grp=b_w_grp, b_scale=b_scale, b_w_out=b_w_out, loss_target=loss_target, m_norm_pre=m_norm_pre, m_norm_post=m_norm_post, m_a_w_in=m_a_w_in, m_a_ln_g=m_a_ln_g, m_a_ln_b=m_a_ln_b, m_a_w_s=m_a_w_s, m_a_b_s=m_a_b_s, m_a_w_out=m_a_w_out, m_b_w_in=m_b_w_in, m_b_w_grp=m_b_w_grp, m_b_scale=m_b_scale, m_b_w_out=m_b_w_out, v_norm_pre=v_norm_pre, v_norm_post=v_norm_post, v_a_w_in=v_a_w_in, v_a_ln_g=v_a_ln_g, v_a_ln_b=v_a_ln_b, v_a_w_s=v_a_w_s, v_a_b_s=v_a_b_s, v_a_w_out=v_a_w_out, v_b_w_in=v_b_w_in, v_b_w_grp=v_b_w_grp, v_b_scale=v_b_scale, v_b_w_out=v_b_w_out)
    weights = {n: given[n] for n in TWIN_WEIGHTS}
    shared = {n: given[n] for n in SHARED_INPUTS}
    per_example = {n: given[n] for n in ['x']}
    grad_fn = _jax.value_and_grad(_loss, argnums=(0, 1))

    def one_microbatch(ex, loss_target):
        ex = dict(ex)
        diff = ex.pop(TWIN_DIFF_INPUT)
        return grad_fn(weights, diff, {**shared, **ex}, loss_target)

    if N_MICROBATCH == 1:
        loss, (grad_w, grad_x) = one_microbatch(per_example, given["loss_target"])
    else:
        def body(carry, xs):
            loss_sum, grad_sum = carry
            l_k, (gw_k, gx_k) = one_microbatch(xs[0], xs[1])
            with _jax.named_scope("update"):
                return (loss_sum + l_k, _jax.tree.map(_jnp.add, grad_sum, gw_k)), gx_k

        init = (_jnp.zeros((), _jnp.float32), _jax.tree.map(_jnp.zeros_like, weights))
        (loss, grad_w), grad_x = _jax.lax.scan(body, init, (per_example, given["loss_target"]))
    with _jax.named_scope("update"):
        delta_w, new_m, new_v = {}, {}, {}
        for n in TWIN_WEIGHTS:
            delta_w[n], new_m[n], new_v[n] = _adamw(weights[n], grad_w[n], given["m_" + n], given["v_" + n])
    return (loss, grad_x, *[grad_w[n] for n in TWIN_WEIGHTS], *[delta_w[n] for n in TWIN_WEIGHTS],
            *[new_m[n] for n in TWIN_WEIGHTS], *[new_v[n] for n in TWIN_WEIGHTS])
```

```python
import jax
import jax.numpy as jnp
from jax import lax
from jax.experimental import pallas as pl
from jax.experimental.pallas import tpu as pltpu

F32 = jnp.float32
BF16 = jnp.bfloat16
MESH = pl.DeviceIdType.MESH

N_DEV = 8
EPS = 1e-6
CHUNK = 64
GMLP_BLOCK = 128
A_GROUPS = 8
POOL_WINDOWS = (2, 4, 8, 16)
HALO = 16
ADAM_LR = 0.001
ADAM_B1 = 0.9
ADAM_B2 = 0.999
ADAM_EPS = 1e-08
ADAM_WD = 0.01
ADAM_STEP = 10
GELU_C = 0.7978845608028654
GELU_A = 0.044715
ROW_CHUNK = 16
VMEM_LIMIT_BYTES = 56 * 1024 * 1024


def _params(**kw):
    return pltpu.CompilerParams(vmem_limit_bytes=VMEM_LIMIT_BYTES, **kw)


def _gelu(x):
    return 0.5 * x * (1.0 + jnp.tanh(GELU_C * (x + GELU_A * (x * x * x))))


def _gelu_and_grad(x):
    x2 = x * x
    t = jnp.tanh(GELU_C * (x + GELU_A * (x2 * x)))
    val = 0.5 * x * (1.0 + t)
    grad = 0.5 * (1.0 + t) + 0.5 * x * (1.0 - t * t) * (GELU_C * (1.0 + 3.0 * GELU_A * x2))
    return val, grad


def _sigmoid(z):
    return 1.0 / (1.0 + jnp.exp(-z))


def _dot(a, b):
    return jnp.dot(a, b, preferred_element_type=F32)


def _dot_nt(a, b):
    return lax.dot_general(a, b, (((1,), (1,)), ((), ())), preferred_element_type=F32)


def _dot_tn(a, b):
    return lax.dot_general(a, b, (((0,), (0,)), ((), ())), preferred_element_type=F32)


def _rms_stats(xf):
    r = lax.rsqrt(jnp.mean(xf * xf, axis=-1, keepdims=True) + EPS)
    return r, xf * r


def _rms_bwd(dy, g, r, xh):
    dxh = dy * g
    return r * (dxh - xh * jnp.mean(dxh * xh, axis=-1, keepdims=True))


def _spatial_mask(transposed=False):
    p = lax.broadcasted_iota(jnp.int32, (GMLP_BLOCK, GMLP_BLOCK), 0)
    q = lax.broadcasted_iota(jnp.int32, (GMLP_BLOCK, GMLP_BLOCK), 1)
    if transposed:
        p, q = q, p
    return (q // CHUNK) <= (p // CHUNK)


def _position():
    x, y, c = lax.axis_index("x"), lax.axis_index("y"), lax.axis_index("c")
    return x, y, c


def _all_gather(arrs, name):
    n = len(arrs)

    def body(*refs):
        ins, outs = refs[:n], refs[n:2 * n]
        send_sems, recv_sems, local_sems = refs[2 * n:]
        x, y, c = _position()
        sibling = (x, y, 1 - c)
        chips = [(1 - x, y), (x, 1 - y), (1 - x, 1 - y)]

        def index(px, py, pc):
            return 4 * px + 2 * py + pc

        me = index(x, y, c)

        def copy(a, k, block, to, src=None):
            return pltpu.make_async_remote_copy(
                src_ref=outs[a].at[block] if src is None else src, dst_ref=outs[a].at[block],
                send_sem=send_sems.at[a, k], recv_sem=recv_sems.at[a, k], device_id=to, device_id_type=MESH)

        own = [pltpu.make_async_copy(ins[a], outs[a].at[me], local_sems.at[a]) for a in range(n)]
        for cp in own:
            cp.start()
        sent = []
        for a in range(n):
            sent.append(copy(a, 0, me, sibling, src=ins[a]))
            for j, chip in enumerate(chips):
                sent.append(copy(a, 1 + j, me, (*chip, c), src=ins[a]))
        for cp in sent:
            cp.start()
        for j, chip in enumerate(chips):
            for a in range(n):
                copy(a, 1 + j, index(*chip, c), (x, y, c)).wait_recv()
                passed = copy(a, 4 + j, index(*chip, c), sibling)
                passed.start()
                sent.append(passed)
        for a in range(n):
            copy(a, 0, index(x, y, 1 - c), (x, y, c)).wait_recv()
        for j, chip in enumerate(chips):
            for a in range(n):
                copy(a, 4 + j, index(*chip, 1 - c), (x, y, c)).wait_recv()
        for cp in sent:
            cp.wait_send()
        for cp in own:
            cp.wait()

    any_spec = pl.BlockSpec(memory_space=pl.ANY)
    return pl.pallas_call(
        body, name=name,
        out_shape=[jax.ShapeDtypeStruct((N_DEV,) + a.shape, a.dtype) for a in arrs],
        in_specs=[any_spec] * n, out_specs=[any_spec] * n,
        scratch_shapes=[pltpu.SemaphoreType.DMA((n, 7)), pltpu.SemaphoreType.DMA((n, 7)),
                        pltpu.SemaphoreType.DMA((n,))],
        compiler_params=pltpu.CompilerParams(has_side_effects=True),
    )(*arrs)


def _exchange_slabs(arrs, name):
    n = len(arrs)

    def body(*refs):
        ins, outs = refs[:n], refs[n:2 * n]
        send_sems, recv_sems, local_sems = refs[2 * n:]
        x, y, c = _position()
        me = 4 * x + 2 * y + c

        def peer(r):
            px = 1 - x if r & 4 else x
            py = 1 - y if r & 2 else y
            pc = 1 - c if r & 1 else c
            return (px, py, pc), 4 * px + 2 * py + pc

        own = [pltpu.make_async_copy(ins[a].at[me], outs[a].at[me], local_sems.at[a]) for a in range(n)]
        for cp in own:
            cp.start()
        sent = []
        for r in range(1, N_DEV):
            to, to_index = peer(r)
            for a in range(n):
                cp = pltpu.make_async_remote_copy(
                    src_ref=ins[a].at[to_index], dst_ref=outs[a].at[me],
                    send_sem=send_sems.at[a, r - 1], recv_sem=recv_sems.at[a, r - 1],
                    device_id=to, device_id_type=MESH)
                cp.start()
                sent.append(cp)
        for r in range(1, N_DEV):
            frm, from_index = peer(r)
            for a in range(n):
                pltpu.make_async_remote_copy(
                    src_ref=ins[a].at[from_index], dst_ref=outs[a].at[from_index],
                    send_sem=send_sems.at[a, r - 1], recv_sem=recv_sems.at[a, r - 1],
                    device_id=frm, device_id_type=MESH).wait_recv()
        for cp in sent:
            cp.wait_send()
        for cp in own:
            cp.wait()

    any_spec = pl.BlockSpec(memory_space=pl.ANY)
    return pl.pallas_call(
        body, name=name,
        out_shape=[jax.ShapeDtypeStruct(a.shape, a.dtype) for a in arrs],
        in_specs=[any_spec] * n, out_specs=[any_spec] * n,
        scratch_shapes=[pltpu.SemaphoreType.DMA((n, 7)), pltpu.SemaphoreType.DMA((n, 7)),
                        pltpu.SemaphoreType.DMA((n,))],
        compiler_params=pltpu.CompilerParams(has_side_effects=True),
    )(*arrs)


def _in_proj_fwd(x, g_row, w_all, layer, name):
    T, D = x.shape
    NS = w_all.shape[-1]
    TM = min(T, 1024)

    def body(x_ref, g_ref, w_ref, proj_ref, h_ref):
        @pl.when(pl.program_id(1) == 0)
        def _():
            _, xh = _rms_stats(x_ref[...])
            h_ref[...] = (xh * g_ref[...]).astype(BF16)

        proj_ref[...] = _dot(h_ref[...], w_ref[...]).astype(BF16)

    return pl.pallas_call(
        body, name=name, grid=(T // TM, N_DEV),
        in_specs=[pl.BlockSpec((TM, D), lambda i, k: (i, 0)),
                  pl.BlockSpec((1, D), lambda i, k: (0, 0)),
                  pl.BlockSpec((None, None, D, NS), lambda i, k: (k, layer, 0, 0))],
        out_specs=[pl.BlockSpec((TM, NS), lambda i, k: (i, k)),
                   pl.BlockSpec((TM, D), lambda i, k: (i, 0))],
        out_shape=[jax.ShapeDtypeStruct((T, N_DEV * NS), BF16), jax.ShapeDtypeStruct((T, D), BF16)],
        compiler_params=_params(),
    )(x, g_row, w_all)


def _a_mix_fwd(proj, ln_g, ln_b, w_s, b_s, name):
    T, E3 = proj.shape
    E = E3 // 3
    G, P = A_GROUPS, GMLP_BLOCK
    GD = E // G
    TB = min(T, 512)

    def body(p_ref, lg_ref, lb_ref, ws_ref, bs_ref, y_ref, v_s, us_s):
        def chunk(ci, carry):
            rows = pl.ds(pl.multiple_of(ci * ROW_CHUNK, ROW_CHUNK), ROW_CHUNK)
            vg = _gelu(p_ref[rows, E:2 * E].astype(F32))
            xc = vg - jnp.mean(vg, axis=-1, keepdims=True)
            rstd = lax.rsqrt(jnp.mean(xc * xc, axis=-1, keepdims=True) + EPS)
            v_s[rows, :] = ((xc * rstd) * lg_ref[...] + lb_ref[...]).astype(BF16)
            z = p_ref[rows, 2 * E:3 * E].astype(F32)
            us_s[rows, :] = _gelu(p_ref[rows, 0:E].astype(F32)) * (z * _sigmoid(z))
            return carry

        lax.fori_loop(0, TB // ROW_CHUNK, chunk, 0)
        mask = _spatial_mask()
        for g in range(G):
            wm = jnp.where(mask, ws_ref[g], 0.0).astype(BF16)
            cols = slice(g * GD, (g + 1) * GD)
            for b in range(TB // P):
                rows = slice(b * P, (b + 1) * P)
                mixed = _dot(wm, v_s[rows, cols]) + bs_ref[g]
                y_ref[rows, cols] = (us_s[rows, cols] * mixed).astype(BF16)

    return pl.pallas_call(
        body, name=name, grid=(T // TB,),
        in_specs=[pl.BlockSpec((TB, E3), lambda i: (i, 0)),
                  pl.BlockSpec((1, E), lambda i: (0, 0)),
                  pl.BlockSpec((1, E), lambda i: (0, 0)),
                  pl.BlockSpec((G, P, P), lambda i: (0, 0, 0)),
                  pl.BlockSpec((G, P, 1), lambda i: (0, 0, 0))],
        out_specs=pl.BlockSpec((TB, E), lambda i: (i, 0)),
        out_shape=jax.ShapeDtypeStruct((T, E), BF16),
        scratch_shapes=[pltpu.VMEM((TB, E), BF16), pltpu.VMEM((TB, E), F32)],
        compiler_params=_params(),
    )(proj, ln_g, ln_b, w_s, b_s)


def _window_sum_back(ext, win):
    s, k = ext, 1
    while k < win:
        s = s + pltpu.roll(s, k, axis=0)
        k *= 2
    return s


def _window_sum_ahead(ext, win):
    n = ext.shape[0]
    s, k = ext, 1
    while k < win:
        s = s + pltpu.roll(s, n - k, axis=0)
        k *= 2
    return s


def _inv_count(t0, rows, win):
    t1 = t0 + 1 + lax.broadcasted_iota(jnp.int32, (rows, 1), 0)
    return 1.0 / jnp.minimum(t1, win).astype(F32)


def _b_mix_fwd(proj, scale, wg_all, layer, name):
    T, E2 = proj.shape
    E = E2 // 2
    NG = len(POOL_WINDOWS)
    GB = E // NG
    TB = min(T, 256)
    RS = wg_all.shape[-2]

    def body(p_ref, sc_ref, wg_ref, y_ref, o_ref, carry_s):
        i = pl.program_id(0)

        @pl.when(i == 0)
        def _():
            carry_s[...] = jnp.zeros_like(carry_s)

        for g, win in enumerate(POOL_WINDOWS):
            cols = slice(g * GB, (g + 1) * GB)
            xg = p_ref[:, cols].astype(F32)
            ext = jnp.concatenate([carry_s[:, cols], xg], axis=0)
            pooled = _window_sum_back(ext, win)[HALO:, :] * _inv_count(i * TB, TB, win) - xg
            carry_s[:, cols] = xg[TB - HALO:, :]
            o = _dot(pooled.astype(BF16), wg_ref[:, g].reshape(GB, GB))
            o_ref[:, cols] = o.astype(BF16)
            z = p_ref[:, E + g * GB:E + (g + 1) * GB].astype(F32)
            y_ref[:, cols] = ((o * sc_ref[:, cols]) * (z * _sigmoid(z))).astype(BF16)

    return pl.pallas_call(
        body, name=name, grid=(T // TB,),
        in_specs=[pl.BlockSpec((TB, E2), lambda i: (i, 0)),
                  pl.BlockSpec((1, E), lambda i: (0, 0)),
                  pl.BlockSpec((N_DEV, None, NG, RS, GB), lambda i: (0, layer, 0, 0, 0))],
        out_specs=[pl.BlockSpec((TB, E), lambda i: (i, 0)), pl.BlockSpec((TB, E), lambda i: (i, 0))],
        out_shape=[jax.ShapeDtypeStruct((T, E), BF16), jax.ShapeDtypeStruct((T, E), BF16)],
        scratch_shapes=[pltpu.VMEM((HALO, E), F32)],
        compiler_params=_params(),
    )(proj, scale, wg_all)


def _out_proj_fwd(y, w_all, x, g_row, layer, name):
    T, E = y.shape
    D = x.shape[1]
    ES = w_all.shape[-2]
    TM = min(T, 512)

    def body(y_ref, w_ref, x_ref, g_ref, xn_ref, out_ref):
        o = _dot(y_ref[...], w_ref[...].reshape(E, D))
        out_ref[...] = o
        _, oh = _rms_stats(o)
        xn_ref[...] = x_ref[...] + oh * g_ref[...]

    return pl.pallas_call(
        body, name=name, grid=(T // TM,),
        in_specs=[pl.BlockSpec((TM, E), lambda i: (i, 0)),
                  pl.BlockSpec((N_DEV, None, ES, D), lambda i: (0, layer, 0, 0)),
                  pl.BlockSpec((TM, D), lambda i: (i, 0)),
                  pl.BlockSpec((1, D), lambda i: (0, 0))],
        out_specs=[pl.BlockSpec((TM, D), lambda i: (i, 0)), pl.BlockSpec((TM, D), lambda i: (i, 0))],
        out_shape=[jax.ShapeDtypeStruct((T, D), F32), jax.ShapeDtypeStruct((T, D), F32)],
        compiler_params=_params(),
    )(y, w_all, x, g_row)


def _loss_head(x, target, name):
    T, D = x.shape
    TM = min(T, 512)
    nT = T // TM

    def body(x_ref, t_ref, dx_ref, loss_ref, acc_s):
        i = pl.program_id(0)

        @pl.when(i == 0)
        def _():
            acc_s[...] = jnp.zeros_like(acc_s)

        e = x_ref[...] - t_ref[...]
        dx_ref[...] = e * (1.0 / D)
        acc_s[...] += jnp.sum(e * e, axis=0, keepdims=True)

        @pl.when(i == nT - 1)
        def _():
            total = jnp.sum(acc_s[...], axis=1, keepdims=True) * (0.5 / D)
            loss_ref[...] = jnp.broadcast_to(total, loss_ref.shape)

    return pl.pallas_call(
        body, name=name, grid=(nT,),
        in_specs=[pl.BlockSpec((TM, D), lambda i: (i, 0)), pl.BlockSpec((TM, D), lambda i: (i, 0))],
        out_specs=[pl.BlockSpec((TM, D), lambda i: (i, 0)), pl.BlockSpec((1, 128), lambda i: (0, 0))],
        out_shape=[jax.ShapeDtypeStruct((T, D), F32), jax.ShapeDtypeStruct((1, 128), F32)],
        scratch_shapes=[pltpu.VMEM((1, D), F32)],
        compiler_params=_params(),
    )(x, target)


def _out_proj_bwd(dxn, out, g_row, w_all, y, layer, name):
    T, D = dxn.shape
    E = y.shape[1]
    ES = w_all.shape[-2]
    TM = min(T, 512)
    nT = T // TM

    def body(dxn_ref, out_ref, g_ref, w_ref, y_ref, dy_ref, dw_ref, dg_ref, acc_s):
        i = pl.program_id(0)

        @pl.when(i == 0)
        def _():
            acc_s[...] = jnp.zeros_like(acc_s)
            dg_ref[...] = jnp.zeros_like(dg_ref)

        dxn_v = dxn_ref[...]
        r, oh = _rms_stats(out_ref[...])
        dg_ref[...] += jnp.sum(dxn_v * oh, axis=0, keepdims=True)
        dout = _rms_bwd(dxn_v, g_ref[...], r, oh).astype(BF16)
        dy_ref[...] = _dot_nt(dout, w_ref[...].reshape(E, D)).astype(BF16)
        acc_s[...] += _dot_tn(y_ref[...], dout)

        @pl.when(i == nT - 1)
        def _():
            dw_ref[...] = acc_s[...].reshape(N_DEV, ES, D).astype(BF16)

    return pl.pallas_call(
        body, name=name, grid=(nT,),
        in_specs=[pl.BlockSpec((TM, D), lambda i: (i, 0)),
                  pl.BlockSpec((TM, D), lambda i: (i, 0)),
                  pl.BlockSpec((1, D), lambda i: (0, 0)),
                  pl.BlockSpec((N_DEV, None, ES, D), lambda i: (0, layer, 0, 0)),
                  pl.BlockSpec((TM, E), lambda i: (i, 0))],
        out_specs=[pl.BlockSpec((TM, E), lambda i: (i, 0)),
                   pl.BlockSpec((N_DEV, ES, D), lambda i: (0, 0, 0)),
                   pl.BlockSpec((1, D), lambda i: (0, 0))],
        out_shape=[jax.ShapeDtypeStruct((T, E), BF16), jax.ShapeDtypeStruct((N_DEV, ES, D), BF16),
                   jax.ShapeDtypeStruct((1, D), F32)],
        scratch_shapes=[pltpu.VMEM((E, D), F32)],
        compiler_params=_params(),
    )(dxn, out, g_row, w_all, y)


def _a_mix_bwd(proj, dy, ln_g, ln_b, w_s, b_s, name):
    T, E3 = proj.shape
    E = E3 // 3
    G, P = A_GROUPS, GMLP_BLOCK
    GD = E // G
    TB = min(T, 256)

    def body(p_ref, dy_ref, lg_ref, lb_ref, ws_ref, bs_ref, dp_ref, dws_ref, dbs_ref, dlg_ref, dlb_ref,
             v_s, xh_s, rstd_s, a_s, bz_s, c_s, dv_s):
        @pl.when(pl.program_id(0) == 0)
        def _():
            dws_ref[...] = jnp.zeros_like(dws_ref)
            dbs_ref[...] = jnp.zeros_like(dbs_ref)
            dlg_ref[...] = jnp.zeros_like(dlg_ref)
            dlb_ref[...] = jnp.zeros_like(dlb_ref)

        def recompute(ci, carry):
            rows = pl.ds(pl.multiple_of(ci * ROW_CHUNK, ROW_CHUNK), ROW_CHUNK)
            vg = _gelu(p_ref[rows, E:2 * E].astype(F32))
            xc = vg - jnp.mean(vg, axis=-1, keepdims=True)
            rstd = lax.rsqrt(jnp.mean(xc * xc, axis=-1, keepdims=True) + EPS)
            xh = xc * rstd
            xh_s[rows, :] = xh
            rstd_s[rows, :] = rstd
            v_s[rows, :] = (xh * lg_ref[...] + lb_ref[...]).astype(BF16)
            u, du = _gelu_and_grad(p_ref[rows, 0:E].astype(F32))
            z = p_ref[rows, 2 * E:3 * E].astype(F32)
            sg = _sigmoid(z)
            s = z * sg
            ds = sg * (1.0 + z * (1.0 - sg))
            dyv = dy_ref[rows, :].astype(F32)
            a_s[rows, :] = dyv * s * du
            bz_s[rows, :] = dyv * u * ds
            c_s[rows, :] = (dyv * u * s).astype(BF16)
            return carry

        lax.fori_loop(0, TB // ROW_CHUNK, recompute, 0)

        mask = _spatial_mask()
        mask_t = _spatial_mask(transposed=True)
        for g in range(G):
            w_g = ws_ref[g]
            wm = jnp.where(mask, w_g, 0.0).astype(BF16)
            wm_t = jnp.where(mask_t, w_g.T, 0.0).astype(BF16)
            cols = slice(g * GD, (g + 1) * GD)
            dws_g = jnp.zeros((P, P), F32)
            dbs_g = jnp.zeros((P, 1), F32)
            for b in range(TB // P):
                rows = slice(b * P, (b + 1) * P)
                vb = v_s[rows, cols]
                cb = c_s[rows, cols]
                mixed = _dot(wm, vb) + bs_ref[g]
                dp_ref[rows, g * GD:(g + 1) * GD] = (a_s[rows, cols] * mixed).astype(BF16)
                dp_ref[rows, 2 * E + g * GD:2 * E + (g + 1) * GD] = (bz_s[rows, cols] * mixed).astype(BF16)
                dv_s[rows, cols] = _dot(wm_t, cb)
                dws_g = dws_g + _dot_nt(cb, vb)
                dbs_g = dbs_g + jnp.sum(cb.astype(F32), axis=1, keepdims=True)
            dws_ref[g] += jnp.where(mask, dws_g, 0.0)
            dbs_ref[g] += dbs_g

        def ln_bwd(ci, carry):
            rows = pl.ds(pl.multiple_of(ci * ROW_CHUNK, ROW_CHUNK), ROW_CHUNK)
            dv = dv_s[rows, :]
            xh = xh_s[rows, :]
            dlg_ref[...] += jnp.sum(dv * xh, axis=0, keepdims=True)
            dlb_ref[...] += jnp.sum(dv, axis=0, keepdims=True)
            dxh = dv * lg_ref[...]
            dvg = rstd_s[rows, :] * (dxh - jnp.mean(dxh, axis=-1, keepdims=True)
                                     - xh * jnp.mean(dxh * xh, axis=-1, keepdims=True))
            _, dgl = _gelu_and_grad(p_ref[rows, E:2 * E].astype(F32))
            dp_ref[rows, E:2 * E] = (dvg * dgl).astype(BF16)
            return carry

        lax.fori_loop(0, TB // ROW_CHUNK, ln_bwd, 0)

    return pl.pallas_call(
        body, name=name, grid=(T // TB,),
        in_specs=[pl.BlockSpec((TB, E3), lambda i: (i, 0)),
                  pl.BlockSpec((TB, E), lambda i: (i, 0)),
                  pl.BlockSpec((1, E), lambda i: (0, 0)),
                  pl.BlockSpec((1, E), lambda i: (0, 0)),
                  pl.BlockSpec((G, P, P), lambda i: (0, 0, 0)),
                  pl.BlockSpec((G, P, 1), lambda i: (0, 0, 0))],
        out_specs=[pl.BlockSpec((TB, E3), lambda i: (i, 0)),
                   pl.BlockSpec((G, P, P), lambda i: (0, 0, 0)),
                   pl.BlockSpec((G, P, 1), lambda i: (0, 0, 0)),
                   pl.BlockSpec((1, E), lambda i: (0, 0)),
                   pl.BlockSpec((1, E), lambda i: (0, 0))],
        out_shape=[jax.ShapeDtypeStruct((T, E3), BF16), jax.ShapeDtypeStruct((G, P, P), F32),
                   jax.ShapeDtypeStruct((G, P, 1), F32), jax.ShapeDtypeStruct((1, E), F32),
                   jax.ShapeDtypeStruct((1, E), F32)],
        scratch_shapes=[pltpu.VMEM((TB, E), BF16), pltpu.VMEM((TB, E), F32), pltpu.VMEM((TB, 1), F32),
                        pltpu.VMEM((TB, E), F32), pltpu.VMEM((TB, E), F32), pltpu.VMEM((TB, E), BF16),
                        pltpu.VMEM((TB, E), F32)],
        compiler_params=_params(),
    )(proj, dy, ln_g, ln_b, w_s, b_s)


def _b_mix_bwd(proj, dy, o, scale, wg_all, layer, name):
    T, E2 = proj.shape
    E = E2 // 2
    NG = len(POOL_WINDOWS)
    GB = E // NG
    TB = min(T, 256)
    nT = T // TB
    RS = wg_all.shape[-2]
    halo_per_tile = TB // HALO

    def body(p_ref, halo_ref, dy_ref, o_ref, sc_ref, wg_ref, dp_ref, dsc_ref, dwg_ref, acc_s, carry_s):
        i = pl.program_id(0)
        tile = nT - 1 - i

        @pl.when(i == 0)
        def _():
            acc_s[...] = jnp.zeros_like(acc_s)
            carry_s[...] = jnp.zeros_like(carry_s)
            dsc_ref[...] = jnp.zeros_like(dsc_ref)

        has_history = (tile > 0).astype(F32)
        for g, win in enumerate(POOL_WINDOWS):
            cols = slice(g * GB, (g + 1) * GB)
            inv = _inv_count(tile * TB, TB, win)
            xg = p_ref[:, cols].astype(F32)
            ext = jnp.concatenate([halo_ref[:, cols].astype(F32) * has_history, xg], axis=0)
            pooled = _window_sum_back(ext, win)[HALO:, :] * inv - xg
            z = p_ref[:, E + g * GB:E + (g + 1) * GB].astype(F32)
            sg = _sigmoid(z)
            dyv = dy_ref[:, cols].astype(F32)
            ov = o_ref[:, cols].astype(F32)
            sc = sc_ref[:, cols]
            dmixed = dyv * (z * sg)
            dsc_ref[:, cols] += jnp.sum(dmixed * ov, axis=0, keepdims=True)
            dz = dyv * (ov * sc) * (sg * (1.0 + z * (1.0 - sg)))
            do = (dmixed * sc).astype(BF16)
            acc_s[:, g] += _dot_tn(pooled.astype(BF16), do).reshape(N_DEV, RS, GB)
            dpool = _dot_nt(do, wg_ref[:, g].reshape(GB, GB))
            q = dpool * inv
            ext_q = jnp.concatenate([q, carry_s[:, cols]], axis=0)
            dxb = _window_sum_ahead(ext_q, win)[:TB, :] - dpool
            carry_s[:, cols] = q[:HALO, :]
            dp_ref[:, cols] = dxb.astype(BF16)
            dp_ref[:, E + g * GB:E + (g + 1) * GB] = dz.astype(BF16)

        @pl.when(i == nT - 1)
        def _():
            dwg_ref[...] = acc_s[...].astype(BF16)

    return pl.pallas_call(
        body, name=name, grid=(nT,),
        in_specs=[pl.BlockSpec((TB, E2), lambda i: (nT - 1 - i, 0)),
                  pl.BlockSpec((HALO, E), lambda i: (jnp.maximum((nT - 1 - i) * halo_per_tile - 1, 0), 0)),
                  pl.BlockSpec((TB, E), lambda i: (nT - 1 - i, 0)),
                  pl.BlockSpec((TB, E), lambda i: (nT - 1 - i, 0)),
                  pl.BlockSpec((1, E), lambda i: (0, 0)),
                  pl.BlockSpec((N_DEV, None, NG, RS, GB), lambda i: (0, layer, 0, 0, 0))],
        out_specs=[pl.BlockSpec((TB, E2), lambda i: (nT - 1 - i, 0)),
                   pl.BlockSpec((1, E), lambda i: (0, 0)),
                   pl.BlockSpec((N_DEV, NG, RS, GB), lambda i: (0, 0, 0, 0))],
        out_shape=[jax.ShapeDtypeStruct((T, E2), BF16), jax.ShapeDtypeStruct((1, E), F32),
                   jax.ShapeDtypeStruct((N_DEV, NG, RS, GB), BF16)],
        scratch_shapes=[pltpu.VMEM((N_DEV, NG, RS, GB), F32), pltpu.VMEM((HALO, E), F32)],
        compiler_params=_params(),
    )(proj, proj, dy, o, scale, wg_all)


def _in_proj_bwd_dx(dproj, w_all, x, g_row, dxn, layer, name):
    T, D = x.shape
    NS = w_all.shape[-1]
    TM = min(T, 1024)

    def body(dp_ref, w_ref, x_ref, g_ref, dxn_ref, dx_ref, dg_ref, acc_s):
        i, k = pl.program_id(0), pl.program_id(1)

        @pl.when(k == 0)
        def _():
            acc_s[...] = jnp.zeros_like(acc_s)

        @pl.when((i == 0) & (k == 0))
        def _():
            dg_ref[...] = jnp.zeros_like(dg_ref)

        acc_s[...] += _dot_nt(dp_ref[...], w_ref[...])

        @pl.when(k == N_DEV - 1)
        def _():
            dh = acc_s[...]
            r, xh = _rms_stats(x_ref[...])
            dg_ref[...] += jnp.sum(dh * xh, axis=0, keepdims=True)
            dx_ref[...] = dxn_ref[...] + _rms_bwd(dh, g_ref[...], r, xh)

    return pl.pallas_call(
        body, name=name, grid=(T // TM, N_DEV),
        in_specs=[pl.BlockSpec((TM, NS), lambda i, k: (i, k)),
                  pl.BlockSpec((None, None, D, NS), lambda i, k: (k, layer, 0, 0)),
                  pl.BlockSpec((TM, D), lambda i, k: (i, 0)),
                  pl.BlockSpec((1, D), lambda i, k: (0, 0)),
                  pl.BlockSpec((TM, D), lambda i, k: (i, 0))],
        out_specs=[pl.BlockSpec((TM, D), lambda i, k: (i, 0)), pl.BlockSpec((1, D), lambda i, k: (0, 0))],
        out_shape=[jax.ShapeDtypeStruct((T, D), F32), jax.ShapeDtypeStruct((1, D), F32)],
        scratch_shapes=[pltpu.VMEM((TM, D), F32)],
        compiler_params=_params(),
    )(dproj, w_all, x, g_row, dxn)


def _dw_in(h, dproj, name):
    T, D = h.shape
    NS = dproj.shape[1] // N_DEV
    TK = min(T, 1024)
    nK = T // TK

    def body(h_ref, dp_ref, dw_ref, acc_s):
        t = pl.program_id(1)

        @pl.when(t == 0)
        def _():
            acc_s[...] = jnp.zeros_like(acc_s)

        acc_s[...] += _dot_tn(h_ref[...], dp_ref[...])

        @pl.when(t == nK - 1)
        def _():
            dw_ref[...] = acc_s[...].astype(BF16)

    return pl.pallas_call(
        body, name=name, grid=(N_DEV, nK),
        in_specs=[pl.BlockSpec((TK, D), lambda k, t: (t, 0)), pl.BlockSpec((TK, NS), lambda k, t: (t, k))],
        out_specs=pl.BlockSpec((None, D, NS), lambda k, t: (k, 0, 0)),
        out_shape=jax.ShapeDtypeStruct((N_DEV, D, NS), BF16),
        scratch_shapes=[pltpu.VMEM((D, NS), F32)],
        compiler_params=_params(),
    )(h, dproj)


def _reduce_adam(recvs, w, m, v, name):
    L, R, C = w.shape
    assert len(recvs) == L
    TR = R
    for cand in (256, 128, 64, 32, 16):
        if R % cand == 0 and R > cand:
            TR = cand
            break
    nR = R // TR
    c1 = 1.0 - ADAM_B1 ** ADAM_STEP
    c2 = 1.0 - ADAM_B2 ** ADAM_STEP

    def body(*refs):
        recv_refs = refs[:L]
        w_ref, m_ref, v_ref, g_ref, d_ref, nm_ref, nv_ref, g_s = refs[L:]
        layer = pl.program_id(0)
        for l in range(L):
            @pl.when(layer == l)
            def _(l=l):
                acc = recv_refs[l][0].astype(F32)
                for j in range(1, N_DEV):
                    acc = acc + recv_refs[l][j].astype(F32)
                g_s[...] = acc

        g = g_s[...]
        g_ref[...] = g
        nm = ADAM_B1 * m_ref[...] + (1.0 - ADAM_B1) * g
        nv = ADAM_B2 * v_ref[...] + (1.0 - ADAM_B2) * (g * g)
        nm_ref[...] = nm
        nv_ref[...] = nv
        d_ref[...] = -ADAM_LR * ((nm / c1) / (jnp.sqrt(nv / c2) + ADAM_EPS) + ADAM_WD * w_ref[...])

    def recv_spec(l):
        def index(layer, t):
            before = jnp.where(layer < l, 0, nR - 1)
            return (0, jnp.where(layer == l, t, before), 0)
        return pl.BlockSpec((N_DEV, TR, C), index)

    wspec = pl.BlockSpec((None, TR, C), lambda layer, t: (layer, t, 0))
    out = jax.ShapeDtypeStruct((L, R, C), F32)
    return pl.pallas_call(
        body, name=name, grid=(L, nR),
        in_specs=[recv_spec(l) for l in range(L)] + [wspec] * 3,
        out_specs=[wspec] * 4, out_shape=[out] * 4,
        scratch_shapes=[pltpu.VMEM((TR, C), F32)],
        compiler_params=_params(),
    )(*recvs, w, m, v)


SMALL = ("norm_pre", "norm_post", "a_ln_g", "a_ln_b", "a_w_s", "a_b_s")
PACK_LANES = 128
PACK_ROWS_MULTIPLE = 256


def _pack(arrays):
    flat = jnp.concatenate([a.reshape(-1) for a in arrays])
    tile = PACK_LANES * PACK_ROWS_MULTIPLE
    padded = -(-flat.shape[0] // tile) * tile
    return jnp.pad(flat, (0, padded - flat.shape[0])).reshape(1, padded // PACK_LANES, PACK_LANES)


def _unpack(packed, like):
    flat = packed.reshape(-1)
    out, at = [], 0
    for a in like:
        out.append(flat[at:at + a.size].reshape(a.shape))
        at += a.size
    return out


def kernel(x, norm_pre, norm_post, a_w_in, a_ln_g, a_ln_b, a_w_s, a_b_s, a_w_out, b_w_in, b_w_grp, b_scale, b_w_out, loss_target, m_norm_pre, m_norm_post, m_a_w_in, m_a_ln_g, m_a_ln_b, m_a_w_s, m_a_b_s, m_a_w_out, m_b_w_in, m_b_w_grp, m_b_scale, m_b_w_out, v_norm_pre, v_norm_post, v_a_w_in, v_a_ln_g, v_a_ln_b, v_a_w_s, v_a_b_s, v_a_w_out, v_b_w_in, v_b_w_grp, v_b_scale, v_b_w_out):
    weights = dict(norm_pre=norm_pre, norm_post=norm_post, a_w_in=a_w_in, a_ln_g=a_ln_g, a_ln_b=a_ln_b, a_w_s=a_w_s,
                   a_b_s=a_b_s, a_w_out=a_w_out, b_w_in=b_w_in, b_w_grp=b_w_grp, b_scale=b_scale, b_w_out=b_w_out)
    mom_m = dict(norm_pre=m_norm_pre, norm_post=m_norm_post, a_w_in=m_a_w_in, a_ln_g=m_a_ln_g, a_ln_b=m_a_ln_b,
                 a_w_s=m_a_w_s, a_b_s=m_a_b_s, a_w_out=m_a_w_out, b_w_in=m_b_w_in, b_w_grp=m_b_w_grp,
                 b_scale=m_b_scale, b_w_out=m_b_w_out)
    mom_v = dict(norm_pre=v_norm_pre, norm_post=v_norm_post, a_w_in=v_a_w_in, a_ln_g=v_a_ln_g, a_ln_b=v_a_ln_b,
                 a_w_s=v_a_w_s, a_b_s=v_a_b_s, a_w_out=v_a_w_out, b_w_in=v_b_w_in, b_w_grp=v_b_w_grp,
                 b_scale=v_b_scale, b_w_out=v_b_w_out)
    names = list(weights)

    depth = norm_pre.shape[0]
    x0 = x[0]
    target = loss_target[0]
    T, D = x0.shape
    E = a_ln_g.shape[1]
    G, P = A_GROUPS, GMLP_BLOCK

    a_in_all, a_out_all, b_in_all, b_grp_all, b_out_all, b_scale_all = _all_gather(
        [a_w_in.astype(BF16), a_w_out.astype(BF16), b_w_in.astype(BF16), b_w_grp.astype(BF16),
         b_w_out.astype(BF16), b_scale], "gather_weights")
    scale_full = jnp.transpose(b_scale_all, (1, 0, 2)).reshape(b_scale.shape[0], 1, E)

    saved = []
    xi = x0
    for i in range(depth):
        j = i // 2
        g_pre, g_post = norm_pre[i:i + 1], norm_post[i:i + 1]
        if i % 2 == 0:
            proj, h = _in_proj_fwd(xi, g_pre, a_in_all, j, f"a_in_fwd_{i}")
            y = _a_mix_fwd(proj, a_ln_g[j:j + 1], a_ln_b[j:j + 1], a_w_s[j], a_b_s[j].reshape(G, P, 1), f"a_mix_fwd_{i}")
            x_next, out = _out_proj_fwd(y, a_out_all, xi, g_post, j, f"a_out_fwd_{i}")
            saved.append((xi, h, proj, y, out, None))
        else:
            proj, h = _in_proj_fwd(xi, g_pre, b_in_all, j, f"b_in_fwd_{i}")
            y, o = _b_mix_fwd(proj, scale_full[j], b_grp_all, j, f"b_mix_fwd_{i}")
            x_next, out = _out_proj_fwd(y, b_out_all, xi, g_post, j, f"b_out_fwd_{i}")
            saved.append((xi, h, proj, y, out, o))
        xi = x_next

    dx, loss_row = _loss_head(xi, target, "loss_head")
    loss = lax.psum(loss_row[0, 0], ("x", "y", "c"))

    n_a, n_b = a_ln_g.shape[0], b_scale.shape[0]
    d_pre, d_post = [None] * depth, [None] * depth
    d_ln_g, d_ln_b, d_w_s, d_b_s = [None] * n_a, [None] * n_a, [None] * n_a, [None] * n_a
    recv = {k: [None] * (n_a if k.startswith("a_") else n_b)
            for k in ("a_w_in", "a_w_out", "b_w_in", "b_w_grp", "b_w_out", "b_scale")}
    for i in reversed(range(depth)):
        j = i // 2
        xi, h, proj, y, out, o = saved[i]
        g_pre, g_post = norm_pre[i:i + 1], norm_post[i:i + 1]
        if i % 2 == 0:
            dy, dw_out, d_post[i] = _out_proj_bwd(dx, out, g_post, a_out_all, y, j, f"a_out_bwd_{i}")
            dproj, d_w_s[j], dbs, d_ln_g[j], d_ln_b[j] = _a_mix_bwd(
                proj, dy, a_ln_g[j:j + 1], a_ln_b[j:j + 1], a_w_s[j], a_b_s[j].reshape(G, P, 1), f"a_mix_bwd_{i}")
            d_b_s[j] = dbs.reshape(G, P)
            dx, d_pre[i] = _in_proj_bwd_dx(dproj, a_in_all, xi, g_pre, dx, j, f"a_in_bwd_{i}")
            dw_in = _dw_in(h, dproj, f"a_dw_in_{i}")
            recv["a_w_in"][j], recv["a_w_out"][j] = _exchange_slabs([dw_in, dw_out], f"exchange_grads_{i}")
        else:
            dy, dw_out, d_post[i] = _out_proj_bwd(dx, out, g_post, b_out_all, y, j, f"b_out_bwd_{i}")
            dproj, dsc, dw_grp = _b_mix_bwd(proj, dy, o, scale_full[j], b_grp_all, j, f"b_mix_bwd_{i}")
            dx, d_pre[i] = _in_proj_bwd_dx(dproj, b_in_all, xi, g_pre, dx, j, f"b_in_bwd_{i}")
            dw_in = _dw_in(h, dproj, f"b_dw_in_{i}")
            recv["b_w_in"][j], recv["b_w_out"][j], recv["b_w_grp"][j], recv["b_scale"][j] = _exchange_slabs(
                [dw_in, dw_out, dw_grp, dsc.reshape(N_DEV, 1, E // N_DEV)], f"exchange_grads_{i}")

    small_grads = [jnp.concatenate(d_pre, axis=0), jnp.concatenate(d_post, axis=0),
                   jnp.concatenate(d_ln_g, axis=0), jnp.concatenate(d_ln_b, axis=0),
                   jnp.stack(d_w_s), jnp.stack(d_b_s)]
    small_like = [weights[k] for k in SMALL]
    (small_all,) = _all_gather([_pack(small_grads)[0]], "gather_small_grads")
    small_out = _reduce_adam([small_all], _pack(small_like), _pack([mom_m[k] for k in SMALL]),
                             _pack([mom_v[k] for k in SMALL]), "adam_small")
    results = {k: [] for k in names}
    for packed in small_out:
        for k, a in zip(SMALL, _unpack(packed, small_like)):
            results[k].append(a)

    def shard_view(a):
        return a.reshape(a.shape[0], -1, a.shape[-1])

    for k in ("a_w_in", "a_w_out", "b_w_in", "b_w_grp", "b_w_out"):
        w3 = shard_view(weights[k])
        recvs = [r.reshape(N_DEV, w3.shape[1], w3.shape[2]) for r in recv[k]]
        outs = _reduce_adam(recvs, w3, shard_view(mom_m[k]), shard_view(mom_v[k]), f"adam_{k}")
        results[k] = [o_.reshape(weights[k].shape) for o_ in outs]
    sc_recv = jnp.concatenate(recv["b_scale"], axis=1)
    outs = _reduce_adam([sc_recv], b_scale[None], m_b_scale[None], v_b_scale[None], "adam_b_scale")
    results["b_scale"] = [o_[0] for o_ in outs]

    grad_x = dx[None]
    return (loss, grad_x, *[results[k][0] for k in names], *[results[k][1] for k in names],
            *[results[k][2] for k in names], *[results[k][3] for k in names])
```

```python
import jax
import jax.numpy as jnp
from jax import lax
from jax.experimental import pallas as pl
from jax.experimental.pallas import tpu as pltpu

F32 = jnp.float32
BF16 = jnp.bfloat16
MESH = pl.DeviceIdType.MESH

N_DEV = 8
EPS = 1e-6
CHUNK = 64
GMLP_BLOCK = 128
A_GROUPS = 8
POOL_WINDOWS = (2, 4, 8, 16)
HALO = 16
ADAM_LR = 0.001
ADAM_B1 = 0.9
ADAM_B2 = 0.999
ADAM_EPS = 1e-08
ADAM_WD = 0.01
ADAM_STEP = 10
GELU_C = 0.7978845608028654
GELU_A = 0.044715
ROW_CHUNK = 16
VMEM_LIMIT_BYTES = 56 * 1024 * 1024


def _params(**kw):
    return pltpu.CompilerParams(vmem_limit_bytes=VMEM_LIMIT_BYTES, **kw)


def _gelu(x):
    return 0.5 * x * (1.0 + jnp.tanh(GELU_C * (x + GELU_A * (x * x * x))))


def _gelu_and_grad(x):
    x2 = x * x
    t = jnp.tanh(GELU_C * (x + GELU_A * (x2 * x)))
    val = 0.5 * x * (1.0 + t)
    grad = 0.5 * (1.0 + t) + 0.5 * x * (1.0 - t * t) * (GELU_C * (1.0 + 3.0 * GELU_A * x2))
    return val, grad


def _sigmoid(z):
    return 1.0 / (1.0 + jnp.exp(-z))


def _dot(a, b):
    return jnp.dot(a, b, preferred_element_type=F32)


def _dot_nt(a, b):
    return lax.dot_general(a, b, (((1,), (1,)), ((), ())), preferred_element_type=F32)


def _dot_tn(a, b):
    return lax.dot_general(a, b, (((0,), (0,)), ((), ())), preferred_element_type=F32)


def _rms_stats(xf):
    r = lax.rsqrt(jnp.mean(xf * xf, axis=-1, keepdims=True) + EPS)
    return r, xf * r


def _rms_bwd(dy, g, r, xh):
    dxh = dy * g
    return r * (dxh - xh * jnp.mean(dxh * xh, axis=-1, keepdims=True))


def _spatial_mask(transposed=False):
    p = lax.broadcasted_iota(jnp.int32, (GMLP_BLOCK, GMLP_BLOCK), 0)
    q = lax.broadcasted_iota(jnp.int32, (GMLP_BLOCK, GMLP_BLOCK), 1)
    if transposed:
        p, q = q, p
    return (q // CHUNK) <= (p // CHUNK)


def _position():
    x, y, c = lax.axis_index("x"), lax.axis_index("y"), lax.axis_index("c")
    return x, y, c


def _comm_scratch(n):
    return [pltpu.SemaphoreType.DMA((n, 7)), pltpu.SemaphoreType.DMA((n, 7)), pltpu.SemaphoreType.DMA((n,))]


class _Gather:
    def __init__(self, arrs):
        self.inputs = list(arrs)
        self.out_shape = [jax.ShapeDtypeStruct((N_DEV,) + a.shape, a.dtype) for a in arrs]
        self.scratch = _comm_scratch(len(arrs))

    def _plan(self, ins, outs, sems):
        send_sems, recv_sems, local_sems = sems
        n = len(ins)
        x, y, c = _position()
        sibling = (x, y, 1 - c)
        chips = [(1 - x, y), (x, 1 - y), (1 - x, 1 - y)]

        def index(px, py, pc):
            return 4 * px + 2 * py + pc

        def copy(a, k, block, to, src=None):
            return pltpu.make_async_remote_copy(
                src_ref=outs[a].at[block] if src is None else src, dst_ref=outs[a].at[block],
                send_sem=send_sems.at[a, k], recv_sem=recv_sems.at[a, k], device_id=to, device_id_type=MESH)

        me = index(x, y, c)
        own = [pltpu.make_async_copy(ins[a], outs[a].at[me], local_sems.at[a]) for a in range(n)]
        first = []
        for a in range(n):
            first.append(copy(a, 0, me, sibling, src=ins[a]))
            for j, chip in enumerate(chips):
                first.append(copy(a, 1 + j, me, (*chip, c), src=ins[a]))
        return n, (x, y, c), sibling, chips, index, copy, own, first

    def start(self, ins, outs, sems):
        _, _, _, _, _, _, own, first = self._plan(ins, outs, sems)
        for cp in own + first:
            cp.start()

    def finish(self, ins, outs, sems):
        n, me, sibling, chips, index, copy, own, first = self._plan(ins, outs, sems)
        c = me[2]
        passed = []
        for j, chip in enumerate(chips):
            for a in range(n):
                copy(a, 1 + j, index(*chip, c), me).wait_recv()
                cp = copy(a, 4 + j, index(*chip, c), sibling)
                cp.start()
                passed.append(cp)
        for a in range(n):
            copy(a, 0, index(me[0], me[1], 1 - c), me).wait_recv()
        for j, chip in enumerate(chips):
            for a in range(n):
                copy(a, 4 + j, index(*chip, 1 - c), me).wait_recv()
        for cp in first + passed:
            cp.wait_send()
        for cp in own:
            cp.wait()


class _Exchange:
    def __init__(self, arrs):
        self.inputs = list(arrs)
        self.out_shape = [jax.ShapeDtypeStruct(a.shape, a.dtype) for a in arrs]
        self.scratch = _comm_scratch(len(arrs))

    def _plan(self, ins, outs, sems):
        send_sems, recv_sems, local_sems = sems
        n = len(ins)
        x, y, c = _position()
        me = 4 * x + 2 * y + c
        own = [pltpu.make_async_copy(ins[a].at[me], outs[a].at[me], local_sems.at[a]) for a in range(n)]
        sends, recvs = [], []
        for r in range(1, N_DEV):
            px = 1 - x if r & 4 else x
            py = 1 - y if r & 2 else y
            pc = 1 - c if r & 1 else c
            peer = 4 * px + 2 * py + pc
            for a in range(n):
                sends.append(pltpu.make_async_remote_copy(
                    src_ref=ins[a].at[peer], dst_ref=outs[a].at[me],
                    send_sem=send_sems.at[a, r - 1], recv_sem=recv_sems.at[a, r - 1],
                    device_id=(px, py, pc), device_id_type=MESH))
                recvs.append(pltpu.make_async_remote_copy(
                    src_ref=ins[a].at[peer], dst_ref=outs[a].at[peer],
                    send_sem=send_sems.at[a, r - 1], recv_sem=recv_sems.at[a, r - 1],
                    device_id=(px, py, pc), device_id_type=MESH))
        return own, sends, recvs

    def start(self, ins, outs, sems):
        own, sends, _ = self._plan(ins, outs, sems)
        for cp in own + sends:
            cp.start()

    def finish(self, ins, outs, sems):
        own, sends, recvs = self._plan(ins, outs, sems)
        for cp in recvs:
            cp.wait_recv()
        for cp in sends:
            cp.wait_send()
        for cp in own:
            cp.wait()


def _comm_only(comm, name):
    n = len(comm.inputs)

    def body(*refs):
        ins, outs, sems = refs[:n], refs[n:2 * n], refs[2 * n:]
        comm.start(ins, outs, sems)
        comm.finish(ins, outs, sems)

    any_spec = pl.BlockSpec(memory_space=pl.ANY)
    return pl.pallas_call(
        body, name=name, out_shape=comm.out_shape, in_specs=[any_spec] * n, out_specs=[any_spec] * n,
        scratch_shapes=comm.scratch, compiler_params=pltpu.CompilerParams(has_side_effects=True),
    )(*comm.inputs)


def _pcall(body, name, grid, in_specs, out_specs, out_shape, args, scratch_shapes=(), comm=None):
    in_specs, out_specs, out_shape, scratch_shapes = list(in_specs), list(out_specs), list(out_shape), list(scratch_shapes)
    if comm is None:
        outs = pl.pallas_call(body, name=name, grid=grid, in_specs=in_specs, out_specs=out_specs, out_shape=out_shape,
                              scratch_shapes=scratch_shapes, compiler_params=_params())(*args)
        return list(outs), []
    n_in, n_out, n_scr, n_c = len(in_specs), len(out_specs), len(scratch_shapes), len(comm.inputs)

    def carrying(*refs):
        ins, refs = refs[:n_in], refs[n_in:]
        c_ins, refs = refs[:n_c], refs[n_c:]
        outs, refs = refs[:n_out], refs[n_out:]
        c_outs, refs = refs[:n_c], refs[n_c:]
        scr, sems = refs[:n_scr], refs[n_scr:]
        first, last = True, True
        for d, size in enumerate(grid):
            first = first & (pl.program_id(d) == 0)
            last = last & (pl.program_id(d) == size - 1)

        @pl.when(first)
        def _():
            comm.start(c_ins, c_outs, sems)

        body(*ins, *outs, *scr)

        @pl.when(last)
        def _():
            comm.finish(c_ins, c_outs, sems)

    any_spec = pl.BlockSpec(memory_space=pl.ANY)
    outs = pl.pallas_call(
        carrying, name=name, grid=grid, in_specs=in_specs + [any_spec] * n_c, out_specs=out_specs + [any_spec] * n_c,
        out_shape=out_shape + comm.out_shape, scratch_shapes=scratch_shapes + comm.scratch,
        compiler_params=_params(has_side_effects=True),
    )(*args, *comm.inputs)
    return list(outs[:n_out]), list(outs[n_out:])


def _in_proj_fwd(x, g_row, w_all, name, comm=None):
    T, D = x.shape
    NS = w_all.shape[-1]
    TM = min(T, 1024)

    def body(x_ref, g_ref, w_ref, proj_ref, h_ref):
        @pl.when(pl.program_id(1) == 0)
        def _():
            _, xh = _rms_stats(x_ref[...])
            h_ref[...] = (xh * g_ref[...]).astype(BF16)

        proj_ref[...] = _dot(h_ref[...], w_ref[...]).astype(BF16)

    return _pcall(
        body, name, (T // TM, N_DEV),
        in_specs=[pl.BlockSpec((TM, D), lambda i, k: (i, 0)),
                  pl.BlockSpec((1, D), lambda i, k: (0, 0)),
                  pl.BlockSpec((None, D, NS), lambda i, k: (k, 0, 0))],
        out_specs=[pl.BlockSpec((TM, NS), lambda i, k: (i, k)),
                   pl.BlockSpec((TM, D), lambda i, k: (i, 0))],
        out_shape=[jax.ShapeDtypeStruct((T, N_DEV * NS), BF16), jax.ShapeDtypeStruct((T, D), BF16)],
        args=(x, g_row, w_all), comm=comm)


def _a_mix_fwd(proj, ln_g, ln_b, w_s, b_s, name, comm=None):
    T, E3 = proj.shape
    E = E3 // 3
    G, P = A_GROUPS, GMLP_BLOCK
    GD = E // G
    TB = min(T, 512)

    def body(p_ref, lg_ref, lb_ref, ws_ref, bs_ref, y_ref, v_s, us_s):
        def chunk(ci, carry):
            rows = pl.ds(pl.multiple_of(ci * ROW_CHUNK, ROW_CHUNK), ROW_CHUNK)
            vg = _gelu(p_ref[rows, E:2 * E].astype(F32))
            xc = vg - jnp.mean(vg, axis=-1, keepdims=True)
            rstd = lax.rsqrt(jnp.mean(xc * xc, axis=-1, keepdims=True) + EPS)
            v_s[rows, :] = ((xc * rstd) * lg_ref[...] + lb_ref[...]).astype(BF16)
            z = p_ref[rows, 2 * E:3 * E].astype(F32)
            us_s[rows, :] = _gelu(p_ref[rows, 0:E].astype(F32)) * (z * _sigmoid(z))
            return carry

        lax.fori_loop(0, TB // ROW_CHUNK, chunk, 0)
        mask = _spatial_mask()
        for g in range(G):
            wm = jnp.where(mask, ws_ref[g], 0.0).astype(BF16)
            cols = slice(g * GD, (g + 1) * GD)
            for b in range(TB // P):
                rows = slice(b * P, (b + 1) * P)
                mixed = _dot(wm, v_s[rows, cols]) + bs_ref[g]
                y_ref[rows, cols] = (us_s[rows, cols] * mixed).astype(BF16)

    return _pcall(
        body, name, (T // TB,),
        in_specs=[pl.BlockSpec((TB, E3), lambda i: (i, 0)),
                  pl.BlockSpec((1, E), lambda i: (0, 0)),
                  pl.BlockSpec((1, E), lambda i: (0, 0)),
                  pl.BlockSpec((G, P, P), lambda i: (0, 0, 0)),
                  pl.BlockSpec((G, P, 1), lambda i: (0, 0, 0))],
        out_specs=[pl.BlockSpec((TB, E), lambda i: (i, 0))],
        out_shape=[jax.ShapeDtypeStruct((T, E), BF16)],
        scratch_shapes=[pltpu.VMEM((TB, E), BF16), pltpu.VMEM((TB, E), F32)],
        args=(proj, ln_g, ln_b, w_s, b_s), comm=comm)


def _window_sum_back(ext, win):
    s, k = ext, 1
    while k < win:
        s = s + pltpu.roll(s, k, axis=0)
        k *= 2
    return s


def _window_sum_ahead(ext, win):
    n = ext.shape[0]
    s, k = ext, 1
    while k < win:
        s = s + pltpu.roll(s, n - k, axis=0)
        k *= 2
    return s


def _inv_count(t0, rows, win):
    t1 = t0 + 1 + lax.broadcasted_iota(jnp.int32, (rows, 1), 0)
    return 1.0 / jnp.minimum(t1, win).astype(F32)


def _b_mix_fwd(proj, scale, wg_all, name, comm=None):
    T, E2 = proj.shape
    E = E2 // 2
    NG = len(POOL_WINDOWS)
    GB = E // NG
    TB = min(T, 256)
    RS = wg_all.shape[-2]

    def body(p_ref, sc_ref, wg_ref, y_ref, o_ref, carry_s):
        i = pl.program_id(0)

        @pl.when(i == 0)
        def _():
            carry_s[...] = jnp.zeros_like(carry_s)

        for g, win in enumerate(POOL_WINDOWS):
            cols = slice(g * GB, (g + 1) * GB)
            xg = p_ref[:, cols].astype(F32)
            ext = jnp.concatenate([carry_s[:, cols], xg], axis=0)
            pooled = _window_sum_back(ext, win)[HALO:, :] * _inv_count(i * TB, TB, win) - xg
            carry_s[:, cols] = xg[TB - HALO:, :]
            o = _dot(pooled.astype(BF16), wg_ref[:, g].reshape(GB, GB))
            o_ref[:, cols] = o.astype(BF16)
            z = p_ref[:, E + g * GB:E + (g + 1) * GB].astype(F32)
            y_ref[:, cols] = ((o * sc_ref[:, cols]) * (z * _sigmoid(z))).astype(BF16)

    return _pcall(
        body, name, (T // TB,),
        in_specs=[pl.BlockSpec((TB, E2), lambda i: (i, 0)),
                  pl.BlockSpec((1, E), lambda i: (0, 0)),
                  pl.BlockSpec((N_DEV, NG, RS, GB), lambda i: (0, 0, 0, 0))],
        out_specs=[pl.BlockSpec((TB, E), lambda i: (i, 0)), pl.BlockSpec((TB, E), lambda i: (i, 0))],
        out_shape=[jax.ShapeDtypeStruct((T, E), BF16), jax.ShapeDtypeStruct((T, E), BF16)],
        scratch_shapes=[pltpu.VMEM((HALO, E), F32)],
        args=(proj, scale, wg_all), comm=comm)


def _out_proj_fwd(y, w_all, x, g_row, name, comm=None):
    T, E = y.shape
    D = x.shape[1]
    ES = w_all.shape[-2]
    TM = min(T, 512)

    def body(y_ref, w_ref, x_ref, g_ref, xn_ref, out_ref):
        o = _dot(y_ref[...], w_ref[...].reshape(E, D))
        out_ref[...] = o
        _, oh = _rms_stats(o)
        xn_ref[...] = x_ref[...] + oh * g_ref[...]

    return _pcall(
        body, name, (T // TM,),
        in_specs=[pl.BlockSpec((TM, E), lambda i: (i, 0)),
                  pl.BlockSpec((N_DEV, ES, D), lambda i: (0, 0, 0)),
                  pl.BlockSpec((TM, D), lambda i: (i, 0)),
                  pl.BlockSpec((1, D), lambda i: (0, 0))],
        out_specs=[pl.BlockSpec((TM, D), lambda i: (i, 0)), pl.BlockSpec((TM, D), lambda i: (i, 0))],
        out_shape=[jax.ShapeDtypeStruct((T, D), F32), jax.ShapeDtypeStruct((T, D), F32)],
        args=(y, w_all, x, g_row), comm=comm)


def _loss_head(x, target, name):
    T, D = x.shape
    TM = min(T, 512)
    nT = T // TM

    def body(x_ref, t_ref, dx_ref, loss_ref, acc_s):
        i = pl.program_id(0)

        @pl.when(i == 0)
        def _():
            acc_s[...] = jnp.zeros_like(acc_s)

        e = x_ref[...] - t_ref[...]
        dx_ref[...] = e * (1.0 / D)
        acc_s[...] += jnp.sum(e * e, axis=0, keepdims=True)

        @pl.when(i == nT - 1)
        def _():
            total = jnp.sum(acc_s[...], axis=1, keepdims=True) * (0.5 / D)
            loss_ref[...] = jnp.broadcast_to(total, loss_ref.shape)

    return _pcall(
        body, name, (nT,),
        in_specs=[pl.BlockSpec((TM, D), lambda i: (i, 0)), pl.BlockSpec((TM, D), lambda i: (i, 0))],
        out_specs=[pl.BlockSpec((TM, D), lambda i: (i, 0)), pl.BlockSpec((1, 128), lambda i: (0, 0))],
        out_shape=[jax.ShapeDtypeStruct((T, D), F32), jax.ShapeDtypeStruct((1, 128), F32)],
        scratch_shapes=[pltpu.VMEM((1, D), F32)],
        args=(x, target))[0]


def _out_proj_bwd(dxn, out, g_row, w_all, y, name, comm=None):
    T, D = dxn.shape
    E = y.shape[1]
    ES = w_all.shape[-2]
    TM = min(T, 512)
    nT = T // TM

    def body(dxn_ref, out_ref, g_ref, w_ref, y_ref, dy_ref, dw_ref, dg_ref, acc_s):
        i = pl.program_id(0)

        @pl.when(i == 0)
        def _():
            acc_s[...] = jnp.zeros_like(acc_s)
            dg_ref[...] = jnp.zeros_like(dg_ref)

        dxn_v = dxn_ref[...]
        r, oh = _rms_stats(out_ref[...])
        dg_ref[...] += jnp.sum(dxn_v * oh, axis=0, keepdims=True)
        dout = _rms_bwd(dxn_v, g_ref[...], r, oh).astype(BF16)
        dy_ref[...] = _dot_nt(dout, w_ref[...].reshape(E, D)).astype(BF16)
        acc_s[...] += _dot_tn(y_ref[...], dout)

        @pl.when(i == nT - 1)
        def _():
            dw_ref[...] = acc_s[...].reshape(N_DEV, ES, D).astype(BF16)

    return _pcall(
        body, name, (nT,),
        in_specs=[pl.BlockSpec((TM, D), lambda i: (i, 0)),
                  pl.BlockSpec((TM, D), lambda i: (i, 0)),
                  pl.BlockSpec((1, D), lambda i: (0, 0)),
                  pl.BlockSpec((N_DEV, ES, D), lambda i: (0, 0, 0)),
                  pl.BlockSpec((TM, E), lambda i: (i, 0))],
        out_specs=[pl.BlockSpec((TM, E), lambda i: (i, 0)),
                   pl.BlockSpec((N_DEV, ES, D), lambda i: (0, 0, 0)),
                   pl.BlockSpec((1, D), lambda i: (0, 0))],
        out_shape=[jax.ShapeDtypeStruct((T, E), BF16), jax.ShapeDtypeStruct((N_DEV, ES, D), BF16),
                   jax.ShapeDtypeStruct((1, D), F32)],
        scratch_shapes=[pltpu.VMEM((E, D), F32)],
        args=(dxn, out, g_row, w_all, y), comm=comm)


def _a_mix_bwd(proj, dy, ln_g, ln_b, w_s, b_s, name, comm=None):
    T, E3 = proj.shape
    E = E3 // 3
    G, P = A_GROUPS, GMLP_BLOCK
    GD = E // G
    TB = min(T, 256)

    def body(p_ref, dy_ref, lg_ref, lb_ref, ws_ref, bs_ref, dp_ref, dws_ref, dbs_ref, dlg_ref, dlb_ref,
             v_s, xh_s, rstd_s, a_s, bz_s, c_s, dv_s):
        @pl.when(pl.program_id(0) == 0)
        def _():
            dws_ref[...] = jnp.zeros_like(dws_ref)
            dbs_ref[...] = jnp.zeros_like(dbs_ref)
            dlg_ref[...] = jnp.zeros_like(dlg_ref)
            dlb_ref[...] = jnp.zeros_like(dlb_ref)

        def recompute(ci, carry):
            rows = pl.ds(pl.multiple_of(ci * ROW_CHUNK, ROW_CHUNK), ROW_CHUNK)
            vg = _gelu(p_ref[rows, E:2 * E].astype(F32))
            xc = vg - jnp.mean(vg, axis=-1, keepdims=True)
            rstd = lax.rsqrt(jnp.mean(xc * xc, axis=-1, keepdims=True) + EPS)
            xh = xc * rstd
            xh_s[rows, :] = xh
            rstd_s[rows, :] = rstd
            v_s[rows, :] = (xh * lg_ref[...] + lb_ref[...]).astype(BF16)
            u, du = _gelu_and_grad(p_ref[rows, 0:E].astype(F32))
            z = p_ref[rows, 2 * E:3 * E].astype(F32)
            sg = _sigmoid(z)
            s = z * sg
            ds = sg * (1.0 + z * (1.0 - sg))
            dyv = dy_ref[rows, :].astype(F32)
            a_s[rows, :] = dyv * s * du
            bz_s[rows, :] = dyv * u * ds
            c_s[rows, :] = (dyv * u * s).astype(BF16)
            return carry

        lax.fori_loop(0, TB // ROW_CHUNK, recompute, 0)

        mask = _spatial_mask()
        mask_t = _spatial_mask(transposed=True)
        for g in range(G):
            w_g = ws_ref[g]
            wm = jnp.where(mask, w_g, 0.0).astype(BF16)
            wm_t = jnp.where(mask_t, w_g.T, 0.0).astype(BF16)
            cols = slice(g * GD, (g + 1) * GD)
            dws_g = jnp.zeros((P, P), F32)
            dbs_g = jnp.zeros((P, 1), F32)
            for b in range(TB // P):
                rows = slice(b * P, (b + 1) * P)
                vb = v_s[rows, cols]
                cb = c_s[rows, cols]
                mixed = _dot(wm, vb) + bs_ref[g]
                dp_ref[rows, g * GD:(g + 1) * GD] = (a_s[rows, cols] * mixed).astype(BF16)
                dp_ref[rows, 2 * E + g * GD:2 * E + (g + 1) * GD] = (bz_s[rows, cols] * mixed).astype(BF16)
                dv_s[rows, cols] = _dot(wm_t, cb)
                dws_g = dws_g + _dot_nt(cb, vb)
                dbs_g = dbs_g + jnp.sum(cb.astype(F32), axis=1, keepdims=True)
            dws_ref[g] += jnp.where(mask, dws_g, 0.0)
            dbs_ref[g] += dbs_g

        def ln_bwd(ci, carry):
            rows = pl.ds(pl.multiple_of(ci * ROW_CHUNK, ROW_CHUNK), ROW_CHUNK)
            dv = dv_s[rows, :]
            xh = xh_s[rows, :]
            dlg_ref[...] += jnp.sum(dv * xh, axis=0, keepdims=True)
            dlb_ref[...] += jnp.sum(dv, axis=0, keepdims=True)
            dxh = dv * lg_ref[...]
            dvg = rstd_s[rows, :] * (dxh - jnp.mean(dxh, axis=-1, keepdims=True)
                                     - xh * jnp.mean(dxh * xh, axis=-1, keepdims=True))
            _, dgl = _gelu_and_grad(p_ref[rows, E:2 * E].astype(F32))
            dp_ref[rows, E:2 * E] = (dvg * dgl).astype(BF16)
            return carry

        lax.fori_loop(0, TB // ROW_CHUNK, ln_bwd, 0)

    return _pcall(
        body, name, (T // TB,),
        in_specs=[pl.BlockSpec((TB, E3), lambda i: (i, 0)),
                  pl.BlockSpec((TB, E), lambda i: (i, 0)),
                  pl.BlockSpec((1, E), lambda i: (0, 0)),
                  pl.BlockSpec((1, E), lambda i: (0, 0)),
                  pl.BlockSpec((G, P, P), lambda i: (0, 0, 0)),
                  pl.BlockSpec((G, P, 1), lambda i: (0, 0, 0))],
        out_specs=[pl.BlockSpec((TB, E3), lambda i: (i, 0)),
                   pl.BlockSpec((G, P, P), lambda i: (0, 0, 0)),
                   pl.BlockSpec((G, P, 1), lambda i: (0, 0, 0)),
                   pl.BlockSpec((1, E), lambda i: (0, 0)),
                   pl.BlockSpec((1, E), lambda i: (0, 0))],
        out_shape=[jax.ShapeDtypeStruct((T, E3), BF16), jax.ShapeDtypeStruct((G, P, P), F32),
                   jax.ShapeDtypeStruct((G, P, 1), F32), jax.ShapeDtypeStruct((1, E), F32),
                   jax.ShapeDtypeStruct((1, E), F32)],
        scratch_shapes=[pltpu.VMEM((TB, E), BF16), pltpu.VMEM((TB, E), F32), pltpu.VMEM((TB, 1), F32),
                        pltpu.VMEM((TB, E), F32), pltpu.VMEM((TB, E), F32), pltpu.VMEM((TB, E), BF16),
                        pltpu.VMEM((TB, E), F32)],
        args=(proj, dy, ln_g, ln_b, w_s, b_s), comm=comm)


def _b_mix_bwd(proj, dy, o, scale, wg_all, name, comm=None):
    T, E2 = proj.shape
    E = E2 // 2
    NG = len(POOL_WINDOWS)
    GB = E // NG
    TB = min(T, 256)
    nT = T // TB
    RS = wg_all.shape[-2]
    halo_per_tile = TB // HALO

    def body(p_ref, halo_ref, dy_ref, o_ref, sc_ref, wg_ref, dp_ref, dsc_ref, dwg_ref, acc_s, carry_s):
        i = pl.program_id(0)
        tile = nT - 1 - i

        @pl.when(i == 0)
        def _():
            acc_s[...] = jnp.zeros_like(acc_s)
            carry_s[...] = jnp.zeros_like(carry_s)
            dsc_ref[...] = jnp.zeros_like(dsc_ref)

        has_history = (tile > 0).astype(F32)
        for g, win in enumerate(POOL_WINDOWS):
            cols = slice(g * GB, (g + 1) * GB)
            inv = _inv_count(tile * TB, TB, win)
            xg = p_ref[:, cols].astype(F32)
            ext = jnp.concatenate([halo_ref[:, cols].astype(F32) * has_history, xg], axis=0)
            pooled = _window_sum_back(ext, win)[HALO:, :] * inv - xg
            z = p_ref[:, E + g * GB:E + (g + 1) * GB].astype(F32)
            sg = _sigmoid(z)
            dyv = dy_ref[:, cols].astype(F32)
            ov = o_ref[:, cols].astype(F32)
            sc = sc_ref[:, cols]
            dmixed = dyv * (z * sg)
            dsc_ref[:, cols] += jnp.sum(dmixed * ov, axis=0, keepdims=True)
            dz = dyv * (ov * sc) * (sg * (1.0 + z * (1.0 - sg)))
            do = (dmixed * sc).astype(BF16)
            acc_s[:, g] += _dot_tn(pooled.astype(BF16), do).reshape(N_DEV, RS, GB)
            dpool = _dot_nt(do, wg_ref[:, g].reshape(GB, GB))
            q = dpool * inv
            ext_q = jnp.concatenate([q, carry_s[:, cols]], axis=0)
            dxb = _window_sum_ahead(ext_q, win)[:TB, :] - dpool
            carry_s[:, cols] = q[:HALO, :]
            dp_ref[:, cols] = dxb.astype(BF16)
            dp_ref[:, E + g * GB:E + (g + 1) * GB] = dz.astype(BF16)

        @pl.when(i == nT - 1)
        def _():
            dwg_ref[...] = acc_s[...].astype(BF16)

    return _pcall(
        body, name, (nT,),
        in_specs=[pl.BlockSpec((TB, E2), lambda i: (nT - 1 - i, 0)),
                  pl.BlockSpec((HALO, E), lambda i: (jnp.maximum((nT - 1 - i) * halo_per_tile - 1, 0), 0)),
                  pl.BlockSpec((TB, E), lambda i: (nT - 1 - i, 0)),
                  pl.BlockSpec((TB, E), lambda i: (nT - 1 - i, 0)),
                  pl.BlockSpec((1, E), lambda i: (0, 0)),
                  pl.BlockSpec((N_DEV, NG, RS, GB), lambda i: (0, 0, 0, 0))],
        out_specs=[pl.BlockSpec((TB, E2), lambda i: (nT - 1 - i, 0)),
                   pl.BlockSpec((1, E), lambda i: (0, 0)),
                   pl.BlockSpec((N_DEV, NG, RS, GB), lambda i: (0, 0, 0, 0))],
        out_shape=[jax.ShapeDtypeStruct((T, E2), BF16), jax.ShapeDtypeStruct((1, E), F32),
                   jax.ShapeDtypeStruct((N_DEV, NG, RS, GB), BF16)],
        scratch_shapes=[pltpu.VMEM((N_DEV, NG, RS, GB), F32), pltpu.VMEM((HALO, E), F32)],
        args=(proj, proj, dy, o, scale, wg_all), comm=comm)


def _in_proj_bwd_dx(dproj, w_all, x, g_row, dxn, name, comm=None):
    T, D = x.shape
    NS = w_all.shape[-1]
    TM = min(T, 1024)

    def body(dp_ref, w_ref, x_ref, g_ref, dxn_ref, dx_ref, dg_ref, acc_s):
        i, k = pl.program_id(0), pl.program_id(1)

        @pl.when(k == 0)
        def _():
            acc_s[...] = jnp.zeros_like(acc_s)

        @pl.when((i == 0) & (k == 0))
        def _():
            dg_ref[...] = jnp.zeros_like(dg_ref)

        acc_s[...] += _dot_nt(dp_ref[...], w_ref[...])

        @pl.when(k == N_DEV - 1)
        def _():
            dh = acc_s[...]
            r, xh = _rms_stats(x_ref[...])
            dg_ref[...] += jnp.sum(dh * xh, axis=0, keepdims=True)
            dx_ref[...] = dxn_ref[...] + _rms_bwd(dh, g_ref[...], r, xh)

    return _pcall(
        body, name, (T // TM, N_DEV),
        in_specs=[pl.BlockSpec((TM, NS), lambda i, k: (i, k)),
                  pl.BlockSpec((None, D, NS), lambda i, k: (k, 0, 0)),
                  pl.BlockSpec((TM, D), lambda i, k: (i, 0)),
                  pl.BlockSpec((1, D), lambda i, k: (0, 0)),
                  pl.BlockSpec((TM, D), lambda i, k: (i, 0))],
        out_specs=[pl.BlockSpec((TM, D), lambda i, k: (i, 0)), pl.BlockSpec((1, D), lambda i, k: (0, 0))],
        out_shape=[jax.ShapeDtypeStruct((T, D), F32), jax.ShapeDtypeStruct((1, D), F32)],
        scratch_shapes=[pltpu.VMEM((TM, D), F32)],
        args=(dproj, w_all, x, g_row, dxn), comm=comm)


def _dw_in(h, dproj, name, comm=None):
    T, D = h.shape
    NS = dproj.shape[1] // N_DEV
    TK = min(T, 1024)
    nK = T // TK

    def body(h_ref, dp_ref, dw_ref, acc_s):
        t = pl.program_id(1)

        @pl.when(t == 0)
        def _():
            acc_s[...] = jnp.zeros_like(acc_s)

        acc_s[...] += _dot_tn(h_ref[...], dp_ref[...])

        @pl.when(t == nK - 1)
        def _():
            dw_ref[...] = acc_s[...].astype(BF16)

    return _pcall(
        body, name, (N_DEV, nK),
        in_specs=[pl.BlockSpec((TK, D), lambda k, t: (t, 0)), pl.BlockSpec((TK, NS), lambda k, t: (t, k))],
        out_specs=[pl.BlockSpec((None, D, NS), lambda k, t: (k, 0, 0))],
        out_shape=[jax.ShapeDtypeStruct((N_DEV, D, NS), BF16)],
        scratch_shapes=[pltpu.VMEM((D, NS), F32)],
        args=(h, dproj), comm=comm)


def _reduce_adam(recvs, w, m, v, name):
    L, R, C = w.shape
    assert len(recvs) == L
    TR = R
    for cand in (256, 128, 64, 32, 16):
        if R % cand == 0 and R > cand:
            TR = cand
            break
    nR = R // TR
    c1 = 1.0 - ADAM_B1 ** ADAM_STEP
    c2 = 1.0 - ADAM_B2 ** ADAM_STEP

    def body(*refs):
        recv_refs = refs[:L]
        w_ref, m_ref, v_ref, g_ref, d_ref, nm_ref, nv_ref, g_s = refs[L:]
        layer = pl.program_id(0)
        for l in range(L):
            @pl.when(layer == l)
            def _(l=l):
                acc = recv_refs[l][0].astype(F32)
                for j in range(1, N_DEV):
                    acc = acc + recv_refs[l][j].astype(F32)
                g_s[...] = acc

        g = g_s[...]
        g_ref[...] = g
        nm = ADAM_B1 * m_ref[...] + (1.0 - ADAM_B1) * g
        nv = ADAM_B2 * v_ref[...] + (1.0 - ADAM_B2) * (g * g)
        nm_ref[...] = nm
        nv_ref[...] = nv
        d_ref[...] = -ADAM_LR * ((nm / c1) / (jnp.sqrt(nv / c2) + ADAM_EPS) + ADAM_WD * w_ref[...])

    def recv_spec(l):
        def index(layer, t):
            before = jnp.where(layer < l, 0, nR - 1)
            return (0, jnp.where(layer == l, t, before), 0)
        return pl.BlockSpec((N_DEV, TR, C), index)

    wspec = pl.BlockSpec((None, TR, C), lambda layer, t: (layer, t, 0))
    out = jax.ShapeDtypeStruct((L, R, C), F32)
    return _pcall(
        body, name, (L, nR),
        in_specs=[recv_spec(l) for l in range(L)] + [wspec] * 3,
        out_specs=[wspec] * 4, out_shape=[out] * 4,
        scratch_shapes=[pltpu.VMEM((TR, C), F32)],
        args=(*recvs, w, m, v))[0]


SMALL = ("norm_pre", "norm_post", "a_ln_g", "a_ln_b", "a_w_s", "a_b_s")
PACK_LANES = 128
PACK_ROWS_MULTIPLE = 256


def _pack(arrays):
    flat = jnp.concatenate([a.reshape(-1) for a in arrays])
    tile = PACK_LANES * PACK_ROWS_MULTIPLE
    padded = -(-flat.shape[0] // tile) * tile
    return jnp.pad(flat, (0, padded - flat.shape[0])).reshape(1, padded // PACK_LANES, PACK_LANES)


def _unpack(packed, like):
    flat = packed.reshape(-1)
    out, at = [], 0
    for a in like:
        out.append(flat[at:at + a.size].reshape(a.shape))
        at += a.size
    return out


def kernel(x, norm_pre, norm_post, a_w_in, a_ln_g, a_ln_b, a_w_s, a_b_s, a_w_out, b_w_in, b_w_grp, b_scale, b_w_out, loss_target, m_norm_pre, m_norm_post, m_a_w_in, m_a_ln_g, m_a_ln_b, m_a_w_s, m_a_b_s, m_a_w_out, m_b_w_in, m_b_w_grp, m_b_scale, m_b_w_out, v_norm_pre, v_norm_post, v_a_w_in, v_a_ln_g, v_a_ln_b, v_a_w_s, v_a_b_s, v_a_w_out, v_b_w_in, v_b_w_grp, v_b_scale, v_b_w_out):
    weights = dict(norm_pre=norm_pre, norm_post=norm_post, a_w_in=a_w_in, a_ln_g=a_ln_g, a_ln_b=a_ln_b, a_w_s=a_w_s,
                   a_b_s=a_b_s, a_w_out=a_w_out, b_w_in=b_w_in, b_w_grp=b_w_grp, b_scale=b_scale, b_w_out=b_w_out)
    mom_m = dict(norm_pre=m_norm_pre, norm_post=m_norm_post, a_w_in=m_a_w_in, a_ln_g=m_a_ln_g, a_ln_b=m_a_ln_b,
                 a_w_s=m_a_w_s, a_b_s=m_a_b_s, a_w_out=m_a_w_out, b_w_in=m_b_w_in, b_w_grp=m_b_w_grp,
                 b_scale=m_b_scale, b_w_out=m_b_w_out)
    mom_v = dict(norm_pre=v_norm_pre, norm_post=v_norm_post, a_w_in=v_a_w_in, a_ln_g=v_a_ln_g, a_ln_b=v_a_ln_b,
                 a_w_s=v_a_w_s, a_b_s=v_a_b_s, a_w_out=v_a_w_out, b_w_in=v_b_w_in, b_w_grp=v_b_w_grp,
                 b_scale=v_b_scale, b_w_out=v_b_w_out)
    names = list(weights)

    depth = norm_pre.shape[0]
    x0 = x[0]
    target = loss_target[0]
    T, D = x0.shape
    E = a_ln_g.shape[1]
    G, P = A_GROUPS, GMLP_BLOCK

    def shards_of(i):
        j = i // 2
        if i % 2 == 0:
            return dict(w_in=a_w_in[j].astype(BF16), w_out=a_w_out[j].astype(BF16))
        return dict(w_in=b_w_in[j].astype(BF16), w_out=b_w_out[j].astype(BF16), grp=b_w_grp[j].astype(BF16))

    shard = [shards_of(i) for i in range(depth)]
    full = [dict() for _ in range(depth)]

    def gather_into(keys, got):
        for (i, key), arr in zip(keys, got):
            full[i][key] = arr

    first = _comm_only(_Gather([shard[0]["w_in"], b_scale]), "gather_first")
    full[0]["w_in"] = first[0]
    scale_full = jnp.transpose(first[1], (1, 0, 2)).reshape(b_scale.shape[0], 1, E)

    saved = []
    xi = x0
    for i in range(depth):
        j = i // 2
        g_pre, g_post = norm_pre[i:i + 1], norm_post[i:i + 1]
        keys_in = [(0, "w_out")] if i == 0 else []
        keys_mix = []
        if i + 1 < depth:
            keys_in.append((i + 1, "w_in"))
            keys_mix = [(i + 1, k) for k in shard[i + 1] if k != "w_in"]
        comm_in = _Gather([shard[a][k] for a, k in keys_in]) if keys_in else None
        comm_mix = _Gather([shard[a][k] for a, k in keys_mix]) if keys_mix else None
        kind = "a" if i % 2 == 0 else "b"
        (proj, h), got = _in_proj_fwd(xi, g_pre, full[i]["w_in"], f"{kind}_in_fwd_{i}", comm_in)
        gather_into(keys_in, got)
        if i % 2 == 0:
            (y,), got = _a_mix_fwd(proj, a_ln_g[j:j + 1], a_ln_b[j:j + 1], a_w_s[j], a_b_s[j].reshape(G, P, 1),
                                   f"a_mix_fwd_{i}", comm_mix)
            o = None
        else:
            (y, o), got = _b_mix_fwd(proj, scale_full[j], full[i]["grp"], f"b_mix_fwd_{i}", comm_mix)
        gather_into(keys_mix, got)
        (x_next, out), _ = _out_proj_fwd(y, full[i]["w_out"], xi, g_post, f"{kind}_out_fwd_{i}")
        saved.append((xi, h, proj, y, out, o))
        xi = x_next

    dx, loss_row = _loss_head(xi, target, "loss_head")
    loss = lax.psum(loss_row[0, 0], ("x", "y", "c"))

    n_a, n_b = a_ln_g.shape[0], b_scale.shape[0]
    d_pre, d_post = [None] * depth, [None] * depth
    d_ln_g, d_ln_b, d_w_s, d_b_s = [None] * n_a, [None] * n_a, [None] * n_a, [None] * n_a
    recv = {k: [None] * (n_a if k.startswith("a_") else n_b)
            for k in ("a_w_in", "a_w_out", "b_w_in", "b_w_grp", "b_w_out", "b_scale")}

    def exchange_of(items):
        return _Exchange([arr for _, _, arr in items]) if items else None

    def received(items, got):
        for (key, j, _), arr in zip(items, got):
            recv[key][j] = arr

    pending = []
    for i in reversed(range(depth)):
        j = i // 2
        xi, h, proj, y, out, o = saved[i]
        g_pre, g_post = norm_pre[i:i + 1], norm_post[i:i + 1]
        if i % 2 == 0:
            (dy, dw_out, d_post[i]), _ = _out_proj_bwd(dx, out, g_post, full[i]["w_out"], y, f"a_out_bwd_{i}")
            items, pending = pending + [("a_w_out", j, dw_out)], []
            (dproj, d_w_s[j], dbs, d_ln_g[j], d_ln_b[j]), got = _a_mix_bwd(
                proj, dy, a_ln_g[j:j + 1], a_ln_b[j:j + 1], a_w_s[j], a_b_s[j].reshape(G, P, 1),
                f"a_mix_bwd_{i}", exchange_of(items))
            received(items, got)
            d_b_s[j] = dbs.reshape(G, P)
            (dw_in,), _ = _dw_in(h, dproj, f"a_dw_in_{i}")
            items = [("a_w_in", j, dw_in)] if i == 0 else []
            (dx, d_pre[i]), got = _in_proj_bwd_dx(dproj, full[i]["w_in"], xi, g_pre, dx, f"a_in_bwd_{i}",
                                                  exchange_of(items))
            received(items, got)
            if i > 0:
                pending.append(("a_w_in", j, dw_in))
        else:
            (dy, dw_out, d_post[i]), _ = _out_proj_bwd(dx, out, g_post, full[i]["w_out"], y, f"b_out_bwd_{i}")
            items, pending = pending, []
            (dproj, dsc, dw_grp), got = _b_mix_bwd(proj, dy, o, scale_full[j], full[i]["grp"], f"b_mix_bwd_{i}",
                                                  exchange_of(items))
            received(items, got)
            items = [("b_w_out", j, dw_out), ("b_w_grp", j, dw_grp),
                     ("b_scale", j, dsc.reshape(N_DEV, 1, E // N_DEV))]
            (dx, d_pre[i]), got = _in_proj_bwd_dx(dproj, full[i]["w_in"], xi, g_pre, dx, f"b_in_bwd_{i}",
                                                  exchange_of(items))
            received(items, got)
            (dw_in,), _ = _dw_in(h, dproj, f"b_dw_in_{i}")
            pending.append(("b_w_in", j, dw_in))
    assert not pending

    small_grads = [jnp.concatenate(d_pre, axis=0), jnp.concatenate(d_post, axis=0),
                   jnp.concatenate(d_ln_g, axis=0), jnp.concatenate(d_ln_b, axis=0),
                   jnp.stack(d_w_s), jnp.stack(d_b_s)]
    small_like = [weights[k] for k in SMALL]
    (small_all,) = _comm_only(_Gather([_pack(small_grads)[0]]), "gather_small_grads")
    small_out = _reduce_adam([small_all], _pack(small_like), _pack([mom_m[k] for k in SMALL]),
                             _pack([mom_v[k] for k in SMALL]), "adam_small")
    results = {k: [] for k in names}
    for packed in small_out:
        for k, a in zip(SMALL, _unpack(packed, small_like)):
            results[k].append(a)

    def shard_view(a):
        return a.reshape(a.shape[0], -1, a.shape[-1])

    for k in ("a_w_in", "a_w_out", "b_w_in", "b_w_grp", "b_w_out"):
        w3 = shard_view(weights[k])
        recvs = [r.reshape(N_DEV, w3.shape[1], w3.shape[2]) for r in recv[k]]
        outs = _reduce_adam(recvs, w3, shard_view(mom_m[k]), shard_view(mom_v[k]), f"adam_{k}")
        results[k] = [o_.reshape(weights[k].shape) for o_ in outs]
    sc_recv = jnp.concatenate(recv["b_scale"], axis=1)
    outs = _reduce_adam([sc_recv], b_scale[None], m_b_scale[None], v_b_scale[None], "adam_b_scale")
    results["b_scale"] = [o_[0] for o_ in outs]

    grad_x = dx[None]
    return (loss, grad_x, *[results[k][0] for k in names], *[results[k][1] for k in names],
            *[results[k][2] for k in names], *[results[k][3] for k in names])
```

```python
import jax
import jax.numpy as jnp
from jax import lax
from jax.experimental import pallas as pl
from jax.experimental.pallas import tpu as pltpu

F32 = jnp.float32
BF16 = jnp.bfloat16
MESH = pl.DeviceIdType.MESH

N_DEV = 8
EPS = 1e-6
CHUNK = 64
GMLP_BLOCK = 128
A_GROUPS = 8
POOL_WINDOWS = (2, 4, 8, 16)
HALO = 16
ADAM_LR = 0.001
ADAM_B1 = 0.9
ADAM_B2 = 0.999
ADAM_EPS = 1e-08
ADAM_WD = 0.01
ADAM_STEP = 10
GELU_C = 0.7978845608028654
GELU_A = 0.044715
ROW_CHUNK = 16
VMEM_LIMIT_BYTES = 56 * 1024 * 1024


def _params(**kw):
    return pltpu.CompilerParams(vmem_limit_bytes=VMEM_LIMIT_BYTES, **kw)


def _gelu(x):
    return 0.5 * x * (1.0 + jnp.tanh(GELU_C * (x + GELU_A * (x * x * x))))


def _gelu_and_grad(x):
    x2 = x * x
    t = jnp.tanh(GELU_C * (x + GELU_A * (x2 * x)))
    val = 0.5 * x * (1.0 + t)
    grad = 0.5 * (1.0 + t) + 0.5 * x * (1.0 - t * t) * (GELU_C * (1.0 + 3.0 * GELU_A * x2))
    return val, grad


def _sigmoid(z):
    return 0.5 * jnp.tanh(0.5 * z) + 0.5


def _dot(a, b):
    return jnp.dot(a, b, preferred_element_type=F32)


def _dot_nt(a, b):
    return lax.dot_general(a, b, (((1,), (1,)), ((), ())), preferred_element_type=F32)


def _dot_tn(a, b):
    return lax.dot_general(a, b, (((0,), (0,)), ((), ())), preferred_element_type=F32)


def _rms_stats(xf):
    r = lax.rsqrt(jnp.mean(xf * xf, axis=-1, keepdims=True) + EPS)
    return r, xf * r


def _rms_bwd(dy, g, r, xh):
    dxh = dy * g
    return r * (dxh - xh * jnp.mean(dxh * xh, axis=-1, keepdims=True))


def _spatial_mask(transposed=False):
    p = lax.broadcasted_iota(jnp.int32, (GMLP_BLOCK, GMLP_BLOCK), 0)
    q = lax.broadcasted_iota(jnp.int32, (GMLP_BLOCK, GMLP_BLOCK), 1)
    if transposed:
        p, q = q, p
    return (q // CHUNK) <= (p // CHUNK)


def _position():
    x, y, c = lax.axis_index("x"), lax.axis_index("y"), lax.axis_index("c")
    return x, y, c


def _comm_scratch(n):
    return [pltpu.SemaphoreType.DMA((n, 7)), pltpu.SemaphoreType.DMA((n, 7)), pltpu.SemaphoreType.DMA((n,))]


class _Gather:
    def __init__(self, arrs):
        self.inputs = list(arrs)
        self.out_shape = [jax.ShapeDtypeStruct((N_DEV,) + a.shape, a.dtype) for a in arrs]
        self.scratch = _comm_scratch(len(arrs))

    def _plan(self, ins, outs, sems):
        send_sems, recv_sems, local_sems = sems
        n = len(ins)
        x, y, c = _position()
        sibling = (x, y, 1 - c)
        chips = [(1 - x, y), (x, 1 - y), (1 - x, 1 - y)]

        def index(px, py, pc):
            return 4 * px + 2 * py + pc

        def copy(a, k, block, to, src=None):
            return pltpu.make_async_remote_copy(
                src_ref=outs[a].at[block] if src is None else src, dst_ref=outs[a].at[block],
                send_sem=send_sems.at[a, k], recv_sem=recv_sems.at[a, k], device_id=to, device_id_type=MESH)

        me = index(x, y, c)
        own = [pltpu.make_async_copy(ins[a], outs[a].at[me], local_sems.at[a]) for a in range(n)]
        first = []
        for a in range(n):
            first.append(copy(a, 0, me, sibling, src=ins[a]))
            for j, chip in enumerate(chips):
                first.append(copy(a, 1 + j, me, (*chip, c), src=ins[a]))
        return n, (x, y, c), sibling, chips, index, copy, own, first

    def start(self, ins, outs, sems):
        _, _, _, _, _, _, own, first = self._plan(ins, outs, sems)
        for cp in own + first:
            cp.start()

    def middle(self, ins, outs, sems):
        n, me, sibling, chips, index, copy, _, _ = self._plan(ins, outs, sems)
        for j, chip in enumerate(chips):
            for a in range(n):
                copy(a, 1 + j, index(*chip, me[2]), me).wait_recv()
                copy(a, 4 + j, index(*chip, me[2]), sibling).start()

    def finish(self, ins, outs, sems):
        n, me, sibling, chips, index, copy, own, first = self._plan(ins, outs, sems)
        c = me[2]
        passed = [copy(a, 4 + j, index(*chip, c), sibling) for j, chip in enumerate(chips) for a in range(n)]
        for a in range(n):
            copy(a, 0, index(me[0], me[1], 1 - c), me).wait_recv()
        for j, chip in enumerate(chips):
            for a in range(n):
                copy(a, 4 + j, index(*chip, 1 - c), me).wait_recv()
        for cp in first + passed:
            cp.wait_send()
        for cp in own:
            cp.wait()


class _Exchange:
    def __init__(self, arrs):
        self.inputs = list(arrs)
        self.out_shape = [jax.ShapeDtypeStruct(a.shape, a.dtype) for a in arrs]
        self.scratch = _comm_scratch(len(arrs))

    def _plan(self, ins, outs, sems):
        send_sems, recv_sems, local_sems = sems
        n = len(ins)
        x, y, c = _position()
        me = 4 * x + 2 * y + c
        own = [pltpu.make_async_copy(ins[a].at[me], outs[a].at[me], local_sems.at[a]) for a in range(n)]
        sends, recvs = [], []
        for r in range(1, N_DEV):
            px = 1 - x if r & 4 else x
            py = 1 - y if r & 2 else y
            pc = 1 - c if r & 1 else c
            peer = 4 * px + 2 * py + pc
            for a in range(n):
                sends.append(pltpu.make_async_remote_copy(
                    src_ref=ins[a].at[peer], dst_ref=outs[a].at[me],
                    send_sem=send_sems.at[a, r - 1], recv_sem=recv_sems.at[a, r - 1],
                    device_id=(px, py, pc), device_id_type=MESH))
                recvs.append(pltpu.make_async_remote_copy(
                    src_ref=ins[a].at[peer], dst_ref=outs[a].at[peer],
                    send_sem=send_sems.at[a, r - 1], recv_sem=recv_sems.at[a, r - 1],
                    device_id=(px, py, pc), device_id_type=MESH))
        return own, sends, recvs

    def start(self, ins, outs, sems):
        own, sends, _ = self._plan(ins, outs, sems)
        for cp in own + sends:
            cp.start()

    def middle(self, ins, outs, sems):
        pass

    def finish(self, ins, outs, sems):
        own, sends, recvs = self._plan(ins, outs, sems)
        for cp in recvs:
            cp.wait_recv()
        for cp in sends:
            cp.wait_send()
        for cp in own:
            cp.wait()


class _Together:
    def __init__(self, comms):
        self.comms = list(comms)
        self.inputs = [a for c in self.comms for a in c.inputs]
        self.out_shape = [s for c in self.comms for s in c.out_shape]
        self.scratch = [s for c in self.comms for s in c.scratch]

    def _each(self, ins, outs, sems):
        at = 0
        for k, c in enumerate(self.comms):
            n = len(c.inputs)
            yield c, ins[at:at + n], outs[at:at + n], sems[3 * k:3 * k + 3]
            at += n

    def start(self, ins, outs, sems):
        for c, i, o, s in self._each(ins, outs, sems):
            c.start(i, o, s)

    def middle(self, ins, outs, sems):
        for c, i, o, s in self._each(ins, outs, sems):
            c.middle(i, o, s)

    def finish(self, ins, outs, sems):
        for c, i, o, s in self._each(ins, outs, sems):
            c.finish(i, o, s)


def _comm_only(comm, name):
    n = len(comm.inputs)

    def body(*refs):
        ins, outs, sems = refs[:n], refs[n:2 * n], refs[2 * n:]
        comm.start(ins, outs, sems)
        comm.middle(ins, outs, sems)
        comm.finish(ins, outs, sems)

    any_spec = pl.BlockSpec(memory_space=pl.ANY)
    return pl.pallas_call(
        body, name=name, out_shape=comm.out_shape, in_specs=[any_spec] * n, out_specs=[any_spec] * n,
        scratch_shapes=comm.scratch, compiler_params=pltpu.CompilerParams(has_side_effects=True),
    )(*comm.inputs)


def _pcall(body, name, grid, in_specs, out_specs, out_shape, args, scratch_shapes=(), comm=None):
    in_specs, out_specs, out_shape, scratch_shapes = list(in_specs), list(out_specs), list(out_shape), list(scratch_shapes)
    if comm is None:
        outs = pl.pallas_call(body, name=name, grid=grid, in_specs=in_specs, out_specs=out_specs, out_shape=out_shape,
                              scratch_shapes=scratch_shapes, compiler_params=_params())(*args)
        return list(outs), []
    n_in, n_out, n_scr, n_c = len(in_specs), len(out_specs), len(scratch_shapes), len(comm.inputs)

    def carrying(*refs):
        ins, refs = refs[:n_in], refs[n_in:]
        c_ins, refs = refs[:n_c], refs[n_c:]
        outs, refs = refs[:n_out], refs[n_out:]
        c_outs, refs = refs[:n_c], refs[n_c:]
        scr, sems = refs[:n_scr], refs[n_scr:]
        step, steps = 0, 1
        for d, size in enumerate(grid):
            step = step * size + pl.program_id(d)
            steps *= size

        @pl.when(step == 0)
        def _():
            comm.start(c_ins, c_outs, sems)

        body(*ins, *outs, *scr)

        @pl.when(step == max(steps - 2, 0))
        def _():
            comm.middle(c_ins, c_outs, sems)

        @pl.when(step == steps - 1)
        def _():
            comm.finish(c_ins, c_outs, sems)

    any_spec = pl.BlockSpec(memory_space=pl.ANY)
    outs = pl.pallas_call(
        carrying, name=name, grid=grid, in_specs=in_specs + [any_spec] * n_c, out_specs=out_specs + [any_spec] * n_c,
        out_shape=out_shape + comm.out_shape, scratch_shapes=scratch_shapes + comm.scratch,
        compiler_params=_params(has_side_effects=True),
    )(*args, *comm.inputs)
    return list(outs[:n_out]), list(outs[n_out:])


def _in_proj_fwd(x, g_row, w_all, name, comm=None):
    T, D = x.shape
    NS = w_all.shape[-1]
    TM = min(T, 1024)

    def body(x_ref, g_ref, w_ref, proj_ref, h_ref):
        @pl.when(pl.program_id(1) == 0)
        def _():
            _, xh = _rms_stats(x_ref[...])
            h_ref[...] = (xh * g_ref[...]).astype(BF16)

        proj_ref[...] = _dot(h_ref[...], w_ref[...]).astype(BF16)

    return _pcall(
        body, name, (T // TM, N_DEV),
        in_specs=[pl.BlockSpec((TM, D), lambda i, k: (i, 0)),
                  pl.BlockSpec((1, D), lambda i, k: (0, 0)),
                  pl.BlockSpec((None, D, NS), lambda i, k: (k, 0, 0))],
        out_specs=[pl.BlockSpec((TM, NS), lambda i, k: (i, k)),
                   pl.BlockSpec((TM, D), lambda i, k: (i, 0))],
        out_shape=[jax.ShapeDtypeStruct((T, N_DEV * NS), BF16), jax.ShapeDtypeStruct((T, D), BF16)],
        args=(x, g_row, w_all), comm=comm)


def _a_mix_fwd(proj, ln_g, ln_b, w_s, b_s, name, comm=None):
    T, E3 = proj.shape
    E = E3 // 3
    G, P = A_GROUPS, GMLP_BLOCK
    GD = E // G
    TB = min(T, 512)

    def body(p_ref, lg_ref, lb_ref, ws_ref, bs_ref, y_ref, v_s, us_s):
        def chunk(ci, carry):
            rows = pl.ds(pl.multiple_of(ci * ROW_CHUNK, ROW_CHUNK), ROW_CHUNK)
            vg = _gelu(p_ref[rows, E:2 * E].astype(F32))
            xc = vg - jnp.mean(vg, axis=-1, keepdims=True)
            rstd = lax.rsqrt(jnp.mean(xc * xc, axis=-1, keepdims=True) + EPS)
            v_s[rows, :] = ((xc * rstd) * lg_ref[...] + lb_ref[...]).astype(BF16)
            z = p_ref[rows, 2 * E:3 * E].astype(F32)
            us_s[rows, :] = _gelu(p_ref[rows, 0:E].astype(F32)) * (z * _sigmoid(z))
            return carry

        lax.fori_loop(0, TB // ROW_CHUNK, chunk, 0, unroll=2)
        mask = _spatial_mask()
        for g in range(G):
            wm = jnp.where(mask, ws_ref[g], 0.0).astype(BF16)
            cols = slice(g * GD, (g + 1) * GD)
            for b in range(TB // P):
                rows = slice(b * P, (b + 1) * P)
                mixed = _dot(wm, v_s[rows, cols]) + bs_ref[g]
                y_ref[rows, cols] = (us_s[rows, cols] * mixed).astype(BF16)

    return _pcall(
        body, name, (T // TB,),
        in_specs=[pl.BlockSpec((TB, E3), lambda i: (i, 0)),
                  pl.BlockSpec((1, E), lambda i: (0, 0)),
                  pl.BlockSpec((1, E), lambda i: (0, 0)),
                  pl.BlockSpec((G, P, P), lambda i: (0, 0, 0)),
                  pl.BlockSpec((G, P, 1), lambda i: (0, 0, 0))],
        out_specs=[pl.BlockSpec((TB, E), lambda i: (i, 0))],
        out_shape=[jax.ShapeDtypeStruct((T, E), BF16)],
        scratch_shapes=[pltpu.VMEM((TB, E), BF16), pltpu.VMEM((TB, E), F32)],
        args=(proj, ln_g, ln_b, w_s, b_s), comm=comm)


def _window_sum_back(ext, win):
    s, k = ext, 1
    while k < win:
        s = s + pltpu.roll(s, k, axis=0)
        k *= 2
    return s


def _window_sum_ahead(ext, win):
    n = ext.shape[0]
    s, k = ext, 1
    while k < win:
        s = s + pltpu.roll(s, n - k, axis=0)
        k *= 2
    return s


def _inv_count(t0, rows, win):
    t1 = t0 + 1 + lax.broadcasted_iota(jnp.int32, (rows, 1), 0)
    return 1.0 / jnp.minimum(t1, win).astype(F32)


def _b_mix_fwd(proj, scale, wg_all, name, comm=None):
    T, E2 = proj.shape
    E = E2 // 2
    NG = len(POOL_WINDOWS)
    GB = E // NG
    TB = min(T, 256)
    RS = wg_all.shape[-2]

    def body(p_ref, sc_ref, wg_ref, y_ref, o_ref, carry_s):
        i = pl.program_id(0)

        @pl.when(i == 0)
        def _():
            carry_s[...] = jnp.zeros_like(carry_s)

        for g, win in enumerate(POOL_WINDOWS):
            cols = slice(g * GB, (g + 1) * GB)
            xg = p_ref[:, cols].astype(F32)
            ext = jnp.concatenate([carry_s[:, cols], xg], axis=0)
            pooled = _window_sum_back(ext, win)[HALO:, :] * _inv_count(i * TB, TB, win) - xg
            carry_s[:, cols] = xg[TB - HALO:, :]
            o = _dot(pooled.astype(BF16), wg_ref[:, g].reshape(GB, GB))
            o_ref[:, cols] = o.astype(BF16)
            z = p_ref[:, E + g * GB:E + (g + 1) * GB].astype(F32)
            y_ref[:, cols] = ((o * sc_ref[:, cols]) * (z * _sigmoid(z))).astype(BF16)

    return _pcall(
        body, name, (T // TB,),
        in_specs=[pl.BlockSpec((TB, E2), lambda i: (i, 0)),
                  pl.BlockSpec((1, E), lambda i: (0, 0)),
                  pl.BlockSpec((N_DEV, NG, RS, GB), lambda i: (0, 0, 0, 0))],
        out_specs=[pl.BlockSpec((TB, E), lambda i: (i, 0)), pl.BlockSpec((TB, E), lambda i: (i, 0))],
        out_shape=[jax.ShapeDtypeStruct((T, E), BF16), jax.ShapeDtypeStruct((T, E), BF16)],
        scratch_shapes=[pltpu.VMEM((HALO, E), F32)],
        args=(proj, scale, wg_all), comm=comm)


def _out_proj_fwd(y, w_all, x, g_row, name, comm=None):
    T, E = y.shape
    D = x.shape[1]
    ES = w_all.shape[-2]
    TM = min(T, 512)

    def body(y_ref, w_ref, x_ref, g_ref, xn_ref, out_ref):
        o = _dot(y_ref[...], w_ref[...].reshape(E, D))
        out_ref[...] = o
        _, oh = _rms_stats(o)
        xn_ref[...] = x_ref[...] + oh * g_ref[...]

    return _pcall(
        body, name, (T // TM,),
        in_specs=[pl.BlockSpec((TM, E), lambda i: (i, 0)),
                  pl.BlockSpec((N_DEV, ES, D), lambda i: (0, 0, 0)),
                  pl.BlockSpec((TM, D), lambda i: (i, 0)),
                  pl.BlockSpec((1, D), lambda i: (0, 0))],
        out_specs=[pl.BlockSpec((TM, D), lambda i: (i, 0)), pl.BlockSpec((TM, D), lambda i: (i, 0))],
        out_shape=[jax.ShapeDtypeStruct((T, D), F32), jax.ShapeDtypeStruct((T, D), F32)],
        args=(y, w_all, x, g_row), comm=comm)


def _loss_head(x, target, name):
    T, D = x.shape
    TM = min(T, 512)
    nT = T // TM

    def body(x_ref, t_ref, dx_ref, loss_ref, acc_s):
        i = pl.program_id(0)

        @pl.when(i == 0)
        def _():
            acc_s[...] = jnp.zeros_like(acc_s)

        e = x_ref[...] - t_ref[...]
        dx_ref[...] = e * (1.0 / D)
        acc_s[...] += jnp.sum(e * e, axis=0, keepdims=True)

        @pl.when(i == nT - 1)
        def _():
            total = jnp.sum(acc_s[...], axis=1, keepdims=True) * (0.5 / D)
            loss_ref[...] = jnp.broadcast_to(total, loss_ref.shape)

    return _pcall(
        body, name, (nT,),
        in_specs=[pl.BlockSpec((TM, D), lambda i: (i, 0)), pl.BlockSpec((TM, D), lambda i: (i, 0))],
        out_specs=[pl.BlockSpec((TM, D), lambda i: (i, 0)), pl.BlockSpec((1, 128), lambda i: (0, 0))],
        out_shape=[jax.ShapeDtypeStruct((T, D), F32), jax.ShapeDtypeStruct((1, 128), F32)],
        scratch_shapes=[pltpu.VMEM((1, D), F32)],
        args=(x, target))[0]


def _out_proj_bwd(dxn, out, g_row, w_all, y, name, comm=None):
    T, D = dxn.shape
    E = y.shape[1]
    ES = w_all.shape[-2]
    TM = min(T, 512)
    nT = T // TM

    def body(dxn_ref, out_ref, g_ref, w_ref, y_ref, dy_ref, dw_ref, dg_ref, acc_s):
        i = pl.program_id(0)

        @pl.when(i == 0)
        def _():
            acc_s[...] = jnp.zeros_like(acc_s)
            dg_ref[...] = jnp.zeros_like(dg_ref)

        dxn_v = dxn_ref[...]
        r, oh = _rms_stats(out_ref[...])
        dg_ref[...] += jnp.sum(dxn_v * oh, axis=0, keepdims=True)
        dout = _rms_bwd(dxn_v, g_ref[...], r, oh).astype(BF16)
        dy_ref[...] = _dot_nt(dout, w_ref[...].reshape(E, D)).astype(BF16)
        acc_s[...] += _dot_tn(y_ref[...], dout)

        @pl.when(i == nT - 1)
        def _():
            dw_ref[...] = acc_s[...].reshape(N_DEV, ES, D).astype(BF16)

    return _pcall(
        body, name, (nT,),
        in_specs=[pl.BlockSpec((TM, D), lambda i: (i, 0)),
                  pl.BlockSpec((TM, D), lambda i: (i, 0)),
                  pl.BlockSpec((1, D), lambda i: (0, 0)),
                  pl.BlockSpec((N_DEV, ES, D), lambda i: (0, 0, 0)),
                  pl.BlockSpec((TM, E), lambda i: (i, 0))],
        out_specs=[pl.BlockSpec((TM, E), lambda i: (i, 0)),
                   pl.BlockSpec((N_DEV, ES, D), lambda i: (0, 0, 0)),
                   pl.BlockSpec((1, D), lambda i: (0, 0))],
        out_shape=[jax.ShapeDtypeStruct((T, E), BF16), jax.ShapeDtypeStruct((N_DEV, ES, D), BF16),
                   jax.ShapeDtypeStruct((1, D), F32)],
        scratch_shapes=[pltpu.VMEM((E, D), F32)],
        args=(dxn, out, g_row, w_all, y), comm=comm)


def _a_mix_bwd(proj, dy, ln_g, ln_b, w_s, b_s, name, comm=None):
    T, E3 = proj.shape
    E = E3 // 3
    G, P = A_GROUPS, GMLP_BLOCK
    GD = E // G
    TB = min(T, 256)

    def body(p_ref, dy_ref, lg_ref, lb_ref, ws_ref, bs_ref, dp_ref, dws_ref, dbs_ref, dlg_ref, dlb_ref,
             v_s, xh_s, rstd_s, a_s, bz_s, c_s, dv_s):
        @pl.when(pl.program_id(0) == 0)
        def _():
            dws_ref[...] = jnp.zeros_like(dws_ref)
            dbs_ref[...] = jnp.zeros_like(dbs_ref)
            dlg_ref[...] = jnp.zeros_like(dlg_ref)
            dlb_ref[...] = jnp.zeros_like(dlb_ref)

        def recompute(ci, carry):
            rows = pl.ds(pl.multiple_of(ci * ROW_CHUNK, ROW_CHUNK), ROW_CHUNK)
            vg = _gelu(p_ref[rows, E:2 * E].astype(F32))
            xc = vg - jnp.mean(vg, axis=-1, keepdims=True)
            rstd = lax.rsqrt(jnp.mean(xc * xc, axis=-1, keepdims=True) + EPS)
            xh = xc * rstd
            xh_s[rows, :] = xh
            rstd_s[rows, :] = rstd
            v_s[rows, :] = (xh * lg_ref[...] + lb_ref[...]).astype(BF16)
            u, du = _gelu_and_grad(p_ref[rows, 0:E].astype(F32))
            z = p_ref[rows, 2 * E:3 * E].astype(F32)
            sg = _sigmoid(z)
            s = z * sg
            ds = sg * (1.0 + z * (1.0 - sg))
            dyv = dy_ref[rows, :].astype(F32)
            a_s[rows, :] = dyv * s * du
            bz_s[rows, :] = dyv * u * ds
            c_s[rows, :] = (dyv * u * s).astype(BF16)
            return carry

        lax.fori_loop(0, TB // ROW_CHUNK, recompute, 0, unroll=2)

        mask = _spatial_mask()
        mask_t = _spatial_mask(transposed=True)
        for g in range(G):
            w_g = ws_ref[g]
            wm = jnp.where(mask, w_g, 0.0).astype(BF16)
            wm_t = jnp.where(mask_t, w_g.T, 0.0).astype(BF16)
            cols = slice(g * GD, (g + 1) * GD)
            dws_g = jnp.zeros((P, P), F32)
            dbs_g = jnp.zeros((P, 1), F32)
            for b in range(TB // P):
                rows = slice(b * P, (b + 1) * P)
                vb = v_s[rows, cols]
                cb = c_s[rows, cols]
                mixed = _dot(wm, vb) + bs_ref[g]
                dp_ref[rows, g * GD:(g + 1) * GD] = (a_s[rows, cols] * mixed).astype(BF16)
                dp_ref[rows, 2 * E + g * GD:2 * E + (g + 1) * GD] = (bz_s[rows, cols] * mixed).astype(BF16)
                dv_s[rows, cols] = _dot(wm_t, cb)
                dws_g = dws_g + _dot_nt(cb, vb)
                dbs_g = dbs_g + jnp.sum(cb.astype(F32), axis=1, keepdims=True)
            dws_ref[g] += jnp.where(mask, dws_g, 0.0)
            dbs_ref[g] += dbs_g

        def ln_bwd(ci, carry):
            rows = pl.ds(pl.multiple_of(ci * ROW_CHUNK, ROW_CHUNK), ROW_CHUNK)
            dv = dv_s[rows, :]
            xh = xh_s[rows, :]
            dlg_ref[...] += jnp.sum(dv * xh, axis=0, keepdims=True)
            dlb_ref[...] += jnp.sum(dv, axis=0, keepdims=True)
            dxh = dv * lg_ref[...]
            dvg = rstd_s[rows, :] * (dxh - jnp.mean(dxh, axis=-1, keepdims=True)
                                     - xh * jnp.mean(dxh * xh, axis=-1, keepdims=True))
            _, dgl = _gelu_and_grad(p_ref[rows, E:2 * E].astype(F32))
            dp_ref[rows, E:2 * E] = (dvg * dgl).astype(BF16)
            return carry

        lax.fori_loop(0, TB // ROW_CHUNK, ln_bwd, 0, unroll=2)

    return _pcall(
        body, name, (T // TB,),
        in_specs=[pl.BlockSpec((TB, E3), lambda i: (i, 0)),
                  pl.BlockSpec((TB, E), lambda i: (i, 0)),
                  pl.BlockSpec((1, E), lambda i: (0, 0)),
                  pl.BlockSpec((1, E), lambda i: (0, 0)),
                  pl.BlockSpec((G, P, P), lambda i: (0, 0, 0)),
                  pl.BlockSpec((G, P, 1), lambda i: (0, 0, 0))],
        out_specs=[pl.BlockSpec((TB, E3), lambda i: (i, 0)),
                   pl.BlockSpec((G, P, P), lambda i: (0, 0, 0)),
                   pl.BlockSpec((G, P, 1), lambda i: (0, 0, 0)),
                   pl.BlockSpec((1, E), lambda i: (0, 0)),
                   pl.BlockSpec((1, E), lambda i: (0, 0))],
        out_shape=[jax.ShapeDtypeStruct((T, E3), BF16), jax.ShapeDtypeStruct((G, P, P), F32),
                   jax.ShapeDtypeStruct((G, P, 1), F32), jax.ShapeDtypeStruct((1, E), F32),
                   jax.ShapeDtypeStruct((1, E), F32)],
        scratch_shapes=[pltpu.VMEM((TB, E), BF16), pltpu.VMEM((TB, E), F32), pltpu.VMEM((TB, 1), F32),
                        pltpu.VMEM((TB, E), F32), pltpu.VMEM((TB, E), F32), pltpu.VMEM((TB, E), BF16),
                        pltpu.VMEM((TB, E), F32)],
        args=(proj, dy, ln_g, ln_b, w_s, b_s), comm=comm)


def _b_mix_bwd(proj, dy, o, scale, wg_all, name, comm=None):
    T, E2 = proj.shape
    E = E2 // 2
    NG = len(POOL_WINDOWS)
    GB = E // NG
    TB = min(T, 256)
    nT = T // TB
    RS = wg_all.shape[-2]
    halo_per_tile = TB // HALO

    def body(p_ref, halo_ref, dy_ref, o_ref, sc_ref, wg_ref, dp_ref, dsc_ref, dwg_ref, acc_s, carry_s):
        i = pl.program_id(0)
        tile = nT - 1 - i

        @pl.when(i == 0)
        def _():
            acc_s[...] = jnp.zeros_like(acc_s)
            carry_s[...] = jnp.zeros_like(carry_s)
            dsc_ref[...] = jnp.zeros_like(dsc_ref)

        has_history = (tile > 0).astype(F32)
        for g, win in enumerate(POOL_WINDOWS):
            cols = slice(g * GB, (g + 1) * GB)
            inv = _inv_count(tile * TB, TB, win)
            xg = p_ref[:, cols].astype(F32)
            ext = jnp.concatenate([halo_ref[:, cols].astype(F32) * has_history, xg], axis=0)
            pooled = _window_sum_back(ext, win)[HALO:, :] * inv - xg
            z = p_ref[:, E + g * GB:E + (g + 1) * GB].astype(F32)
            sg = _sigmoid(z)
            dyv = dy_ref[:, cols].astype(F32)
            ov = o_ref[:, cols].astype(F32)
            sc = sc_ref[:, cols]
            dmixed = dyv * (z * sg)
            dsc_ref[:, cols] += jnp.sum(dmixed * ov, axis=0, keepdims=True)
            dz = dyv * (ov * sc) * (sg * (1.0 + z * (1.0 - sg)))
            do = (dmixed * sc).astype(BF16)
            acc_s[:, g] += _dot_tn(pooled.astype(BF16), do).reshape(N_DEV, RS, GB)
            dpool = _dot_nt(do, wg_ref[:, g].reshape(GB, GB))
            q = dpool * inv
            ext_q = jnp.concatenate([q, carry_s[:, cols]], axis=0)
            dxb = _window_sum_ahead(ext_q, win)[:TB, :] - dpool
            carry_s[:, cols] = q[:HALO, :]
            dp_ref[:, cols] = dxb.astype(BF16)
            dp_ref[:, E + g * GB:E + (g + 1) * GB] = dz.astype(BF16)

        @pl.when(i == nT - 1)
        def _():
            dwg_ref[...] = acc_s[...].astype(BF16)

    return _pcall(
        body, name, (nT,),
        in_specs=[pl.BlockSpec((TB, E2), lambda i: (nT - 1 - i, 0)),
                  pl.BlockSpec((HALO, E), lambda i: (jnp.maximum((nT - 1 - i) * halo_per_tile - 1, 0), 0)),
                  pl.BlockSpec((TB, E), lambda i: (nT - 1 - i, 0)),
                  pl.BlockSpec((TB, E), lambda i: (nT - 1 - i, 0)),
                  pl.BlockSpec((1, E), lambda i: (0, 0)),
                  pl.BlockSpec((N_DEV, NG, RS, GB), lambda i: (0, 0, 0, 0))],
        out_specs=[pl.BlockSpec((TB, E2), lambda i: (nT - 1 - i, 0)),
                   pl.BlockSpec((1, E), lambda i: (0, 0)),
                   pl.BlockSpec((N_DEV, NG, RS, GB), lambda i: (0, 0, 0, 0))],
        out_shape=[jax.ShapeDtypeStruct((T, E2), BF16), jax.ShapeDtypeStruct((1, E), F32),
                   jax.ShapeDtypeStruct((N_DEV, NG, RS, GB), BF16)],
        scratch_shapes=[pltpu.VMEM((N_DEV, NG, RS, GB), F32), pltpu.VMEM((HALO, E), F32)],
        args=(proj, proj, dy, o, scale, wg_all), comm=comm)


def _in_proj_bwd_dx(dproj, w_all, x, g_row, dxn, name, comm=None):
    T, D = x.shape
    NS = w_all.shape[-1]
    TM = min(T, 1024)

    def body(dp_ref, w_ref, x_ref, g_ref, dxn_ref, dx_ref, dg_ref, acc_s):
        i, k = pl.program_id(0), pl.program_id(1)

        @pl.when(k == 0)
        def _():
            acc_s[...] = jnp.zeros_like(acc_s)

        @pl.when((i == 0) & (k == 0))
        def _():
            dg_ref[...] = jnp.zeros_like(dg_ref)

        acc_s[...] += _dot_nt(dp_ref[...], w_ref[...])

        @pl.when(k == N_DEV - 1)
        def _():
            dh = acc_s[...]
            r, xh = _rms_stats(x_ref[...])
            dg_ref[...] += jnp.sum(dh * xh, axis=0, keepdims=True)
            dx_ref[...] = dxn_ref[...] + _rms_bwd(dh, g_ref[...], r, xh)

    return _pcall(
        body, name, (T // TM, N_DEV),
        in_specs=[pl.BlockSpec((TM, NS), lambda i, k: (i, k)),
                  pl.BlockSpec((None, D, NS), lambda i, k: (k, 0, 0)),
                  pl.BlockSpec((TM, D), lambda i, k: (i, 0)),
                  pl.BlockSpec((1, D), lambda i, k: (0, 0)),
                  pl.BlockSpec((TM, D), lambda i, k: (i, 0))],
        out_specs=[pl.BlockSpec((TM, D), lambda i, k: (i, 0)), pl.BlockSpec((1, D), lambda i, k: (0, 0))],
        out_shape=[jax.ShapeDtypeStruct((T, D), F32), jax.ShapeDtypeStruct((1, D), F32)],
        scratch_shapes=[pltpu.VMEM((TM, D), F32)],
        args=(dproj, w_all, x, g_row, dxn), comm=comm)


def _dw_in(h, dproj, name, part=0, parts=1, comm=None):
    T = h.shape[0]
    D = h.shape[1] // parts
    NS = dproj.shape[1] // N_DEV
    TK = min(T, 1024)
    nK = T // TK

    def body(h_ref, dp_ref, dw_ref, acc_s):
        t = pl.program_id(1)

        @pl.when(t == 0)
        def _():
            acc_s[...] = jnp.zeros_like(acc_s)

        acc_s[...] += _dot_tn(h_ref[...], dp_ref[...])

        @pl.when(t == nK - 1)
        def _():
            dw_ref[...] = acc_s[...].astype(BF16)

    return _pcall(
        body, name, (N_DEV, nK),
        in_specs=[pl.BlockSpec((TK, D), lambda k, t: (t, part)), pl.BlockSpec((TK, NS), lambda k, t: (t, k))],
        out_specs=[pl.BlockSpec((None, D, NS), lambda k, t: (k, 0, 0))],
        out_shape=[jax.ShapeDtypeStruct((N_DEV, D, NS), BF16)],
        scratch_shapes=[pltpu.VMEM((D, NS), F32)],
        args=(h, dproj), comm=comm)


def _reduce_adam(recvs, w, m, v, name):
    L, R, C = w.shape
    assert len(recvs) == L
    TR = R
    for cand in (256, 128, 64, 32, 16):
        if R % cand == 0 and R > cand:
            TR = cand
            break
    nR = R // TR
    c1 = 1.0 - ADAM_B1 ** ADAM_STEP
    c2 = 1.0 - ADAM_B2 ** ADAM_STEP

    def body(*refs):
        recv_refs = refs[:L]
        w_ref, m_ref, v_ref, g_ref, d_ref, nm_ref, nv_ref, g_s = refs[L:]
        layer = pl.program_id(0)
        for l in range(L):
            @pl.when(layer == l)
            def _(l=l):
                acc = recv_refs[l][0].astype(F32)
                for j in range(1, N_DEV):
                    acc = acc + recv_refs[l][j].astype(F32)
                g_s[...] = acc

        g = g_s[...]
        g_ref[...] = g
        nm = ADAM_B1 * m_ref[...] + (1.0 - ADAM_B1) * g
        nv = ADAM_B2 * v_ref[...] + (1.0 - ADAM_B2) * (g * g)
        nm_ref[...] = nm
        nv_ref[...] = nv
        d_ref[...] = -ADAM_LR * ((nm / c1) / (jnp.sqrt(nv / c2) + ADAM_EPS) + ADAM_WD * w_ref[...])

    def recv_spec(l):
        def index(layer, t):
            before = jnp.where(layer < l, 0, nR - 1)
            return (0, jnp.where(layer == l, t, before), 0)
        return pl.BlockSpec((N_DEV, TR, C), index)

    wspec = pl.BlockSpec((None, TR, C), lambda layer, t: (layer, t, 0))
    out = jax.ShapeDtypeStruct((L, R, C), F32)
    return _pcall(
        body, name, (L, nR),
        in_specs=[recv_spec(l) for l in range(L)] + [wspec] * 3,
        out_specs=[wspec] * 4, out_shape=[out] * 4,
        scratch_shapes=[pltpu.VMEM((TR, C), F32)],
        args=(*recvs, w, m, v))[0]


PACK_LANES = 128
PACK_ROWS_MULTIPLE = 256


def _pack(arrays):
    flat = jnp.concatenate([a.reshape(-1) for a in arrays])
    tile = PACK_LANES * PACK_ROWS_MULTIPLE
    padded = -(-flat.shape[0] // tile) * tile
    return jnp.pad(flat, (0, padded - flat.shape[0])).reshape(1, padded // PACK_LANES, PACK_LANES)


def _unpack(packed, like):
    flat = packed.reshape(-1)
    out, at = [], 0
    for a in like:
        out.append(flat[at:at + a.size].reshape(a.shape))
        at += a.size
    return out


def kernel(x, norm_pre, norm_post, a_w_in, a_ln_g, a_ln_b, a_w_s, a_b_s, a_w_out, b_w_in, b_w_grp, b_scale, b_w_out, loss_target, m_norm_pre, m_norm_post, m_a_w_in, m_a_ln_g, m_a_ln_b, m_a_w_s, m_a_b_s, m_a_w_out, m_b_w_in, m_b_w_grp, m_b_scale, m_b_w_out, v_norm_pre, v_norm_post, v_a_w_in, v_a_ln_g, v_a_ln_b, v_a_w_s, v_a_b_s, v_a_w_out, v_b_w_in, v_b_w_grp, v_b_scale, v_b_w_out):
    weights = dict(norm_pre=norm_pre, norm_post=norm_post, a_w_in=a_w_in, a_ln_g=a_ln_g, a_ln_b=a_ln_b, a_w_s=a_w_s,
                   a_b_s=a_b_s, a_w_out=a_w_out, b_w_in=b_w_in, b_w_grp=b_w_grp, b_scale=b_scale, b_w_out=b_w_out)
    mom_m = dict(norm_pre=m_norm_pre, norm_post=m_norm_post, a_w_in=m_a_w_in, a_ln_g=m_a_ln_g, a_ln_b=m_a_ln_b,
                 a_w_s=m_a_w_s, a_b_s=m_a_b_s, a_w_out=m_a_w_out, b_w_in=m_b_w_in, b_w_grp=m_b_w_grp,
                 b_scale=m_b_scale, b_w_out=m_b_w_out)
    mom_v = dict(norm_pre=v_norm_pre, norm_post=v_norm_post, a_w_in=v_a_w_in, a_ln_g=v_a_ln_g, a_ln_b=v_a_ln_b,
                 a_w_s=v_a_w_s, a_b_s=v_a_b_s, a_w_out=v_a_w_out, b_w_in=v_b_w_in, b_w_grp=v_b_w_grp,
                 b_scale=v_b_scale, b_w_out=v_b_w_out)
    names = list(weights)

    depth = norm_pre.shape[0]
    x0 = x[0]
    target = loss_target[0]
    T, D = x0.shape
    E = a_ln_g.shape[1]
    G, P = A_GROUPS, GMLP_BLOCK

    def shards_of(i):
        j = i // 2
        if i % 2 == 0:
            return dict(w_in=a_w_in[j].astype(BF16), w_out=a_w_out[j].astype(BF16))
        return dict(w_in=b_w_in[j].astype(BF16), w_out=b_w_out[j].astype(BF16), grp=b_w_grp[j].astype(BF16))

    shard = [shards_of(i) for i in range(depth)]
    full = [dict() for _ in range(depth)]

    def gather_into(keys, got):
        for (i, key), arr in zip(keys, got):
            full[i][key] = arr

    first = _comm_only(_Gather([shard[0]["w_in"], b_scale]), "gather_first")
    full[0]["w_in"] = first[0]
    scale_full = jnp.transpose(first[1], (1, 0, 2)).reshape(b_scale.shape[0], 1, E)

    def rest_of(i):
        return [(i, k) for k in shard[i] if k != "w_in"]

    saved = []
    xi = x0
    for i in range(depth):
        j = i // 2
        g_pre, g_post = norm_pre[i:i + 1], norm_post[i:i + 1]
        keys_in = [(0, "w_out")] if i == 0 else []
        keys_mix = []
        if i + 1 < depth:
            keys_in.append((i + 1, "w_in"))
            if i + 1 == depth - 1:
                keys_in += rest_of(i + 1)
            elif i % 2 == 0:
                keys_mix = rest_of(i + 1) + ([(i + 2, "w_out")] if i + 2 < depth - 1 else [])
        comm_in = _Gather([shard[a][k] for a, k in keys_in]) if keys_in else None
        comm_mix = _Gather([shard[a][k] for a, k in keys_mix]) if keys_mix else None
        kind = "a" if i % 2 == 0 else "b"
        (proj, h), got = _in_proj_fwd(xi, g_pre, full[i]["w_in"], f"{kind}_in_fwd_{i}", comm_in)
        gather_into(keys_in, got)
        if i % 2 == 0:
            (y,), got = _a_mix_fwd(proj, a_ln_g[j:j + 1], a_ln_b[j:j + 1], a_w_s[j], a_b_s[j].reshape(G, P, 1),
                                   f"a_mix_fwd_{i}", comm_mix)
            o = None
        else:
            (y, o), got = _b_mix_fwd(proj, scale_full[j], full[i]["grp"], f"b_mix_fwd_{i}", comm_mix)
        gather_into(keys_mix, got)
        (x_next, out), _ = _out_proj_fwd(y, full[i]["w_out"], xi, g_post, f"{kind}_out_fwd_{i}")
        saved.append((xi, h, proj, y, out, o))
        xi = x_next

    dx, loss_row = _loss_head(xi, target, "loss_head")
    loss = lax.psum(loss_row[0, 0], ("x", "y", "c"))

    n_a, n_b = a_ln_g.shape[0], b_scale.shape[0]
    a_parts = 2
    d_pre, d_post = [None] * depth, [None] * depth
    recv = {"a_w_in": [None] * (n_a * a_parts), "a_w_out": [None] * n_a, "b_w_in": [None] * n_b,
            "b_w_grp": [None] * n_b, "b_w_out": [None] * n_b, "b_scale": [None] * n_b}
    small_a = [None] * n_a

    def carried(items):
        slabs = [it[3] for it in items if it[0] == "x"]
        packs = [it[2] for it in items if it[0] == "g"]
        comms = ([_Exchange(slabs)] if slabs else []) + ([_Gather(packs)] if packs else [])
        return None if not comms else comms[0] if len(comms) == 1 else _Together(comms)

    def received(items, got):
        ordered = [it for it in items if it[0] == "x"] + [it for it in items if it[0] == "g"]
        for it, arr in zip(ordered, got):
            if it[0] == "x":
                recv[it[1]][it[2]] = arr
            else:
                small_a[it[1]] = arr

    pending = []
    small_pending = []
    for i in reversed(range(depth)):
        j = i // 2
        xi, h, proj, y, out, o = saved[i]
        g_pre, g_post = norm_pre[i:i + 1], norm_post[i:i + 1]
        if i % 2 == 0:
            (dy, dw_out, d_post[i]), _ = _out_proj_bwd(dx, out, g_post, full[i]["w_out"], y, f"a_out_bwd_{i}")
            items, pending = pending, []
            (dproj, d_w_s, dbs, d_ln_g, d_ln_b), got = _a_mix_bwd(
                proj, dy, a_ln_g[j:j + 1], a_ln_b[j:j + 1], a_w_s[j], a_b_s[j].reshape(G, P, 1),
                f"a_mix_bwd_{i}", carried(items))
            received(items, got)
            small_pending.append(("g", j, _pack([d_w_s, dbs, d_ln_g, d_ln_b])[0]))
            items = [("x", "a_w_out", j, dw_out)] if i == 0 else []
            (dw_lo,), got = _dw_in(h, dproj, f"a_dw_in_lo_{i}", 0, a_parts, carried(items))
            received(items, got)
            items = [("x", "a_w_in", a_parts * j, dw_lo)] if i == 0 else []
            (dw_hi,), got = _dw_in(h, dproj, f"a_dw_in_hi_{i}", 1, a_parts, carried(items))
            received(items, got)
            items = [("x", "a_w_in", a_parts * j + (1 if i == 0 else 0), dw_hi if i == 0 else dw_lo)]
            if i == 0:
                items, small_pending = items + small_pending, []
            (dx, d_pre[i]), got = _in_proj_bwd_dx(dproj, full[i]["w_in"], xi, g_pre, dx, f"a_in_bwd_{i}",
                                                  carried(items))
            received(items, got)
            if i > 0:
                pending += [("x", "a_w_in", a_parts * j + 1, dw_hi), ("x", "a_w_out", j, dw_out)]
        else:
            items, small_pending = small_pending, []
            (dy, dw_out, d_post[i]), got = _out_proj_bwd(dx, out, g_post, full[i]["w_out"], y, f"b_out_bwd_{i}",
                                                        carried(items))
            received(items, got)
            items, pending = pending[:1], pending[1:]
            (dproj, dsc, dw_grp), got = _b_mix_bwd(proj, dy, o, scale_full[j], full[i]["grp"], f"b_mix_bwd_{i}",
                                                  carried(items))
            received(items, got)
            items, pending = pending, []
            (dx, d_pre[i]), got = _in_proj_bwd_dx(dproj, full[i]["w_in"], xi, g_pre, dx, f"b_in_bwd_{i}",
                                                  carried(items))
            received(items, got)
            (dw_in,), _ = _dw_in(h, dproj, f"b_dw_in_{i}")
            pending += [("x", "b_w_out", j, dw_out), ("x", "b_w_grp", j, dw_grp),
                        ("x", "b_scale", j, dsc.reshape(N_DEV, 1, E // N_DEV)), ("x", "b_w_in", j, dw_in)]
    assert not pending and not small_pending

    gathered = _comm_only(_Gather([_pack([*d_pre, *d_post])[0]]), "gather_norm_grads")
    results = {k: [None] * 4 for k in names}
    norm_like = [norm_pre, norm_post]
    outs = _reduce_adam([gathered[-1]], _pack(norm_like), _pack([m_norm_pre, m_norm_post]),
                        _pack([v_norm_pre, v_norm_post]), "adam_norms")
    for q, packed in enumerate(outs):
        results["norm_pre"][q], results["norm_post"][q] = _unpack(packed, norm_like)
    a_small = ("a_w_s", "a_b_s", "a_ln_g", "a_ln_b")
    per_layer = []
    for j in range(n_a):
        like = [weights[k][j] for k in a_small]
        outs = _reduce_adam([small_a[j]], _pack(like), _pack([mom_m[k][j] for k in a_small]),
                            _pack([mom_v[k][j] for k in a_small]), f"adam_small_{j}")
        per_layer.append([_unpack(packed, like) for packed in outs])
    for q in range(4):
        for n, k in enumerate(a_small):
            results[k][q] = jnp.stack([per_layer[j][q][n] for j in range(n_a)])

    def shard_view(a, parts=1):
        return a.reshape(a.shape[0] * parts, -1, a.shape[-1])

    for k in ("a_w_in", "a_w_out", "b_w_in", "b_w_grp", "b_w_out"):
        parts = a_parts if k == "a_w_in" else 1
        w3 = shard_view(weights[k], parts)
        recvs = [r.reshape(N_DEV, w3.shape[1], w3.shape[2]) for r in recv[k]]
        outs = _reduce_adam(recvs, w3, shard_view(mom_m[k], parts), shard_view(mom_v[k], parts), f"adam_{k}")
        results[k] = [o_.reshape(weights[k].shape) for o_ in outs]
    sc_recv = jnp.concatenate(recv["b_scale"], axis=1)
    outs = _reduce_adam([sc_recv], b_scale[None], m_b_scale[None], v_b_scale[None], "adam_b_scale")
    results["b_scale"] = [o_[0] for o_ in outs]

    grad_x = dx[None]
    return (loss, grad_x, *[results[k][0] for k in names], *[results[k][1] for k in names],
            *[results[k][2] for k in names], *[results[k][3] for k in names])
```

```python
import jax
import jax.numpy as jnp
from jax import lax
from jax.experimental import pallas as pl
from jax.experimental.pallas import tpu as pltpu

F32 = jnp.float32
BF16 = jnp.bfloat16
MESH = pl.DeviceIdType.MESH

N_DEV = 8
EPS = 1e-6
CHUNK = 64
GMLP_BLOCK = 128
A_GROUPS = 8
POOL_WINDOWS = (2, 4, 8, 16)
HALO = 16
ADAM_LR = 0.001
ADAM_B1 = 0.9
ADAM_B2 = 0.999
ADAM_EPS = 1e-08
ADAM_WD = 0.01
ADAM_STEP = 10
GELU_C = 0.7978845608028654
GELU_A = 0.044715
ROW_CHUNK = 16
VMEM_LIMIT_BYTES = 56 * 1024 * 1024


def _params(**kw):
    return pltpu.CompilerParams(vmem_limit_bytes=VMEM_LIMIT_BYTES, **kw)


def _gelu(x):
    return 0.5 * x * (1.0 + jnp.tanh(GELU_C * (x + GELU_A * (x * x * x))))


def _gelu_and_grad(x):
    x2 = x * x
    t = jnp.tanh(GELU_C * (x + GELU_A * (x2 * x)))
    val = 0.5 * x * (1.0 + t)
    grad = 0.5 * (1.0 + t) + 0.5 * x * (1.0 - t * t) * (GELU_C * (1.0 + 3.0 * GELU_A * x2))
    return val, grad


def _sigmoid(z):
    return 0.5 * jnp.tanh(0.5 * z) + 0.5


def _dot(a, b):
    return jnp.dot(a, b, preferred_element_type=F32)


def _dot_nt(a, b):
    return lax.dot_general(a, b, (((1,), (1,)), ((), ())), preferred_element_type=F32)


def _dot_tn(a, b):
    return lax.dot_general(a, b, (((0,), (0,)), ((), ())), preferred_element_type=F32)


def _rms_stats(xf):
    r = lax.rsqrt(jnp.mean(xf * xf, axis=-1, keepdims=True) + EPS)
    return r, xf * r


def _rms_bwd(dy, g, r, xh):
    dxh = dy * g
    return r * (dxh - xh * jnp.mean(dxh * xh, axis=-1, keepdims=True))


def _resident(shape):
    return pl.BlockSpec(shape, lambda *_: (0,) * len(shape), pipeline_mode=pl.Buffered(1))


def _spatial_mask(transposed=False):
    p = lax.broadcasted_iota(jnp.int32, (GMLP_BLOCK, GMLP_BLOCK), 0)
    q = lax.broadcasted_iota(jnp.int32, (GMLP_BLOCK, GMLP_BLOCK), 1)
    if transposed:
        p, q = q, p
    return (q // CHUNK) <= (p // CHUNK)


def _position():
    x, y, c = lax.axis_index("x"), lax.axis_index("y"), lax.axis_index("c")
    return x, y, c


def _comm_scratch(n):
    return [pltpu.SemaphoreType.DMA((n, 7)), pltpu.SemaphoreType.DMA((n, 7)), pltpu.SemaphoreType.DMA((n,))]


class _Gather:
    def __init__(self, arrs):
        self.inputs = list(arrs)
        self.out_shape = [jax.ShapeDtypeStruct((N_DEV,) + a.shape, a.dtype) for a in arrs]
        self.scratch = _comm_scratch(len(arrs))

    def _plan(self, ins, outs, sems):
        send_sems, recv_sems, local_sems = sems
        n = len(ins)
        x, y, c = _position()
        sibling = (x, y, 1 - c)
        chips = [(1 - x, y), (x, 1 - y), (1 - x, 1 - y)]

        def index(px, py, pc):
            return 4 * px + 2 * py + pc

        def copy(a, k, block, to, src=None):
            return pltpu.make_async_remote_copy(
                src_ref=outs[a].at[block] if src is None else src, dst_ref=outs[a].at[block],
                send_sem=send_sems.at[a, k], recv_sem=recv_sems.at[a, k], device_id=to, device_id_type=MESH)

        me = index(x, y, c)
        own = [pltpu.make_async_copy(ins[a], outs[a].at[me], local_sems.at[a]) for a in range(n)]
        first = []
        for a in range(n):
            first.append(copy(a, 0, me, sibling, src=ins[a]))
            for j, chip in enumerate(chips):
                first.append(copy(a, 1 + j, me, (*chip, c), src=ins[a]))
        return n, (x, y, c), sibling, chips, index, copy, own, first

    def start(self, ins, outs, sems):
        _, _, _, _, _, _, own, first = self._plan(ins, outs, sems)
        for cp in own + first:
            cp.start()

    def middle(self, ins, outs, sems):
        n, me, sibling, chips, index, copy, _, _ = self._plan(ins, outs, sems)
        for j, chip in enumerate(chips):
            for a in range(n):
                copy(a, 1 + j, index(*chip, me[2]), me).wait_recv()
                copy(a, 4 + j, index(*chip, me[2]), sibling).start()

    def finish(self, ins, outs, sems):
        n, me, sibling, chips, index, copy, own, first = self._plan(ins, outs, sems)
        c = me[2]
        passed = [copy(a, 4 + j, index(*chip, c), sibling) for j, chip in enumerate(chips) for a in range(n)]
        for a in range(n):
            copy(a, 0, index(me[0], me[1], 1 - c), me).wait_recv()
        for j, chip in enumerate(chips):
            for a in range(n):
                copy(a, 4 + j, index(*chip, 1 - c), me).wait_recv()
        for cp in first + passed:
            cp.wait_send()
        for cp in own:
            cp.wait()


class _Exchange:
    def __init__(self, arrs):
        self.inputs = list(arrs)
        self.out_shape = [jax.ShapeDtypeStruct(a.shape, a.dtype) for a in arrs]
        self.scratch = _comm_scratch(len(arrs))

    def _plan(self, ins, outs, sems):
        send_sems, recv_sems, local_sems = sems
        n = len(ins)
        x, y, c = _position()
        me = 4 * x + 2 * y + c
        own = [pltpu.make_async_copy(ins[a].at[me], outs[a].at[me], local_sems.at[a]) for a in range(n)]
        sends, recvs = [], []
        for r in range(1, N_DEV):
            px = 1 - x if r & 4 else x
            py = 1 - y if r & 2 else y
            pc = 1 - c if r & 1 else c
            peer = 4 * px + 2 * py + pc
            for a in range(n):
                sends.append(pltpu.make_async_remote_copy(
                    src_ref=ins[a].at[peer], dst_ref=outs[a].at[me],
                    send_sem=send_sems.at[a, r - 1], recv_sem=recv_sems.at[a, r - 1],
                    device_id=(px, py, pc), device_id_type=MESH))
                recvs.append(pltpu.make_async_remote_copy(
                    src_ref=ins[a].at[peer], dst_ref=outs[a].at[peer],
                    send_sem=send_sems.at[a, r - 1], recv_sem=recv_sems.at[a, r - 1],
                    device_id=(px, py, pc), device_id_type=MESH))
        return own, sends, recvs

    def start(self, ins, outs, sems):
        own, sends, _ = self._plan(ins, outs, sems)
        for cp in own + sends:
            cp.start()

    def middle(self, ins, outs, sems):
        pass

    def finish(self, ins, outs, sems):
        own, sends, recvs = self._plan(ins, outs, sems)
        for cp in recvs:
            cp.wait_recv()
        for cp in sends:
            cp.wait_send()
        for cp in own:
            cp.wait()


class _Together:
    def __init__(self, comms):
        self.comms = list(comms)
        self.inputs = [a for c in self.comms for a in c.inputs]
        self.out_shape = [s for c in self.comms for s in c.out_shape]
        self.scratch = [s for c in self.comms for s in c.scratch]

    def _each(self, ins, outs, sems):
        at = 0
        for k, c in enumerate(self.comms):
            n = len(c.inputs)
            yield c, ins[at:at + n], outs[at:at + n], sems[3 * k:3 * k + 3]
            at += n

    def start(self, ins, outs, sems):
        for c, i, o, s in self._each(ins, outs, sems):
            c.start(i, o, s)

    def middle(self, ins, outs, sems):
        for c, i, o, s in self._each(ins, outs, sems):
            c.middle(i, o, s)

    def finish(self, ins, outs, sems):
        for c, i, o, s in self._each(ins, outs, sems):
            c.finish(i, o, s)


def _comm_only(comm, name):
    n = len(comm.inputs)

    def body(*refs):
        ins, outs, sems = refs[:n], refs[n:2 * n], refs[2 * n:]
        comm.start(ins, outs, sems)
        comm.middle(ins, outs, sems)
        comm.finish(ins, outs, sems)

    any_spec = pl.BlockSpec(memory_space=pl.ANY)
    return pl.pallas_call(
        body, name=name, out_shape=comm.out_shape, in_specs=[any_spec] * n, out_specs=[any_spec] * n,
        scratch_shapes=comm.scratch, compiler_params=pltpu.CompilerParams(has_side_effects=True),
    )(*comm.inputs)


def _pcall(body, name, grid, in_specs, out_specs, out_shape, args, scratch_shapes=(), comm=None):
    in_specs, out_specs, out_shape, scratch_shapes = list(in_specs), list(out_specs), list(out_shape), list(scratch_shapes)
    if comm is None:
        outs = pl.pallas_call(body, name=name, grid=grid, in_specs=in_specs, out_specs=out_specs, out_shape=out_shape,
                              scratch_shapes=scratch_shapes, compiler_params=_params())(*args)
        return list(outs), []
    n_in, n_out, n_scr, n_c = len(in_specs), len(out_specs), len(scratch_shapes), len(comm.inputs)

    def carrying(*refs):
        ins, refs = refs[:n_in], refs[n_in:]
        c_ins, refs = refs[:n_c], refs[n_c:]
        outs, refs = refs[:n_out], refs[n_out:]
        c_outs, refs = refs[:n_c], refs[n_c:]
        scr, sems = refs[:n_scr], refs[n_scr:]
        step, steps = 0, 1
        for d, size in enumerate(grid):
            step = step * size + pl.program_id(d)
            steps *= size

        @pl.when(step == 0)
        def _():
            comm.start(c_ins, c_outs, sems)

        body(*ins, *outs, *scr)

        @pl.when(step == max(steps - 2, 0))
        def _():
            comm.middle(c_ins, c_outs, sems)

        @pl.when(step == steps - 1)
        def _():
            comm.finish(c_ins, c_outs, sems)

    any_spec = pl.BlockSpec(memory_space=pl.ANY)
    outs = pl.pallas_call(
        carrying, name=name, grid=grid, in_specs=in_specs + [any_spec] * n_c, out_specs=out_specs + [any_spec] * n_c,
        out_shape=out_shape + comm.out_shape, scratch_shapes=scratch_shapes + comm.scratch,
        compiler_params=_params(has_side_effects=True),
    )(*args, *comm.inputs)
    return list(outs[:n_out]), list(outs[n_out:])


def _in_proj_fwd(x, g_row, w_all, name, comm=None):
    T, D = x.shape
    NS = w_all.shape[-1]
    TM = min(T, 512)

    def body(x_ref, g_ref, w_ref, proj_ref, h_ref):
        _, xh = _rms_stats(x_ref[...])
        h = (xh * g_ref[...]).astype(BF16)
        h_ref[...] = h
        for k in range(N_DEV):
            proj_ref[:, k * NS:(k + 1) * NS] = _dot(h, w_ref[k]).astype(BF16)

    return _pcall(
        body, name, (T // TM,),
        in_specs=[pl.BlockSpec((TM, D), lambda i: (i, 0)),
                  pl.BlockSpec((1, D), lambda i: (0, 0)),
                  _resident((N_DEV, D, NS))],
        out_specs=[pl.BlockSpec((TM, N_DEV * NS), lambda i: (i, 0)),
                   pl.BlockSpec((TM, D), lambda i: (i, 0))],
        out_shape=[jax.ShapeDtypeStruct((T, N_DEV * NS), BF16), jax.ShapeDtypeStruct((T, D), BF16)],
        args=(x, g_row, w_all), comm=comm)


def _a_mix_fwd(proj, ln_g, ln_b, w_s, b_s, name, comm=None):
    T, E3 = proj.shape
    E = E3 // 3
    G, P = A_GROUPS, GMLP_BLOCK
    GD = E // G
    TB = min(T, 512)

    def body(p_ref, lg_ref, lb_ref, ws_ref, bs_ref, y_ref, v_s, us_s):
        def chunk(ci, carry):
            rows = pl.ds(pl.multiple_of(ci * ROW_CHUNK, ROW_CHUNK), ROW_CHUNK)
            vg = _gelu(p_ref[rows, E:2 * E].astype(F32))
            xc = vg - jnp.mean(vg, axis=-1, keepdims=True)
            rstd = lax.rsqrt(jnp.mean(xc * xc, axis=-1, keepdims=True) + EPS)
            v_s[rows, :] = ((xc * rstd) * lg_ref[...] + lb_ref[...]).astype(BF16)
            z = p_ref[rows, 2 * E:3 * E].astype(F32)
            us_s[rows, :] = _gelu(p_ref[rows, 0:E].astype(F32)) * (z * _sigmoid(z))
            return carry

        lax.fori_loop(0, TB // ROW_CHUNK, chunk, 0, unroll=2)
        mask = _spatial_mask()
        for g in range(G):
            wm = jnp.where(mask, ws_ref[g], 0.0).astype(BF16)
            cols = slice(g * GD, (g + 1) * GD)
            for b in range(TB // P):
                rows = slice(b * P, (b + 1) * P)
                mixed = _dot(wm, v_s[rows, cols]) + bs_ref[g]
                y_ref[rows, cols] = (us_s[rows, cols] * mixed).astype(BF16)

    return _pcall(
        body, name, (T // TB,),
        in_specs=[pl.BlockSpec((TB, E3), lambda i: (i, 0)),
                  pl.BlockSpec((1, E), lambda i: (0, 0)),
                  pl.BlockSpec((1, E), lambda i: (0, 0)),
                  pl.BlockSpec((G, P, P), lambda i: (0, 0, 0)),
                  pl.BlockSpec((G, P, 1), lambda i: (0, 0, 0))],
        out_specs=[pl.BlockSpec((TB, E), lambda i: (i, 0))],
        out_shape=[jax.ShapeDtypeStruct((T, E), BF16)],
        scratch_shapes=[pltpu.VMEM((TB, E), BF16), pltpu.VMEM((TB, E), F32)],
        args=(proj, ln_g, ln_b, w_s, b_s), comm=comm)


def _window_sum_back(ext, win):
    s, k = ext, 1
    while k < win:
        s = s + pltpu.roll(s, k, axis=0)
        k *= 2
    return s


def _window_sum_ahead(ext, win):
    n = ext.shape[0]
    s, k = ext, 1
    while k < win:
        s = s + pltpu.roll(s, n - k, axis=0)
        k *= 2
    return s


def _inv_count(t0, rows, win):
    t1 = t0 + 1 + lax.broadcasted_iota(jnp.int32, (rows, 1), 0)
    return 1.0 / jnp.minimum(t1, win).astype(F32)


def _b_mix_fwd(proj, scale, wg_all, name, comm=None):
    T, E2 = proj.shape
    E = E2 // 2
    NG = len(POOL_WINDOWS)
    GB = E // NG
    TB = min(T, 256)
    RS = wg_all.shape[-2]

    def body(p_ref, sc_ref, wg_ref, y_ref, o_ref, carry_s):
        i = pl.program_id(0)

        @pl.when(i == 0)
        def _():
            carry_s[...] = jnp.zeros_like(carry_s)

        for g, win in enumerate(POOL_WINDOWS):
            cols = slice(g * GB, (g + 1) * GB)
            xg = p_ref[:, cols].astype(F32)
            ext = jnp.concatenate([carry_s[:, cols], xg], axis=0)
            pooled = _window_sum_back(ext, win)[HALO:, :] * _inv_count(i * TB, TB, win) - xg
            carry_s[:, cols] = xg[TB - HALO:, :]
            o = _dot(pooled.astype(BF16), wg_ref[:, g].reshape(GB, GB))
            o_ref[:, cols] = o.astype(BF16)
            z = p_ref[:, E + g * GB:E + (g + 1) * GB].astype(F32)
            y_ref[:, cols] = ((o * sc_ref[:, cols]) * (z * _sigmoid(z))).astype(BF16)

    return _pcall(
        body, name, (T // TB,),
        in_specs=[pl.BlockSpec((TB, E2), lambda i: (i, 0)),
                  pl.BlockSpec((1, E), lambda i: (0, 0)),
                  pl.BlockSpec((N_DEV, NG, RS, GB), lambda i: (0, 0, 0, 0))],
        out_specs=[pl.BlockSpec((TB, E), lambda i: (i, 0)), pl.BlockSpec((TB, E), lambda i: (i, 0))],
        out_shape=[jax.ShapeDtypeStruct((T, E), BF16), jax.ShapeDtypeStruct((T, E), BF16)],
        scratch_shapes=[pltpu.VMEM((HALO, E), F32)],
        args=(proj, scale, wg_all), comm=comm)


def _out_proj_fwd(y, w_all, x, g_row, name, comm=None):
    T, E = y.shape
    D = x.shape[1]
    ES = w_all.shape[-2]
    TM = min(T, 512)

    def body(y_ref, w_ref, x_ref, g_ref, xn_ref, out_ref):
        o = _dot(y_ref[...], w_ref[...].reshape(E, D))
        out_ref[...] = o
        _, oh = _rms_stats(o)
        xn_ref[...] = x_ref[...] + oh * g_ref[...]

    return _pcall(
        body, name, (T // TM,),
        in_specs=[pl.BlockSpec((TM, E), lambda i: (i, 0)),
                  pl.BlockSpec((N_DEV, ES, D), lambda i: (0, 0, 0)),
                  pl.BlockSpec((TM, D), lambda i: (i, 0)),
                  pl.BlockSpec((1, D), lambda i: (0, 0))],
        out_specs=[pl.BlockSpec((TM, D), lambda i: (i, 0)), pl.BlockSpec((TM, D), lambda i: (i, 0))],
        out_shape=[jax.ShapeDtypeStruct((T, D), F32), jax.ShapeDtypeStruct((T, D), F32)],
        args=(y, w_all, x, g_row), comm=comm)


def _loss_head(x, target, name):
    T, D = x.shape
    TM = min(T, 512)
    nT = T // TM

    def body(x_ref, t_ref, dx_ref, loss_ref, acc_s):
        i = pl.program_id(0)

        @pl.when(i == 0)
        def _():
            acc_s[...] = jnp.zeros_like(acc_s)

        e = x_ref[...] - t_ref[...]
        dx_ref[...] = e * (1.0 / D)
        acc_s[...] += jnp.sum(e * e, axis=0, keepdims=True)

        @pl.when(i == nT - 1)
        def _():
            total = jnp.sum(acc_s[...], axis=1, keepdims=True) * (0.5 / D)
            loss_ref[...] = jnp.broadcast_to(total, loss_ref.shape)

    return _pcall(
        body, name, (nT,),
        in_specs=[pl.BlockSpec((TM, D), lambda i: (i, 0)), pl.BlockSpec((TM, D), lambda i: (i, 0))],
        out_specs=[pl.BlockSpec((TM, D), lambda i: (i, 0)), pl.BlockSpec((1, 128), lambda i: (0, 0))],
        out_shape=[jax.ShapeDtypeStruct((T, D), F32), jax.ShapeDtypeStruct((1, 128), F32)],
        scratch_shapes=[pltpu.VMEM((1, D), F32)],
        args=(x, target))[0]


def _out_proj_bwd(dxn, out, g_row, w_all, y, name, comm=None):
    T, D = dxn.shape
    E = y.shape[1]
    ES = w_all.shape[-2]
    TM = min(T, 512)
    nT = T // TM

    def body(dxn_ref, out_ref, g_ref, w_ref, y_ref, dy_ref, dw_ref, dg_ref, acc_s):
        i = pl.program_id(0)

        @pl.when(i == 0)
        def _():
            acc_s[...] = jnp.zeros_like(acc_s)
            dg_ref[...] = jnp.zeros_like(dg_ref)

        dxn_v = dxn_ref[...]
        r, oh = _rms_stats(out_ref[...])
        dg_ref[...] += jnp.sum(dxn_v * oh, axis=0, keepdims=True)
        dout = _rms_bwd(dxn_v, g_ref[...], r, oh).astype(BF16)
        dy_ref[...] = _dot_nt(dout, w_ref[...].reshape(E, D)).astype(BF16)
        acc_s[...] += _dot_tn(y_ref[...], dout)

        @pl.when(i == nT - 1)
        def _():
            dw_ref[...] = acc_s[...].reshape(N_DEV, ES, D).astype(BF16)

    return _pcall(
        body, name, (nT,),
        in_specs=[pl.BlockSpec((TM, D), lambda i: (i, 0)),
                  pl.BlockSpec((TM, D), lambda i: (i, 0)),
                  pl.BlockSpec((1, D), lambda i: (0, 0)),
                  pl.BlockSpec((N_DEV, ES, D), lambda i: (0, 0, 0)),
                  pl.BlockSpec((TM, E), lambda i: (i, 0))],
        out_specs=[pl.BlockSpec((TM, E), lambda i: (i, 0)),
                   pl.BlockSpec((N_DEV, ES, D), lambda i: (0, 0, 0)),
                   pl.BlockSpec((1, D), lambda i: (0, 0))],
        out_shape=[jax.ShapeDtypeStruct((T, E), BF16), jax.ShapeDtypeStruct((N_DEV, ES, D), BF16),
                   jax.ShapeDtypeStruct((1, D), F32)],
        scratch_shapes=[pltpu.VMEM((E, D), F32)],
        args=(dxn, out, g_row, w_all, y), comm=comm)


def _a_mix_bwd(proj, dy, ln_g, ln_b, w_s, b_s, name, comm=None):
    T, E3 = proj.shape
    E = E3 // 3
    G, P = A_GROUPS, GMLP_BLOCK
    GD = E // G
    TB = min(T, 256)

    def body(p_ref, dy_ref, lg_ref, lb_ref, ws_ref, bs_ref, dp_ref, dws_ref, dbs_ref, dlg_ref, dlb_ref,
             v_s, xh_s, rstd_s, a_s, bz_s, c_s, dv_s):
        @pl.when(pl.program_id(0) == 0)
        def _():
            dws_ref[...] = jnp.zeros_like(dws_ref)
            dbs_ref[...] = jnp.zeros_like(dbs_ref)
            dlg_ref[...] = jnp.zeros_like(dlg_ref)
            dlb_ref[...] = jnp.zeros_like(dlb_ref)

        def recompute(ci, carry):
            rows = pl.ds(pl.multiple_of(ci * ROW_CHUNK, ROW_CHUNK), ROW_CHUNK)
            vg = _gelu(p_ref[rows, E:2 * E].astype(F32))
            xc = vg - jnp.mean(vg, axis=-1, keepdims=True)
            rstd = lax.rsqrt(jnp.mean(xc * xc, axis=-1, keepdims=True) + EPS)
            xh = xc * rstd
            xh_s[rows, :] = xh
            rstd_s[rows, :] = rstd
            v_s[rows, :] = (xh * lg_ref[...] + lb_ref[...]).astype(BF16)
            u, du = _gelu_and_grad(p_ref[rows, 0:E].astype(F32))
            z = p_ref[rows, 2 * E:3 * E].astype(F32)
            sg = _sigmoid(z)
            s = z * sg
            ds = sg * (1.0 + z * (1.0 - sg))
            dyv = dy_ref[rows, :].astype(F32)
            a_s[rows, :] = dyv * s * du
            bz_s[rows, :] = dyv * u * ds
            c_s[rows, :] = (dyv * u * s).astype(BF16)
            return carry

        lax.fori_loop(0, TB // ROW_CHUNK, recompute, 0, unroll=2)

        mask = _spatial_mask()
        mask_t = _spatial_mask(transposed=True)
        for g in range(G):
            w_g = ws_ref[g]
            wm = jnp.where(mask, w_g, 0.0).astype(BF16)
            wm_t = jnp.where(mask_t, w_g.T, 0.0).astype(BF16)
            cols = slice(g * GD, (g + 1) * GD)
            dws_g = jnp.zeros((P, P), F32)
            dbs_g = jnp.zeros((P, 1), F32)
            for b in range(TB // P):
                rows = slice(b * P, (b + 1) * P)
                vb = v_s[rows, cols]
                cb = c_s[rows, cols]
                mixed = _dot(wm, vb) + bs_ref[g]
                dp_ref[rows, g * GD:(g + 1) * GD] = (a_s[rows, cols] * mixed).astype(BF16)
                dp_ref[rows, 2 * E + g * GD:2 * E + (g + 1) * GD] = (bz_s[rows, cols] * mixed).astype(BF16)
                dv_s[rows, cols] = _dot(wm_t, cb)
                dws_g = dws_g + _dot_nt(cb, vb)
                dbs_g = dbs_g + jnp.sum(cb.astype(F32), axis=1, keepdims=True)
            dws_ref[g] += jnp.where(mask, dws_g, 0.0)
            dbs_ref[g] += dbs_g

        def ln_bwd(ci, carry):
            rows = pl.ds(pl.multiple_of(ci * ROW_CHUNK, ROW_CHUNK), ROW_CHUNK)
            dv = dv_s[rows, :]
            xh = xh_s[rows, :]
            dlg_ref[...] += jnp.sum(dv * xh, axis=0, keepdims=True)
            dlb_ref[...] += jnp.sum(dv, axis=0, keepdims=True)
            dxh = dv * lg_ref[...]
            dvg = rstd_s[rows, :] * (dxh - jnp.mean(dxh, axis=-1, keepdims=True)
                                     - xh * jnp.mean(dxh * xh, axis=-1, keepdims=True))
            _, dgl = _gelu_and_grad(p_ref[rows, E:2 * E].astype(F32))
            dp_ref[rows, E:2 * E] = (dvg * dgl).astype(BF16)
            return carry

        lax.fori_loop(0, TB // ROW_CHUNK, ln_bwd, 0, unroll=2)

    return _pcall(
        body, name, (T // TB,),
        in_specs=[pl.BlockSpec((TB, E3), lambda i: (i, 0)),
                  pl.BlockSpec((TB, E), lambda i: (i, 0)),
                  pl.BlockSpec((1, E), lambda i: (0, 0)),
                  pl.BlockSpec((1, E), lambda i: (0, 0)),
                  pl.BlockSpec((G, P, P), lambda i: (0, 0, 0)),
                  pl.BlockSpec((G, P, 1), lambda i: (0, 0, 0))],
        out_specs=[pl.BlockSpec((TB, E3), lambda i: (i, 0)),
                   pl.BlockSpec((G, P, P), lambda i: (0, 0, 0)),
                   pl.BlockSpec((G, P, 1), lambda i: (0, 0, 0)),
                   pl.BlockSpec((1, E), lambda i: (0, 0)),
                   pl.BlockSpec((1, E), lambda i: (0, 0))],
        out_shape=[jax.ShapeDtypeStruct((T, E3), BF16), jax.ShapeDtypeStruct((G, P, P), F32),
                   jax.ShapeDtypeStruct((G, P, 1), F32), jax.ShapeDtypeStruct((1, E), F32),
                   jax.ShapeDtypeStruct((1, E), F32)],
        scratch_shapes=[pltpu.VMEM((TB, E), BF16), pltpu.VMEM((TB, E), F32), pltpu.VMEM((TB, 1), F32),
                        pltpu.VMEM((TB, E), F32), pltpu.VMEM((TB, E), F32), pltpu.VMEM((TB, E), BF16),
                        pltpu.VMEM((TB, E), F32)],
        args=(proj, dy, ln_g, ln_b, w_s, b_s), comm=comm)


def _b_mix_bwd(proj, dy, o, scale, wg_all, name, comm=None):
    T, E2 = proj.shape
    E = E2 // 2
    NG = len(POOL_WINDOWS)
    GB = E // NG
    TB = min(T, 256)
    nT = T // TB
    RS = wg_all.shape[-2]
    halo_per_tile = TB // HALO

    def body(p_ref, halo_ref, dy_ref, o_ref, sc_ref, wg_ref, dp_ref, dsc_ref, dwg_ref, acc_s, carry_s):
        i = pl.program_id(0)
        tile = nT - 1 - i

        @pl.when(i == 0)
        def _():
            acc_s[...] = jnp.zeros_like(acc_s)
            carry_s[...] = jnp.zeros_like(carry_s)
            dsc_ref[...] = jnp.zeros_like(dsc_ref)

        has_history = (tile > 0).astype(F32)
        for g, win in enumerate(POOL_WINDOWS):
            cols = slice(g * GB, (g + 1) * GB)
            inv = _inv_count(tile * TB, TB, win)
            xg = p_ref[:, cols].astype(F32)
            ext = jnp.concatenate([halo_ref[:, cols].astype(F32) * has_history, xg], axis=0)
            pooled = _window_sum_back(ext, win)[HALO:, :] * inv - xg
            z = p_ref[:, E + g * GB:E + (g + 1) * GB].astype(F32)
            sg = _sigmoid(z)
            dyv = dy_ref[:, cols].astype(F32)
            ov = o_ref[:, cols].astype(F32)
            sc = sc_ref[:, cols]
            dmixed = dyv * (z * sg)
            dsc_ref[:, cols] += jnp.sum(dmixed * ov, axis=0, keepdims=True)
            dz = dyv * (ov * sc) * (sg * (1.0 + z * (1.0 - sg)))
            do = (dmixed * sc).astype(BF16)
            acc_s[:, g] += _dot_tn(pooled.astype(BF16), do).reshape(N_DEV, RS, GB)
            dpool = _dot_nt(do, wg_ref[:, g].reshape(GB, GB))
            q = dpool * inv
            ext_q = jnp.concatenate([q, carry_s[:, cols]], axis=0)
            dxb = _window_sum_ahead(ext_q, win)[:TB, :] - dpool
            carry_s[:, cols] = q[:HALO, :]
            dp_ref[:, cols] = dxb.astype(BF16)
            dp_ref[:, E + g * GB:E + (g + 1) * GB] = dz.astype(BF16)

        @pl.when(i == nT - 1)
        def _():
            dwg_ref[...] = acc_s[...].astype(BF16)

    return _pcall(
        body, name, (nT,),
        in_specs=[pl.BlockSpec((TB, E2), lambda i: (nT - 1 - i, 0)),
                  pl.BlockSpec((HALO, E), lambda i: (jnp.maximum((nT - 1 - i) * halo_per_tile - 1, 0), 0)),
                  pl.BlockSpec((TB, E), lambda i: (nT - 1 - i, 0)),
                  pl.BlockSpec((TB, E), lambda i: (nT - 1 - i, 0)),
                  pl.BlockSpec((1, E), lambda i: (0, 0)),
                  pl.BlockSpec((N_DEV, NG, RS, GB), lambda i: (0, 0, 0, 0))],
        out_specs=[pl.BlockSpec((TB, E2), lambda i: (nT - 1 - i, 0)),
                   pl.BlockSpec((1, E), lambda i: (0, 0)),
                   pl.BlockSpec((N_DEV, NG, RS, GB), lambda i: (0, 0, 0, 0))],
        out_shape=[jax.ShapeDtypeStruct((T, E2), BF16), jax.ShapeDtypeStruct((1, E), F32),
                   jax.ShapeDtypeStruct((N_DEV, NG, RS, GB), BF16)],
        scratch_shapes=[pltpu.VMEM((N_DEV, NG, RS, GB), F32), pltpu.VMEM((HALO, E), F32)],
        args=(proj, proj, dy, o, scale, wg_all), comm=comm)


def _in_proj_bwd_dx(dproj, w_all, x, g_row, dxn, name, comm=None):
    T, D = x.shape
    NS = w_all.shape[-1]
    TM = min(T, 512)

    def body(dp_ref, w_ref, x_ref, g_ref, dxn_ref, dx_ref, dg_ref):
        @pl.when(pl.program_id(0) == 0)
        def _():
            dg_ref[...] = jnp.zeros_like(dg_ref)

        dh = _dot_nt(dp_ref[:, 0:NS], w_ref[0])
        for k in range(1, N_DEV):
            dh = dh + _dot_nt(dp_ref[:, k * NS:(k + 1) * NS], w_ref[k])
        r, xh = _rms_stats(x_ref[...])
        dg_ref[...] += jnp.sum(dh * xh, axis=0, keepdims=True)
        dx_ref[...] = dxn_ref[...] + _rms_bwd(dh, g_ref[...], r, xh)

    return _pcall(
        body, name, (T // TM,),
        in_specs=[pl.BlockSpec((TM, N_DEV * NS), lambda i: (i, 0)),
                  _resident((N_DEV, D, NS)),
                  pl.BlockSpec((TM, D), lambda i: (i, 0)),
                  pl.BlockSpec((1, D), lambda i: (0, 0)),
                  pl.BlockSpec((TM, D), lambda i: (i, 0))],
        out_specs=[pl.BlockSpec((TM, D), lambda i: (i, 0)), pl.BlockSpec((1, D), lambda i: (0, 0))],
        out_shape=[jax.ShapeDtypeStruct((T, D), F32), jax.ShapeDtypeStruct((1, D), F32)],
        args=(dproj, w_all, x, g_row, dxn), comm=comm)


def _dw_in(h, dproj, name, part=0, parts=1, comm=None):
    T = h.shape[0]
    D = h.shape[1] // parts
    NS = dproj.shape[1] // N_DEV
    TK = min(T, 2048)
    nK = T // TK

    def body(h_ref, dp_ref, dw_ref, acc_s):
        t = pl.program_id(1)

        @pl.when(t == 0)
        def _():
            acc_s[...] = jnp.zeros_like(acc_s)

        acc_s[...] += _dot_tn(h_ref[...], dp_ref[...])

        @pl.when(t == nK - 1)
        def _():
            dw_ref[...] = acc_s[...].astype(BF16)

    return _pcall(
        body, name, (N_DEV, nK),
        in_specs=[pl.BlockSpec((TK, D), lambda k, t: (t, part)), pl.BlockSpec((TK, NS), lambda k, t: (t, k))],
        out_specs=[pl.BlockSpec((None, D, NS), lambda k, t: (k, 0, 0))],
        out_shape=[jax.ShapeDtypeStruct((N_DEV, D, NS), BF16)],
        scratch_shapes=[pltpu.VMEM((D, NS), F32)],
        args=(h, dproj), comm=comm)


def _reduce_adam(recvs, w, m, v, name):
    L, R, C = w.shape
    assert len(recvs) == L
    TR = R
    for cand in (256, 128, 64, 32, 16):
        if R % cand == 0 and R > cand:
            TR = cand
            break
    nR = R // TR
    c1 = 1.0 - ADAM_B1 ** ADAM_STEP
    c2 = 1.0 - ADAM_B2 ** ADAM_STEP

    def body(*refs):
        recv_refs = refs[:L]
        w_ref, m_ref, v_ref, g_ref, d_ref, nm_ref, nv_ref, g_s = refs[L:]
        layer = pl.program_id(0)
        for l in range(L):
            @pl.when(layer == l)
            def _(l=l):
                acc = recv_refs[l][0].astype(F32)
                for j in range(1, N_DEV):
                    acc = acc + recv_refs[l][j].astype(F32)
                g_s[...] = acc

        g = g_s[...]
        g_ref[...] = g
        nm = ADAM_B1 * m_ref[...] + (1.0 - ADAM_B1) * g
        nv = ADAM_B2 * v_ref[...] + (1.0 - ADAM_B2) * (g * g)
        nm_ref[...] = nm
        nv_ref[...] = nv
        d_ref[...] = -ADAM_LR * ((nm / c1) / (jnp.sqrt(nv / c2) + ADAM_EPS) + ADAM_WD * w_ref[...])

    def recv_spec(l):
        def index(layer, t):
            before = jnp.where(layer < l, 0, nR - 1)
            return (0, jnp.where(layer == l, t, before), 0)
        return pl.BlockSpec((N_DEV, TR, C), index)

    wspec = pl.BlockSpec((None, TR, C), lambda layer, t: (layer, t, 0))
    out = jax.ShapeDtypeStruct((L, R, C), F32)
    return _pcall(
        body, name, (L, nR),
        in_specs=[recv_spec(l) for l in range(L)] + [wspec] * 3,
        out_specs=[wspec] * 4, out_shape=[out] * 4,
        scratch_shapes=[pltpu.VMEM((TR, C), F32)],
        args=(*recvs, w, m, v))[0]


PACK_LANES = 128
PACK_ROWS_MULTIPLE = 256


def _pack(arrays):
    flat = jnp.concatenate([a.reshape(-1) for a in arrays])
    tile = PACK_LANES * PACK_ROWS_MULTIPLE
    padded = -(-flat.shape[0] // tile) * tile
    return jnp.pad(flat, (0, padded - flat.shape[0])).reshape(1, padded // PACK_LANES, PACK_LANES)


def _unpack(packed, like):
    flat = packed.reshape(-1)
    out, at = [], 0
    for a in like:
        out.append(flat[at:at + a.size].reshape(a.shape))
        at += a.size
    return out


def kernel(x, norm_pre, norm_post, a_w_in, a_ln_g, a_ln_b, a_w_s, a_b_s, a_w_out, b_w_in, b_w_grp, b_scale, b_w_out, loss_target, m_norm_pre, m_norm_post, m_a_w_in, m_a_ln_g, m_a_ln_b, m_a_w_s, m_a_b_s, m_a_w_out, m_b_w_in, m_b_w_grp, m_b_scale, m_b_w_out, v_norm_pre, v_norm_post, v_a_w_in, v_a_ln_g, v_a_ln_b, v_a_w_s, v_a_b_s, v_a_w_out, v_b_w_in, v_b_w_grp, v_b_scale, v_b_w_out):
    weights = dict(norm_pre=norm_pre, norm_post=norm_post, a_w_in=a_w_in, a_ln_g=a_ln_g, a_ln_b=a_ln_b, a_w_s=a_w_s,
                   a_b_s=a_b_s, a_w_out=a_w_out, b_w_in=b_w_in, b_w_grp=b_w_grp, b_scale=b_scale, b_w_out=b_w_out)
    mom_m = dict(norm_pre=m_norm_pre, norm_post=m_norm_post, a_w_in=m_a_w_in, a_ln_g=m_a_ln_g, a_ln_b=m_a_ln_b,
                 a_w_s=m_a_w_s, a_b_s=m_a_b_s, a_w_out=m_a_w_out, b_w_in=m_b_w_in, b_w_grp=m_b_w_grp,
                 b_scale=m_b_scale, b_w_out=m_b_w_out)
    mom_v = dict(norm_pre=v_norm_pre, norm_post=v_norm_post, a_w_in=v_a_w_in, a_ln_g=v_a_ln_g, a_ln_b=v_a_ln_b,
                 a_w_s=v_a_w_s, a_b_s=v_a_b_s, a_w_out=v_a_w_out, b_w_in=v_b_w_in, b_w_grp=v_b_w_grp,
                 b_scale=v_b_scale, b_w_out=v_b_w_out)
    names = list(weights)

    depth = norm_pre.shape[0]
    x0 = x[0]
    target = loss_target[0]
    T, D = x0.shape
    E = a_ln_g.shape[1]
    G, P = A_GROUPS, GMLP_BLOCK

    def shards_of(i):
        j = i // 2
        if i % 2 == 0:
            return dict(w_in=a_w_in[j].astype(BF16), w_out=a_w_out[j].astype(BF16))
        return dict(w_in=b_w_in[j].astype(BF16), w_out=b_w_out[j].astype(BF16), grp=b_w_grp[j].astype(BF16))

    shard = [shards_of(i) for i in range(depth)]
    full = [dict() for _ in range(depth)]

    def gather_into(keys, got):
        for (i, key), arr in zip(keys, got):
            full[i][key] = arr

    first = _comm_only(_Gather([shard[0]["w_in"], b_scale]), "gather_first")
    full[0]["w_in"] = first[0]
    scale_full = jnp.transpose(first[1], (1, 0, 2)).reshape(b_scale.shape[0], 1, E)

    def rest_of(i):
        return [(i, k) for k in shard[i] if k != "w_in"]

    saved = []
    xi = x0
    for i in range(depth):
        j = i // 2
        g_pre, g_post = norm_pre[i:i + 1], norm_post[i:i + 1]
        keys_in = [(0, "w_out")] if i == 0 else []
        keys_mix = []
        if i + 1 < depth:
            keys_in.append((i + 1, "w_in"))
            if i + 1 == depth - 1:
                keys_in += rest_of(i + 1)
            elif i % 2 == 0:
                keys_mix = rest_of(i + 1) + ([(i + 2, "w_out")] if i + 2 < depth - 1 else [])
        comm_in = _Gather([shard[a][k] for a, k in keys_in]) if keys_in else None
        comm_mix = _Gather([shard[a][k] for a, k in keys_mix]) if keys_mix else None
        kind = "a" if i % 2 == 0 else "b"
        (proj, h), got = _in_proj_fwd(xi, g_pre, full[i]["w_in"], f"{kind}_in_fwd_{i}", comm_in)
        gather_into(keys_in, got)
        if i % 2 == 0:
            (y,), got = _a_mix_fwd(proj, a_ln_g[j:j + 1], a_ln_b[j:j + 1], a_w_s[j], a_b_s[j].reshape(G, P, 1),
                                   f"a_mix_fwd_{i}", comm_mix)
            o = None
        else:
            (y, o), got = _b_mix_fwd(proj, scale_full[j], full[i]["grp"], f"b_mix_fwd_{i}", comm_mix)
        gather_into(keys_mix, got)
        (x_next, out), _ = _out_proj_fwd(y, full[i]["w_out"], xi, g_post, f"{kind}_out_fwd_{i}")
        saved.append((xi, h, proj, y, out, o))
        xi = x_next

    dx, loss_row = _loss_head(xi, target, "loss_head")
    loss = lax.psum(loss_row[0, 0], ("x", "y", "c"))

    n_a, n_b = a_ln_g.shape[0], b_scale.shape[0]
    a_parts = 2
    d_pre, d_post = [None] * depth, [None] * depth
    recv = {"a_w_in": [None] * (n_a * a_parts), "a_w_out": [None] * n_a, "b_w_in": [None] * n_b,
            "b_w_grp": [None] * n_b, "b_w_out": [None] * n_b, "b_scale": [None] * n_b}
    small_a = [None] * n_a

    def carried(items):
        slabs = [it[3] for it in items if it[0] == "x"]
        packs = [it[2] for it in items if it[0] == "g"]
        comms = ([_Exchange(slabs)] if slabs else []) + ([_Gather(packs)] if packs else [])
        return None if not comms else comms[0] if len(comms) == 1 else _Together(comms)

    def received(items, got):
        ordered = [it for it in items if it[0] == "x"] + [it for it in items if it[0] == "g"]
        for it, arr in zip(ordered, got):
            if it[0] == "x":
                recv[it[1]][it[2]] = arr
            else:
                small_a[it[1]] = arr

    pending = []
    small_pending = []
    for i in reversed(range(depth)):
        j = i // 2
        xi, h, proj, y, out, o = saved[i]
        g_pre, g_post = norm_pre[i:i + 1], norm_post[i:i + 1]
        if i % 2 == 0:
            (dy, dw_out, d_post[i]), _ = _out_proj_bwd(dx, out, g_post, full[i]["w_out"], y, f"a_out_bwd_{i}")
            items, pending = pending, []
            (dproj, d_w_s, dbs, d_ln_g, d_ln_b), got = _a_mix_bwd(
                proj, dy, a_ln_g[j:j + 1], a_ln_b[j:j + 1], a_w_s[j], a_b_s[j].reshape(G, P, 1),
                f"a_mix_bwd_{i}", carried(items))
            received(items, got)
            small_pending.append(("g", j, _pack([d_w_s, dbs, d_ln_g, d_ln_b])[0]))
            items = [("x", "a_w_out", j, dw_out)] if i == 0 else []
            (dw_lo,), got = _dw_in(h, dproj, f"a_dw_in_lo_{i}", 0, a_parts, carried(items))
            received(items, got)
            items = [("x", "a_w_in", a_parts * j, dw_lo)] if i == 0 else []
            (dw_hi,), got = _dw_in(h, dproj, f"a_dw_in_hi_{i}", 1, a_parts, carried(items))
            received(items, got)
            items = [("x", "a_w_in", a_parts * j + (1 if i == 0 else 0), dw_hi if i == 0 else dw_lo)]
            if i == 0:
                items, small_pending = items + small_pending, []
            (dx, d_pre[i]), got = _in_proj_bwd_dx(dproj, full[i]["w_in"], xi, g_pre, dx, f"a_in_bwd_{i}",
                                                  carried(items))
            received(items, got)
            if i > 0:
                pending += [("x", "a_w_in", a_parts * j + 1, dw_hi), ("x", "a_w_out", j, dw_out)]
        else:
            items, small_pending = small_pending, []
            (dy, dw_out, d_post[i]), got = _out_proj_bwd(dx, out, g_post, full[i]["w_out"], y, f"b_out_bwd_{i}",
                                                        carried(items))
            received(items, got)
            items, pending = pending[:1], pending[1:]
            (dproj, dsc, dw_grp), got = _b_mix_bwd(proj, dy, o, scale_full[j], full[i]["grp"], f"b_mix_bwd_{i}",
                                                  carried(items))
            received(items, got)
            items, pending = pending, []
            (dx, d_pre[i]), got = _in_proj_bwd_dx(dproj, full[i]["w_in"], xi, g_pre, dx, f"b_in_bwd_{i}",
                                                  carried(items))
            received(items, got)
            (dw_in,), _ = _dw_in(h, dproj, f"b_dw_in_{i}")
            pending += [("x", "b_w_out", j, dw_out), ("x", "b_w_grp", j, dw_grp),
                        ("x", "b_scale", j, dsc.reshape(N_DEV, 1, E // N_DEV)), ("x", "b_w_in", j, dw_in)]
    assert not pending and not small_pending

    gathered = _comm_only(_Gather([_pack([*d_pre, *d_post])[0]]), "gather_norm_grads")
    results = {k: [None] * 4 for k in names}
    norm_like = [norm_pre, norm_post]
    outs = _reduce_adam([gathered[-1]], _pack(norm_like), _pack([m_norm_pre, m_norm_post]),
                        _pack([v_norm_pre, v_norm_post]), "adam_norms")
    for q, packed in enumerate(outs):
        results["norm_pre"][q], results["norm_post"][q] = _unpack(packed, norm_like)
    a_small = ("a_w_s", "a_b_s", "a_ln_g", "a_ln_b")
    per_layer = []
    for j in range(n_a):
        like = [weights[k][j] for k in a_small]
        outs = _reduce_adam([small_a[j]], _pack(like), _pack([mom_m[k][j] for k in a_small]),
                            _pack([mom_v[k][j] for k in a_small]), f"adam_small_{j}")
        per_layer.append([_unpack(packed, like) for packed in outs])
    for q in range(4):
        for n, k in enumerate(a_small):
            results[k][q] = jnp.stack([per_layer[j][q][n] for j in range(n_a)])

    def shard_view(a, parts=1):
        return a.reshape(a.shape[0] * parts, -1, a.shape[-1])

    for k in ("a_w_in", "a_w_out", "b_w_in", "b_w_grp", "b_w_out"):
        parts = a_parts if k == "a_w_in" else 1
        w3 = shard_view(weights[k], parts)
        recvs = [r.reshape(N_DEV, w3.shape[1], w3.shape[2]) for r in recv[k]]
        outs = _reduce_adam(recvs, w3, shard_view(mom_m[k], parts), shard_view(mom_v[k], parts), f"adam_{k}")
        results[k] = [o_.reshape(weights[k].shape) for o_ in outs]
    sc_recv = jnp.concatenate(recv["b_scale"], axis=1)
    outs = _reduce_adam([sc_recv], b_scale[None], m_b_scale[None], v_b_scale[None], "adam_b_scale")
    results["b_scale"] = [o_[0] for o_ in outs]

    grad_x = dx[None]
    return (loss, grad_x, *[results[k][0] for k in names], *[results[k][1] for k in names],
            *[results[k][2] for k in names], *[results[k][3] for k in names])
```

```python
import jax
import jax.numpy as jnp
from jax import lax
from jax.experimental import pallas as pl
from jax.experimental.pallas import tpu as pltpu

F32 = jnp.float32
BF16 = jnp.bfloat16
MESH = pl.DeviceIdType.MESH

N_DEV = 8
EPS = 1e-6
CHUNK = 64
GMLP_BLOCK = 128
A_GROUPS = 8
POOL_WINDOWS = (2, 4, 8, 16)
HALO = 16
ADAM_LR = 0.001
ADAM_B1 = 0.9
ADAM_B2 = 0.999
ADAM_EPS = 1e-08
ADAM_WD = 0.01
ADAM_STEP = 10
GELU_C = 0.7978845608028654
GELU_A = 0.044715
ROW_CHUNK = 16
VMEM_LIMIT_BYTES = 56 * 1024 * 1024


def _params(**kw):
    return pltpu.CompilerParams(vmem_limit_bytes=VMEM_LIMIT_BYTES, **kw)


def _gelu(x):
    return 0.5 * x * (1.0 + jnp.tanh(GELU_C * (x + GELU_A * (x * x * x))))


def _gelu_and_grad(x):
    x2 = x * x
    t = jnp.tanh(GELU_C * (x + GELU_A * (x2 * x)))
    val = 0.5 * x * (1.0 + t)
    grad = 0.5 * (1.0 + t) + 0.5 * x * (1.0 - t * t) * (GELU_C * (1.0 + 3.0 * GELU_A * x2))
    return val, grad


def _sigmoid(z):
    return 0.5 * jnp.tanh(0.5 * z) + 0.5


def _dot(a, b):
    return jnp.dot(a, b, preferred_element_type=F32)


def _dot_nt(a, b):
    return lax.dot_general(a, b, (((1,), (1,)), ((), ())), preferred_element_type=F32)


def _dot_tn(a, b):
    return lax.dot_general(a, b, (((0,), (0,)), ((), ())), preferred_element_type=F32)


def _rms_stats(xf):
    r = lax.rsqrt(jnp.mean(xf * xf, axis=-1, keepdims=True) + EPS)
    return r, xf * r


def _rms_bwd(dy, g, r, xh):
    dxh = dy * g
    return r * (dxh - xh * jnp.mean(dxh * xh, axis=-1, keepdims=True))


def _resident(shape):
    return pl.BlockSpec(shape, lambda *_: (0,) * len(shape), pipeline_mode=pl.Buffered(1))


def _spatial_mask(transposed=False):
    p = lax.broadcasted_iota(jnp.int32, (GMLP_BLOCK, GMLP_BLOCK), 0)
    q = lax.broadcasted_iota(jnp.int32, (GMLP_BLOCK, GMLP_BLOCK), 1)
    if transposed:
        p, q = q, p
    return (q // CHUNK) <= (p // CHUNK)


def _position():
    x, y, c = lax.axis_index("x"), lax.axis_index("y"), lax.axis_index("c")
    return x, y, c


def _comm_scratch(n):
    return [pltpu.SemaphoreType.DMA((n, 7)), pltpu.SemaphoreType.DMA((n, 7)), pltpu.SemaphoreType.DMA((n,))]


class _Gather:
    def __init__(self, arrs):
        self.inputs = list(arrs)
        self.out_shape = [jax.ShapeDtypeStruct((N_DEV,) + a.shape, a.dtype) for a in arrs]
        self.scratch = _comm_scratch(len(arrs))

    def _plan(self, ins, outs, sems):
        send_sems, recv_sems, local_sems = sems
        n = len(ins)
        x, y, c = _position()
        sibling = (x, y, 1 - c)
        chips = [(1 - x, y), (x, 1 - y), (1 - x, 1 - y)]

        def index(px, py, pc):
            return 4 * px + 2 * py + pc

        def copy(a, k, block, to, src=None):
            return pltpu.make_async_remote_copy(
                src_ref=outs[a].at[block] if src is None else src, dst_ref=outs[a].at[block],
                send_sem=send_sems.at[a, k], recv_sem=recv_sems.at[a, k], device_id=to, device_id_type=MESH)

        me = index(x, y, c)
        own = [pltpu.make_async_copy(ins[a], outs[a].at[me], local_sems.at[a]) for a in range(n)]
        first = []
        for a in range(n):
            first.append(copy(a, 0, me, sibling, src=ins[a]))
            for j, chip in enumerate(chips):
                first.append(copy(a, 1 + j, me, (*chip, c), src=ins[a]))
        return n, (x, y, c), sibling, chips, index, copy, own, first

    def start(self, ins, outs, sems):
        _, _, _, _, _, _, own, first = self._plan(ins, outs, sems)
        for cp in own + first:
            cp.start()

    def middle(self, ins, outs, sems):
        n, me, sibling, chips, index, copy, _, _ = self._plan(ins, outs, sems)
        for j, chip in enumerate(chips):
            for a in range(n):
                copy(a, 1 + j, index(*chip, me[2]), me).wait_recv()
                copy(a, 4 + j, index(*chip, me[2]), sibling).start()

    def finish(self, ins, outs, sems):
        n, me, sibling, chips, index, copy, own, first = self._plan(ins, outs, sems)
        c = me[2]
        passed = [copy(a, 4 + j, index(*chip, c), sibling) for j, chip in enumerate(chips) for a in range(n)]
        for a in range(n):
            copy(a, 0, index(me[0], me[1], 1 - c), me).wait_recv()
        for j, chip in enumerate(chips):
            for a in range(n):
                copy(a, 4 + j, index(*chip, 1 - c), me).wait_recv()
        for cp in first + passed:
            cp.wait_send()
        for cp in own:
            cp.wait()


class _Exchange:
    def __init__(self, arrs):
        self.inputs = list(arrs)
        self.out_shape = [jax.ShapeDtypeStruct(a.shape, a.dtype) for a in arrs]
        self.scratch = _comm_scratch(len(arrs))

    def _plan(self, ins, outs, sems):
        send_sems, recv_sems, local_sems = sems
        n = len(ins)
        x, y, c = _position()
        me = 4 * x + 2 * y + c
        own = [pltpu.make_async_copy(ins[a].at[me], outs[a].at[me], local_sems.at[a]) for a in range(n)]
        sends, recvs = [], []
        for r in range(1, N_DEV):
            px = 1 - x if r & 4 else x
            py = 1 - y if r & 2 else y
            pc = 1 - c if r & 1 else c
            peer = 4 * px + 2 * py + pc
            for a in range(n):
                sends.append(pltpu.make_async_remote_copy(
                    src_ref=ins[a].at[peer], dst_ref=outs[a].at[me],
                    send_sem=send_sems.at[a, r - 1], recv_sem=recv_sems.at[a, r - 1],
                    device_id=(px, py, pc), device_id_type=MESH))
                recvs.append(pltpu.make_async_remote_copy(
                    src_ref=ins[a].at[peer], dst_ref=outs[a].at[peer],
                    send_sem=send_sems.at[a, r - 1], recv_sem=recv_sems.at[a, r - 1],
                    device_id=(px, py, pc), device_id_type=MESH))
        return own, sends, recvs

    def start(self, ins, outs, sems):
        own, sends, _ = self._plan(ins, outs, sems)
        for cp in own + sends:
            cp.start()

    def middle(self, ins, outs, sems):
        pass

    def finish(self, ins, outs, sems):
        own, sends, recvs = self._plan(ins, outs, sems)
        for cp in recvs:
            cp.wait_recv()
        for cp in sends:
            cp.wait_send()
        for cp in own:
            cp.wait()


class _Together:
    def __init__(self, comms):
        self.comms = list(comms)
        self.inputs = [a for c in self.comms for a in c.inputs]
        self.out_shape = [s for c in self.comms for s in c.out_shape]
        self.scratch = [s for c in self.comms for s in c.scratch]

    def _each(self, ins, outs, sems):
        at = 0
        for k, c in enumerate(self.comms):
            n = len(c.inputs)
            yield c, ins[at:at + n], outs[at:at + n], sems[3 * k:3 * k + 3]
            at += n

    def start(self, ins, outs, sems):
        for c, i, o, s in self._each(ins, outs, sems):
            c.start(i, o, s)

    def middle(self, ins, outs, sems):
        for c, i, o, s in self._each(ins, outs, sems):
            c.middle(i, o, s)

    def finish(self, ins, outs, sems):
        for c, i, o, s in self._each(ins, outs, sems):
            c.finish(i, o, s)


def _comm_only(comm, name):
    n = len(comm.inputs)

    def body(*refs):
        ins, outs, sems = refs[:n], refs[n:2 * n], refs[2 * n:]
        comm.start(ins, outs, sems)
        comm.middle(ins, outs, sems)
        comm.finish(ins, outs, sems)

    any_spec = pl.BlockSpec(memory_space=pl.ANY)
    return pl.pallas_call(
        body, name=name, out_shape=comm.out_shape, in_specs=[any_spec] * n, out_specs=[any_spec] * n,
        scratch_shapes=comm.scratch, compiler_params=pltpu.CompilerParams(has_side_effects=True),
    )(*comm.inputs)


def _pcall(body, name, grid, in_specs, out_specs, out_shape, args, scratch_shapes=(), comm=None):
    in_specs, out_specs, out_shape, scratch_shapes = list(in_specs), list(out_specs), list(out_shape), list(scratch_shapes)
    if comm is None:
        outs = pl.pallas_call(body, name=name, grid=grid, in_specs=in_specs, out_specs=out_specs, out_shape=out_shape,
                              scratch_shapes=scratch_shapes, compiler_params=_params())(*args)
        return list(outs), []
    n_in, n_out, n_scr, n_c = len(in_specs), len(out_specs), len(scratch_shapes), len(comm.inputs)

    def carrying(*refs):
        ins, refs = refs[:n_in], refs[n_in:]
        c_ins, refs = refs[:n_c], refs[n_c:]
        outs, refs = refs[:n_out], refs[n_out:]
        c_outs, refs = refs[:n_c], refs[n_c:]
        scr, sems = refs[:n_scr], refs[n_scr:]
        step, steps = 0, 1
        for d, size in enumerate(grid):
            step = step * size + pl.program_id(d)
            steps *= size

        @pl.when(step == 0)
        def _():
            comm.start(c_ins, c_outs, sems)

        body(*ins, *outs, *scr)

        @pl.when(step == max(steps - 2, 0))
        def _():
            comm.middle(c_ins, c_outs, sems)

        @pl.when(step == steps - 1)
        def _():
            comm.finish(c_ins, c_outs, sems)

    any_spec = pl.BlockSpec(memory_space=pl.ANY)
    outs = pl.pallas_call(
        carrying, name=name, grid=grid, in_specs=in_specs + [any_spec] * n_c, out_specs=out_specs + [any_spec] * n_c,
        out_shape=out_shape + comm.out_shape, scratch_shapes=scratch_shapes + comm.scratch,
        compiler_params=_params(has_side_effects=True),
    )(*args, *comm.inputs)
    return list(outs[:n_out]), list(outs[n_out:])


def _in_proj_fwd(x, g_row, w_all, name, comm=None):
    T, D = x.shape
    NS = w_all.shape[-1]
    TM = min(T, 512)

    def body(x_ref, g_ref, w_ref, proj_ref, h_ref):
        _, xh = _rms_stats(x_ref[...])
        h = (xh * g_ref[...]).astype(BF16)
        h_ref[...] = h
        for k in range(N_DEV):
            proj_ref[:, k * NS:(k + 1) * NS] = _dot(h, w_ref[k]).astype(BF16)

    return _pcall(
        body, name, (T // TM,),
        in_specs=[pl.BlockSpec((TM, D), lambda i: (i, 0)),
                  pl.BlockSpec((1, D), lambda i: (0, 0)),
                  _resident((N_DEV, D, NS))],
        out_specs=[pl.BlockSpec((TM, N_DEV * NS), lambda i: (i, 0)),
                   pl.BlockSpec((TM, D), lambda i: (i, 0))],
        out_shape=[jax.ShapeDtypeStruct((T, N_DEV * NS), BF16), jax.ShapeDtypeStruct((T, D), BF16)],
        args=(x, g_row, w_all), comm=comm)


def _a_mix_fwd(proj, ln_g, ln_b, w_s, b_s, name, comm=None):
    T, E3 = proj.shape
    E = E3 // 3
    G, P = A_GROUPS, GMLP_BLOCK
    GD = E // G
    TB = min(T, 512)

    def body(p_ref, lg_ref, lb_ref, ws_ref, bs_ref, y_ref, xh_ref, dgl_ref, rstd_ref, v_s, us_s):
        def norm_chunk(ci, carry):
            rows = pl.ds(pl.multiple_of(ci * ROW_CHUNK, ROW_CHUNK), ROW_CHUNK)
            vg, dgl = _gelu_and_grad(p_ref[rows, E:2 * E].astype(F32))
            dgl_ref[rows, :] = dgl.astype(BF16)
            xc = vg - jnp.mean(vg, axis=-1, keepdims=True)
            rstd = lax.rsqrt(jnp.mean(xc * xc, axis=-1, keepdims=True) + EPS)
            rstd_ref[rows, :] = rstd
            xh = xc * rstd
            xh_ref[rows, :] = xh.astype(BF16)
            v_s[rows, :] = (xh * lg_ref[...] + lb_ref[...]).astype(BF16)
            return carry

        def gate_chunk(ci, carry):
            rows = pl.ds(pl.multiple_of(ci * ROW_CHUNK, ROW_CHUNK), ROW_CHUNK)
            z = p_ref[rows, 2 * E:3 * E].astype(F32)
            us_s[rows, :] = _gelu(p_ref[rows, 0:E].astype(F32)) * (z * _sigmoid(z))
            return carry

        lax.fori_loop(0, TB // ROW_CHUNK, norm_chunk, 0, unroll=2)
        lax.fori_loop(0, TB // ROW_CHUNK, gate_chunk, 0, unroll=2)
        mask = _spatial_mask()
        for g in range(G):
            wm = jnp.where(mask, ws_ref[g], 0.0).astype(BF16)
            cols = slice(g * GD, (g + 1) * GD)
            for b in range(TB // P):
                rows = slice(b * P, (b + 1) * P)
                mixed = _dot(wm, v_s[rows, cols]) + bs_ref[g]
                y_ref[rows, cols] = (us_s[rows, cols] * mixed).astype(BF16)

    return _pcall(
        body, name, (T // TB,),
        in_specs=[pl.BlockSpec((TB, E3), lambda i: (i, 0)),
                  pl.BlockSpec((1, E), lambda i: (0, 0)),
                  pl.BlockSpec((1, E), lambda i: (0, 0)),
                  pl.BlockSpec((G, P, P), lambda i: (0, 0, 0)),
                  pl.BlockSpec((G, P, 1), lambda i: (0, 0, 0))],
        out_specs=[pl.BlockSpec((TB, E), lambda i: (i, 0)), pl.BlockSpec((TB, E), lambda i: (i, 0)),
                   pl.BlockSpec((TB, E), lambda i: (i, 0)), pl.BlockSpec((TB, 1), lambda i: (i, 0))],
        out_shape=[jax.ShapeDtypeStruct((T, E), BF16), jax.ShapeDtypeStruct((T, E), BF16),
                   jax.ShapeDtypeStruct((T, E), BF16), jax.ShapeDtypeStruct((T, 1), F32)],
        scratch_shapes=[pltpu.VMEM((TB, E), BF16), pltpu.VMEM((TB, E), F32)],
        args=(proj, ln_g, ln_b, w_s, b_s), comm=comm)


def _window_sum_back(ext, win):
    s, k = ext, 1
    while k < win:
        s = s + pltpu.roll(s, k, axis=0)
        k *= 2
    return s


def _window_sum_ahead(ext, win):
    n = ext.shape[0]
    s, k = ext, 1
    while k < win:
        s = s + pltpu.roll(s, n - k, axis=0)
        k *= 2
    return s


def _inv_count(t0, rows, win):
    t1 = t0 + 1 + lax.broadcasted_iota(jnp.int32, (rows, 1), 0)
    return 1.0 / jnp.minimum(t1, win).astype(F32)


def _b_mix_fwd(proj, scale, wg_all, name, comm=None):
    T, E2 = proj.shape
    E = E2 // 2
    NG = len(POOL_WINDOWS)
    GB = E // NG
    TB = min(T, 256)
    RS = wg_all.shape[-2]

    def body(p_ref, sc_ref, wg_ref, y_ref, o_ref, carry_s):
        i = pl.program_id(0)

        @pl.when(i == 0)
        def _():
            carry_s[...] = jnp.zeros_like(carry_s)

        for g, win in enumerate(POOL_WINDOWS):
            cols = slice(g * GB, (g + 1) * GB)
            xg = p_ref[:, cols].astype(F32)
            ext = jnp.concatenate([carry_s[:, cols], xg], axis=0)
            pooled = _window_sum_back(ext, win)[HALO:, :] * _inv_count(i * TB, TB, win) - xg
            carry_s[:, cols] = xg[TB - HALO:, :]
            o = _dot(pooled.astype(BF16), wg_ref[:, g].reshape(GB, GB))
            o_ref[:, cols] = o.astype(BF16)
            z = p_ref[:, E + g * GB:E + (g + 1) * GB].astype(F32)
            y_ref[:, cols] = ((o * sc_ref[:, cols]) * (z * _sigmoid(z))).astype(BF16)

    return _pcall(
        body, name, (T // TB,),
        in_specs=[pl.BlockSpec((TB, E2), lambda i: (i, 0)),
                  pl.BlockSpec((1, E), lambda i: (0, 0)),
                  pl.BlockSpec((N_DEV, NG, RS, GB), lambda i: (0, 0, 0, 0))],
        out_specs=[pl.BlockSpec((TB, E), lambda i: (i, 0)), pl.BlockSpec((TB, E), lambda i: (i, 0))],
        out_shape=[jax.ShapeDtypeStruct((T, E), BF16), jax.ShapeDtypeStruct((T, E), BF16)],
        scratch_shapes=[pltpu.VMEM((HALO, E), F32)],
        args=(proj, scale, wg_all), comm=comm)


def _out_proj_fwd(y, w_all, x, g_row, name, comm=None):
    T, E = y.shape
    D = x.shape[1]
    ES = w_all.shape[-2]
    TM = min(T, 512)

    def body(y_ref, w_ref, x_ref, g_ref, xn_ref, out_ref):
        o = _dot(y_ref[...], w_ref[...].reshape(E, D))
        out_ref[...] = o
        _, oh = _rms_stats(o)
        xn_ref[...] = x_ref[...] + oh * g_ref[...]

    return _pcall(
        body, name, (T // TM,),
        in_specs=[pl.BlockSpec((TM, E), lambda i: (i, 0)),
                  pl.BlockSpec((N_DEV, ES, D), lambda i: (0, 0, 0)),
                  pl.BlockSpec((TM, D), lambda i: (i, 0)),
                  pl.BlockSpec((1, D), lambda i: (0, 0))],
        out_specs=[pl.BlockSpec((TM, D), lambda i: (i, 0)), pl.BlockSpec((TM, D), lambda i: (i, 0))],
        out_shape=[jax.ShapeDtypeStruct((T, D), F32), jax.ShapeDtypeStruct((T, D), F32)],
        args=(y, w_all, x, g_row), comm=comm)


def _loss_head(x, target, name):
    T, D = x.shape
    TM = min(T, 512)
    nT = T // TM

    def body(x_ref, t_ref, dx_ref, loss_ref, acc_s):
        i = pl.program_id(0)

        @pl.when(i == 0)
        def _():
            acc_s[...] = jnp.zeros_like(acc_s)

        e = x_ref[...] - t_ref[...]
        dx_ref[...] = e * (1.0 / D)
        acc_s[...] += jnp.sum(e * e, axis=0, keepdims=True)

        @pl.when(i == nT - 1)
        def _():
            total = jnp.sum(acc_s[...], axis=1, keepdims=True) * (0.5 / D)
            loss_ref[...] = jnp.broadcast_to(total, loss_ref.shape)

    return _pcall(
        body, name, (nT,),
        in_specs=[pl.BlockSpec((TM, D), lambda i: (i, 0)), pl.BlockSpec((TM, D), lambda i: (i, 0))],
        out_specs=[pl.BlockSpec((TM, D), lambda i: (i, 0)), pl.BlockSpec((1, 128), lambda i: (0, 0))],
        out_shape=[jax.ShapeDtypeStruct((T, D), F32), jax.ShapeDtypeStruct((1, 128), F32)],
        scratch_shapes=[pltpu.VMEM((1, D), F32)],
        args=(x, target))[0]


def _out_proj_bwd(dxn, out, g_row, w_all, y, name, comm=None):
    T, D = dxn.shape
    E = y.shape[1]
    ES = w_all.shape[-2]
    TM = min(T, 512)
    nT = T // TM

    def body(dxn_ref, out_ref, g_ref, w_ref, y_ref, dy_ref, dw_ref, dg_ref, acc_s):
        i = pl.program_id(0)

        @pl.when(i == 0)
        def _():
            acc_s[...] = jnp.zeros_like(acc_s)
            dg_ref[...] = jnp.zeros_like(dg_ref)

        dxn_v = dxn_ref[...]
        r, oh = _rms_stats(out_ref[...])
        dg_ref[...] += jnp.sum(dxn_v * oh, axis=0, keepdims=True)
        dout = _rms_bwd(dxn_v, g_ref[...], r, oh).astype(BF16)
        dy_ref[...] = _dot_nt(dout, w_ref[...].reshape(E, D)).astype(BF16)
        acc_s[...] += _dot_tn(y_ref[...], dout)

        @pl.when(i == nT - 1)
        def _():
            dw_ref[...] = acc_s[...].reshape(N_DEV, ES, D).astype(BF16)

    return _pcall(
        body, name, (nT,),
        in_specs=[pl.BlockSpec((TM, D), lambda i: (i, 0)),
                  pl.BlockSpec((TM, D), lambda i: (i, 0)),
                  pl.BlockSpec((1, D), lambda i: (0, 0)),
                  pl.BlockSpec((N_DEV, ES, D), lambda i: (0, 0, 0)),
                  pl.BlockSpec((TM, E), lambda i: (i, 0))],
        out_specs=[pl.BlockSpec((TM, E), lambda i: (i, 0)),
                   pl.BlockSpec((N_DEV, ES, D), lambda i: (0, 0, 0)),
                   pl.BlockSpec((1, D), lambda i: (0, 0))],
        out_shape=[jax.ShapeDtypeStruct((T, E), BF16), jax.ShapeDtypeStruct((N_DEV, ES, D), BF16),
                   jax.ShapeDtypeStruct((1, D), F32)],
        scratch_shapes=[pltpu.VMEM((E, D), F32)],
        args=(dxn, out, g_row, w_all, y), comm=comm)


def _a_mix_bwd(proj, dy, xh, dgl, rstd, ln_g, ln_b, w_s, b_s, name, comm=None):
    T, E3 = proj.shape
    E = E3 // 3
    G, P = A_GROUPS, GMLP_BLOCK
    GD = E // G
    TB = min(T, 256)

    def body(up_ref, zp_ref, dy_ref, xh_ref, dgl_ref, rstd_ref, lg_ref, lb_ref, ws_ref, bs_ref,
             dp_ref, dws_ref, dbs_ref, dlg_ref, dlb_ref, v_s, a_s, bz_s, c_s, dv_s):
        @pl.when(pl.program_id(0) == 0)
        def _():
            dws_ref[...] = jnp.zeros_like(dws_ref)
            dbs_ref[...] = jnp.zeros_like(dbs_ref)
            dlg_ref[...] = jnp.zeros_like(dlg_ref)
            dlb_ref[...] = jnp.zeros_like(dlb_ref)

        def recompute(ci, carry):
            rows = pl.ds(pl.multiple_of(ci * ROW_CHUNK, ROW_CHUNK), ROW_CHUNK)
            v_s[rows, :] = (xh_ref[rows, :].astype(F32) * lg_ref[...] + lb_ref[...]).astype(BF16)
            u, du = _gelu_and_grad(up_ref[rows, :].astype(F32))
            z = zp_ref[rows, :].astype(F32)
            sg = _sigmoid(z)
            s = z * sg
            ds = sg * (1.0 + z * (1.0 - sg))
            dyv = dy_ref[rows, :].astype(F32)
            a_s[rows, :] = dyv * s * du
            bz_s[rows, :] = dyv * u * ds
            c_s[rows, :] = (dyv * u * s).astype(BF16)
            return carry

        lax.fori_loop(0, TB // ROW_CHUNK, recompute, 0, unroll=2)

        mask = _spatial_mask()
        mask_t = _spatial_mask(transposed=True)
        for g in range(G):
            w_g = ws_ref[g]
            wm = jnp.where(mask, w_g, 0.0).astype(BF16)
            wm_t = jnp.where(mask_t, w_g.T, 0.0).astype(BF16)
            cols = slice(g * GD, (g + 1) * GD)
            dws_g = jnp.zeros((P, P), F32)
            dbs_g = jnp.zeros((P, 1), F32)
            for b in range(TB // P):
                rows = slice(b * P, (b + 1) * P)
                vb = v_s[rows, cols]
                cb = c_s[rows, cols]
                mixed = _dot(wm, vb) + bs_ref[g]
                dp_ref[rows, g * GD:(g + 1) * GD] = (a_s[rows, cols] * mixed).astype(BF16)
                dp_ref[rows, 2 * E + g * GD:2 * E + (g + 1) * GD] = (bz_s[rows, cols] * mixed).astype(BF16)
                dv_s[rows, cols] = _dot(wm_t, cb)
                dws_g = dws_g + _dot_nt(cb, vb)
                dbs_g = dbs_g + jnp.sum(cb.astype(F32), axis=1, keepdims=True)
            dws_ref[g] += jnp.where(mask, dws_g, 0.0)
            dbs_ref[g] += dbs_g

        def ln_bwd(ci, carry):
            rows = pl.ds(pl.multiple_of(ci * ROW_CHUNK, ROW_CHUNK), ROW_CHUNK)
            dv = dv_s[rows, :]
            xh = xh_ref[rows, :].astype(F32)
            dlg_ref[...] += jnp.sum(dv * xh, axis=0, keepdims=True)
            dlb_ref[...] += jnp.sum(dv, axis=0, keepdims=True)
            dxh = dv * lg_ref[...]
            dvg = rstd_ref[rows, :] * (dxh - jnp.mean(dxh, axis=-1, keepdims=True)
                                       - xh * jnp.mean(dxh * xh, axis=-1, keepdims=True))
            dp_ref[rows, E:2 * E] = (dvg * dgl_ref[rows, :].astype(F32)).astype(BF16)
            return carry

        lax.fori_loop(0, TB // ROW_CHUNK, ln_bwd, 0, unroll=2)

    return _pcall(
        body, name, (T // TB,),
        in_specs=[pl.BlockSpec((TB, E), lambda i: (i, 0)),
                  pl.BlockSpec((TB, E), lambda i: (i, 2)),
                  pl.BlockSpec((TB, E), lambda i: (i, 0)),
                  pl.BlockSpec((TB, E), lambda i: (i, 0)),
                  pl.BlockSpec((TB, E), lambda i: (i, 0)),
                  pl.BlockSpec((TB, 1), lambda i: (i, 0)),
                  pl.BlockSpec((1, E), lambda i: (0, 0)),
                  pl.BlockSpec((1, E), lambda i: (0, 0)),
                  pl.BlockSpec((G, P, P), lambda i: (0, 0, 0)),
                  pl.BlockSpec((G, P, 1), lambda i: (0, 0, 0))],
        out_specs=[pl.BlockSpec((TB, E3), lambda i: (i, 0)),
                   pl.BlockSpec((G, P, P), lambda i: (0, 0, 0)),
                   pl.BlockSpec((G, P, 1), lambda i: (0, 0, 0)),
                   pl.BlockSpec((1, E), lambda i: (0, 0)),
                   pl.BlockSpec((1, E), lambda i: (0, 0))],
        out_shape=[jax.ShapeDtypeStruct((T, E3), BF16), jax.ShapeDtypeStruct((G, P, P), F32),
                   jax.ShapeDtypeStruct((G, P, 1), F32), jax.ShapeDtypeStruct((1, E), F32),
                   jax.ShapeDtypeStruct((1, E), F32)],
        scratch_shapes=[pltpu.VMEM((TB, E), BF16), pltpu.VMEM((TB, E), F32), pltpu.VMEM((TB, E), F32),
                        pltpu.VMEM((TB, E), BF16), pltpu.VMEM((TB, E), F32)],
        args=(proj, proj, dy, xh, dgl, rstd, ln_g, ln_b, w_s, b_s), comm=comm)


def _b_mix_bwd(proj, dy, o, scale, wg_all, name, comm=None):
    T, E2 = proj.shape
    E = E2 // 2
    NG = len(POOL_WINDOWS)
    GB = E // NG
    TB = min(T, 256)
    nT = T // TB
    RS = wg_all.shape[-2]
    halo_per_tile = TB // HALO

    def body(p_ref, halo_ref, dy_ref, o_ref, sc_ref, wg_ref, dp_ref, dsc_ref, dwg_ref, acc_s, carry_s):
        i = pl.program_id(0)
        tile = nT - 1 - i

        @pl.when(i == 0)
        def _():
            acc_s[...] = jnp.zeros_like(acc_s)
            carry_s[...] = jnp.zeros_like(carry_s)
            dsc_ref[...] = jnp.zeros_like(dsc_ref)

        has_history = (tile > 0).astype(F32)
        for g, win in enumerate(POOL_WINDOWS):
            cols = slice(g * GB, (g + 1) * GB)
            inv = _inv_count(tile * TB, TB, win)
            xg = p_ref[:, cols].astype(F32)
            ext = jnp.concatenate([halo_ref[:, cols].astype(F32) * has_history, xg], axis=0)
            pooled = _window_sum_back(ext, win)[HALO:, :] * inv - xg
            z = p_ref[:, E + g * GB:E + (g + 1) * GB].astype(F32)
            sg = _sigmoid(z)
            dyv = dy_ref[:, cols].astype(F32)
            ov = o_ref[:, cols].astype(F32)
            sc = sc_ref[:, cols]
            dmixed = dyv * (z * sg)
            dsc_ref[:, cols] += jnp.sum(dmixed * ov, axis=0, keepdims=True)
            dz = dyv * (ov * sc) * (sg * (1.0 + z * (1.0 - sg)))
            do = (dmixed * sc).astype(BF16)
            acc_s[:, g] += _dot_tn(pooled.astype(BF16), do).reshape(N_DEV, RS, GB)
            dpool = _dot_nt(do, wg_ref[:, g].reshape(GB, GB))
            q = dpool * inv
            ext_q = jnp.concatenate([q, carry_s[:, cols]], axis=0)
            dxb = _window_sum_ahead(ext_q, win)[:TB, :] - dpool
            carry_s[:, cols] = q[:HALO, :]
            dp_ref[:, cols] = dxb.astype(BF16)
            dp_ref[:, E + g * GB:E + (g + 1) * GB] = dz.astype(BF16)

        @pl.when(i == nT - 1)
        def _():
            dwg_ref[...] = acc_s[...].astype(BF16)

    return _pcall(
        body, name, (nT,),
        in_specs=[pl.BlockSpec((TB, E2), lambda i: (nT - 1 - i, 0)),
                  pl.BlockSpec((HALO, E), lambda i: (jnp.maximum((nT - 1 - i) * halo_per_tile - 1, 0), 0)),
                  pl.BlockSpec((TB, E), lambda i: (nT - 1 - i, 0)),
                  pl.BlockSpec((TB, E), lambda i: (nT - 1 - i, 0)),
                  pl.BlockSpec((1, E), lambda i: (0, 0)),
                  pl.BlockSpec((N_DEV, NG, RS, GB), lambda i: (0, 0, 0, 0))],
        out_specs=[pl.BlockSpec((TB, E2), lambda i: (nT - 1 - i, 0)),
                   pl.BlockSpec((1, E), lambda i: (0, 0)),
                   pl.BlockSpec((N_DEV, NG, RS, GB), lambda i: (0, 0, 0, 0))],
        out_shape=[jax.ShapeDtypeStruct((T, E2), BF16), jax.ShapeDtypeStruct((1, E), F32),
                   jax.ShapeDtypeStruct((N_DEV, NG, RS, GB), BF16)],
        scratch_shapes=[pltpu.VMEM((N_DEV, NG, RS, GB), F32), pltpu.VMEM((HALO, E), F32)],
        args=(proj, proj, dy, o, scale, wg_all), comm=comm)


def _in_proj_bwd_dx(dproj, w_all, x, g_row, dxn, name, comm=None):
    T, D = x.shape
    NS = w_all.shape[-1]
    TM = min(T, 512)

    def body(dp_ref, w_ref, x_ref, g_ref, dxn_ref, dx_ref, dg_ref):
        @pl.when(pl.program_id(0) == 0)
        def _():
            dg_ref[...] = jnp.zeros_like(dg_ref)

        dh = _dot_nt(dp_ref[:, 0:NS], w_ref[0])
        for k in range(1, N_DEV):
            dh = dh + _dot_nt(dp_ref[:, k * NS:(k + 1) * NS], w_ref[k])
        r, xh = _rms_stats(x_ref[...])
        dg_ref[...] += jnp.sum(dh * xh, axis=0, keepdims=True)
        dx_ref[...] = dxn_ref[...] + _rms_bwd(dh, g_ref[...], r, xh)

    return _pcall(
        body, name, (T // TM,),
        in_specs=[pl.BlockSpec((TM, N_DEV * NS), lambda i: (i, 0)),
                  _resident((N_DEV, D, NS)),
                  pl.BlockSpec((TM, D), lambda i: (i, 0)),
                  pl.BlockSpec((1, D), lambda i: (0, 0)),
                  pl.BlockSpec((TM, D), lambda i: (i, 0))],
        out_specs=[pl.BlockSpec((TM, D), lambda i: (i, 0)), pl.BlockSpec((1, D), lambda i: (0, 0))],
        out_shape=[jax.ShapeDtypeStruct((T, D), F32), jax.ShapeDtypeStruct((1, D), F32)],
        args=(dproj, w_all, x, g_row, dxn), comm=comm)


def _dw_in(h, dproj, name, part=0, parts=1, comm=None):
    T = h.shape[0]
    D = h.shape[1] // parts
    NS = dproj.shape[1] // N_DEV
    TK = min(T, 2048)
    nK = T // TK

    def body(h_ref, dp_ref, dw_ref, acc_s):
        t = pl.program_id(1)

        @pl.when(t == 0)
        def _():
            acc_s[...] = jnp.zeros_like(acc_s)

        acc_s[...] += _dot_tn(h_ref[...], dp_ref[...])

        @pl.when(t == nK - 1)
        def _():
            dw_ref[...] = acc_s[...].astype(BF16)

    return _pcall(
        body, name, (N_DEV, nK),
        in_specs=[pl.BlockSpec((TK, D), lambda k, t: (t, part)), pl.BlockSpec((TK, NS), lambda k, t: (t, k))],
        out_specs=[pl.BlockSpec((None, D, NS), lambda k, t: (k, 0, 0))],
        out_shape=[jax.ShapeDtypeStruct((N_DEV, D, NS), BF16)],
        scratch_shapes=[pltpu.VMEM((D, NS), F32)],
        args=(h, dproj), comm=comm)


def _reduce_adam(recvs, w, m, v, name):
    L, R, C = w.shape
    assert len(recvs) == L
    TR = R
    for cand in (256, 128, 64, 32, 16):
        if R % cand == 0 and R > cand:
            TR = cand
            break
    nR = R // TR
    c1 = 1.0 - ADAM_B1 ** ADAM_STEP
    c2 = 1.0 - ADAM_B2 ** ADAM_STEP

    def body(*refs):
        recv_refs = refs[:L]
        w_ref, m_ref, v_ref, g_ref, d_ref, nm_ref, nv_ref, g_s = refs[L:]
        layer = pl.program_id(0)
        for l in range(L):
            @pl.when(layer == l)
            def _(l=l):
                acc = recv_refs[l][0].astype(F32)
                for j in range(1, N_DEV):
                    acc = acc + recv_refs[l][j].astype(F32)
                g_s[...] = acc

        g = g_s[...]
        g_ref[...] = g
        nm = ADAM_B1 * m_ref[...] + (1.0 - ADAM_B1) * g
        nv = ADAM_B2 * v_ref[...] + (1.0 - ADAM_B2) * (g * g)
        nm_ref[...] = nm
        nv_ref[...] = nv
        d_ref[...] = -ADAM_LR * ((nm / c1) / (jnp.sqrt(nv / c2) + ADAM_EPS) + ADAM_WD * w_ref[...])

    def recv_spec(l):
        def index(layer, t):
            before = jnp.where(layer < l, 0, nR - 1)
            return (0, jnp.where(layer == l, t, before), 0)
        return pl.BlockSpec((N_DEV, TR, C), index)

    wspec = pl.BlockSpec((None, TR, C), lambda layer, t: (layer, t, 0))
    out = jax.ShapeDtypeStruct((L, R, C), F32)
    return _pcall(
        body, name, (L, nR),
        in_specs=[recv_spec(l) for l in range(L)] + [wspec] * 3,
        out_specs=[wspec] * 4, out_shape=[out] * 4,
        scratch_shapes=[pltpu.VMEM((TR, C), F32)],
        args=(*recvs, w, m, v))[0]


PACK_LANES = 128
PACK_ROWS_MULTIPLE = 256


def _pack(arrays):
    flat = jnp.concatenate([a.reshape(-1) for a in arrays])
    tile = PACK_LANES * PACK_ROWS_MULTIPLE
    padded = -(-flat.shape[0] // tile) * tile
    return jnp.pad(flat, (0, padded - flat.shape[0])).reshape(1, padded // PACK_LANES, PACK_LANES)


def _unpack(packed, like):
    flat = packed.reshape(-1)
    out, at = [], 0
    for a in like:
        out.append(flat[at:at + a.size].reshape(a.shape))
        at += a.size
    return out


def kernel(x, norm_pre, norm_post, a_w_in, a_ln_g, a_ln_b, a_w_s, a_b_s, a_w_out, b_w_in, b_w_grp, b_scale, b_w_out, loss_target, m_norm_pre, m_norm_post, m_a_w_in, m_a_ln_g, m_a_ln_b, m_a_w_s, m_a_b_s, m_a_w_out, m_b_w_in, m_b_w_grp, m_b_scale, m_b_w_out, v_norm_pre, v_norm_post, v_a_w_in, v_a_ln_g, v_a_ln_b, v_a_w_s, v_a_b_s, v_a_w_out, v_b_w_in, v_b_w_grp, v_b_scale, v_b_w_out):
    weights = dict(norm_pre=norm_pre, norm_post=norm_post, a_w_in=a_w_in, a_ln_g=a_ln_g, a_ln_b=a_ln_b, a_w_s=a_w_s,
                   a_b_s=a_b_s, a_w_out=a_w_out, b_w_in=b_w_in, b_w_grp=b_w_grp, b_scale=b_scale, b_w_out=b_w_out)
    mom_m = dict(norm_pre=m_norm_pre, norm_post=m_norm_post, a_w_in=m_a_w_in, a_ln_g=m_a_ln_g, a_ln_b=m_a_ln_b,
                 a_w_s=m_a_w_s, a_b_s=m_a_b_s, a_w_out=m_a_w_out, b_w_in=m_b_w_in, b_w_grp=m_b_w_grp,
                 b_scale=m_b_scale, b_w_out=m_b_w_out)
    mom_v = dict(norm_pre=v_norm_pre, norm_post=v_norm_post, a_w_in=v_a_w_in, a_ln_g=v_a_ln_g, a_ln_b=v_a_ln_b,
                 a_w_s=v_a_w_s, a_b_s=v_a_b_s, a_w_out=v_a_w_out, b_w_in=v_b_w_in, b_w_grp=v_b_w_grp,
                 b_scale=v_b_scale, b_w_out=v_b_w_out)
    names = list(weights)

    depth = norm_pre.shape[0]
    x0 = x[0]
    target = loss_target[0]
    T, D = x0.shape
    E = a_ln_g.shape[1]
    G, P = A_GROUPS, GMLP_BLOCK

    def shards_of(i):
        j = i // 2
        if i % 2 == 0:
            return dict(w_in=a_w_in[j].astype(BF16), w_out=a_w_out[j].astype(BF16))
        return dict(w_in=b_w_in[j].astype(BF16), w_out=b_w_out[j].astype(BF16), grp=b_w_grp[j].astype(BF16))

    shard = [shards_of(i) for i in range(depth)]
    full = [dict() for _ in range(depth)]

    def gather_into(keys, got):
        for (i, key), arr in zip(keys, got):
            full[i][key] = arr

    first = _comm_only(_Gather([shard[0]["w_in"], b_scale]), "gather_first")
    full[0]["w_in"] = first[0]
    scale_full = jnp.transpose(first[1], (1, 0, 2)).reshape(b_scale.shape[0], 1, E)

    def rest_of(i):
        return [(i, k) for k in shard[i] if k != "w_in"]

    saved = []
    xi = x0
    for i in range(depth):
        j = i // 2
        g_pre, g_post = norm_pre[i:i + 1], norm_post[i:i + 1]
        keys_in = [(0, "w_out")] if i == 0 else []
        keys_mix = []
        if i + 1 < depth:
            keys_in.append((i + 1, "w_in"))
            if i + 1 == depth - 1:
                keys_in += rest_of(i + 1)
            elif i % 2 == 0:
                keys_mix = rest_of(i + 1) + ([(i + 2, "w_out")] if i + 2 < depth - 1 else [])
        comm_in = _Gather([shard[a][k] for a, k in keys_in]) if keys_in else None
        comm_mix = _Gather([shard[a][k] for a, k in keys_mix]) if keys_mix else None
        kind = "a" if i % 2 == 0 else "b"
        (proj, h), got = _in_proj_fwd(xi, g_pre, full[i]["w_in"], f"{kind}_in_fwd_{i}", comm_in)
        gather_into(keys_in, got)
        if i % 2 == 0:
            (y, *o), got = _a_mix_fwd(proj, a_ln_g[j:j + 1], a_ln_b[j:j + 1], a_w_s[j], a_b_s[j].reshape(G, P, 1),
                                      f"a_mix_fwd_{i}", comm_mix)
        else:
            (y, o), got = _b_mix_fwd(proj, scale_full[j], full[i]["grp"], f"b_mix_fwd_{i}", comm_mix)
        gather_into(keys_mix, got)
        (x_next, out), _ = _out_proj_fwd(y, full[i]["w_out"], xi, g_post, f"{kind}_out_fwd_{i}")
        saved.append((xi, h, proj, y, out, o))
        xi = x_next

    dx, loss_row = _loss_head(xi, target, "loss_head")
    loss = lax.psum(loss_row[0, 0], ("x", "y", "c"))

    n_a, n_b = a_ln_g.shape[0], b_scale.shape[0]
    a_parts = 2
    d_pre, d_post = [None] * depth, [None] * depth
    recv = {"a_w_in": [None] * (n_a * a_parts), "a_w_out": [None] * n_a, "b_w_in": [None] * n_b,
            "b_w_grp": [None] * n_b, "b_w_out": [None] * n_b, "b_scale": [None] * n_b}
    small_a = [None] * n_a

    def carried(items):
        slabs = [it[3] for it in items if it[0] == "x"]
        packs = [it[2] for it in items if it[0] == "g"]
        comms = ([_Exchange(slabs)] if slabs else []) + ([_Gather(packs)] if packs else [])
        return None if not comms else comms[0] if len(comms) == 1 else _Together(comms)

    def received(items, got):
        ordered = [it for it in items if it[0] == "x"] + [it for it in items if it[0] == "g"]
        for it, arr in zip(ordered, got):
            if it[0] == "x":
                recv[it[1]][it[2]] = arr
            else:
                small_a[it[1]] = arr

    pending = []
    small_pending = []
    for i in reversed(range(depth)):
        j = i // 2
        xi, h, proj, y, out, o = saved[i]
        g_pre, g_post = norm_pre[i:i + 1], norm_post[i:i + 1]
        if i % 2 == 0:
            (dy, dw_out, d_post[i]), _ = _out_proj_bwd(dx, out, g_post, full[i]["w_out"], y, f"a_out_bwd_{i}")
            items, pending = pending, []
            (dproj, d_w_s, dbs, d_ln_g, d_ln_b), got = _a_mix_bwd(
                proj, dy, *o, a_ln_g[j:j + 1], a_ln_b[j:j + 1], a_w_s[j], a_b_s[j].reshape(G, P, 1),
                f"a_mix_bwd_{i}", carried(items))
            received(items, got)
            small_pending.append(("g", j, _pack([d_w_s, dbs, d_ln_g, d_ln_b])[0]))
            items = [("x", "a_w_out", j, dw_out)] if i == 0 else []
            (dw_lo,), got = _dw_in(h, dproj, f"a_dw_in_lo_{i}", 0, a_parts, carried(items))
            received(items, got)
            items = [("x", "a_w_in", a_parts * j, dw_lo)] if i == 0 else []
            (dw_hi,), got = _dw_in(h, dproj, f"a_dw_in_hi_{i}", 1, a_parts, carried(items))
            received(items, got)
            items = [("x", "a_w_in", a_parts * j + (1 if i == 0 else 0), dw_hi if i == 0 else dw_lo)]
            if i == 0:
                items, small_pending = items + small_pending, []
            (dx, d_pre[i]), got = _in_proj_bwd_dx(dproj, full[i]["w_in"], xi, g_pre, dx, f"a_in_bwd_{i}",
                                                  carried(items))
            received(items, got)
            if i > 0:
                pending += [("x", "a_w_in", a_parts * j + 1, dw_hi), ("x", "a_w_out", j, dw_out)]
        else:
            items, small_pending = small_pending, []
            (dy, dw_out, d_post[i]), got = _out_proj_bwd(dx, out, g_post, full[i]["w_out"], y, f"b_out_bwd_{i}",
                                                        carried(items))
            received(items, got)
            items, pending = pending[:1], pending[1:]
            (dproj, dsc, dw_grp), got = _b_mix_bwd(proj, dy, o, scale_full[j], full[i]["grp"], f"b_mix_bwd_{i}",
                                                  carried(items))
            received(items, got)
            items, pending = pending, []
            (dx, d_pre[i]), got = _in_proj_bwd_dx(dproj, full[i]["w_in"], xi, g_pre, dx, f"b_in_bwd_{i}",
                                                  carried(items))
            received(items, got)
            (dw_in,), _ = _dw_in(h, dproj, f"b_dw_in_{i}")
            pending += [("x", "b_w_out", j, dw_out), ("x", "b_w_grp", j, dw_grp),
                        ("x", "b_scale", j, dsc.reshape(N_DEV, 1, E // N_DEV)), ("x", "b_w_in", j, dw_in)]
    assert not pending and not small_pending

    gathered = _comm_only(_Gather([_pack([*d_pre, *d_post])[0]]), "gather_norm_grads")
    results = {k: [None] * 4 for k in names}
    norm_like = [norm_pre, norm_post]
    outs = _reduce_adam([gathered[-1]], _pack(norm_like), _pack([m_norm_pre, m_norm_post]),
                        _pack([v_norm_pre, v_norm_post]), "adam_norms")
    for q, packed in enumerate(outs):
        results["norm_pre"][q], results["norm_post"][q] = _unpack(packed, norm_like)
    a_small = ("a_w_s", "a_b_s", "a_ln_g", "a_ln_b")
    per_layer = []
    for j in range(n_a):
        like = [weights[k][j] for k in a_small]
        outs = _reduce_adam([small_a[j]], _pack(like), _pack([mom_m[k][j] for k in a_small]),
                            _pack([mom_v[k][j] for k in a_small]), f"adam_small_{j}")
        per_layer.append([_unpack(packed, like) for packed in outs])
    for q in range(4):
        for n, k in enumerate(a_small):
            results[k][q] = jnp.stack([per_layer[j][q][n] for j in range(n_a)])

    def shard_view(a, parts=1):
        return a.reshape(a.shape[0] * parts, -1, a.shape[-1])

    for k in ("a_w_in", "a_w_out", "b_w_in", "b_w_grp", "b_w_out"):
        parts = a_parts if k == "a_w_in" else 1
        w3 = shard_view(weights[k], parts)
        recvs = [r.reshape(N_DEV, w3.shape[1], w3.shape[2]) for r in recv[k]]
        outs = _reduce_adam(recvs, w3, shard_view(mom_m[k], parts), shard_view(mom_v[k], parts), f"adam_{k}")
        results[k] = [o_.reshape(weights[k].shape) for o_ in outs]
    sc_recv = jnp.concatenate(recv["b_scale"], axis=1)
    outs = _reduce_adam([sc_recv], b_scale[None], m_b_scale[None], v_b_scale[None], "adam_b_scale")
    results["b_scale"] = [o_[0] for o_ in outs]

    grad_x = dx[None]
    return (loss, grad_x, *[results[k][0] for k in names], *[results[k][1] for k in names],
            *[results[k][2] for k in names], *[results[k][3] for k in names])
```

```python
import jax
import jax.numpy as jnp
from jax import lax
from jax.experimental import pallas as pl
from jax.experimental.pallas import tpu as pltpu

F32 = jnp.float32
BF16 = jnp.bfloat16
MESH = pl.DeviceIdType.MESH

N_DEV = 8
EPS = 1e-6
CHUNK = 64
GMLP_BLOCK = 128
A_GROUPS = 8
POOL_WINDOWS = (2, 4, 8, 16)
HALO = 16
ADAM_LR = 0.001
ADAM_B1 = 0.9
ADAM_B2 = 0.999
ADAM_EPS = 1e-08
ADAM_WD = 0.01
ADAM_STEP = 10
GELU_C = 0.7978845608028654
GELU_A = 0.044715
ROW_CHUNK = 16
VMEM_LIMIT_BYTES = 56 * 1024 * 1024


def _params(**kw):
    return pltpu.CompilerParams(vmem_limit_bytes=VMEM_LIMIT_BYTES, **kw)


def _gelu(x):
    return 0.5 * x * (1.0 + jnp.tanh(GELU_C * (x + GELU_A * (x * x * x))))


def _gelu_and_grad(x):
    x2 = x * x
    t = jnp.tanh(GELU_C * (x + GELU_A * (x2 * x)))
    val = 0.5 * x * (1.0 + t)
    grad = 0.5 * (1.0 + t) + 0.5 * x * (1.0 - t * t) * (GELU_C * (1.0 + 3.0 * GELU_A * x2))
    return val, grad


def _sigmoid(z):
    return 0.5 * jnp.tanh(0.5 * z) + 0.5


def _dot(a, b):
    return jnp.dot(a, b, preferred_element_type=F32)


def _dot_nt(a, b):
    return lax.dot_general(a, b, (((1,), (1,)), ((), ())), preferred_element_type=F32)


def _dot_tn(a, b):
    return lax.dot_general(a, b, (((0,), (0,)), ((), ())), preferred_element_type=F32)


def _rms_stats(xf):
    r = lax.rsqrt(jnp.mean(xf * xf, axis=-1, keepdims=True) + EPS)
    return r, xf * r


def _rms_bwd(dy, g, r, xh):
    dxh = dy * g
    return r * (dxh - xh * jnp.mean(dxh * xh, axis=-1, keepdims=True))


def _resident(shape):
    return pl.BlockSpec(shape, lambda *_: (0,) * len(shape), pipeline_mode=pl.Buffered(1))


def _spatial_mask(transposed=False):
    p = lax.broadcasted_iota(jnp.int32, (GMLP_BLOCK, GMLP_BLOCK), 0)
    q = lax.broadcasted_iota(jnp.int32, (GMLP_BLOCK, GMLP_BLOCK), 1)
    if transposed:
        p, q = q, p
    return (q // CHUNK) <= (p // CHUNK)


def _position():
    x, y, c = lax.axis_index("x"), lax.axis_index("y"), lax.axis_index("c")
    return x, y, c


def _comm_scratch(n):
    return [pltpu.SemaphoreType.DMA((n, 7)), pltpu.SemaphoreType.DMA((n, 7)), pltpu.SemaphoreType.DMA((n,))]


class _Gather:
    def __init__(self, arrs):
        self.inputs = list(arrs)
        self.out_shape = [jax.ShapeDtypeStruct((N_DEV,) + a.shape, a.dtype) for a in arrs]
        self.scratch = _comm_scratch(len(arrs))

    def _plan(self, ins, outs, sems):
        send_sems, recv_sems, local_sems = sems
        n = len(ins)
        x, y, c = _position()
        sibling = (x, y, 1 - c)
        chips = [(1 - x, y), (x, 1 - y), (1 - x, 1 - y)]

        def index(px, py, pc):
            return 4 * px + 2 * py + pc

        def copy(a, k, block, to, src=None):
            return pltpu.make_async_remote_copy(
                src_ref=outs[a].at[block] if src is None else src, dst_ref=outs[a].at[block],
                send_sem=send_sems.at[a, k], recv_sem=recv_sems.at[a, k], device_id=to, device_id_type=MESH)

        me = index(x, y, c)
        own = [pltpu.make_async_copy(ins[a], outs[a].at[me], local_sems.at[a]) for a in range(n)]
        first = []
        for a in range(n):
            first.append(copy(a, 0, me, sibling, src=ins[a]))
            for j, chip in enumerate(chips):
                first.append(copy(a, 1 + j, me, (*chip, c), src=ins[a]))
        return n, (x, y, c), sibling, chips, index, copy, own, first

    def start(self, ins, outs, sems):
        _, _, _, _, _, _, own, first = self._plan(ins, outs, sems)
        for cp in own + first:
            cp.start()

    def middle(self, ins, outs, sems):
        n, me, sibling, chips, index, copy, _, _ = self._plan(ins, outs, sems)
        for j, chip in enumerate(chips):
            for a in range(n):
                copy(a, 1 + j, index(*chip, me[2]), me).wait_recv()
                copy(a, 4 + j, index(*chip, me[2]), sibling).start()

    def finish(self, ins, outs, sems):
        n, me, sibling, chips, index, copy, own, first = self._plan(ins, outs, sems)
        c = me[2]
        passed = [copy(a, 4 + j, index(*chip, c), sibling) for j, chip in enumerate(chips) for a in range(n)]
        for a in range(n):
            copy(a, 0, index(me[0], me[1], 1 - c), me).wait_recv()
        for j, chip in enumerate(chips):
            for a in range(n):
                copy(a, 4 + j, index(*chip, 1 - c), me).wait_recv()
        for cp in first + passed:
            cp.wait_send()
        for cp in own:
            cp.wait()


class _Exchange:
    def __init__(self, arrs, chips_only=False):
        self.inputs = list(arrs)
        self.chips_only = chips_only
        self.out_shape = [jax.ShapeDtypeStruct(a.shape, a.dtype) for a in arrs]
        self.scratch = _comm_scratch(len(arrs))

    def _plan(self, ins, outs, sems):
        send_sems, recv_sems, local_sems = sems
        n = len(ins)
        x, y, c = _position()
        scale = 1 if self.chips_only else 2
        me = 2 * x + y if self.chips_only else 4 * x + 2 * y + c
        own = [pltpu.make_async_copy(ins[a].at[me], outs[a].at[me], local_sems.at[a]) for a in range(n)]
        sends, recvs = [], []
        for r in range(1, 4 * scale):
            px = 1 - x if r & (2 * scale) else x
            py = 1 - y if r & scale else y
            pc = 1 - c if (r & 1 and not self.chips_only) else c
            peer = 2 * px + py if self.chips_only else 4 * px + 2 * py + pc
            for a in range(n):
                sends.append(pltpu.make_async_remote_copy(
                    src_ref=ins[a].at[peer], dst_ref=outs[a].at[me],
                    send_sem=send_sems.at[a, r - 1], recv_sem=recv_sems.at[a, r - 1],
                    device_id=(px, py, pc), device_id_type=MESH))
                recvs.append(pltpu.make_async_remote_copy(
                    src_ref=ins[a].at[peer], dst_ref=outs[a].at[peer],
                    send_sem=send_sems.at[a, r - 1], recv_sem=recv_sems.at[a, r - 1],
                    device_id=(px, py, pc), device_id_type=MESH))
        return own, sends, recvs

    def start(self, ins, outs, sems):
        own, sends, _ = self._plan(ins, outs, sems)
        for cp in own + sends:
            cp.start()

    def middle(self, ins, outs, sems):
        pass

    def finish(self, ins, outs, sems):
        own, sends, recvs = self._plan(ins, outs, sems)
        for cp in recvs:
            cp.wait_recv()
        for cp in sends:
            cp.wait_send()
        for cp in own:
            cp.wait()


CHIPS = N_DEV // 2


def _sibling_scratch(slab_shape, stage_slots=CHIPS):
    return [pltpu.VMEM((stage_slots,) + tuple(slab_shape), BF16), pltpu.VMEM((CHIPS,) + tuple(slab_shape), BF16),
            pltpu.SemaphoreType.DMA((CHIPS,)), pltpu.SemaphoreType.DMA((CHIPS,))]


def _to_sibling(stage_s, land_s, send_sems, recv_sems, slot, q):
    x, y, c = _position()
    return pltpu.make_async_remote_copy(
        src_ref=stage_s.at[slot], dst_ref=land_s.at[q], send_sem=send_sems.at[q], recv_sem=recv_sems.at[q],
        device_id=(x, y, 1 - c), device_id_type=MESH)


def _sum_with_sibling(slab_of, q_ref, sibling_scratch):
    stage_s, land_s, send_sems, recv_sems = sibling_scratch
    c = lax.axis_index("c")
    for q in range(CHIPS):
        stage_s[q] = slab_of(2 * q + 1 - c).astype(BF16)
        _to_sibling(stage_s, land_s, send_sems, recv_sems, q, q).start()
    for q in range(CHIPS):
        _to_sibling(stage_s, land_s, send_sems, recv_sems, q, q).wait_recv()
        q_ref[q] = (slab_of(2 * q + c) + land_s[q].astype(F32)).astype(BF16)
    for q in range(CHIPS):
        _to_sibling(stage_s, land_s, send_sems, recv_sems, q, q).wait_send()


class _Together:
    def __init__(self, comms):
        self.comms = list(comms)
        self.inputs = [a for c in self.comms for a in c.inputs]
        self.out_shape = [s for c in self.comms for s in c.out_shape]
        self.scratch = [s for c in self.comms for s in c.scratch]

    def _each(self, ins, outs, sems):
        at = 0
        for k, c in enumerate(self.comms):
            n = len(c.inputs)
            yield c, ins[at:at + n], outs[at:at + n], sems[3 * k:3 * k + 3]
            at += n

    def start(self, ins, outs, sems):
        for c, i, o, s in self._each(ins, outs, sems):
            c.start(i, o, s)

    def middle(self, ins, outs, sems):
        for c, i, o, s in self._each(ins, outs, sems):
            c.middle(i, o, s)

    def finish(self, ins, outs, sems):
        for c, i, o, s in self._each(ins, outs, sems):
            c.finish(i, o, s)


def _comm_only(comm, name):
    n = len(comm.inputs)

    def body(*refs):
        ins, outs, sems = refs[:n], refs[n:2 * n], refs[2 * n:]
        comm.start(ins, outs, sems)
        comm.middle(ins, outs, sems)
        comm.finish(ins, outs, sems)

    any_spec = pl.BlockSpec(memory_space=pl.ANY)
    return pl.pallas_call(
        body, name=name, out_shape=comm.out_shape, in_specs=[any_spec] * n, out_specs=[any_spec] * n,
        scratch_shapes=comm.scratch, compiler_params=pltpu.CompilerParams(has_side_effects=True),
    )(*comm.inputs)


def _pcall(body, name, grid, in_specs, out_specs, out_shape, args, scratch_shapes=(), comm=None):
    in_specs, out_specs, out_shape, scratch_shapes = list(in_specs), list(out_specs), list(out_shape), list(scratch_shapes)
    if comm is None:
        outs = pl.pallas_call(body, name=name, grid=grid, in_specs=in_specs, out_specs=out_specs, out_shape=out_shape,
                              scratch_shapes=scratch_shapes, compiler_params=_params())(*args)
        return list(outs), []
    n_in, n_out, n_scr, n_c = len(in_specs), len(out_specs), len(scratch_shapes), len(comm.inputs)

    def carrying(*refs):
        ins, refs = refs[:n_in], refs[n_in:]
        c_ins, refs = refs[:n_c], refs[n_c:]
        outs, refs = refs[:n_out], refs[n_out:]
        c_outs, refs = refs[:n_c], refs[n_c:]
        scr, sems = refs[:n_scr], refs[n_scr:]
        step, steps = 0, 1
        for d, size in enumerate(grid):
            step = step * size + pl.program_id(d)
            steps *= size

        @pl.when(step == 0)
        def _():
            comm.start(c_ins, c_outs, sems)

        body(*ins, *outs, *scr)

        @pl.when(step == max(steps - 2, 0))
        def _():
            comm.middle(c_ins, c_outs, sems)

        @pl.when(step == steps - 1)
        def _():
            comm.finish(c_ins, c_outs, sems)

    any_spec = pl.BlockSpec(memory_space=pl.ANY)
    outs = pl.pallas_call(
        carrying, name=name, grid=grid, in_specs=in_specs + [any_spec] * n_c, out_specs=out_specs + [any_spec] * n_c,
        out_shape=out_shape + comm.out_shape, scratch_shapes=scratch_shapes + comm.scratch,
        compiler_params=_params(has_side_effects=True),
    )(*args, *comm.inputs)
    return list(outs[:n_out]), list(outs[n_out:])


def _in_proj_fwd(x, g_row, w_all, name, comm=None):
    T, D = x.shape
    NS = w_all.shape[-1]
    TM = min(T, 512)

    def body(x_ref, g_ref, w_ref, proj_ref, h_ref):
        _, xh = _rms_stats(x_ref[...])
        h = (xh * g_ref[...]).astype(BF16)
        h_ref[...] = h
        for k in range(N_DEV):
            proj_ref[:, k * NS:(k + 1) * NS] = _dot(h, w_ref[k]).astype(BF16)

    return _pcall(
        body, name, (T // TM,),
        in_specs=[pl.BlockSpec((TM, D), lambda i: (i, 0)),
                  pl.BlockSpec((1, D), lambda i: (0, 0)),
                  _resident((N_DEV, D, NS))],
        out_specs=[pl.BlockSpec((TM, N_DEV * NS), lambda i: (i, 0)),
                   pl.BlockSpec((TM, D), lambda i: (i, 0))],
        out_shape=[jax.ShapeDtypeStruct((T, N_DEV * NS), BF16), jax.ShapeDtypeStruct((T, D), BF16)],
        args=(x, g_row, w_all), comm=comm)


def _a_mix_fwd(proj, ln_g, ln_b, w_s, b_s, name, comm=None):
    T, E3 = proj.shape
    E = E3 // 3
    G, P = A_GROUPS, GMLP_BLOCK
    GD = E // G
    TB = min(T, 512)

    def body(p_ref, lg_ref, lb_ref, ws_ref, bs_ref, y_ref, xh_ref, dgl_ref, rstd_ref, v_s, us_s):
        def norm_chunk(ci, carry):
            rows = pl.ds(pl.multiple_of(ci * ROW_CHUNK, ROW_CHUNK), ROW_CHUNK)
            vg, dgl = _gelu_and_grad(p_ref[rows, E:2 * E].astype(F32))
            dgl_ref[rows, :] = dgl.astype(BF16)
            xc = vg - jnp.mean(vg, axis=-1, keepdims=True)
            rstd = lax.rsqrt(jnp.mean(xc * xc, axis=-1, keepdims=True) + EPS)
            rstd_ref[rows, :] = rstd
            xh = xc * rstd
            xh_ref[rows, :] = xh.astype(BF16)
            v_s[rows, :] = (xh * lg_ref[...] + lb_ref[...]).astype(BF16)
            return carry

        def gate_chunk(ci, carry):
            rows = pl.ds(pl.multiple_of(ci * ROW_CHUNK, ROW_CHUNK), ROW_CHUNK)
            z = p_ref[rows, 2 * E:3 * E].astype(F32)
            us_s[rows, :] = _gelu(p_ref[rows, 0:E].astype(F32)) * (z * _sigmoid(z))
            return carry

        lax.fori_loop(0, TB // ROW_CHUNK, norm_chunk, 0, unroll=2)
        lax.fori_loop(0, TB // ROW_CHUNK, gate_chunk, 0, unroll=2)
        mask = _spatial_mask()
        for g in range(G):
            wm = jnp.where(mask, ws_ref[g], 0.0).astype(BF16)
            cols = slice(g * GD, (g + 1) * GD)
            for b in range(TB // P):
                rows = slice(b * P, (b + 1) * P)
                mixed = _dot(wm, v_s[rows, cols]) + bs_ref[g]
                y_ref[rows, cols] = (us_s[rows, cols] * mixed).astype(BF16)

    return _pcall(
        body, name, (T // TB,),
        in_specs=[pl.BlockSpec((TB, E3), lambda i: (i, 0)),
                  pl.BlockSpec((1, E), lambda i: (0, 0)),
                  pl.BlockSpec((1, E), lambda i: (0, 0)),
                  pl.BlockSpec((G, P, P), lambda i: (0, 0, 0)),
                  pl.BlockSpec((G, P, 1), lambda i: (0, 0, 0))],
        out_specs=[pl.BlockSpec((TB, E), lambda i: (i, 0)), pl.BlockSpec((TB, E), lambda i: (i, 0)),
                   pl.BlockSpec((TB, E), lambda i: (i, 0)), pl.BlockSpec((TB, 1), lambda i: (i, 0))],
        out_shape=[jax.ShapeDtypeStruct((T, E), BF16), jax.ShapeDtypeStruct((T, E), BF16),
                   jax.ShapeDtypeStruct((T, E), BF16), jax.ShapeDtypeStruct((T, 1), F32)],
        scratch_shapes=[pltpu.VMEM((TB, E), BF16), pltpu.VMEM((TB, E), F32)],
        args=(proj, ln_g, ln_b, w_s, b_s), comm=comm)


def _window_sum_back(ext, win):
    s, k = ext, 1
    while k < win:
        s = s + pltpu.roll(s, k, axis=0)
        k *= 2
    return s


def _window_sum_ahead(ext, win):
    n = ext.shape[0]
    s, k = ext, 1
    while k < win:
        s = s + pltpu.roll(s, n - k, axis=0)
        k *= 2
    return s


def _inv_count(t0, rows, win):
    t1 = t0 + 1 + lax.broadcasted_iota(jnp.int32, (rows, 1), 0)
    return 1.0 / jnp.minimum(t1, win).astype(F32)


def _b_mix_fwd(proj, scale, wg_all, name, comm=None):
    T, E2 = proj.shape
    E = E2 // 2
    NG = len(POOL_WINDOWS)
    GB = E // NG
    TB = min(T, 256)
    RS = wg_all.shape[-2]

    def body(p_ref, sc_ref, wg_ref, y_ref, o_ref, carry_s):
        i = pl.program_id(0)

        @pl.when(i == 0)
        def _():
            carry_s[...] = jnp.zeros_like(carry_s)

        for g, win in enumerate(POOL_WINDOWS):
            cols = slice(g * GB, (g + 1) * GB)
            xg = p_ref[:, cols].astype(F32)
            ext = jnp.concatenate([carry_s[:, cols], xg], axis=0)
            pooled = _window_sum_back(ext, win)[HALO:, :] * _inv_count(i * TB, TB, win) - xg
            carry_s[:, cols] = xg[TB - HALO:, :]
            o = _dot(pooled.astype(BF16), wg_ref[:, g].reshape(GB, GB))
            o_ref[:, cols] = o.astype(BF16)
            z = p_ref[:, E + g * GB:E + (g + 1) * GB].astype(F32)
            y_ref[:, cols] = ((o * sc_ref[:, cols]) * (z * _sigmoid(z))).astype(BF16)

    return _pcall(
        body, name, (T // TB,),
        in_specs=[pl.BlockSpec((TB, E2), lambda i: (i, 0)),
                  pl.BlockSpec((1, E), lambda i: (0, 0)),
                  pl.BlockSpec((N_DEV, NG, RS, GB), lambda i: (0, 0, 0, 0))],
        out_specs=[pl.BlockSpec((TB, E), lambda i: (i, 0)), pl.BlockSpec((TB, E), lambda i: (i, 0))],
        out_shape=[jax.ShapeDtypeStruct((T, E), BF16), jax.ShapeDtypeStruct((T, E), BF16)],
        scratch_shapes=[pltpu.VMEM((HALO, E), F32)],
        args=(proj, scale, wg_all), comm=comm)


def _out_proj_fwd(y, w_all, x, g_row, name, comm=None):
    T, E = y.shape
    D = x.shape[1]
    ES = w_all.shape[-2]
    TM = min(T, 512)

    def body(y_ref, w_ref, x_ref, g_ref, xn_ref, out_ref):
        o = _dot(y_ref[...], w_ref[...].reshape(E, D))
        out_ref[...] = o
        _, oh = _rms_stats(o)
        xn_ref[...] = x_ref[...] + oh * g_ref[...]

    return _pcall(
        body, name, (T // TM,),
        in_specs=[pl.BlockSpec((TM, E), lambda i: (i, 0)),
                  pl.BlockSpec((N_DEV, ES, D), lambda i: (0, 0, 0)),
                  pl.BlockSpec((TM, D), lambda i: (i, 0)),
                  pl.BlockSpec((1, D), lambda i: (0, 0))],
        out_specs=[pl.BlockSpec((TM, D), lambda i: (i, 0)), pl.BlockSpec((TM, D), lambda i: (i, 0))],
        out_shape=[jax.ShapeDtypeStruct((T, D), F32), jax.ShapeDtypeStruct((T, D), F32)],
        args=(y, w_all, x, g_row), comm=comm)


def _loss_head(x, target, name):
    T, D = x.shape
    TM = min(T, 512)
    nT = T // TM

    def body(x_ref, t_ref, dx_ref, loss_ref, acc_s):
        i = pl.program_id(0)

        @pl.when(i == 0)
        def _():
            acc_s[...] = jnp.zeros_like(acc_s)

        e = x_ref[...] - t_ref[...]
        dx_ref[...] = e * (1.0 / D)
        acc_s[...] += jnp.sum(e * e, axis=0, keepdims=True)

        @pl.when(i == nT - 1)
        def _():
            total = jnp.sum(acc_s[...], axis=1, keepdims=True) * (0.5 / D)
            loss_ref[...] = jnp.broadcast_to(total, loss_ref.shape)

    return _pcall(
        body, name, (nT,),
        in_specs=[pl.BlockSpec((TM, D), lambda i: (i, 0)), pl.BlockSpec((TM, D), lambda i: (i, 0))],
        out_specs=[pl.BlockSpec((TM, D), lambda i: (i, 0)), pl.BlockSpec((1, 128), lambda i: (0, 0))],
        out_shape=[jax.ShapeDtypeStruct((T, D), F32), jax.ShapeDtypeStruct((1, 128), F32)],
        scratch_shapes=[pltpu.VMEM((1, D), F32)],
        args=(x, target))[0]


def _out_proj_bwd(dxn, out, g_row, w_all, y, name, comm=None):
    T, D = dxn.shape
    E = y.shape[1]
    ES = w_all.shape[-2]
    TM = min(T, 512)
    nT = T // TM

    def body(dxn_ref, out_ref, g_ref, w_ref, y_ref, dy_ref, dw_ref, dg_ref, acc_s, *sibling_scratch):
        i = pl.program_id(0)

        @pl.when(i == 0)
        def _():
            acc_s[...] = jnp.zeros_like(acc_s)
            dg_ref[...] = jnp.zeros_like(dg_ref)

        dxn_v = dxn_ref[...]
        r, oh = _rms_stats(out_ref[...])
        dg_ref[...] += jnp.sum(dxn_v * oh, axis=0, keepdims=True)
        dout = _rms_bwd(dxn_v, g_ref[...], r, oh).astype(BF16)
        dy_ref[...] = _dot_nt(dout, w_ref[...].reshape(E, D)).astype(BF16)
        acc_s[...] += _dot_tn(y_ref[...], dout)

        @pl.when(i == nT - 1)
        def _():
            def slab_of(k):
                return acc_s[pl.ds(pl.multiple_of(k * ES, ES), ES), :]

            _sum_with_sibling(slab_of, dw_ref, sibling_scratch)

    return _pcall(
        body, name, (nT,),
        in_specs=[pl.BlockSpec((TM, D), lambda i: (i, 0)),
                  pl.BlockSpec((TM, D), lambda i: (i, 0)),
                  pl.BlockSpec((1, D), lambda i: (0, 0)),
                  pl.BlockSpec((N_DEV, ES, D), lambda i: (0, 0, 0)),
                  pl.BlockSpec((TM, E), lambda i: (i, 0))],
        out_specs=[pl.BlockSpec((TM, E), lambda i: (i, 0)),
                   pl.BlockSpec((CHIPS, ES, D), lambda i: (0, 0, 0)),
                   pl.BlockSpec((1, D), lambda i: (0, 0))],
        out_shape=[jax.ShapeDtypeStruct((T, E), BF16), jax.ShapeDtypeStruct((CHIPS, ES, D), BF16),
                   jax.ShapeDtypeStruct((1, D), F32)],
        scratch_shapes=[pltpu.VMEM((E, D), F32)] + _sibling_scratch((ES, D)),
        args=(dxn, out, g_row, w_all, y), comm=comm)


def _a_mix_bwd(proj, dy, xh, dgl, rstd, ln_g, ln_b, w_s, b_s, name, comm=None):
    T, E3 = proj.shape
    E = E3 // 3
    G, P = A_GROUPS, GMLP_BLOCK
    GD = E // G
    TB = min(T, 256)

    def body(up_ref, zp_ref, dy_ref, xh_ref, dgl_ref, rstd_ref, lg_ref, lb_ref, ws_ref, bs_ref,
             dp_ref, dws_ref, dbs_ref, dlg_ref, dlb_ref, v_s, a_s, bz_s, c_s, dv_s):
        @pl.when(pl.program_id(0) == 0)
        def _():
            dws_ref[...] = jnp.zeros_like(dws_ref)
            dbs_ref[...] = jnp.zeros_like(dbs_ref)
            dlg_ref[...] = jnp.zeros_like(dlg_ref)
            dlb_ref[...] = jnp.zeros_like(dlb_ref)

        def recompute(ci, carry):
            rows = pl.ds(pl.multiple_of(ci * ROW_CHUNK, ROW_CHUNK), ROW_CHUNK)
            v_s[rows, :] = (xh_ref[rows, :].astype(F32) * lg_ref[...] + lb_ref[...]).astype(BF16)
            u, du = _gelu_and_grad(up_ref[rows, :].astype(F32))
            z = zp_ref[rows, :].astype(F32)
            sg = _sigmoid(z)
            s = z * sg
            ds = sg * (1.0 + z * (1.0 - sg))
            dyv = dy_ref[rows, :].astype(F32)
            a_s[rows, :] = dyv * s * du
            bz_s[rows, :] = dyv * u * ds
            c_s[rows, :] = (dyv * u * s).astype(BF16)
            return carry

        lax.fori_loop(0, TB // ROW_CHUNK, recompute, 0, unroll=2)

        mask = _spatial_mask()
        mask_t = _spatial_mask(transposed=True)
        for g in range(G):
            w_g = ws_ref[g]
            wm = jnp.where(mask, w_g, 0.0).astype(BF16)
            wm_t = jnp.where(mask_t, w_g.T, 0.0).astype(BF16)
            cols = slice(g * GD, (g + 1) * GD)
            dws_g = jnp.zeros((P, P), F32)
            dbs_g = jnp.zeros((P, 1), F32)
            for b in range(TB // P):
                rows = slice(b * P, (b + 1) * P)
                vb = v_s[rows, cols]
                cb = c_s[rows, cols]
                mixed = _dot(wm, vb) + bs_ref[g]
                dp_ref[rows, g * GD:(g + 1) * GD] = (a_s[rows, cols] * mixed).astype(BF16)
                dp_ref[rows, 2 * E + g * GD:2 * E + (g + 1) * GD] = (bz_s[rows, cols] * mixed).astype(BF16)
                dv_s[rows, cols] = _dot(wm_t, cb)
                dws_g = dws_g + _dot_nt(cb, vb)
                dbs_g = dbs_g + jnp.sum(cb.astype(F32), axis=1, keepdims=True)
            dws_ref[g] += jnp.where(mask, dws_g, 0.0)
            dbs_ref[g] += dbs_g

        def ln_bwd(ci, carry):
            rows = pl.ds(pl.multiple_of(ci * ROW_CHUNK, ROW_CHUNK), ROW_CHUNK)
            dv = dv_s[rows, :]
            xh = xh_ref[rows, :].astype(F32)
            dlg_ref[...] += jnp.sum(dv * xh, axis=0, keepdims=True)
            dlb_ref[...] += jnp.sum(dv, axis=0, keepdims=True)
            dxh = dv * lg_ref[...]
            dvg = rstd_ref[rows, :] * (dxh - jnp.mean(dxh, axis=-1, keepdims=True)
                                       - xh * jnp.mean(dxh * xh, axis=-1, keepdims=True))
            dp_ref[rows, E:2 * E] = (dvg * dgl_ref[rows, :].astype(F32)).astype(BF16)
            return carry

        lax.fori_loop(0, TB // ROW_CHUNK, ln_bwd, 0, unroll=2)

    return _pcall(
        body, name, (T // TB,),
        in_specs=[pl.BlockSpec((TB, E), lambda i: (i, 0)),
                  pl.BlockSpec((TB, E), lambda i: (i, 2)),
                  pl.BlockSpec((TB, E), lambda i: (i, 0)),
                  pl.BlockSpec((TB, E), lambda i: (i, 0)),
                  pl.BlockSpec((TB, E), lambda i: (i, 0)),
                  pl.BlockSpec((TB, 1), lambda i: (i, 0)),
                  pl.BlockSpec((1, E), lambda i: (0, 0)),
                  pl.BlockSpec((1, E), lambda i: (0, 0)),
                  pl.BlockSpec((G, P, P), lambda i: (0, 0, 0)),
                  pl.BlockSpec((G, P, 1), lambda i: (0, 0, 0))],
        out_specs=[pl.BlockSpec((TB, E3), lambda i: (i, 0)),
                   pl.BlockSpec((G, P, P), lambda i: (0, 0, 0)),
                   pl.BlockSpec((G, P, 1), lambda i: (0, 0, 0)),
                   pl.BlockSpec((1, E), lambda i: (0, 0)),
                   pl.BlockSpec((1, E), lambda i: (0, 0))],
        out_shape=[jax.ShapeDtypeStruct((T, E3), BF16), jax.ShapeDtypeStruct((G, P, P), F32),
                   jax.ShapeDtypeStruct((G, P, 1), F32), jax.ShapeDtypeStruct((1, E), F32),
                   jax.ShapeDtypeStruct((1, E), F32)],
        scratch_shapes=[pltpu.VMEM((TB, E), BF16), pltpu.VMEM((TB, E), F32), pltpu.VMEM((TB, E), F32),
                        pltpu.VMEM((TB, E), BF16), pltpu.VMEM((TB, E), F32)],
        args=(proj, proj, dy, xh, dgl, rstd, ln_g, ln_b, w_s, b_s), comm=comm)


def _b_mix_bwd(proj, dy, o, scale, wg_all, name, comm=None):
    T, E2 = proj.shape
    E = E2 // 2
    NG = len(POOL_WINDOWS)
    GB = E // NG
    TB = min(T, 256)
    nT = T // TB
    RS = wg_all.shape[-2]
    halo_per_tile = TB // HALO

    def body(p_ref, halo_ref, dy_ref, o_ref, sc_ref, wg_ref, dp_ref, dsc_ref, dwg_ref, acc_s, carry_s,
             *sibling_scratch):
        i = pl.program_id(0)
        tile = nT - 1 - i

        @pl.when(i == 0)
        def _():
            acc_s[...] = jnp.zeros_like(acc_s)
            carry_s[...] = jnp.zeros_like(carry_s)
            dsc_ref[...] = jnp.zeros_like(dsc_ref)

        has_history = (tile > 0).astype(F32)
        for g, win in enumerate(POOL_WINDOWS):
            cols = slice(g * GB, (g + 1) * GB)
            inv = _inv_count(tile * TB, TB, win)
            xg = p_ref[:, cols].astype(F32)
            ext = jnp.concatenate([halo_ref[:, cols].astype(F32) * has_history, xg], axis=0)
            pooled = _window_sum_back(ext, win)[HALO:, :] * inv - xg
            z = p_ref[:, E + g * GB:E + (g + 1) * GB].astype(F32)
            sg = _sigmoid(z)
            dyv = dy_ref[:, cols].astype(F32)
            ov = o_ref[:, cols].astype(F32)
            sc = sc_ref[:, cols]
            dmixed = dyv * (z * sg)
            dsc_ref[:, cols] += jnp.sum(dmixed * ov, axis=0, keepdims=True)
            dz = dyv * (ov * sc) * (sg * (1.0 + z * (1.0 - sg)))
            do = (dmixed * sc).astype(BF16)
            acc_s[:, g] += _dot_tn(pooled.astype(BF16), do).reshape(N_DEV, RS, GB)
            dpool = _dot_nt(do, wg_ref[:, g].reshape(GB, GB))
            q = dpool * inv
            ext_q = jnp.concatenate([q, carry_s[:, cols]], axis=0)
            dxb = _window_sum_ahead(ext_q, win)[:TB, :] - dpool
            carry_s[:, cols] = q[:HALO, :]
            dp_ref[:, cols] = dxb.astype(BF16)
            dp_ref[:, E + g * GB:E + (g + 1) * GB] = dz.astype(BF16)

        @pl.when(i == nT - 1)
        def _():
            _sum_with_sibling(lambda k: acc_s[k], dwg_ref, sibling_scratch)

    return _pcall(
        body, name, (nT,),
        in_specs=[pl.BlockSpec((TB, E2), lambda i: (nT - 1 - i, 0)),
                  pl.BlockSpec((HALO, E), lambda i: (jnp.maximum((nT - 1 - i) * halo_per_tile - 1, 0), 0)),
                  pl.BlockSpec((TB, E), lambda i: (nT - 1 - i, 0)),
                  pl.BlockSpec((TB, E), lambda i: (nT - 1 - i, 0)),
                  pl.BlockSpec((1, E), lambda i: (0, 0)),
                  pl.BlockSpec((N_DEV, NG, RS, GB), lambda i: (0, 0, 0, 0))],
        out_specs=[pl.BlockSpec((TB, E2), lambda i: (nT - 1 - i, 0)),
                   pl.BlockSpec((1, E), lambda i: (0, 0)),
                   pl.BlockSpec((CHIPS, NG, RS, GB), lambda i: (0, 0, 0, 0))],
        out_shape=[jax.ShapeDtypeStruct((T, E2), BF16), jax.ShapeDtypeStruct((1, E), F32),
                   jax.ShapeDtypeStruct((CHIPS, NG, RS, GB), BF16)],
        scratch_shapes=[pltpu.VMEM((N_DEV, NG, RS, GB), F32), pltpu.VMEM((HALO, E), F32)]
        + _sibling_scratch((NG, RS, GB)),
        args=(proj, proj, dy, o, scale, wg_all), comm=comm)


def _in_proj_bwd_dx(dproj, w_all, x, g_row, dxn, name, comm=None):
    T, D = x.shape
    NS = w_all.shape[-1]
    TM = min(T, 512)

    def body(dp_ref, w_ref, x_ref, g_ref, dxn_ref, dx_ref, dg_ref):
        @pl.when(pl.program_id(0) == 0)
        def _():
            dg_ref[...] = jnp.zeros_like(dg_ref)

        dh = _dot_nt(dp_ref[:, 0:NS], w_ref[0])
        for k in range(1, N_DEV):
            dh = dh + _dot_nt(dp_ref[:, k * NS:(k + 1) * NS], w_ref[k])
        r, xh = _rms_stats(x_ref[...])
        dg_ref[...] += jnp.sum(dh * xh, axis=0, keepdims=True)
        dx_ref[...] = dxn_ref[...] + _rms_bwd(dh, g_ref[...], r, xh)

    return _pcall(
        body, name, (T // TM,),
        in_specs=[pl.BlockSpec((TM, N_DEV * NS), lambda i: (i, 0)),
                  _resident((N_DEV, D, NS)),
                  pl.BlockSpec((TM, D), lambda i: (i, 0)),
                  pl.BlockSpec((1, D), lambda i: (0, 0)),
                  pl.BlockSpec((TM, D), lambda i: (i, 0))],
        out_specs=[pl.BlockSpec((TM, D), lambda i: (i, 0)), pl.BlockSpec((1, D), lambda i: (0, 0))],
        out_shape=[jax.ShapeDtypeStruct((T, D), F32), jax.ShapeDtypeStruct((1, D), F32)],
        args=(dproj, w_all, x, g_row, dxn), comm=comm)


STAGE_SLOTS = 2


def _dw_in(h, dproj, name, comm=None):
    T, D = h.shape
    NS = dproj.shape[1] // N_DEV
    TK = min(T, 2048)
    nK = T // TK

    def body(h_ref, dp_ref, q_ref, acc_s, stage_s, land_s, send_sems, recv_sems):
        k, t = pl.program_id(0), pl.program_id(1)
        c = lax.axis_index("c")

        def to_sibling(q):
            return _to_sibling(stage_s, land_s, send_sems, recv_sems, q % STAGE_SLOTS, q)

        @pl.when(t == 0)
        def _():
            acc_s[...] = jnp.zeros_like(acc_s)

        acc_s[...] += _dot_tn(h_ref[...], dp_ref[...])

        @pl.when(t == nK - 1)
        def _():
            q = k // 2

            @pl.when(k % 2 == c)
            def _():
                q_ref[q] = acc_s[...].astype(BF16)

            @pl.when(k % 2 != c)
            def _():
                @pl.when(q >= STAGE_SLOTS)
                def _():
                    to_sibling(q - STAGE_SLOTS).wait_send()

                stage_s[q % STAGE_SLOTS] = acc_s[...].astype(BF16)
                to_sibling(q).start()

        @pl.when((k == N_DEV - 1) & (t == nK - 1))
        def _():
            for q in range(CHIPS - STAGE_SLOTS, CHIPS):
                to_sibling(q).wait_send()
            for q in range(CHIPS):
                to_sibling(q).wait_recv()
                q_ref[q] = (q_ref[q].astype(F32) + land_s[q].astype(F32)).astype(BF16)

    return _pcall(
        body, name, (N_DEV, nK),
        in_specs=[pl.BlockSpec((TK, D), lambda k, t: (t, 0)), pl.BlockSpec((TK, NS), lambda k, t: (t, k))],
        out_specs=[pl.BlockSpec((CHIPS, D, NS), lambda k, t: (0, 0, 0))],
        out_shape=[jax.ShapeDtypeStruct((CHIPS, D, NS), BF16)],
        scratch_shapes=[pltpu.VMEM((D, NS), F32)] + _sibling_scratch((D, NS), STAGE_SLOTS),
        args=(h, dproj), comm=comm)


def _reduce_adam(recvs, w, m, v, name):
    L, R, C = w.shape
    assert len(recvs) == L
    senders = recvs[0].shape[0]
    TR = R
    for cand in (256, 128, 64, 32, 16):
        if R % cand == 0 and R > cand:
            TR = cand
            break
    nR = R // TR
    c1 = 1.0 - ADAM_B1 ** ADAM_STEP
    c2 = 1.0 - ADAM_B2 ** ADAM_STEP

    def body(*refs):
        recv_refs = refs[:L]
        w_ref, m_ref, v_ref, g_ref, d_ref, nm_ref, nv_ref, g_s = refs[L:]
        layer = pl.program_id(0)
        for l in range(L):
            @pl.when(layer == l)
            def _(l=l):
                acc = recv_refs[l][0].astype(F32)
                for j in range(1, senders):
                    acc = acc + recv_refs[l][j].astype(F32)
                g_s[...] = acc

        g = g_s[...]
        g_ref[...] = g
        nm = ADAM_B1 * m_ref[...] + (1.0 - ADAM_B1) * g
        nv = ADAM_B2 * v_ref[...] + (1.0 - ADAM_B2) * (g * g)
        nm_ref[...] = nm
        nv_ref[...] = nv
        d_ref[...] = -ADAM_LR * ((nm / c1) / (jnp.sqrt(nv / c2) + ADAM_EPS) + ADAM_WD * w_ref[...])

    def recv_spec(l):
        def index(layer, t):
            before = jnp.where(layer < l, 0, nR - 1)
            return (0, jnp.where(layer == l, t, before), 0)
        return pl.BlockSpec((senders, TR, C), index)

    wspec = pl.BlockSpec((None, TR, C), lambda layer, t: (layer, t, 0))
    out = jax.ShapeDtypeStruct((L, R, C), F32)
    return _pcall(
        body, name, (L, nR),
        in_specs=[recv_spec(l) for l in range(L)] + [wspec] * 3,
        out_specs=[wspec] * 4, out_shape=[out] * 4,
        scratch_shapes=[pltpu.VMEM((TR, C), F32)],
        args=(*recvs, w, m, v))[0]


PACK_LANES = 128
PACK_ROWS_MULTIPLE = 256


def _pack(arrays):
    flat = jnp.concatenate([a.reshape(-1) for a in arrays])
    tile = PACK_LANES * PACK_ROWS_MULTIPLE
    padded = -(-flat.shape[0] // tile) * tile
    return jnp.pad(flat, (0, padded - flat.shape[0])).reshape(1, padded // PACK_LANES, PACK_LANES)


def _unpack(packed, like):
    flat = packed.reshape(-1)
    out, at = [], 0
    for a in like:
        out.append(flat[at:at + a.size].reshape(a.shape))
        at += a.size
    return out


def kernel(x, norm_pre, norm_post, a_w_in, a_ln_g, a_ln_b, a_w_s, a_b_s, a_w_out, b_w_in, b_w_grp, b_scale, b_w_out, loss_target, m_norm_pre, m_norm_post, m_a_w_in, m_a_ln_g, m_a_ln_b, m_a_w_s, m_a_b_s, m_a_w_out, m_b_w_in, m_b_w_grp, m_b_scale, m_b_w_out, v_norm_pre, v_norm_post, v_a_w_in, v_a_ln_g, v_a_ln_b, v_a_w_s, v_a_b_s, v_a_w_out, v_b_w_in, v_b_w_grp, v_b_scale, v_b_w_out):
    weights = dict(norm_pre=norm_pre, norm_post=norm_post, a_w_in=a_w_in, a_ln_g=a_ln_g, a_ln_b=a_ln_b, a_w_s=a_w_s,
                   a_b_s=a_b_s, a_w_out=a_w_out, b_w_in=b_w_in, b_w_grp=b_w_grp, b_scale=b_scale, b_w_out=b_w_out)
    mom_m = dict(norm_pre=m_norm_pre, norm_post=m_norm_post, a_w_in=m_a_w_in, a_ln_g=m_a_ln_g, a_ln_b=m_a_ln_b,
                 a_w_s=m_a_w_s, a_b_s=m_a_b_s, a_w_out=m_a_w_out, b_w_in=m_b_w_in, b_w_grp=m_b_w_grp,
                 b_scale=m_b_scale, b_w_out=m_b_w_out)
    mom_v = dict(norm_pre=v_norm_pre, norm_post=v_norm_post, a_w_in=v_a_w_in, a_ln_g=v_a_ln_g, a_ln_b=v_a_ln_b,
                 a_w_s=v_a_w_s, a_b_s=v_a_b_s, a_w_out=v_a_w_out, b_w_in=v_b_w_in, b_w_grp=v_b_w_grp,
                 b_scale=v_b_scale, b_w_out=v_b_w_out)
    names = list(weights)

    depth = norm_pre.shape[0]
    x0 = x[0]
    target = loss_target[0]
    T, D = x0.shape
    E = a_ln_g.shape[1]
    G, P = A_GROUPS, GMLP_BLOCK

    def shards_of(i):
        j = i // 2
        if i % 2 == 0:
            return dict(w_in=a_w_in[j].astype(BF16), w_out=a_w_out[j].astype(BF16))
        return dict(w_in=b_w_in[j].astype(BF16), w_out=b_w_out[j].astype(BF16), grp=b_w_grp[j].astype(BF16))

    shard = [shards_of(i) for i in range(depth)]
    full = [dict() for _ in range(depth)]

    def gather_into(keys, got):
        for (i, key), arr in zip(keys, got):
            full[i][key] = arr

    first = _comm_only(_Gather([shard[0]["w_in"], b_scale]), "gather_first")
    full[0]["w_in"] = first[0]
    scale_full = jnp.transpose(first[1], (1, 0, 2)).reshape(b_scale.shape[0], 1, E)

    def rest_of(i):
        return [(i, k) for k in shard[i] if k != "w_in"]

    saved = []
    xi = x0
    for i in range(depth):
        j = i // 2
        g_pre, g_post = norm_pre[i:i + 1], norm_post[i:i + 1]
        keys_in = [(0, "w_out")] if i == 0 else []
        keys_mix = []
        if i + 1 < depth:
            keys_in.append((i + 1, "w_in"))
            if i + 1 == depth - 1:
                keys_in += rest_of(i + 1)
            elif i % 2 == 0:
                keys_mix = rest_of(i + 1) + ([(i + 2, "w_out")] if i + 2 < depth - 1 else [])
        comm_in = _Gather([shard[a][k] for a, k in keys_in]) if keys_in else None
        comm_mix = _Gather([shard[a][k] for a, k in keys_mix]) if keys_mix else None
        kind = "a" if i % 2 == 0 else "b"
        (proj, h), got = _in_proj_fwd(xi, g_pre, full[i]["w_in"], f"{kind}_in_fwd_{i}", comm_in)
        gather_into(keys_in, got)
        if i % 2 == 0:
            (y, *o), got = _a_mix_fwd(proj, a_ln_g[j:j + 1], a_ln_b[j:j + 1], a_w_s[j], a_b_s[j].reshape(G, P, 1),
                                      f"a_mix_fwd_{i}", comm_mix)
        else:
            (y, o), got = _b_mix_fwd(proj, scale_full[j], full[i]["grp"], f"b_mix_fwd_{i}", comm_mix)
        gather_into(keys_mix, got)
        (x_next, out), _ = _out_proj_fwd(y, full[i]["w_out"], xi, g_post, f"{kind}_out_fwd_{i}")
        saved.append((xi, h, proj, y, out, o))
        xi = x_next

    dx, loss_row = _loss_head(xi, target, "loss_head")
    loss = lax.psum(loss_row[0, 0], ("x", "y", "c"))

    n_a, n_b = a_ln_g.shape[0], b_scale.shape[0]
    d_pre, d_post = [None] * depth, [None] * depth
    recv = {"a_w_in": [None] * n_a, "a_w_out": [None] * n_a, "b_w_in": [None] * n_b,
            "b_w_grp": [None] * n_b, "b_w_out": [None] * n_b, "b_scale": [None] * n_b}
    small_a = [None] * n_a
    kinds = ("chips", "devices", "gather")

    def carried(items):
        of = {kind: [it[3] for it in items if it[0] == kind] for kind in kinds}
        comms = (([_Exchange(of["chips"], chips_only=True)] if of["chips"] else [])
                 + ([_Exchange(of["devices"])] if of["devices"] else [])
                 + ([_Gather(of["gather"])] if of["gather"] else []))
        return None if not comms else comms[0] if len(comms) == 1 else _Together(comms)

    def received(items, got):
        ordered = [it for kind in kinds for it in items if it[0] == kind]
        for it, arr in zip(ordered, got):
            if it[0] == "gather":
                small_a[it[2]] = arr
            else:
                recv[it[1]][it[2]] = arr

    pending = []
    small_pending = []
    for i in reversed(range(depth)):
        j = i // 2
        xi, h, proj, y, out, o = saved[i]
        g_pre, g_post = norm_pre[i:i + 1], norm_post[i:i + 1]
        if i % 2 == 0:
            (dy, dw_out, d_post[i]), _ = _out_proj_bwd(dx, out, g_post, full[i]["w_out"], y, f"a_out_bwd_{i}")
            items, pending = pending, []
            (dproj, d_w_s, dbs, d_ln_g, d_ln_b), got = _a_mix_bwd(
                proj, dy, *o, a_ln_g[j:j + 1], a_ln_b[j:j + 1], a_w_s[j], a_b_s[j].reshape(G, P, 1),
                f"a_mix_bwd_{i}", carried(items))
            received(items, got)
            small_pending.append(("gather", "small", j, _pack([d_w_s, dbs, d_ln_g, d_ln_b])[0]))
            items = []
            if i == 0:
                items, small_pending = [("chips", "a_w_out", j, dw_out)] + small_pending, []
            (dw_in,), got = _dw_in(h, dproj, f"a_dw_in_{i}", carried(items))
            received(items, got)
            items = [("chips", "a_w_in", j, dw_in)]
            (dx, d_pre[i]), got = _in_proj_bwd_dx(dproj, full[i]["w_in"], xi, g_pre, dx, f"a_in_bwd_{i}",
                                                  carried(items))
            received(items, got)
            if i > 0:
                pending.append(("chips", "a_w_out", j, dw_out))
        else:
            items, small_pending = small_pending, []
            (dy, dw_out, d_post[i]), got = _out_proj_bwd(dx, out, g_post, full[i]["w_out"], y, f"b_out_bwd_{i}",
                                                        carried(items))
            received(items, got)
            items, pending = pending, []
            (dproj, dsc, dw_grp), got = _b_mix_bwd(proj, dy, o, scale_full[j], full[i]["grp"], f"b_mix_bwd_{i}",
                                                  carried(items))
            received(items, got)
            (dx, d_pre[i]), _ = _in_proj_bwd_dx(dproj, full[i]["w_in"], xi, g_pre, dx, f"b_in_bwd_{i}")
            (dw_in,), _ = _dw_in(h, dproj, f"b_dw_in_{i}")
            pending += [("chips", "b_w_out", j, dw_out), ("chips", "b_w_grp", j, dw_grp), ("chips", "b_w_in", j, dw_in),
                        ("devices", "b_scale", j, dsc.reshape(N_DEV, 1, E // N_DEV))]
    assert not pending and not small_pending

    gathered = _comm_only(_Gather([_pack([*d_pre, *d_post])[0]]), "gather_norm_grads")
    results = {k: [None] * 4 for k in names}
    norm_like = [norm_pre, norm_post]
    outs = _reduce_adam([gathered[-1]], _pack(norm_like), _pack([m_norm_pre, m_norm_post]),
                        _pack([v_norm_pre, v_norm_post]), "adam_norms")
    for q, packed in enumerate(outs):
        results["norm_pre"][q], results["norm_post"][q] = _unpack(packed, norm_like)
    a_small = ("a_w_s", "a_b_s", "a_ln_g", "a_ln_b")
    per_layer = []
    for j in range(n_a):
        like = [weights[k][j] for k in a_small]
        outs = _reduce_adam([small_a[j]], _pack(like), _pack([mom_m[k][j] for k in a_small]),
                            _pack([mom_v[k][j] for k in a_small]), f"adam_small_{j}")
        per_layer.append([_unpack(packed, like) for packed in outs])
    for q in range(4):
        for n, k in enumerate(a_small):
            results[k][q] = jnp.stack([per_layer[j][q][n] for j in range(n_a)])

    def shard_view(a):
        return a.reshape(a.shape[0], -1, a.shape[-1])

    for k in ("a_w_in", "a_w_out", "b_w_in", "b_w_grp", "b_w_out"):
        w3 = shard_view(weights[k])
        recvs = [r.reshape(r.shape[0], w3.shape[1], w3.shape[2]) for r in recv[k]]
        outs = _reduce_adam(recvs, w3, shard_view(mom_m[k]), shard_view(mom_v[k]), f"adam_{k}")
        results[k] = [o_.reshape(weights[k].shape) for o_ in outs]
    sc_recv = jnp.concatenate(recv["b_scale"], axis=1)
    outs = _reduce_adam([sc_recv], b_scale[None], m_b_scale[None], v_b_scale[None], "adam_b_scale")
    results["b_scale"] = [o_[0] for o_ in outs]

    grad_x = dx[None]
    return (loss, grad_x, *[results[k][0] for k in names], *[results[k][1] for k in names],
            *[results[k][2] for k in names], *[results[k][3] for k in names])
```

```python
import jax
import jax.numpy as jnp
from jax import lax
from jax.experimental import pallas as pl
from jax.experimental.pallas import tpu as pltpu

F32 = jnp.float32
BF16 = jnp.bfloat16
MESH = pl.DeviceIdType.MESH

N_DEV = 8
EPS = 1e-6
CHUNK = 64
GMLP_BLOCK = 128
A_GROUPS = 8
POOL_WINDOWS = (2, 4, 8, 16)
HALO = 16
ADAM_LR = 0.001
ADAM_B1 = 0.9
ADAM_B2 = 0.999
ADAM_EPS = 1e-08
ADAM_WD = 0.01
ADAM_STEP = 10
GELU_C = 0.7978845608028654
GELU_A = 0.044715
ROW_CHUNK = 16
VMEM_LIMIT_BYTES = 56 * 1024 * 1024


def _params(**kw):
    return pltpu.CompilerParams(vmem_limit_bytes=VMEM_LIMIT_BYTES, **kw)


def _gelu(x):
    return 0.5 * x * (1.0 + jnp.tanh(GELU_C * (x + GELU_A * (x * x * x))))


def _gelu_and_grad(x):
    x2 = x * x
    t = jnp.tanh(GELU_C * (x + GELU_A * (x2 * x)))
    val = 0.5 * x * (1.0 + t)
    grad = 0.5 * (1.0 + t) + 0.5 * x * (1.0 - t * t) * (GELU_C * (1.0 + 3.0 * GELU_A * x2))
    return val, grad


def _sigmoid(z):
    return 0.5 * jnp.tanh(0.5 * z) + 0.5


def _dot(a, b):
    return jnp.dot(a, b, preferred_element_type=F32)


def _dot_nt(a, b):
    return lax.dot_general(a, b, (((1,), (1,)), ((), ())), preferred_element_type=F32)


def _dot_tn(a, b):
    return lax.dot_general(a, b, (((0,), (0,)), ((), ())), preferred_element_type=F32)


def _rms_stats(xf):
    r = lax.rsqrt(jnp.mean(xf * xf, axis=-1, keepdims=True) + EPS)
    return r, xf * r


def _rms_bwd(dy, g, r, xh):
    dxh = dy * g
    return r * (dxh - xh * jnp.mean(dxh * xh, axis=-1, keepdims=True))


def _resident(shape):
    return pl.BlockSpec(shape, lambda *_: (0,) * len(shape), pipeline_mode=pl.Buffered(1))


def _spatial_mask(transposed=False):
    p = lax.broadcasted_iota(jnp.int32, (GMLP_BLOCK, GMLP_BLOCK), 0)
    q = lax.broadcasted_iota(jnp.int32, (GMLP_BLOCK, GMLP_BLOCK), 1)
    if transposed:
        p, q = q, p
    return (q // CHUNK) <= (p // CHUNK)


def _position():
    x, y, c = lax.axis_index("x"), lax.axis_index("y"), lax.axis_index("c")
    return x, y, c


def _comm_scratch(n):
    return [pltpu.SemaphoreType.DMA((n, 7)), pltpu.SemaphoreType.DMA((n, 7)), pltpu.SemaphoreType.DMA((n,))]


class _Gather:
    def __init__(self, arrs):
        self.inputs = list(arrs)
        self.out_shape = [jax.ShapeDtypeStruct((N_DEV,) + a.shape, a.dtype) for a in arrs]
        self.scratch = _comm_scratch(len(arrs))

    def _plan(self, ins, outs, sems):
        send_sems, recv_sems, local_sems = sems
        n = len(ins)
        x, y, c = _position()
        sibling = (x, y, 1 - c)
        chips = [(1 - x, y), (x, 1 - y), (1 - x, 1 - y)]

        def index(px, py, pc):
            return 4 * px + 2 * py + pc

        def copy(a, k, block, to, src=None):
            return pltpu.make_async_remote_copy(
                src_ref=outs[a].at[block] if src is None else src, dst_ref=outs[a].at[block],
                send_sem=send_sems.at[a, k], recv_sem=recv_sems.at[a, k], device_id=to, device_id_type=MESH)

        me = index(x, y, c)
        own = [pltpu.make_async_copy(ins[a], outs[a].at[me], local_sems.at[a]) for a in range(n)]
        first = []
        for a in range(n):
            first.append(copy(a, 0, me, sibling, src=ins[a]))
            for j, chip in enumerate(chips):
                first.append(copy(a, 1 + j, me, (*chip, c), src=ins[a]))
        return n, (x, y, c), sibling, chips, index, copy, own, first

    def start(self, ins, outs, sems):
        _, _, _, _, _, _, own, first = self._plan(ins, outs, sems)
        for cp in own + first:
            cp.start()

    def middle(self, ins, outs, sems):
        n, me, sibling, chips, index, copy, _, _ = self._plan(ins, outs, sems)
        for j, chip in enumerate(chips):
            for a in range(n):
                copy(a, 1 + j, index(*chip, me[2]), me).wait_recv()
                copy(a, 4 + j, index(*chip, me[2]), sibling).start()

    def finish(self, ins, outs, sems):
        n, me, sibling, chips, index, copy, own, first = self._plan(ins, outs, sems)
        c = me[2]
        passed = [copy(a, 4 + j, index(*chip, c), sibling) for j, chip in enumerate(chips) for a in range(n)]
        for a in range(n):
            copy(a, 0, index(me[0], me[1], 1 - c), me).wait_recv()
        for j, chip in enumerate(chips):
            for a in range(n):
                copy(a, 4 + j, index(*chip, 1 - c), me).wait_recv()
        for cp in first + passed:
            cp.wait_send()
        for cp in own:
            cp.wait()


class _Exchange:
    def __init__(self, arrs, chips_only=False):
        self.inputs = list(arrs)
        self.chips_only = chips_only
        self.out_shape = [jax.ShapeDtypeStruct(a.shape, a.dtype) for a in arrs]
        self.scratch = _comm_scratch(len(arrs))

    def _plan(self, ins, outs, sems):
        send_sems, recv_sems, local_sems = sems
        n = len(ins)
        x, y, c = _position()
        scale = 1 if self.chips_only else 2
        me = 2 * x + y if self.chips_only else 4 * x + 2 * y + c
        own = [pltpu.make_async_copy(ins[a].at[me], outs[a].at[me], local_sems.at[a]) for a in range(n)]
        sends, recvs = [], []
        for r in range(1, 4 * scale):
            px = 1 - x if r & (2 * scale) else x
            py = 1 - y if r & scale else y
            pc = 1 - c if (r & 1 and not self.chips_only) else c
            peer = 2 * px + py if self.chips_only else 4 * px + 2 * py + pc
            for a in range(n):
                sends.append(pltpu.make_async_remote_copy(
                    src_ref=ins[a].at[peer], dst_ref=outs[a].at[me],
                    send_sem=send_sems.at[a, r - 1], recv_sem=recv_sems.at[a, r - 1],
                    device_id=(px, py, pc), device_id_type=MESH))
                recvs.append(pltpu.make_async_remote_copy(
                    src_ref=ins[a].at[peer], dst_ref=outs[a].at[peer],
                    send_sem=send_sems.at[a, r - 1], recv_sem=recv_sems.at[a, r - 1],
                    device_id=(px, py, pc), device_id_type=MESH))
        return own, sends, recvs

    def start(self, ins, outs, sems):
        own, sends, _ = self._plan(ins, outs, sems)
        for cp in own + sends:
            cp.start()

    def middle(self, ins, outs, sems):
        pass

    def finish(self, ins, outs, sems):
        own, sends, recvs = self._plan(ins, outs, sems)
        for cp in recvs:
            cp.wait_recv()
        for cp in sends:
            cp.wait_send()
        for cp in own:
            cp.wait()


CHIPS = N_DEV // 2


def _sibling_scratch(slab_shape, stage_slots=CHIPS):
    return [pltpu.VMEM((stage_slots,) + tuple(slab_shape), BF16), pltpu.VMEM((CHIPS,) + tuple(slab_shape), BF16),
            pltpu.SemaphoreType.DMA((CHIPS,)), pltpu.SemaphoreType.DMA((CHIPS,))]


def _to_sibling(stage_s, land_s, send_sems, recv_sems, slot, q):
    x, y, c = _position()
    return pltpu.make_async_remote_copy(
        src_ref=stage_s.at[slot], dst_ref=land_s.at[q], send_sem=send_sems.at[q], recv_sem=recv_sems.at[q],
        device_id=(x, y, 1 - c), device_id_type=MESH)


def _sum_with_sibling(slab_of, q_ref, sibling_scratch):
    stage_s, land_s, send_sems, recv_sems = sibling_scratch
    c = lax.axis_index("c")
    for q in range(CHIPS):
        stage_s[q] = slab_of(2 * q + 1 - c).astype(BF16)
        _to_sibling(stage_s, land_s, send_sems, recv_sems, q, q).start()
    for q in range(CHIPS):
        _to_sibling(stage_s, land_s, send_sems, recv_sems, q, q).wait_recv()
        q_ref[q] = (slab_of(2 * q + c) + land_s[q].astype(F32)).astype(BF16)
    for q in range(CHIPS):
        _to_sibling(stage_s, land_s, send_sems, recv_sems, q, q).wait_send()


class _Together:
    def __init__(self, comms):
        self.comms = list(comms)
        self.inputs = [a for c in self.comms for a in c.inputs]
        self.out_shape = [s for c in self.comms for s in c.out_shape]
        self.scratch = [s for c in self.comms for s in c.scratch]

    def _each(self, ins, outs, sems):
        at = 0
        for k, c in enumerate(self.comms):
            n = len(c.inputs)
            yield c, ins[at:at + n], outs[at:at + n], sems[3 * k:3 * k + 3]
            at += n

    def start(self, ins, outs, sems):
        for c, i, o, s in self._each(ins, outs, sems):
            c.start(i, o, s)

    def middle(self, ins, outs, sems):
        for c, i, o, s in self._each(ins, outs, sems):
            c.middle(i, o, s)

    def finish(self, ins, outs, sems):
        for c, i, o, s in self._each(ins, outs, sems):
            c.finish(i, o, s)


def _comm_only(comm, name):
    n = len(comm.inputs)

    def body(*refs):
        ins, outs, sems = refs[:n], refs[n:2 * n], refs[2 * n:]
        comm.start(ins, outs, sems)
        comm.middle(ins, outs, sems)
        comm.finish(ins, outs, sems)

    any_spec = pl.BlockSpec(memory_space=pl.ANY)
    return pl.pallas_call(
        body, name=name, out_shape=comm.out_shape, in_specs=[any_spec] * n, out_specs=[any_spec] * n,
        scratch_shapes=comm.scratch, compiler_params=pltpu.CompilerParams(has_side_effects=True),
    )(*comm.inputs)


def _pcall(body, name, grid, in_specs, out_specs, out_shape, args, scratch_shapes=(), comm=None):
    in_specs, out_specs, out_shape, scratch_shapes = list(in_specs), list(out_specs), list(out_shape), list(scratch_shapes)
    if comm is None:
        outs = pl.pallas_call(body, name=name, grid=grid, in_specs=in_specs, out_specs=out_specs, out_shape=out_shape,
                              scratch_shapes=scratch_shapes, compiler_params=_params())(*args)
        return list(outs), []
    n_in, n_out, n_scr, n_c = len(in_specs), len(out_specs), len(scratch_shapes), len(comm.inputs)

    def carrying(*refs):
        ins, refs = refs[:n_in], refs[n_in:]
        c_ins, refs = refs[:n_c], refs[n_c:]
        outs, refs = refs[:n_out], refs[n_out:]
        c_outs, refs = refs[:n_c], refs[n_c:]
        scr, sems = refs[:n_scr], refs[n_scr:]
        step, steps = 0, 1
        for d, size in enumerate(grid):
            step = step * size + pl.program_id(d)
            steps *= size

        @pl.when(step == 0)
        def _():
            comm.start(c_ins, c_outs, sems)

        body(*ins, *outs, *scr)

        @pl.when(step == max(steps - 2, 0))
        def _():
            comm.middle(c_ins, c_outs, sems)

        @pl.when(step == steps - 1)
        def _():
            comm.finish(c_ins, c_outs, sems)

    any_spec = pl.BlockSpec(memory_space=pl.ANY)
    outs = pl.pallas_call(
        carrying, name=name, grid=grid, in_specs=in_specs + [any_spec] * n_c, out_specs=out_specs + [any_spec] * n_c,
        out_shape=out_shape + comm.out_shape, scratch_shapes=scratch_shapes + comm.scratch,
        compiler_params=_params(has_side_effects=True),
    )(*args, *comm.inputs)
    return list(outs[:n_out]), list(outs[n_out:])


def _in_proj_fwd(x, g_row, w_parts, name, comm=None):
    T, D = x.shape
    NS = w_parts[0].shape[-1]
    DP = w_parts[0].shape[-2]
    n_parts = len(w_parts)
    assert DP * n_parts == D
    TM = min(T, 512)

    def body(x_ref, g_ref, *refs):
        w_refs, (proj_ref, h_ref) = refs[:n_parts], refs[n_parts:]
        _, xh = _rms_stats(x_ref[...])
        h = (xh * g_ref[...]).astype(BF16)
        h_ref[...] = h
        for k in range(N_DEV):
            acc = _dot(h[:, 0:DP], w_refs[0][k])
            for p in range(1, n_parts):
                acc = acc + _dot(h[:, p * DP:(p + 1) * DP], w_refs[p][k])
            proj_ref[:, k * NS:(k + 1) * NS] = acc.astype(BF16)

    return _pcall(
        body, name, (T // TM,),
        in_specs=[pl.BlockSpec((TM, D), lambda i: (i, 0)),
                  pl.BlockSpec((1, D), lambda i: (0, 0))] + [_resident((N_DEV, DP, NS))] * n_parts,
        out_specs=[pl.BlockSpec((TM, N_DEV * NS), lambda i: (i, 0)),
                   pl.BlockSpec((TM, D), lambda i: (i, 0))],
        out_shape=[jax.ShapeDtypeStruct((T, N_DEV * NS), BF16), jax.ShapeDtypeStruct((T, D), BF16)],
        args=(x, g_row, *w_parts), comm=comm)


def _a_mix_fwd(proj, ln_g, ln_b, w_s, b_s, name, comm=None):
    T, E3 = proj.shape
    E = E3 // 3
    G, P = A_GROUPS, GMLP_BLOCK
    GD = E // G
    TB = min(T, 512)

    def body(p_ref, lg_ref, lb_ref, ws_ref, bs_ref, y_ref, xh_ref, dgl_ref, rstd_ref, v_s, us_s):
        def norm_chunk(ci, carry):
            rows = pl.ds(pl.multiple_of(ci * ROW_CHUNK, ROW_CHUNK), ROW_CHUNK)
            vg, dgl = _gelu_and_grad(p_ref[rows, E:2 * E].astype(F32))
            dgl_ref[rows, :] = dgl.astype(BF16)
            xc = vg - jnp.mean(vg, axis=-1, keepdims=True)
            rstd = lax.rsqrt(jnp.mean(xc * xc, axis=-1, keepdims=True) + EPS)
            rstd_ref[rows, :] = rstd
            xh = xc * rstd
            xh_ref[rows, :] = xh.astype(BF16)
            v_s[rows, :] = (xh * lg_ref[...] + lb_ref[...]).astype(BF16)
            return carry

        def gate_chunk(ci, carry):
            rows = pl.ds(pl.multiple_of(ci * ROW_CHUNK, ROW_CHUNK), ROW_CHUNK)
            z = p_ref[rows, 2 * E:3 * E].astype(F32)
            us_s[rows, :] = _gelu(p_ref[rows, 0:E].astype(F32)) * (z * _sigmoid(z))
            return carry

        lax.fori_loop(0, TB // ROW_CHUNK, norm_chunk, 0, unroll=2)
        lax.fori_loop(0, TB // ROW_CHUNK, gate_chunk, 0, unroll=2)
        mask = _spatial_mask()
        for g in range(G):
            wm = jnp.where(mask, ws_ref[g], 0.0).astype(BF16)
            cols = slice(g * GD, (g + 1) * GD)
            for b in range(TB // P):
                rows = slice(b * P, (b + 1) * P)
                mixed = _dot(wm, v_s[rows, cols]) + bs_ref[g]
                y_ref[rows, cols] = (us_s[rows, cols] * mixed).astype(BF16)

    return _pcall(
        body, name, (T // TB,),
        in_specs=[pl.BlockSpec((TB, E3), lambda i: (i, 0)),
                  pl.BlockSpec((1, E), lambda i: (0, 0)),
                  pl.BlockSpec((1, E), lambda i: (0, 0)),
                  pl.BlockSpec((G, P, P), lambda i: (0, 0, 0)),
                  pl.BlockSpec((G, P, 1), lambda i: (0, 0, 0))],
        out_specs=[pl.BlockSpec((TB, E), lambda i: (i, 0)), pl.BlockSpec((TB, E), lambda i: (i, 0)),
                   pl.BlockSpec((TB, E), lambda i: (i, 0)), pl.BlockSpec((TB, 1), lambda i: (i, 0))],
        out_shape=[jax.ShapeDtypeStruct((T, E), BF16), jax.ShapeDtypeStruct((T, E), BF16),
                   jax.ShapeDtypeStruct((T, E), BF16), jax.ShapeDtypeStruct((T, 1), F32)],
        scratch_shapes=[pltpu.VMEM((TB, E), BF16), pltpu.VMEM((TB, E), F32)],
        args=(proj, ln_g, ln_b, w_s, b_s), comm=comm)


def _window_sum_back(ext, win):
    s, k = ext, 1
    while k < win:
        s = s + pltpu.roll(s, k, axis=0)
        k *= 2
    return s


def _window_sum_ahead(ext, win):
    n = ext.shape[0]
    s, k = ext, 1
    while k < win:
        s = s + pltpu.roll(s, n - k, axis=0)
        k *= 2
    return s


def _inv_count(t0, rows, win):
    t1 = t0 + 1 + lax.broadcasted_iota(jnp.int32, (rows, 1), 0)
    return 1.0 / jnp.minimum(t1, win).astype(F32)


def _b_mix_fwd(proj, scale, wg_all, name, comm=None):
    T, E2 = proj.shape
    E = E2 // 2
    NG = len(POOL_WINDOWS)
    GB = E // NG
    TB = min(T, 256)
    RS = wg_all.shape[-2]

    def body(p_ref, sc_ref, wg_ref, y_ref, o_ref, carry_s):
        i = pl.program_id(0)

        @pl.when(i == 0)
        def _():
            carry_s[...] = jnp.zeros_like(carry_s)

        for g, win in enumerate(POOL_WINDOWS):
            cols = slice(g * GB, (g + 1) * GB)
            xg = p_ref[:, cols].astype(F32)
            ext = jnp.concatenate([carry_s[:, cols], xg], axis=0)
            pooled = _window_sum_back(ext, win)[HALO:, :] * _inv_count(i * TB, TB, win) - xg
            carry_s[:, cols] = xg[TB - HALO:, :]
            o = _dot(pooled.astype(BF16), wg_ref[:, g].reshape(GB, GB))
            o_ref[:, cols] = o.astype(BF16)
            z = p_ref[:, E + g * GB:E + (g + 1) * GB].astype(F32)
            y_ref[:, cols] = ((o * sc_ref[:, cols]) * (z * _sigmoid(z))).astype(BF16)

    return _pcall(
        body, name, (T // TB,),
        in_specs=[pl.BlockSpec((TB, E2), lambda i: (i, 0)),
                  pl.BlockSpec((1, E), lambda i: (0, 0)),
                  pl.BlockSpec((N_DEV, NG, RS, GB), lambda i: (0, 0, 0, 0))],
        out_specs=[pl.BlockSpec((TB, E), lambda i: (i, 0)), pl.BlockSpec((TB, E), lambda i: (i, 0))],
        out_shape=[jax.ShapeDtypeStruct((T, E), BF16), jax.ShapeDtypeStruct((T, E), BF16)],
        scratch_shapes=[pltpu.VMEM((HALO, E), F32)],
        args=(proj, scale, wg_all), comm=comm)


def _out_proj_fwd(y, w_all, x, g_row, name, comm=None):
    T, E = y.shape
    D = x.shape[1]
    ES = w_all.shape[-2]
    TM = min(T, 512)

    def body(y_ref, w_ref, x_ref, g_ref, xn_ref, out_ref):
        o = _dot(y_ref[...], w_ref[...].reshape(E, D))
        out_ref[...] = o
        _, oh = _rms_stats(o)
        xn_ref[...] = x_ref[...] + oh * g_ref[...]

    return _pcall(
        body, name, (T // TM,),
        in_specs=[pl.BlockSpec((TM, E), lambda i: (i, 0)),
                  pl.BlockSpec((N_DEV, ES, D), lambda i: (0, 0, 0)),
                  pl.BlockSpec((TM, D), lambda i: (i, 0)),
                  pl.BlockSpec((1, D), lambda i: (0, 0))],
        out_specs=[pl.BlockSpec((TM, D), lambda i: (i, 0)), pl.BlockSpec((TM, D), lambda i: (i, 0))],
        out_shape=[jax.ShapeDtypeStruct((T, D), F32), jax.ShapeDtypeStruct((T, D), F32)],
        args=(y, w_all, x, g_row), comm=comm)


def _loss_head(x, target, name):
    T, D = x.shape
    TM = min(T, 512)
    nT = T // TM

    def body(x_ref, t_ref, dx_ref, loss_ref, acc_s):
        i = pl.program_id(0)

        @pl.when(i == 0)
        def _():
            acc_s[...] = jnp.zeros_like(acc_s)

        e = x_ref[...] - t_ref[...]
        dx_ref[...] = e * (1.0 / D)
        acc_s[...] += jnp.sum(e * e, axis=0, keepdims=True)

        @pl.when(i == nT - 1)
        def _():
            total = jnp.sum(acc_s[...], axis=1, keepdims=True) * (0.5 / D)
            loss_ref[...] = jnp.broadcast_to(total, loss_ref.shape)

    return _pcall(
        body, name, (nT,),
        in_specs=[pl.BlockSpec((TM, D), lambda i: (i, 0)), pl.BlockSpec((TM, D), lambda i: (i, 0))],
        out_specs=[pl.BlockSpec((TM, D), lambda i: (i, 0)), pl.BlockSpec((1, 128), lambda i: (0, 0))],
        out_shape=[jax.ShapeDtypeStruct((T, D), F32), jax.ShapeDtypeStruct((1, 128), F32)],
        scratch_shapes=[pltpu.VMEM((1, D), F32)],
        args=(x, target))[0]


def _out_proj_bwd(dxn, out, g_row, w_all, y, name, comm=None):
    T, D = dxn.shape
    E = y.shape[1]
    ES = w_all.shape[-2]
    TM = min(T, 512)
    nT = T // TM

    def body(dxn_ref, out_ref, g_ref, w_ref, y_ref, dy_ref, dw_ref, dg_ref, acc_s, *sibling_scratch):
        i = pl.program_id(0)

        @pl.when(i == 0)
        def _():
            acc_s[...] = jnp.zeros_like(acc_s)
            dg_ref[...] = jnp.zeros_like(dg_ref)

        dxn_v = dxn_ref[...]
        r, oh = _rms_stats(out_ref[...])
        dg_ref[...] += jnp.sum(dxn_v * oh, axis=0, keepdims=True)
        dout = _rms_bwd(dxn_v, g_ref[...], r, oh).astype(BF16)
        dy_ref[...] = _dot_nt(dout, w_ref[...].reshape(E, D)).astype(BF16)
        acc_s[...] += _dot_tn(y_ref[...], dout)

        @pl.when(i == nT - 1)
        def _():
            def slab_of(k):
                return acc_s[pl.ds(pl.multiple_of(k * ES, ES), ES), :]

            _sum_with_sibling(slab_of, dw_ref, sibling_scratch)

    return _pcall(
        body, name, (nT,),
        in_specs=[pl.BlockSpec((TM, D), lambda i: (i, 0)),
                  pl.BlockSpec((TM, D), lambda i: (i, 0)),
                  pl.BlockSpec((1, D), lambda i: (0, 0)),
                  pl.BlockSpec((N_DEV, ES, D), lambda i: (0, 0, 0)),
                  pl.BlockSpec((TM, E), lambda i: (i, 0))],
        out_specs=[pl.BlockSpec((TM, E), lambda i: (i, 0)),
                   pl.BlockSpec((CHIPS, ES, D), lambda i: (0, 0, 0)),
                   pl.BlockSpec((1, D), lambda i: (0, 0))],
        out_shape=[jax.ShapeDtypeStruct((T, E), BF16), jax.ShapeDtypeStruct((CHIPS, ES, D), BF16),
                   jax.ShapeDtypeStruct((1, D), F32)],
        scratch_shapes=[pltpu.VMEM((E, D), F32)] + _sibling_scratch((ES, D)),
        args=(dxn, out, g_row, w_all, y), comm=comm)


def _a_mix_bwd(proj, dy, xh, dgl, rstd, ln_g, ln_b, w_s, b_s, name, comm=None):
    T, E3 = proj.shape
    E = E3 // 3
    G, P = A_GROUPS, GMLP_BLOCK
    GD = E // G
    TB = min(T, 256)

    def body(up_ref, zp_ref, dy_ref, xh_ref, dgl_ref, rstd_ref, lg_ref, lb_ref, ws_ref, bs_ref,
             dp_ref, dws_ref, dbs_ref, dlg_ref, dlb_ref, v_s, a_s, bz_s, c_s, dv_s):
        @pl.when(pl.program_id(0) == 0)
        def _():
            dws_ref[...] = jnp.zeros_like(dws_ref)
            dbs_ref[...] = jnp.zeros_like(dbs_ref)
            dlg_ref[...] = jnp.zeros_like(dlg_ref)
            dlb_ref[...] = jnp.zeros_like(dlb_ref)

        def recompute(ci, carry):
            rows = pl.ds(pl.multiple_of(ci * ROW_CHUNK, ROW_CHUNK), ROW_CHUNK)
            v_s[rows, :] = (xh_ref[rows, :].astype(F32) * lg_ref[...] + lb_ref[...]).astype(BF16)
            u, du = _gelu_and_grad(up_ref[rows, :].astype(F32))
            z = zp_ref[rows, :].astype(F32)
            sg = _sigmoid(z)
            s = z * sg
            ds = sg * (1.0 + z * (1.0 - sg))
            dyv = dy_ref[rows, :].astype(F32)
            a_s[rows, :] = dyv * s * du
            bz_s[rows, :] = dyv * u * ds
            c_s[rows, :] = (dyv * u * s).astype(BF16)
            return carry

        lax.fori_loop(0, TB // ROW_CHUNK, recompute, 0, unroll=2)

        mask = _spatial_mask()
        mask_t = _spatial_mask(transposed=True)
        for g in range(G):
            w_g = ws_ref[g]
            wm = jnp.where(mask, w_g, 0.0).astype(BF16)
            wm_t = jnp.where(mask_t, w_g.T, 0.0).astype(BF16)
            cols = slice(g * GD, (g + 1) * GD)
            dws_g = jnp.zeros((P, P), F32)
            dbs_g = jnp.zeros((P, 1), F32)
            for b in range(TB // P):
                rows = slice(b * P, (b + 1) * P)
                vb = v_s[rows, cols]
                cb = c_s[rows, cols]
                mixed = _dot(wm, vb) + bs_ref[g]
                dp_ref[rows, g * GD:(g + 1) * GD] = (a_s[rows, cols] * mixed).astype(BF16)
                dp_ref[rows, 2 * E + g * GD:2 * E + (g + 1) * GD] = (bz_s[rows, cols] * mixed).astype(BF16)
                dv_s[rows, cols] = _dot(wm_t, cb)
                dws_g = dws_g + _dot_nt(cb, vb)
                dbs_g = dbs_g + jnp.sum(cb.astype(F32), axis=1, keepdims=True)
            dws_ref[g] += jnp.where(mask, dws_g, 0.0)
            dbs_ref[g] += dbs_g

        def ln_bwd(ci, carry):
            rows = pl.ds(pl.multiple_of(ci * ROW_CHUNK, ROW_CHUNK), ROW_CHUNK)
            dv = dv_s[rows, :]
            xh = xh_ref[rows, :].astype(F32)
            dlg_ref[...] += jnp.sum(dv * xh, axis=0, keepdims=True)
            dlb_ref[...] += jnp.sum(dv, axis=0, keepdims=True)
            dxh = dv * lg_ref[...]
            dvg = rstd_ref[rows, :] * (dxh - jnp.mean(dxh, axis=-1, keepdims=True)
                                       - xh * jnp.mean(dxh * xh, axis=-1, keepdims=True))
            dp_ref[rows, E:2 * E] = (dvg * dgl_ref[rows, :].astype(F32)).astype(BF16)
            return carry

        lax.fori_loop(0, TB // ROW_CHUNK, ln_bwd, 0, unroll=2)

    return _pcall(
        body, name, (T // TB,),
        in_specs=[pl.BlockSpec((TB, E), lambda i: (i, 0)),
                  pl.BlockSpec((TB, E), lambda i: (i, 2)),
                  pl.BlockSpec((TB, E), lambda i: (i, 0)),
                  pl.BlockSpec((TB, E), lambda i: (i, 0)),
                  pl.BlockSpec((TB, E), lambda i: (i, 0)),
                  pl.BlockSpec((TB, 1), lambda i: (i, 0)),
                  pl.BlockSpec((1, E), lambda i: (0, 0)),
                  pl.BlockSpec((1, E), lambda i: (0, 0)),
                  pl.BlockSpec((G, P, P), lambda i: (0, 0, 0)),
                  pl.BlockSpec((G, P, 1), lambda i: (0, 0, 0))],
        out_specs=[pl.BlockSpec((TB, E3), lambda i: (i, 0)),
                   pl.BlockSpec((G, P, P), lambda i: (0, 0, 0)),
                   pl.BlockSpec((G, P, 1), lambda i: (0, 0, 0)),
                   pl.BlockSpec((1, E), lambda i: (0, 0)),
                   pl.BlockSpec((1, E), lambda i: (0, 0))],
        out_shape=[jax.ShapeDtypeStruct((T, E3), BF16), jax.ShapeDtypeStruct((G, P, P), F32),
                   jax.ShapeDtypeStruct((G, P, 1), F32), jax.ShapeDtypeStruct((1, E), F32),
                   jax.ShapeDtypeStruct((1, E), F32)],
        scratch_shapes=[pltpu.VMEM((TB, E), BF16), pltpu.VMEM((TB, E), F32), pltpu.VMEM((TB, E), F32),
                        pltpu.VMEM((TB, E), BF16), pltpu.VMEM((TB, E), F32)],
        args=(proj, proj, dy, xh, dgl, rstd, ln_g, ln_b, w_s, b_s), comm=comm)


def _b_mix_bwd(proj, dy, o, scale, wg_all, name, comm=None):
    T, E2 = proj.shape
    E = E2 // 2
    NG = len(POOL_WINDOWS)
    GB = E // NG
    TB = min(T, 256)
    nT = T // TB
    RS = wg_all.shape[-2]
    halo_per_tile = TB // HALO

    def body(p_ref, halo_ref, dy_ref, o_ref, sc_ref, wg_ref, dp_ref, dsc_ref, dwg_ref, acc_s, carry_s,
             *sibling_scratch):
        i = pl.program_id(0)
        tile = nT - 1 - i

        @pl.when(i == 0)
        def _():
            acc_s[...] = jnp.zeros_like(acc_s)
            carry_s[...] = jnp.zeros_like(carry_s)
            dsc_ref[...] = jnp.zeros_like(dsc_ref)

        has_history = (tile > 0).astype(F32)
        for g, win in enumerate(POOL_WINDOWS):
            cols = slice(g * GB, (g + 1) * GB)
            inv = _inv_count(tile * TB, TB, win)
            xg = p_ref[:, cols].astype(F32)
            ext = jnp.concatenate([halo_ref[:, cols].astype(F32) * has_history, xg], axis=0)
            pooled = _window_sum_back(ext, win)[HALO:, :] * inv - xg
            z = p_ref[:, E + g * GB:E + (g + 1) * GB].astype(F32)
            sg = _sigmoid(z)
            dyv = dy_ref[:, cols].astype(F32)
            ov = o_ref[:, cols].astype(F32)
            sc = sc_ref[:, cols]
            dmixed = dyv * (z * sg)
            dsc_ref[:, cols] += jnp.sum(dmixed * ov, axis=0, keepdims=True)
            dz = dyv * (ov * sc) * (sg * (1.0 + z * (1.0 - sg)))
            do = (dmixed * sc).astype(BF16)
            acc_s[:, g] += _dot_tn(pooled.astype(BF16), do).reshape(N_DEV, RS, GB)
            dpool = _dot_nt(do, wg_ref[:, g].reshape(GB, GB))
            q = dpool * inv
            ext_q = jnp.concatenate([q, carry_s[:, cols]], axis=0)
            dxb = _window_sum_ahead(ext_q, win)[:TB, :] - dpool
            carry_s[:, cols] = q[:HALO, :]
            dp_ref[:, cols] = dxb.astype(BF16)
            dp_ref[:, E + g * GB:E + (g + 1) * GB] = dz.astype(BF16)

        @pl.when(i == nT - 1)
        def _():
            _sum_with_sibling(lambda k: acc_s[k], dwg_ref, sibling_scratch)

    return _pcall(
        body, name, (nT,),
        in_specs=[pl.BlockSpec((TB, E2), lambda i: (nT - 1 - i, 0)),
                  pl.BlockSpec((HALO, E), lambda i: (jnp.maximum((nT - 1 - i) * halo_per_tile - 1, 0), 0)),
                  pl.BlockSpec((TB, E), lambda i: (nT - 1 - i, 0)),
                  pl.BlockSpec((TB, E), lambda i: (nT - 1 - i, 0)),
                  pl.BlockSpec((1, E), lambda i: (0, 0)),
                  pl.BlockSpec((N_DEV, NG, RS, GB), lambda i: (0, 0, 0, 0))],
        out_specs=[pl.BlockSpec((TB, E2), lambda i: (nT - 1 - i, 0)),
                   pl.BlockSpec((1, E), lambda i: (0, 0)),
                   pl.BlockSpec((CHIPS, NG, RS, GB), lambda i: (0, 0, 0, 0))],
        out_shape=[jax.ShapeDtypeStruct((T, E2), BF16), jax.ShapeDtypeStruct((1, E), F32),
                   jax.ShapeDtypeStruct((CHIPS, NG, RS, GB), BF16)],
        scratch_shapes=[pltpu.VMEM((N_DEV, NG, RS, GB), F32), pltpu.VMEM((HALO, E), F32)]
        + _sibling_scratch((NG, RS, GB)),
        args=(proj, proj, dy, o, scale, wg_all), comm=comm)


def _in_proj_bwd_dx(dproj, w_parts, x, g_row, dxn, name, comm=None):
    T, D = x.shape
    NS = w_parts[0].shape[-1]
    DP = w_parts[0].shape[-2]
    n_parts = len(w_parts)
    TM = min(T, 512)

    def body(dp_ref, *refs):
        w_refs, (x_ref, g_ref, dxn_ref, dx_ref, dg_ref) = refs[:n_parts], refs[n_parts:]

        @pl.when(pl.program_id(0) == 0)
        def _():
            dg_ref[...] = jnp.zeros_like(dg_ref)

        pieces = []
        for w_ref in w_refs:
            piece = _dot_nt(dp_ref[:, 0:NS], w_ref[0])
            for k in range(1, N_DEV):
                piece = piece + _dot_nt(dp_ref[:, k * NS:(k + 1) * NS], w_ref[k])
            pieces.append(piece)
        dh = pieces[0] if n_parts == 1 else jnp.concatenate(pieces, axis=1)
        r, xh = _rms_stats(x_ref[...])
        dg_ref[...] += jnp.sum(dh * xh, axis=0, keepdims=True)
        dx_ref[...] = dxn_ref[...] + _rms_bwd(dh, g_ref[...], r, xh)

    return _pcall(
        body, name, (T // TM,),
        in_specs=[pl.BlockSpec((TM, N_DEV * NS), lambda i: (i, 0))] + [_resident((N_DEV, DP, NS))] * n_parts
        + [pl.BlockSpec((TM, D), lambda i: (i, 0)),
           pl.BlockSpec((1, D), lambda i: (0, 0)),
           pl.BlockSpec((TM, D), lambda i: (i, 0))],
        out_specs=[pl.BlockSpec((TM, D), lambda i: (i, 0)), pl.BlockSpec((1, D), lambda i: (0, 0))],
        out_shape=[jax.ShapeDtypeStruct((T, D), F32), jax.ShapeDtypeStruct((1, D), F32)],
        args=(dproj, *w_parts, x, g_row, dxn), comm=comm)


STAGE_SLOTS = 2


def _dw_in(h, dproj, name, comm=None):
    T, D = h.shape
    NS = dproj.shape[1] // N_DEV
    TK = min(T, 2048)
    nK = T // TK

    def body(h_ref, dp_ref, q_ref, acc_s, stage_s, land_s, send_sems, recv_sems):
        k, t = pl.program_id(0), pl.program_id(1)
        c = lax.axis_index("c")

        def to_sibling(q):
            return _to_sibling(stage_s, land_s, send_sems, recv_sems, q % STAGE_SLOTS, q)

        @pl.when(t == 0)
        def _():
            acc_s[...] = jnp.zeros_like(acc_s)

        acc_s[...] += _dot_tn(h_ref[...], dp_ref[...])

        @pl.when(t == nK - 1)
        def _():
            q = k // 2

            @pl.when(k % 2 == c)
            def _():
                q_ref[q] = acc_s[...].astype(BF16)

            @pl.when(k % 2 != c)
            def _():
                @pl.when(q >= STAGE_SLOTS)
                def _():
                    to_sibling(q - STAGE_SLOTS).wait_send()

                stage_s[q % STAGE_SLOTS] = acc_s[...].astype(BF16)
                to_sibling(q).start()

        @pl.when((k == N_DEV - 1) & (t == nK - 1))
        def _():
            for q in range(CHIPS - STAGE_SLOTS, CHIPS):
                to_sibling(q).wait_send()
            for q in range(CHIPS):
                to_sibling(q).wait_recv()
                q_ref[q] = (q_ref[q].astype(F32) + land_s[q].astype(F32)).astype(BF16)

    return _pcall(
        body, name, (N_DEV, nK),
        in_specs=[pl.BlockSpec((TK, D), lambda k, t: (t, 0)), pl.BlockSpec((TK, NS), lambda k, t: (t, k))],
        out_specs=[pl.BlockSpec((CHIPS, D, NS), lambda k, t: (0, 0, 0))],
        out_shape=[jax.ShapeDtypeStruct((CHIPS, D, NS), BF16)],
        scratch_shapes=[pltpu.VMEM((D, NS), F32)] + _sibling_scratch((D, NS), STAGE_SLOTS),
        args=(h, dproj), comm=comm)


def _reduce_adam(recvs, w, m, v, name):
    L, R, C = w.shape
    assert len(recvs) == L
    senders = recvs[0].shape[0]
    TR = R
    for cand in (256, 128, 64, 32, 16):
        if R % cand == 0 and R > cand:
            TR = cand
            break
    nR = R // TR
    c1 = 1.0 - ADAM_B1 ** ADAM_STEP
    c2 = 1.0 - ADAM_B2 ** ADAM_STEP

    def body(*refs):
        recv_refs = refs[:L]
        w_ref, m_ref, v_ref, g_ref, d_ref, nm_ref, nv_ref, g_s = refs[L:]
        layer = pl.program_id(0)
        for l in range(L):
            @pl.when(layer == l)
            def _(l=l):
                acc = recv_refs[l][0].astype(F32)
                for j in range(1, senders):
                    acc = acc + recv_refs[l][j].astype(F32)
                g_s[...] = acc

        g = g_s[...]
        g_ref[...] = g
        nm = ADAM_B1 * m_ref[...] + (1.0 - ADAM_B1) * g
        nv = ADAM_B2 * v_ref[...] + (1.0 - ADAM_B2) * (g * g)
        nm_ref[...] = nm
        nv_ref[...] = nv
        d_ref[...] = -ADAM_LR * ((nm / c1) / (jnp.sqrt(nv / c2) + ADAM_EPS) + ADAM_WD * w_ref[...])

    def recv_spec(l):
        def index(layer, t):
            before = jnp.where(layer < l, 0, nR - 1)
            return (0, jnp.where(layer == l, t, before), 0)
        return pl.BlockSpec((senders, TR, C), index)

    wspec = pl.BlockSpec((None, TR, C), lambda layer, t: (layer, t, 0))
    out = jax.ShapeDtypeStruct((L, R, C), F32)
    return _pcall(
        body, name, (L, nR),
        in_specs=[recv_spec(l) for l in range(L)] + [wspec] * 3,
        out_specs=[wspec] * 4, out_shape=[out] * 4,
        scratch_shapes=[pltpu.VMEM((TR, C), F32)],
        args=(*recvs, w, m, v))[0]


PACK_LANES = 128
PACK_ROWS_MULTIPLE = 256


def _pack(arrays):
    flat = jnp.concatenate([a.reshape(-1) for a in arrays])
    tile = PACK_LANES * PACK_ROWS_MULTIPLE
    padded = -(-flat.shape[0] // tile) * tile
    return jnp.pad(flat, (0, padded - flat.shape[0])).reshape(1, padded // PACK_LANES, PACK_LANES)


def _unpack(packed, like):
    flat = packed.reshape(-1)
    out, at = [], 0
    for a in like:
        out.append(flat[at:at + a.size].reshape(a.shape))
        at += a.size
    return out


def kernel(x, norm_pre, norm_post, a_w_in, a_ln_g, a_ln_b, a_w_s, a_b_s, a_w_out, b_w_in, b_w_grp, b_scale, b_w_out, loss_target, m_norm_pre, m_norm_post, m_a_w_in, m_a_ln_g, m_a_ln_b, m_a_w_s, m_a_b_s, m_a_w_out, m_b_w_in, m_b_w_grp, m_b_scale, m_b_w_out, v_norm_pre, v_norm_post, v_a_w_in, v_a_ln_g, v_a_ln_b, v_a_w_s, v_a_b_s, v_a_w_out, v_b_w_in, v_b_w_grp, v_b_scale, v_b_w_out):
    weights = dict(norm_pre=norm_pre, norm_post=norm_post, a_w_in=a_w_in, a_ln_g=a_ln_g, a_ln_b=a_ln_b, a_w_s=a_w_s,
                   a_b_s=a_b_s, a_w_out=a_w_out, b_w_in=b_w_in, b_w_grp=b_w_grp, b_scale=b_scale, b_w_out=b_w_out)
    mom_m = dict(norm_pre=m_norm_pre, norm_post=m_norm_post, a_w_in=m_a_w_in, a_ln_g=m_a_ln_g, a_ln_b=m_a_ln_b,
                 a_w_s=m_a_w_s, a_b_s=m_a_b_s, a_w_out=m_a_w_out, b_w_in=m_b_w_in, b_w_grp=m_b_w_grp,
                 b_scale=m_b_scale, b_w_out=m_b_w_out)
    mom_v = dict(norm_pre=v_norm_pre, norm_post=v_norm_post, a_w_in=v_a_w_in, a_ln_g=v_a_ln_g, a_ln_b=v_a_ln_b,
                 a_w_s=v_a_w_s, a_b_s=v_a_b_s, a_w_out=v_a_w_out, b_w_in=v_b_w_in, b_w_grp=v_b_w_grp,
                 b_scale=v_b_scale, b_w_out=v_b_w_out)
    names = list(weights)

    depth = norm_pre.shape[0]
    x0 = x[0]
    target = loss_target[0]
    T, D = x0.shape
    E = a_ln_g.shape[1]
    G, P = A_GROUPS, GMLP_BLOCK

    def shards_of(i):
        j = i // 2
        if i % 2 == 0:
            w = a_w_in[j].astype(BF16)
            half = w.shape[0] // 2
            return dict(w_in_lo=w[:half], w_in_hi=w[half:], w_out=a_w_out[j].astype(BF16))
        return dict(w_in=b_w_in[j].astype(BF16), w_out=b_w_out[j].astype(BF16), grp=b_w_grp[j].astype(BF16))

    shard = [shards_of(i) for i in range(depth)]
    full = [dict() for _ in range(depth)]

    def gather_into(keys, got):
        for (i, key), arr in zip(keys, got):
            full[i][key] = arr

    def w_in_of(i):
        return [(i, k) for k in shard[i] if k.startswith("w_in")]

    def rest_of(i):
        return [(i, k) for k in shard[i] if not k.startswith("w_in")]

    def gather_of(keys):
        return _Gather([shard[a][k] for a, k in keys]) if keys else None

    first = _comm_only(_Gather([shard[0][k] for _, k in w_in_of(0)] + [b_scale]), "gather_first")
    gather_into(w_in_of(0), first)
    scale_full = jnp.transpose(first[-1], (1, 0, 2)).reshape(b_scale.shape[0], 1, E)

    saved = []
    xi = x0
    for i in range(depth):
        j = i // 2
        g_pre, g_post = norm_pre[i:i + 1], norm_post[i:i + 1]
        keys_in, keys_mix, keys_out = [], [], []
        if i % 2 == 0:
            keys_mix = [(0, "w_out")] if i == 0 else []
            if i + 1 < depth:
                keys_in = w_in_of(i + 1)
                keys_mix = keys_mix + rest_of(i + 1)
            if i + 2 < depth:
                keys_out = [(i + 2, "w_in_lo")]
        elif i + 1 < depth:
            keys_in = [(i + 1, "w_in_hi")]
            keys_mix = rest_of(i + 1)
        kind = "a" if i % 2 == 0 else "b"
        (proj, h), got = _in_proj_fwd(xi, g_pre, [full[i][k] for _, k in w_in_of(i)], f"{kind}_in_fwd_{i}",
                                      gather_of(keys_in))
        gather_into(keys_in, got)
        if i % 2 == 0:
            (y, *o), got = _a_mix_fwd(proj, a_ln_g[j:j + 1], a_ln_b[j:j + 1], a_w_s[j], a_b_s[j].reshape(G, P, 1),
                                      f"a_mix_fwd_{i}", gather_of(keys_mix))
        else:
            (y, o), got = _b_mix_fwd(proj, scale_full[j], full[i]["grp"], f"b_mix_fwd_{i}", gather_of(keys_mix))
        gather_into(keys_mix, got)
        (x_next, out), got = _out_proj_fwd(y, full[i]["w_out"], xi, g_post, f"{kind}_out_fwd_{i}",
                                           gather_of(keys_out))
        gather_into(keys_out, got)
        saved.append((xi, h, proj, y, out, o))
        xi = x_next

    dx, loss_row = _loss_head(xi, target, "loss_head")
    loss = lax.psum(loss_row[0, 0], ("x", "y", "c"))

    n_a, n_b = a_ln_g.shape[0], b_scale.shape[0]
    d_pre, d_post = [None] * depth, [None] * depth
    recv = {"a_w_in": [None] * n_a, "a_w_out": [None] * n_a, "b_w_in": [None] * n_b,
            "b_w_grp": [None] * n_b, "b_w_out": [None] * n_b, "b_scale": [None] * n_b}
    small_a = [None] * n_a
    kinds = ("chips", "devices", "gather")

    def carried(items):
        of = {kind: [it[3] for it in items if it[0] == kind] for kind in kinds}
        comms = (([_Exchange(of["chips"], chips_only=True)] if of["chips"] else [])
                 + ([_Exchange(of["devices"])] if of["devices"] else [])
                 + ([_Gather(of["gather"])] if of["gather"] else []))
        return None if not comms else comms[0] if len(comms) == 1 else _Together(comms)

    def received(items, got):
        ordered = [it for kind in kinds for it in items if it[0] == kind]
        for it, arr in zip(ordered, got):
            if it[0] == "gather":
                small_a[it[2]] = arr
            else:
                recv[it[1]][it[2]] = arr

    pending = []
    small_pending = []
    for i in reversed(range(depth)):
        j = i // 2
        xi, h, proj, y, out, o = saved[i]
        g_pre, g_post = norm_pre[i:i + 1], norm_post[i:i + 1]
        if i % 2 == 0:
            (dy, dw_out, d_post[i]), _ = _out_proj_bwd(dx, out, g_post, full[i]["w_out"], y, f"a_out_bwd_{i}")
            items, pending = pending, []
            (dproj, d_w_s, dbs, d_ln_g, d_ln_b), got = _a_mix_bwd(
                proj, dy, *o, a_ln_g[j:j + 1], a_ln_b[j:j + 1], a_w_s[j], a_b_s[j].reshape(G, P, 1),
                f"a_mix_bwd_{i}", carried(items))
            received(items, got)
            small_pending.append(("gather", "small", j, _pack([d_w_s, dbs, d_ln_g, d_ln_b])[0]))
            items = []
            if i == 0:
                items, small_pending = [("chips", "a_w_out", j, dw_out)] + small_pending, []
            (dw_in,), got = _dw_in(h, dproj, f"a_dw_in_{i}", carried(items))
            received(items, got)
            items = [("chips", "a_w_in", j, dw_in)] if i == 0 else [("chips", "a_w_out", j, dw_out)]
            (dx, d_pre[i]), got = _in_proj_bwd_dx(dproj, [full[i][k] for _, k in w_in_of(i)], xi, g_pre, dx,
                                                  f"a_in_bwd_{i}", carried(items))
            received(items, got)
            if i > 0:
                pending.append(("chips", "a_w_in", j, dw_in))
        else:
            items, small_pending = small_pending, []
            (dy, dw_out, d_post[i]), got = _out_proj_bwd(dx, out, g_post, full[i]["w_out"], y, f"b_out_bwd_{i}",
                                                        carried(items))
            received(items, got)
            items, pending = pending, []
            (dproj, dsc, dw_grp), got = _b_mix_bwd(proj, dy, o, scale_full[j], full[i]["grp"], f"b_mix_bwd_{i}",
                                                  carried(items))
            received(items, got)
            (dx, d_pre[i]), _ = _in_proj_bwd_dx(dproj, [full[i][k] for _, k in w_in_of(i)], xi, g_pre, dx,
                                                f"b_in_bwd_{i}")
            (dw_in,), _ = _dw_in(h, dproj, f"b_dw_in_{i}")
            pending += [("chips", "b_w_out", j, dw_out), ("chips", "b_w_grp", j, dw_grp), ("chips", "b_w_in", j, dw_in),
                        ("devices", "b_scale", j, dsc.reshape(N_DEV, 1, E // N_DEV))]
    assert not pending and not small_pending

    gathered = _comm_only(_Gather([_pack([*d_pre, *d_post])[0]]), "gather_norm_grads")
    results = {k: [None] * 4 for k in names}
    norm_like = [norm_pre, norm_post]
    outs = _reduce_adam([gathered[-1]], _pack(norm_like), _pack([m_norm_pre, m_norm_post]),
                        _pack([v_norm_pre, v_norm_post]), "adam_norms")
    for q, packed in enumerate(outs):
        results["norm_pre"][q], results["norm_post"][q] = _unpack(packed, norm_like)
    a_small = ("a_w_s", "a_b_s", "a_ln_g", "a_ln_b")
    per_layer = []
    for j in range(n_a):
        like = [weights[k][j] for k in a_small]
        outs = _reduce_adam([small_a[j]], _pack(like), _pack([mom_m[k][j] for k in a_small]),
                            _pack([mom_v[k][j] for k in a_small]), f"adam_small_{j}")
        per_layer.append([_unpack(packed, like) for packed in outs])
    for q in range(4):
        for n, k in enumerate(a_small):
            results[k][q] = jnp.stack([per_layer[j][q][n] for j in range(n_a)])

    def shard_view(a):
        return a.reshape(a.shape[0], -1, a.shape[-1])

    for k in ("a_w_in", "a_w_out", "b_w_in", "b_w_grp", "b_w_out"):
        w3 = shard_view(weights[k])
        recvs = [r.reshape(r.shape[0], w3.shape[1], w3.shape[2]) for r in recv[k]]
        outs = _reduce_adam(recvs, w3, shard_view(mom_m[k]), shard_view(mom_v[k]), f"adam_{k}")
        results[k] = [o_.reshape(weights[k].shape) for o_ in outs]
    sc_recv = jnp.concatenate(recv["b_scale"], axis=1)
    outs = _reduce_adam([sc_recv], b_scale[None], m_b_scale[None], v_b_scale[None], "adam_b_scale")
    results["b_scale"] = [o_[0] for o_ in outs]

    grad_x = dx[None]
    return (loss, grad_x, *[results[k][0] for k in names], *[results[k][1] for k in names],
            *[results[k][2] for k in names], *[results[k][3] for k in names])
```

```python
import jax
import jax.numpy as jnp
from jax import lax
from jax.experimental import pallas as pl
from jax.experimental.pallas import tpu as pltpu

F32 = jnp.float32
BF16 = jnp.bfloat16
MESH = pl.DeviceIdType.MESH

N_DEV = 8
EPS = 1e-6
CHUNK = 64
GMLP_BLOCK = 128
A_GROUPS = 8
POOL_WINDOWS = (2, 4, 8, 16)
HALO = 16
ADAM_LR = 0.001
ADAM_B1 = 0.9
ADAM_B2 = 0.999
ADAM_EPS = 1e-08
ADAM_WD = 0.01
ADAM_STEP = 10
GELU_C = 0.7978845608028654
GELU_A = 0.044715
ROW_CHUNK = 16
VMEM_LIMIT_BYTES = 56 * 1024 * 1024


def _params(**kw):
    return pltpu.CompilerParams(vmem_limit_bytes=VMEM_LIMIT_BYTES, **kw)


def _gelu(x):
    return 0.5 * x * (1.0 + jnp.tanh(GELU_C * (x + GELU_A * (x * x * x))))


def _gelu_and_grad(x):
    x2 = x * x
    t = jnp.tanh(GELU_C * (x + GELU_A * (x2 * x)))
    val = 0.5 * x * (1.0 + t)
    grad = 0.5 * (1.0 + t) + 0.5 * x * (1.0 - t * t) * (GELU_C * (1.0 + 3.0 * GELU_A * x2))
    return val, grad


def _sigmoid(z):
    return 0.5 * jnp.tanh(0.5 * z) + 0.5


def _dot(a, b):
    return jnp.dot(a, b, preferred_element_type=F32)


def _dot_nt(a, b):
    return lax.dot_general(a, b, (((1,), (1,)), ((), ())), preferred_element_type=F32)


def _dot_tn(a, b):
    return lax.dot_general(a, b, (((0,), (0,)), ((), ())), preferred_element_type=F32)


def _rms_stats(xf):
    r = lax.rsqrt(jnp.mean(xf * xf, axis=-1, keepdims=True) + EPS)
    return r, xf * r


def _rms_bwd(dy, g, r, xh):
    dxh = dy * g
    return r * (dxh - xh * jnp.mean(dxh * xh, axis=-1, keepdims=True))


def _resident(shape):
    return pl.BlockSpec(shape, lambda *_: (0,) * len(shape), pipeline_mode=pl.Buffered(1))


def _pick(stacked):
    arr, index = stacked
    return pl.BlockSpec((None,) + arr.shape[1:], lambda *_: (index,) + (0,) * (arr.ndim - 1))


def _spatial_mask(transposed=False):
    p = lax.broadcasted_iota(jnp.int32, (GMLP_BLOCK, GMLP_BLOCK), 0)
    q = lax.broadcasted_iota(jnp.int32, (GMLP_BLOCK, GMLP_BLOCK), 1)
    if transposed:
        p, q = q, p
    return (q // CHUNK) <= (p // CHUNK)


def _position():
    x, y, c = lax.axis_index("x"), lax.axis_index("y"), lax.axis_index("c")
    return x, y, c


def _comm_scratch(n):
    return [pltpu.SemaphoreType.DMA((n, 7)), pltpu.SemaphoreType.DMA((n, 7)), pltpu.SemaphoreType.DMA((n,))]


class _Gather:
    def __init__(self, arrs):
        self.inputs = list(arrs)
        self.out_shape = [jax.ShapeDtypeStruct((N_DEV,) + a.shape, a.dtype) for a in arrs]
        self.scratch = _comm_scratch(len(arrs))

    def _plan(self, ins, outs, sems):
        send_sems, recv_sems, local_sems = sems
        n = len(ins)
        x, y, c = _position()
        sibling = (x, y, 1 - c)
        chips = [(1 - x, y), (x, 1 - y), (1 - x, 1 - y)]

        def index(px, py, pc):
            return 4 * px + 2 * py + pc

        def copy(a, k, block, to, src=None):
            return pltpu.make_async_remote_copy(
                src_ref=outs[a].at[block] if src is None else src, dst_ref=outs[a].at[block],
                send_sem=send_sems.at[a, k], recv_sem=recv_sems.at[a, k], device_id=to, device_id_type=MESH)

        me = index(x, y, c)
        own = [pltpu.make_async_copy(ins[a], outs[a].at[me], local_sems.at[a]) for a in range(n)]
        first = []
        for a in range(n):
            first.append(copy(a, 0, me, sibling, src=ins[a]))
            for j, chip in enumerate(chips):
                first.append(copy(a, 1 + j, me, (*chip, c), src=ins[a]))
        return n, (x, y, c), sibling, chips, index, copy, own, first

    def start(self, ins, outs, sems):
        _, _, _, _, _, _, own, first = self._plan(ins, outs, sems)
        for cp in own + first:
            cp.start()

    def middle(self, ins, outs, sems):
        n, me, sibling, chips, index, copy, _, _ = self._plan(ins, outs, sems)
        for j, chip in enumerate(chips):
            for a in range(n):
                copy(a, 1 + j, index(*chip, me[2]), me).wait_recv()
                copy(a, 4 + j, index(*chip, me[2]), sibling).start()

    def finish(self, ins, outs, sems):
        n, me, sibling, chips, index, copy, own, first = self._plan(ins, outs, sems)
        c = me[2]
        passed = [copy(a, 4 + j, index(*chip, c), sibling) for j, chip in enumerate(chips) for a in range(n)]
        for a in range(n):
            copy(a, 0, index(me[0], me[1], 1 - c), me).wait_recv()
        for j, chip in enumerate(chips):
            for a in range(n):
                copy(a, 4 + j, index(*chip, 1 - c), me).wait_recv()
        for cp in first + passed:
            cp.wait_send()
        for cp in own:
            cp.wait()


class _Exchange:
    def __init__(self, arrs, chips_only=False):
        self.inputs = list(arrs)
        self.chips_only = chips_only
        self.out_shape = [jax.ShapeDtypeStruct(a.shape, a.dtype) for a in arrs]
        self.scratch = _comm_scratch(len(arrs))

    def _plan(self, ins, outs, sems):
        send_sems, recv_sems, local_sems = sems
        n = len(ins)
        x, y, c = _position()
        scale = 1 if self.chips_only else 2
        me = 2 * x + y if self.chips_only else 4 * x + 2 * y + c
        own = [pltpu.make_async_copy(ins[a].at[me], outs[a].at[me], local_sems.at[a]) for a in range(n)]
        sends, recvs = [], []
        for r in range(1, 4 * scale):
            px = 1 - x if r & (2 * scale) else x
            py = 1 - y if r & scale else y
            pc = 1 - c if (r & 1 and not self.chips_only) else c
            peer = 2 * px + py if self.chips_only else 4 * px + 2 * py + pc
            for a in range(n):
                sends.append(pltpu.make_async_remote_copy(
                    src_ref=ins[a].at[peer], dst_ref=outs[a].at[me],
                    send_sem=send_sems.at[a, r - 1], recv_sem=recv_sems.at[a, r - 1],
                    device_id=(px, py, pc), device_id_type=MESH))
                recvs.append(pltpu.make_async_remote_copy(
                    src_ref=ins[a].at[peer], dst_ref=outs[a].at[peer],
                    send_sem=send_sems.at[a, r - 1], recv_sem=recv_sems.at[a, r - 1],
                    device_id=(px, py, pc), device_id_type=MESH))
        return own, sends, recvs

    def start(self, ins, outs, sems):
        own, sends, _ = self._plan(ins, outs, sems)
        for cp in own + sends:
            cp.start()

    def middle(self, ins, outs, sems):
        pass

    def finish(self, ins, outs, sems):
        own, sends, recvs = self._plan(ins, outs, sems)
        for cp in recvs:
            cp.wait_recv()
        for cp in sends:
            cp.wait_send()
        for cp in own:
            cp.wait()


CHIPS = N_DEV // 2


def _sibling_scratch(slab_shape, stage_slots=CHIPS):
    return [pltpu.VMEM((stage_slots,) + tuple(slab_shape), BF16), pltpu.VMEM((CHIPS,) + tuple(slab_shape), BF16),
            pltpu.SemaphoreType.DMA((CHIPS,)), pltpu.SemaphoreType.DMA((CHIPS,))]


def _to_sibling(stage_s, land_s, send_sems, recv_sems, slot, q):
    x, y, c = _position()
    return pltpu.make_async_remote_copy(
        src_ref=stage_s.at[slot], dst_ref=land_s.at[q], send_sem=send_sems.at[q], recv_sem=recv_sems.at[q],
        device_id=(x, y, 1 - c), device_id_type=MESH)


def _sum_with_sibling(slab_of, q_ref, sibling_scratch):
    stage_s, land_s, send_sems, recv_sems = sibling_scratch
    c = lax.axis_index("c")
    for q in range(CHIPS):
        stage_s[q] = slab_of(2 * q + 1 - c).astype(BF16)
        _to_sibling(stage_s, land_s, send_sems, recv_sems, q, q).start()
    for q in range(CHIPS):
        _to_sibling(stage_s, land_s, send_sems, recv_sems, q, q).wait_recv()
        q_ref[q] = (slab_of(2 * q + c) + land_s[q].astype(F32)).astype(BF16)
    for q in range(CHIPS):
        _to_sibling(stage_s, land_s, send_sems, recv_sems, q, q).wait_send()


class _Together:
    def __init__(self, comms):
        self.comms = list(comms)
        self.inputs = [a for c in self.comms for a in c.inputs]
        self.out_shape = [s for c in self.comms for s in c.out_shape]
        self.scratch = [s for c in self.comms for s in c.scratch]

    def _each(self, ins, outs, sems):
        at = 0
        for k, c in enumerate(self.comms):
            n = len(c.inputs)
            yield c, ins[at:at + n], outs[at:at + n], sems[3 * k:3 * k + 3]
            at += n

    def start(self, ins, outs, sems):
        for c, i, o, s in self._each(ins, outs, sems):
            c.start(i, o, s)

    def middle(self, ins, outs, sems):
        for c, i, o, s in self._each(ins, outs, sems):
            c.middle(i, o, s)

    def finish(self, ins, outs, sems):
        for c, i, o, s in self._each(ins, outs, sems):
            c.finish(i, o, s)


def _comm_only(comm, name):
    n = len(comm.inputs)

    def body(*refs):
        ins, outs, sems = refs[:n], refs[n:2 * n], refs[2 * n:]
        comm.start(ins, outs, sems)
        comm.middle(ins, outs, sems)
        comm.finish(ins, outs, sems)

    any_spec = pl.BlockSpec(memory_space=pl.ANY)
    return pl.pallas_call(
        body, name=name, out_shape=comm.out_shape, in_specs=[any_spec] * n, out_specs=[any_spec] * n,
        scratch_shapes=comm.scratch, compiler_params=pltpu.CompilerParams(has_side_effects=True),
    )(*comm.inputs)


def _pcall(body, name, grid, in_specs, out_specs, out_shape, args, scratch_shapes=(), comm=None):
    in_specs, out_specs, out_shape, scratch_shapes = list(in_specs), list(out_specs), list(out_shape), list(scratch_shapes)
    if comm is None:
        outs = pl.pallas_call(body, name=name, grid=grid, in_specs=in_specs, out_specs=out_specs, out_shape=out_shape,
                              scratch_shapes=scratch_shapes, compiler_params=_params())(*args)
        return list(outs), []
    n_in, n_out, n_scr, n_c = len(in_specs), len(out_specs), len(scratch_shapes), len(comm.inputs)

    def carrying(*refs):
        ins, refs = refs[:n_in], refs[n_in:]
        c_ins, refs = refs[:n_c], refs[n_c:]
        outs, refs = refs[:n_out], refs[n_out:]
        c_outs, refs = refs[:n_c], refs[n_c:]
        scr, sems = refs[:n_scr], refs[n_scr:]
        step, steps = 0, 1
        for d, size in enumerate(grid):
            step = step * size + pl.program_id(d)
            steps *= size

        @pl.when(step == 0)
        def _():
            comm.start(c_ins, c_outs, sems)

        body(*ins, *outs, *scr)

        @pl.when(step == max(steps - 2, 0))
        def _():
            comm.middle(c_ins, c_outs, sems)

        @pl.when(step == steps - 1)
        def _():
            comm.finish(c_ins, c_outs, sems)

    any_spec = pl.BlockSpec(memory_space=pl.ANY)
    outs = pl.pallas_call(
        carrying, name=name, grid=grid, in_specs=in_specs + [any_spec] * n_c, out_specs=out_specs + [any_spec] * n_c,
        out_shape=out_shape + comm.out_shape, scratch_shapes=scratch_shapes + comm.scratch,
        compiler_params=_params(has_side_effects=True),
    )(*args, *comm.inputs)
    return list(outs[:n_out]), list(outs[n_out:])


def _in_proj_fwd(x, g_row, w_parts, name, comm=None):
    T, D = x.shape
    NS = w_parts[0].shape[-1]
    DP = w_parts[0].shape[-2]
    n_parts = len(w_parts)
    assert DP * n_parts == D
    TM = min(T, 512)

    def body(x_ref, g_ref, *refs):
        w_refs, (proj_ref, h_ref) = refs[:n_parts], refs[n_parts:]
        _, xh = _rms_stats(x_ref[...])
        h = (xh * g_ref[...]).astype(BF16)
        h_ref[...] = h
        for k in range(N_DEV):
            acc = _dot(h[:, 0:DP], w_refs[0][k])
            for p in range(1, n_parts):
                acc = acc + _dot(h[:, p * DP:(p + 1) * DP], w_refs[p][k])
            proj_ref[:, k * NS:(k + 1) * NS] = acc.astype(BF16)

    return _pcall(
        body, name, (T // TM,),
        in_specs=[pl.BlockSpec((TM, D), lambda i: (i, 0)), _pick(g_row)] + [_resident((N_DEV, DP, NS))] * n_parts,
        out_specs=[pl.BlockSpec((TM, N_DEV * NS), lambda i: (i, 0)),
                   pl.BlockSpec((TM, D), lambda i: (i, 0))],
        out_shape=[jax.ShapeDtypeStruct((T, N_DEV * NS), BF16), jax.ShapeDtypeStruct((T, D), BF16)],
        args=(x, g_row[0], *w_parts), comm=comm)


def _a_mix_fwd(proj, ln_g, ln_b, w_s, b_s, name, comm=None):
    T, E3 = proj.shape
    E = E3 // 3
    G, P = A_GROUPS, GMLP_BLOCK
    GD = E // G
    TB = min(T, 512)

    def body(p_ref, lg_ref, lb_ref, ws_ref, bs_ref, y_ref, xh_ref, dgl_ref, rstd_ref, v_s, us_s):
        def norm_chunk(ci, carry):
            rows = pl.ds(pl.multiple_of(ci * ROW_CHUNK, ROW_CHUNK), ROW_CHUNK)
            vg, dgl = _gelu_and_grad(p_ref[rows, E:2 * E].astype(F32))
            dgl_ref[rows, :] = dgl.astype(BF16)
            xc = vg - jnp.mean(vg, axis=-1, keepdims=True)
            rstd = lax.rsqrt(jnp.mean(xc * xc, axis=-1, keepdims=True) + EPS)
            rstd_ref[rows, :] = rstd
            xh = xc * rstd
            xh_ref[rows, :] = xh.astype(BF16)
            v_s[rows, :] = (xh * lg_ref[...] + lb_ref[...]).astype(BF16)
            return carry

        def gate_chunk(ci, carry):
            rows = pl.ds(pl.multiple_of(ci * ROW_CHUNK, ROW_CHUNK), ROW_CHUNK)
            z = p_ref[rows, 2 * E:3 * E].astype(F32)
            us_s[rows, :] = _gelu(p_ref[rows, 0:E].astype(F32)) * (z * _sigmoid(z))
            return carry

        lax.fori_loop(0, TB // ROW_CHUNK, norm_chunk, 0, unroll=2)
        lax.fori_loop(0, TB // ROW_CHUNK, gate_chunk, 0, unroll=2)
        mask = _spatial_mask()
        for g in range(G):
            wm = jnp.where(mask, ws_ref[g], 0.0).astype(BF16)
            cols = slice(g * GD, (g + 1) * GD)
            for b in range(TB // P):
                rows = slice(b * P, (b + 1) * P)
                mixed = _dot(wm, v_s[rows, cols]) + bs_ref[g]
                y_ref[rows, cols] = (us_s[rows, cols] * mixed).astype(BF16)

    return _pcall(
        body, name, (T // TB,),
        in_specs=[pl.BlockSpec((TB, E3), lambda i: (i, 0)),
                  _pick(ln_g), _pick(ln_b), _pick(w_s), _pick(b_s)],
        out_specs=[pl.BlockSpec((TB, E), lambda i: (i, 0)), pl.BlockSpec((TB, E), lambda i: (i, 0)),
                   pl.BlockSpec((TB, E), lambda i: (i, 0)), pl.BlockSpec((TB, 1), lambda i: (i, 0))],
        out_shape=[jax.ShapeDtypeStruct((T, E), BF16), jax.ShapeDtypeStruct((T, E), BF16),
                   jax.ShapeDtypeStruct((T, E), BF16), jax.ShapeDtypeStruct((T, 1), F32)],
        scratch_shapes=[pltpu.VMEM((TB, E), BF16), pltpu.VMEM((TB, E), F32)],
        args=(proj, ln_g[0], ln_b[0], w_s[0], b_s[0]), comm=comm)


def _window_sum_back(ext, win):
    s, k = ext, 1
    while k < win:
        s = s + pltpu.roll(s, k, axis=0)
        k *= 2
    return s


def _window_sum_ahead(ext, win):
    n = ext.shape[0]
    s, k = ext, 1
    while k < win:
        s = s + pltpu.roll(s, n - k, axis=0)
        k *= 2
    return s


def _inv_count(t0, rows, win):
    t1 = t0 + 1 + lax.broadcasted_iota(jnp.int32, (rows, 1), 0)
    return 1.0 / jnp.minimum(t1, win).astype(F32)


def _b_mix_fwd(proj, scale, wg_all, name, comm=None):
    T, E2 = proj.shape
    E = E2 // 2
    NG = len(POOL_WINDOWS)
    GB = E // NG
    TB = min(T, 256)
    RS = wg_all.shape[-2]

    def body(p_ref, sc_ref, wg_ref, y_ref, o_ref, carry_s):
        i = pl.program_id(0)

        @pl.when(i == 0)
        def _():
            carry_s[...] = jnp.zeros_like(carry_s)

        for g, win in enumerate(POOL_WINDOWS):
            cols = slice(g * GB, (g + 1) * GB)
            xg = p_ref[:, cols].astype(F32)
            ext = jnp.concatenate([carry_s[:, cols], xg], axis=0)
            pooled = _window_sum_back(ext, win)[HALO:, :] * _inv_count(i * TB, TB, win) - xg
            carry_s[:, cols] = xg[TB - HALO:, :]
            o = _dot(pooled.astype(BF16), wg_ref[:, g].reshape(GB, GB))
            o_ref[:, cols] = o.astype(BF16)
            z = p_ref[:, E + g * GB:E + (g + 1) * GB].astype(F32)
            y_ref[:, cols] = ((o * sc_ref[:, cols]) * (z * _sigmoid(z))).astype(BF16)

    return _pcall(
        body, name, (T // TB,),
        in_specs=[pl.BlockSpec((TB, E2), lambda i: (i, 0)),
                  pl.BlockSpec((1, E), lambda i: (0, 0)),
                  pl.BlockSpec((N_DEV, NG, RS, GB), lambda i: (0, 0, 0, 0))],
        out_specs=[pl.BlockSpec((TB, E), lambda i: (i, 0)), pl.BlockSpec((TB, E), lambda i: (i, 0))],
        out_shape=[jax.ShapeDtypeStruct((T, E), BF16), jax.ShapeDtypeStruct((T, E), BF16)],
        scratch_shapes=[pltpu.VMEM((HALO, E), F32)],
        args=(proj, scale, wg_all), comm=comm)


def _out_proj_fwd(y, w_all, x, g_row, name, comm=None):
    T, E = y.shape
    D = x.shape[1]
    ES = w_all.shape[-2]
    TM = min(T, 512)

    def body(y_ref, w_ref, x_ref, g_ref, xn_ref, out_ref):
        o = _dot(y_ref[...], w_ref[...].reshape(E, D))
        out_ref[...] = o
        _, oh = _rms_stats(o)
        xn_ref[...] = x_ref[...] + oh * g_ref[...]

    return _pcall(
        body, name, (T // TM,),
        in_specs=[pl.BlockSpec((TM, E), lambda i: (i, 0)),
                  pl.BlockSpec((N_DEV, ES, D), lambda i: (0, 0, 0)),
                  pl.BlockSpec((TM, D), lambda i: (i, 0)),
                  _pick(g_row)],
        out_specs=[pl.BlockSpec((TM, D), lambda i: (i, 0)), pl.BlockSpec((TM, D), lambda i: (i, 0))],
        out_shape=[jax.ShapeDtypeStruct((T, D), F32), jax.ShapeDtypeStruct((T, D), F32)],
        args=(y, w_all, x, g_row[0]), comm=comm)


def _loss_head(x, target, name):
    T, D = x.shape
    TM = min(T, 512)
    nT = T // TM

    def body(x_ref, t_ref, dx_ref, loss_ref, acc_s):
        i = pl.program_id(0)

        @pl.when(i == 0)
        def _():
            acc_s[...] = jnp.zeros_like(acc_s)

        e = x_ref[...] - t_ref[...]
        dx_ref[...] = e * (1.0 / D)
        acc_s[...] += jnp.sum(e * e, axis=0, keepdims=True)

        @pl.when(i == nT - 1)
        def _():
            total = jnp.sum(acc_s[...], axis=1, keepdims=True) * (0.5 / D)
            loss_ref[...] = jnp.broadcast_to(total, loss_ref.shape)

    return _pcall(
        body, name, (nT,),
        in_specs=[pl.BlockSpec((TM, D), lambda i: (i, 0)), pl.BlockSpec((TM, D), lambda i: (i, 0))],
        out_specs=[pl.BlockSpec((TM, D), lambda i: (i, 0)), pl.BlockSpec((1, 128), lambda i: (0, 0))],
        out_shape=[jax.ShapeDtypeStruct((T, D), F32), jax.ShapeDtypeStruct((1, 128), F32)],
        scratch_shapes=[pltpu.VMEM((1, D), F32)],
        args=(x, target))[0]


def _out_proj_bwd(dxn, out, g_row, w_all, y, name, comm=None):
    T, D = dxn.shape
    E = y.shape[1]
    ES = w_all.shape[-2]
    TM = min(T, 512)
    nT = T // TM

    def body(dxn_ref, out_ref, g_ref, w_ref, y_ref, dy_ref, dw_ref, dg_ref, acc_s, *sibling_scratch):
        i = pl.program_id(0)

        @pl.when(i == 0)
        def _():
            acc_s[...] = jnp.zeros_like(acc_s)
            dg_ref[...] = jnp.zeros_like(dg_ref)

        dxn_v = dxn_ref[...]
        r, oh = _rms_stats(out_ref[...])
        dg_ref[...] += jnp.sum(dxn_v * oh, axis=0, keepdims=True)
        dout = _rms_bwd(dxn_v, g_ref[...], r, oh).astype(BF16)
        dy_ref[...] = _dot_nt(dout, w_ref[...].reshape(E, D)).astype(BF16)
        acc_s[...] += _dot_tn(y_ref[...], dout)

        @pl.when(i == nT - 1)
        def _():
            def slab_of(k):
                return acc_s[pl.ds(pl.multiple_of(k * ES, ES), ES), :]

            _sum_with_sibling(slab_of, dw_ref, sibling_scratch)

    return _pcall(
        body, name, (nT,),
        in_specs=[pl.BlockSpec((TM, D), lambda i: (i, 0)),
                  pl.BlockSpec((TM, D), lambda i: (i, 0)),
                  _pick(g_row),
                  pl.BlockSpec((N_DEV, ES, D), lambda i: (0, 0, 0)),
                  pl.BlockSpec((TM, E), lambda i: (i, 0))],
        out_specs=[pl.BlockSpec((TM, E), lambda i: (i, 0)),
                   pl.BlockSpec((CHIPS, ES, D), lambda i: (0, 0, 0)),
                   pl.BlockSpec((1, D), lambda i: (0, 0))],
        out_shape=[jax.ShapeDtypeStruct((T, E), BF16), jax.ShapeDtypeStruct((CHIPS, ES, D), BF16),
                   jax.ShapeDtypeStruct((1, D), F32)],
        scratch_shapes=[pltpu.VMEM((E, D), F32)] + _sibling_scratch((ES, D)),
        args=(dxn, out, g_row[0], w_all, y), comm=comm)


def _a_mix_bwd(proj, dy, xh, dgl, rstd, ln_g, ln_b, w_s, b_s, name, comm=None):
    T, E3 = proj.shape
    E = E3 // 3
    G, P = A_GROUPS, GMLP_BLOCK
    GD = E // G
    TB = min(T, 256)

    def body(up_ref, zp_ref, dy_ref, xh_ref, dgl_ref, rstd_ref, lg_ref, lb_ref, ws_ref, bs_ref,
             dp_ref, dws_ref, dbs_ref, dlg_ref, dlb_ref, v_s, a_s, bz_s, c_s, dv_s):
        @pl.when(pl.program_id(0) == 0)
        def _():
            dws_ref[...] = jnp.zeros_like(dws_ref)
            dbs_ref[...] = jnp.zeros_like(dbs_ref)
            dlg_ref[...] = jnp.zeros_like(dlg_ref)
            dlb_ref[...] = jnp.zeros_like(dlb_ref)

        def recompute(ci, carry):
            rows = pl.ds(pl.multiple_of(ci * ROW_CHUNK, ROW_CHUNK), ROW_CHUNK)
            v_s[rows, :] = (xh_ref[rows, :].astype(F32) * lg_ref[...] + lb_ref[...]).astype(BF16)
            u, du = _gelu_and_grad(up_ref[rows, :].astype(F32))
            z = zp_ref[rows, :].astype(F32)
            sg = _sigmoid(z)
            s = z * sg
            ds = sg * (1.0 + z * (1.0 - sg))
            dyv = dy_ref[rows, :].astype(F32)
            a_s[rows, :] = dyv * s * du
            bz_s[rows, :] = dyv * u * ds
            c_s[rows, :] = (dyv * u * s).astype(BF16)
            return carry

        lax.fori_loop(0, TB // ROW_CHUNK, recompute, 0, unroll=2)

        mask = _spatial_mask()
        mask_t = _spatial_mask(transposed=True)
        for g in range(G):
            w_g = ws_ref[g]
            wm = jnp.where(mask, w_g, 0.0).astype(BF16)
            wm_t = jnp.where(mask_t, w_g.T, 0.0).astype(BF16)
            cols = slice(g * GD, (g + 1) * GD)
            dws_g = jnp.zeros((P, P), F32)
            dbs_g = jnp.zeros((P, 1), F32)
            for b in range(TB // P):
                rows = slice(b * P, (b + 1) * P)
                vb = v_s[rows, cols]
                cb = c_s[rows, cols]
                mixed = _dot(wm, vb) + bs_ref[g]
                dp_ref[rows, g * GD:(g + 1) * GD] = (a_s[rows, cols] * mixed).astype(BF16)
                dp_ref[rows, 2 * E + g * GD:2 * E + (g + 1) * GD] = (bz_s[rows, cols] * mixed).astype(BF16)
                dv_s[rows, cols] = _dot(wm_t, cb)
                dws_g = dws_g + _dot_nt(cb, vb)
                dbs_g = dbs_g + jnp.sum(cb.astype(F32), axis=1, keepdims=True)
            dws_ref[g] += jnp.where(mask, dws_g, 0.0)
            dbs_ref[g] += dbs_g

        def ln_bwd(ci, carry):
            rows = pl.ds(pl.multiple_of(ci * ROW_CHUNK, ROW_CHUNK), ROW_CHUNK)
            dv = dv_s[rows, :]
            xh = xh_ref[rows, :].astype(F32)
            dlg_ref[...] += jnp.sum(dv * xh, axis=0, keepdims=True)
            dlb_ref[...] += jnp.sum(dv, axis=0, keepdims=True)
            dxh = dv * lg_ref[...]
            dvg = rstd_ref[rows, :] * (dxh - jnp.mean(dxh, axis=-1, keepdims=True)
                                       - xh * jnp.mean(dxh * xh, axis=-1, keepdims=True))
            dp_ref[rows, E:2 * E] = (dvg * dgl_ref[rows, :].astype(F32)).astype(BF16)
            return carry

        lax.fori_loop(0, TB // ROW_CHUNK, ln_bwd, 0, unroll=2)

    return _pcall(
        body, name, (T // TB,),
        in_specs=[pl.BlockSpec((TB, E), lambda i: (i, 0)),
                  pl.BlockSpec((TB, E), lambda i: (i, 2)),
                  pl.BlockSpec((TB, E), lambda i: (i, 0)),
                  pl.BlockSpec((TB, E), lambda i: (i, 0)),
                  pl.BlockSpec((TB, E), lambda i: (i, 0)),
                  pl.BlockSpec((TB, 1), lambda i: (i, 0)),
                  _pick(ln_g), _pick(ln_b), _pick(w_s), _pick(b_s)],
        out_specs=[pl.BlockSpec((TB, E3), lambda i: (i, 0)),
                   pl.BlockSpec((G, P, P), lambda i: (0, 0, 0)),
                   pl.BlockSpec((G, P, 1), lambda i: (0, 0, 0)),
                   pl.BlockSpec((1, E), lambda i: (0, 0)),
                   pl.BlockSpec((1, E), lambda i: (0, 0))],
        out_shape=[jax.ShapeDtypeStruct((T, E3), BF16), jax.ShapeDtypeStruct((G, P, P), F32),
                   jax.ShapeDtypeStruct((G, P, 1), F32), jax.ShapeDtypeStruct((1, E), F32),
                   jax.ShapeDtypeStruct((1, E), F32)],
        scratch_shapes=[pltpu.VMEM((TB, E), BF16), pltpu.VMEM((TB, E), F32), pltpu.VMEM((TB, E), F32),
                        pltpu.VMEM((TB, E), BF16), pltpu.VMEM((TB, E), F32)],
        args=(proj, proj, dy, xh, dgl, rstd, ln_g[0], ln_b[0], w_s[0], b_s[0]), comm=comm)


def _b_mix_bwd(proj, dy, o, scale, wg_all, name, comm=None):
    T, E2 = proj.shape
    E = E2 // 2
    NG = len(POOL_WINDOWS)
    GB = E // NG
    TB = min(T, 256)
    nT = T // TB
    RS = wg_all.shape[-2]
    halo_per_tile = TB // HALO

    def body(p_ref, halo_ref, dy_ref, o_ref, sc_ref, wg_ref, dp_ref, dsc_ref, dwg_ref, acc_s, carry_s,
             *sibling_scratch):
        i = pl.program_id(0)
        tile = nT - 1 - i

        @pl.when(i == 0)
        def _():
            acc_s[...] = jnp.zeros_like(acc_s)
            carry_s[...] = jnp.zeros_like(carry_s)
            dsc_ref[...] = jnp.zeros_like(dsc_ref)

        has_history = (tile > 0).astype(F32)
        for g, win in enumerate(POOL_WINDOWS):
            cols = slice(g * GB, (g + 1) * GB)
            inv = _inv_count(tile * TB, TB, win)
            xg = p_ref[:, cols].astype(F32)
            ext = jnp.concatenate([halo_ref[:, cols].astype(F32) * has_history, xg], axis=0)
            pooled = _window_sum_back(ext, win)[HALO:, :] * inv - xg
            z = p_ref[:, E + g * GB:E + (g + 1) * GB].astype(F32)
            sg = _sigmoid(z)
            dyv = dy_ref[:, cols].astype(F32)
            ov = o_ref[:, cols].astype(F32)
            sc = sc_ref[:, cols]
            dmixed = dyv * (z * sg)
            dsc_ref[:, cols] += jnp.sum(dmixed * ov, axis=0, keepdims=True)
            dz = dyv * (ov * sc) * (sg * (1.0 + z * (1.0 - sg)))
            do = (dmixed * sc).astype(BF16)
            acc_s[:, g] += _dot_tn(pooled.astype(BF16), do).reshape(N_DEV, RS, GB)
            dpool = _dot_nt(do, wg_ref[:, g].reshape(GB, GB))
            q = dpool * inv
            ext_q = jnp.concatenate([q, carry_s[:, cols]], axis=0)
            dxb = _window_sum_ahead(ext_q, win)[:TB, :] - dpool
            carry_s[:, cols] = q[:HALO, :]
            dp_ref[:, cols] = dxb.astype(BF16)
            dp_ref[:, E + g * GB:E + (g + 1) * GB] = dz.astype(BF16)

        @pl.when(i == nT - 1)
        def _():
            _sum_with_sibling(lambda k: acc_s[k], dwg_ref, sibling_scratch)

    return _pcall(
        body, name, (nT,),
        in_specs=[pl.BlockSpec((TB, E2), lambda i: (nT - 1 - i, 0)),
                  pl.BlockSpec((HALO, E), lambda i: (jnp.maximum((nT - 1 - i) * halo_per_tile - 1, 0), 0)),
                  pl.BlockSpec((TB, E), lambda i: (nT - 1 - i, 0)),
                  pl.BlockSpec((TB, E), lambda i: (nT - 1 - i, 0)),
                  pl.BlockSpec((1, E), lambda i: (0, 0)),
                  pl.BlockSpec((N_DEV, NG, RS, GB), lambda i: (0, 0, 0, 0))],
        out_specs=[pl.BlockSpec((TB, E2), lambda i: (nT - 1 - i, 0)),
                   pl.BlockSpec((1, E), lambda i: (0, 0)),
                   pl.BlockSpec((CHIPS, NG, RS, GB), lambda i: (0, 0, 0, 0))],
        out_shape=[jax.ShapeDtypeStruct((T, E2), BF16), jax.ShapeDtypeStruct((1, E), F32),
                   jax.ShapeDtypeStruct((CHIPS, NG, RS, GB), BF16)],
        scratch_shapes=[pltpu.VMEM((N_DEV, NG, RS, GB), F32), pltpu.VMEM((HALO, E), F32)]
        + _sibling_scratch((NG, RS, GB)),
        args=(proj, proj, dy, o, scale, wg_all), comm=comm)


def _in_proj_bwd_dx(dproj, w_parts, x, g_row, dxn, name, comm=None):
    T, D = x.shape
    NS = w_parts[0].shape[-1]
    DP = w_parts[0].shape[-2]
    n_parts = len(w_parts)
    TM = min(T, 512)

    def body(dp_ref, *refs):
        w_refs, (x_ref, g_ref, dxn_ref, dx_ref, dg_ref) = refs[:n_parts], refs[n_parts:]

        @pl.when(pl.program_id(0) == 0)
        def _():
            dg_ref[...] = jnp.zeros_like(dg_ref)

        pieces = []
        for w_ref in w_refs:
            piece = _dot_nt(dp_ref[:, 0:NS], w_ref[0])
            for k in range(1, N_DEV):
                piece = piece + _dot_nt(dp_ref[:, k * NS:(k + 1) * NS], w_ref[k])
            pieces.append(piece)
        dh = pieces[0] if n_parts == 1 else jnp.concatenate(pieces, axis=1)
        r, xh = _rms_stats(x_ref[...])
        dg_ref[...] += jnp.sum(dh * xh, axis=0, keepdims=True)
        dx_ref[...] = dxn_ref[...] + _rms_bwd(dh, g_ref[...], r, xh)

    return _pcall(
        body, name, (T // TM,),
        in_specs=[pl.BlockSpec((TM, N_DEV * NS), lambda i: (i, 0))] + [_resident((N_DEV, DP, NS))] * n_parts
        + [pl.BlockSpec((TM, D), lambda i: (i, 0)),
           _pick(g_row),
           pl.BlockSpec((TM, D), lambda i: (i, 0))],
        out_specs=[pl.BlockSpec((TM, D), lambda i: (i, 0)), pl.BlockSpec((1, D), lambda i: (0, 0))],
        out_shape=[jax.ShapeDtypeStruct((T, D), F32), jax.ShapeDtypeStruct((1, D), F32)],
        args=(dproj, *w_parts, x, g_row[0], dxn), comm=comm)


STAGE_SLOTS = 2


def _dw_in(h, dproj, name, comm=None):
    T, D = h.shape
    NS = dproj.shape[1] // N_DEV
    TK = min(T, 2048)
    nK = T // TK

    def body(h_ref, dp_ref, q_ref, acc_s, stage_s, land_s, send_sems, recv_sems):
        k, t = pl.program_id(0), pl.program_id(1)
        c = lax.axis_index("c")

        def to_sibling(q):
            return _to_sibling(stage_s, land_s, send_sems, recv_sems, q % STAGE_SLOTS, q)

        @pl.when(t == 0)
        def _():
            acc_s[...] = jnp.zeros_like(acc_s)

        acc_s[...] += _dot_tn(h_ref[...], dp_ref[...])

        @pl.when(t == nK - 1)
        def _():
            q = k // 2

            @pl.when(k % 2 == c)
            def _():
                q_ref[q] = acc_s[...].astype(BF16)

            @pl.when(k % 2 != c)
            def _():
                @pl.when(q >= STAGE_SLOTS)
                def _():
                    to_sibling(q - STAGE_SLOTS).wait_send()

                stage_s[q % STAGE_SLOTS] = acc_s[...].astype(BF16)
                to_sibling(q).start()

        @pl.when((k == N_DEV - 1) & (t == nK - 1))
        def _():
            for q in range(CHIPS - STAGE_SLOTS, CHIPS):
                to_sibling(q).wait_send()
            for q in range(CHIPS):
                to_sibling(q).wait_recv()
                q_ref[q] = (q_ref[q].astype(F32) + land_s[q].astype(F32)).astype(BF16)

    return _pcall(
        body, name, (N_DEV, nK),
        in_specs=[pl.BlockSpec((TK, D), lambda k, t: (t, 0)), pl.BlockSpec((TK, NS), lambda k, t: (t, k))],
        out_specs=[pl.BlockSpec((CHIPS, D, NS), lambda k, t: (0, 0, 0))],
        out_shape=[jax.ShapeDtypeStruct((CHIPS, D, NS), BF16)],
        scratch_shapes=[pltpu.VMEM((D, NS), F32)] + _sibling_scratch((D, NS), STAGE_SLOTS),
        args=(h, dproj), comm=comm)


def _reduce_adam(recvs, w, m, v, name):
    L, R, C = w.shape
    assert len(recvs) == L
    senders = recvs[0].shape[0]
    TR = R
    for cand in (256, 128, 64, 32, 16):
        if R % cand == 0 and R > cand:
            TR = cand
            break
    nR = R // TR
    c1 = 1.0 - ADAM_B1 ** ADAM_STEP
    c2 = 1.0 - ADAM_B2 ** ADAM_STEP

    def body(*refs):
        recv_refs = refs[:L]
        w_ref, m_ref, v_ref, g_ref, d_ref, nm_ref, nv_ref, g_s = refs[L:]
        layer = pl.program_id(0)
        for l in range(L):
            @pl.when(layer == l)
            def _(l=l):
                acc = recv_refs[l][0].astype(F32)
                for j in range(1, senders):
                    acc = acc + recv_refs[l][j].astype(F32)
                g_s[...] = acc

        g = g_s[...]
        g_ref[...] = g
        nm = ADAM_B1 * m_ref[...] + (1.0 - ADAM_B1) * g
        nv = ADAM_B2 * v_ref[...] + (1.0 - ADAM_B2) * (g * g)
        nm_ref[...] = nm
        nv_ref[...] = nv
        d_ref[...] = -ADAM_LR * ((nm / c1) / (jnp.sqrt(nv / c2) + ADAM_EPS) + ADAM_WD * w_ref[...])

    def recv_spec(l):
        def index(layer, t):
            before = jnp.where(layer < l, 0, nR - 1)
            return (0, jnp.where(layer == l, t, before), 0)
        return pl.BlockSpec((senders, TR, C), index)

    wspec = pl.BlockSpec((None, TR, C), lambda layer, t: (layer, t, 0))
    out = jax.ShapeDtypeStruct((L, R, C), F32)
    return _pcall(
        body, name, (L, nR),
        in_specs=[recv_spec(l) for l in range(L)] + [wspec] * 3,
        out_specs=[wspec] * 4, out_shape=[out] * 4,
        scratch_shapes=[pltpu.VMEM((TR, C), F32)],
        args=(*recvs, w, m, v))[0]


A_W_IN_PARTS = 4
PACK_LANES = 128
PACK_ROWS_MULTIPLE = 256


def _pack(arrays):
    flat = jnp.concatenate([a.reshape(-1) for a in arrays])
    tile = PACK_LANES * PACK_ROWS_MULTIPLE
    padded = -(-flat.shape[0] // tile) * tile
    return jnp.pad(flat, (0, padded - flat.shape[0])).reshape(1, padded // PACK_LANES, PACK_LANES)


def _unpack(packed, like):
    flat = packed.reshape(-1)
    out, at = [], 0
    for a in like:
        out.append(flat[at:at + a.size].reshape(a.shape))
        at += a.size
    return out


def kernel(x, norm_pre, norm_post, a_w_in, a_ln_g, a_ln_b, a_w_s, a_b_s, a_w_out, b_w_in, b_w_grp, b_scale, b_w_out, loss_target, m_norm_pre, m_norm_post, m_a_w_in, m_a_ln_g, m_a_ln_b, m_a_w_s, m_a_b_s, m_a_w_out, m_b_w_in, m_b_w_grp, m_b_scale, m_b_w_out, v_norm_pre, v_norm_post, v_a_w_in, v_a_ln_g, v_a_ln_b, v_a_w_s, v_a_b_s, v_a_w_out, v_b_w_in, v_b_w_grp, v_b_scale, v_b_w_out):
    weights = dict(norm_pre=norm_pre, norm_post=norm_post, a_w_in=a_w_in, a_ln_g=a_ln_g, a_ln_b=a_ln_b, a_w_s=a_w_s,
                   a_b_s=a_b_s, a_w_out=a_w_out, b_w_in=b_w_in, b_w_grp=b_w_grp, b_scale=b_scale, b_w_out=b_w_out)
    mom_m = dict(norm_pre=m_norm_pre, norm_post=m_norm_post, a_w_in=m_a_w_in, a_ln_g=m_a_ln_g, a_ln_b=m_a_ln_b,
                 a_w_s=m_a_w_s, a_b_s=m_a_b_s, a_w_out=m_a_w_out, b_w_in=m_b_w_in, b_w_grp=m_b_w_grp,
                 b_scale=m_b_scale, b_w_out=m_b_w_out)
    mom_v = dict(norm_pre=v_norm_pre, norm_post=v_norm_post, a_w_in=v_a_w_in, a_ln_g=v_a_ln_g, a_ln_b=v_a_ln_b,
                 a_w_s=v_a_w_s, a_b_s=v_a_b_s, a_w_out=v_a_w_out, b_w_in=v_b_w_in, b_w_grp=v_b_w_grp,
                 b_scale=v_b_scale, b_w_out=v_b_w_out)
    names = list(weights)

    depth = norm_pre.shape[0]
    x0 = x[0]
    target = loss_target[0]
    T, D = x0.shape
    E = a_ln_g.shape[1]
    G, P = A_GROUPS, GMLP_BLOCK
    pre3, post3 = norm_pre.reshape(depth, 1, D), norm_post.reshape(depth, 1, D)
    ln_g3, ln_b3 = a_ln_g.reshape(-1, 1, E), a_ln_b.reshape(-1, 1, E)
    b_s4 = a_b_s.reshape(-1, G, P, 1)

    def shards_of(i):
        j = i // 2
        if i % 2 == 0:
            w = a_w_in[j].astype(BF16)
            rows = w.shape[0] // A_W_IN_PARTS
            parts = {f"w_in_{p}": w[p * rows:(p + 1) * rows] for p in range(A_W_IN_PARTS)}
            return dict(**parts, w_out=a_w_out[j].astype(BF16))
        return dict(w_in=b_w_in[j].astype(BF16), w_out=b_w_out[j].astype(BF16), grp=b_w_grp[j].astype(BF16))

    shard = [shards_of(i) for i in range(depth)]
    full = [dict() for _ in range(depth)]

    def gather_into(keys, got):
        for (i, key), arr in zip(keys, got):
            full[i][key] = arr

    def w_in_of(i):
        return [(i, k) for k in shard[i] if k.startswith("w_in")]

    def rest_of(i):
        return [(i, k) for k in shard[i] if not k.startswith("w_in")]

    def gather_of(keys):
        return _Gather([shard[a][k] for a, k in keys]) if keys else None

    first = _comm_only(_Gather([shard[0][k] for _, k in w_in_of(0)] + [b_scale]), "gather_first")
    gather_into(w_in_of(0), first)
    scale_full = jnp.transpose(first[-1], (1, 0, 2)).reshape(b_scale.shape[0], 1, E)

    saved = []
    xi = x0
    for i in range(depth):
        j = i // 2
        g_pre, g_post = (pre3, i), (post3, i)
        keys_in, keys_mix, keys_out = [], [], []
        if i % 2 == 0:
            keys_mix = [(0, "w_out")] if i == 0 else []
            if i + 1 < depth:
                keys_in = w_in_of(i + 1)
                keys_mix = keys_mix + rest_of(i + 1)
            if i + 2 < depth:
                keys_mix = keys_mix + [(i + 2, "w_in_0")]
                keys_out = [(i + 2, "w_in_1")]
        elif i + 1 < depth:
            keys_in = w_in_of(i + 1)[2:]
            keys_mix = rest_of(i + 1)
        kind = "a" if i % 2 == 0 else "b"
        (proj, h), got = _in_proj_fwd(xi, g_pre, [full[i][k] for _, k in w_in_of(i)], f"{kind}_in_fwd_{i}",
                                      gather_of(keys_in))
        gather_into(keys_in, got)
        if i % 2 == 0:
            (y, *o), got = _a_mix_fwd(proj, (ln_g3, j), (ln_b3, j), (a_w_s, j), (b_s4, j),
                                      f"a_mix_fwd_{i}", gather_of(keys_mix))
        else:
            (y, o), got = _b_mix_fwd(proj, scale_full[j], full[i]["grp"], f"b_mix_fwd_{i}", gather_of(keys_mix))
        gather_into(keys_mix, got)
        (x_next, out), got = _out_proj_fwd(y, full[i]["w_out"], xi, g_post, f"{kind}_out_fwd_{i}",
                                           gather_of(keys_out))
        gather_into(keys_out, got)
        saved.append((xi, h, proj, y, out, o))
        xi = x_next

    dx, loss_row = _loss_head(xi, target, "loss_head")

    n_a, n_b = a_ln_g.shape[0], b_scale.shape[0]
    d_pre, d_post = [None] * depth, [None] * depth
    recv = {"a_w_in": [None] * n_a, "a_w_out": [None] * n_a, "b_w_in": [None] * n_b,
            "b_w_grp": [None] * n_b, "b_w_out": [None] * n_b, "b_scale": [None] * n_b}
    small_a = [None] * n_a
    kinds = ("chips", "devices", "gather")

    def carried(items):
        of = {kind: [it[3] for it in items if it[0] == kind] for kind in kinds}
        comms = (([_Exchange(of["chips"], chips_only=True)] if of["chips"] else [])
                 + ([_Exchange(of["devices"])] if of["devices"] else [])
                 + ([_Gather(of["gather"])] if of["gather"] else []))
        return None if not comms else comms[0] if len(comms) == 1 else _Together(comms)

    def received(items, got):
        ordered = [it for kind in kinds for it in items if it[0] == kind]
        for it, arr in zip(ordered, got):
            if it[0] == "gather":
                small_a[it[2]] = arr
            else:
                recv[it[1]][it[2]] = arr

    pending = []
    small_pending = []
    for i in reversed(range(depth)):
        j = i // 2
        xi, h, proj, y, out, o = saved[i]
        g_pre, g_post = (pre3, i), (post3, i)
        if i % 2 == 0:
            (dy, dw_out, d_post[i]), _ = _out_proj_bwd(dx, out, g_post, full[i]["w_out"], y, f"a_out_bwd_{i}")
            items, pending = pending, []
            (dproj, d_w_s, dbs, d_ln_g, d_ln_b), got = _a_mix_bwd(
                proj, dy, *o, (ln_g3, j), (ln_b3, j), (a_w_s, j), (b_s4, j),
                f"a_mix_bwd_{i}", carried(items))
            received(items, got)
            small_pending.append(("gather", "small", j, _pack([d_w_s, dbs, d_ln_g, d_ln_b])[0]))
            items = []
            if i == 0:
                items, small_pending = [("chips", "a_w_out", j, dw_out)] + small_pending, []
            (dw_in,), got = _dw_in(h, dproj, f"a_dw_in_{i}", carried(items))
            received(items, got)
            items = [("chips", "a_w_in", j, dw_in)] if i == 0 else [("chips", "a_w_out", j, dw_out)]
            (dx, d_pre[i]), got = _in_proj_bwd_dx(dproj, [full[i][k] for _, k in w_in_of(i)], xi, g_pre, dx,
                                                  f"a_in_bwd_{i}", carried(items))
            received(items, got)
            if i > 0:
                pending.append(("chips", "a_w_in", j, dw_in))
        else:
            items, small_pending = small_pending, []
            (dy, dw_out, d_post[i]), got = _out_proj_bwd(dx, out, g_post, full[i]["w_out"], y, f"b_out_bwd_{i}",
                                                        carried(items))
            received(items, got)
            items, pending = pending, []
            (dproj, dsc, dw_grp), got = _b_mix_bwd(proj, dy, o, scale_full[j], full[i]["grp"], f"b_mix_bwd_{i}",
                                                  carried(items))
            received(items, got)
            (dx, d_pre[i]), _ = _in_proj_bwd_dx(dproj, [full[i][k] for _, k in w_in_of(i)], xi, g_pre, dx,
                                                f"b_in_bwd_{i}")
            (dw_in,), _ = _dw_in(h, dproj, f"b_dw_in_{i}")
            pending += [("chips", "b_w_out", j, dw_out), ("chips", "b_w_grp", j, dw_grp), ("chips", "b_w_in", j, dw_in),
                        ("devices", "b_scale", j, dsc.reshape(N_DEV, 1, E // N_DEV))]
    assert not pending and not small_pending

    gathered = _comm_only(_Gather([_pack([*d_pre, *d_post, loss_row[:, :1]])[0]]), "gather_norm_grads")
    results = {k: [None] * 4 for k in names}
    no_state = jnp.zeros((1, 1), F32)
    norm_like = [norm_pre, norm_post, no_state]
    outs = _reduce_adam([gathered[-1]], _pack(norm_like), _pack([m_norm_pre, m_norm_post, no_state]),
                        _pack([v_norm_pre, v_norm_post, no_state]), "adam_norms")
    for q, packed in enumerate(outs):
        results["norm_pre"][q], results["norm_post"][q], summed = _unpack(packed, norm_like)
        if q == 0:
            loss = summed[0, 0]
    a_small = ("a_w_s", "a_b_s", "a_ln_g", "a_ln_b")
    per_layer = []
    for j in range(n_a):
        like = [weights[k][j] for k in a_small]
        outs = _reduce_adam([small_a[j]], _pack(like), _pack([mom_m[k][j] for k in a_small]),
                            _pack([mom_v[k][j] for k in a_small]), f"adam_small_{j}")
        per_layer.append([_unpack(packed, like) for packed in outs])
    for q in range(4):
        for n, k in enumerate(a_small):
            results[k][q] = jnp.stack([per_layer[j][q][n] for j in range(n_a)])

    def shard_view(a):
        return a.reshape(a.shape[0], -1, a.shape[-1])

    for k in ("a_w_in", "a_w_out", "b_w_in", "b_w_grp", "b_w_out"):
        w3 = shard_view(weights[k])
        recvs = [r.reshape(r.shape[0], w3.shape[1], w3.shape[2]) for r in recv[k]]
        outs = _reduce_adam(recvs, w3, shard_view(mom_m[k]), shard_view(mom_v[k]), f"adam_{k}")
        results[k] = [o_.reshape(weights[k].shape) for o_ in outs]
    sc_recv = jnp.concatenate(recv["b_scale"], axis=1)
    outs = _reduce_adam([sc_recv], b_scale[None], m_b_scale[None], v_b_scale[None], "adam_b_scale")
    results["b_scale"] = [o_[0] for o_ in outs]

    grad_x = dx[None]
    return (loss, grad_x, *[results[k][0] for k in names], *[results[k][1] for k in names],
            *[results[k][2] for k in names], *[results[k][3] for k in names])
```

```python
import jax
import jax.numpy as jnp
from jax import lax
from jax.experimental import pallas as pl
from jax.experimental.pallas import tpu as pltpu

F32 = jnp.float32
BF16 = jnp.bfloat16
MESH = pl.DeviceIdType.MESH

N_DEV = 8
EPS = 1e-6
CHUNK = 64
GMLP_BLOCK = 128
A_GROUPS = 8
POOL_WINDOWS = (2, 4, 8, 16)
HALO = 16
ADAM_LR = 0.001
ADAM_B1 = 0.9
ADAM_B2 = 0.999
ADAM_EPS = 1e-08
ADAM_WD = 0.01
ADAM_STEP = 10
GELU_C = 0.7978845608028654
GELU_A = 0.044715
ROW_CHUNK = 16
VMEM_LIMIT_BYTES = 56 * 1024 * 1024


def _params(**kw):
    return pltpu.CompilerParams(vmem_limit_bytes=VMEM_LIMIT_BYTES, **kw)


def _gelu(x):
    return 0.5 * x * (1.0 + jnp.tanh(GELU_C * (x + GELU_A * (x * x * x))))


def _gelu_and_grad(x):
    x2 = x * x
    t = jnp.tanh(GELU_C * (x + GELU_A * (x2 * x)))
    val = 0.5 * x * (1.0 + t)
    grad = 0.5 * (1.0 + t) + 0.5 * x * (1.0 - t * t) * (GELU_C * (1.0 + 3.0 * GELU_A * x2))
    return val, grad


def _sigmoid(z):
    return 0.5 * jnp.tanh(0.5 * z) + 0.5


def _dot(a, b):
    return jnp.dot(a, b, preferred_element_type=F32)


def _dot_nt(a, b):
    return lax.dot_general(a, b, (((1,), (1,)), ((), ())), preferred_element_type=F32)


def _dot_tn(a, b):
    return lax.dot_general(a, b, (((0,), (0,)), ((), ())), preferred_element_type=F32)


def _rms_stats(xf):
    r = lax.rsqrt(jnp.mean(xf * xf, axis=-1, keepdims=True) + EPS)
    return r, xf * r


def _rms_bwd(dy, g, r, xh):
    dxh = dy * g
    return r * (dxh - xh * jnp.mean(dxh * xh, axis=-1, keepdims=True))


def _resident(shape):
    return pl.BlockSpec(shape, lambda *_: (0,) * len(shape), pipeline_mode=pl.Buffered(1))


def _pick(stacked):
    arr, index = stacked
    return pl.BlockSpec((None,) + arr.shape[1:], lambda *_: (index,) + (0,) * (arr.ndim - 1))


def _spatial_mask(transposed=False):
    p = lax.broadcasted_iota(jnp.int32, (GMLP_BLOCK, GMLP_BLOCK), 0)
    q = lax.broadcasted_iota(jnp.int32, (GMLP_BLOCK, GMLP_BLOCK), 1)
    if transposed:
        p, q = q, p
    return (q // CHUNK) <= (p // CHUNK)


def _position():
    x, y, c = lax.axis_index("x"), lax.axis_index("y"), lax.axis_index("c")
    return x, y, c


def _comm_scratch(n):
    return [pltpu.SemaphoreType.DMA((n, 7)), pltpu.SemaphoreType.DMA((n, 7)), pltpu.SemaphoreType.DMA((n,))]


class _Gather:
    def __init__(self, arrs):
        self.inputs = list(arrs)
        self.out_shape = [jax.ShapeDtypeStruct((N_DEV,) + a.shape, a.dtype) for a in arrs]
        self.scratch = _comm_scratch(len(arrs))

    def _plan(self, ins, outs, sems):
        send_sems, recv_sems, local_sems = sems
        n = len(ins)
        x, y, c = _position()
        sibling = (x, y, 1 - c)
        chips = [(1 - x, y), (x, 1 - y), (1 - x, 1 - y)]

        def index(px, py, pc):
            return 4 * px + 2 * py + pc

        def copy(a, k, block, to, src=None):
            return pltpu.make_async_remote_copy(
                src_ref=outs[a].at[block] if src is None else src, dst_ref=outs[a].at[block],
                send_sem=send_sems.at[a, k], recv_sem=recv_sems.at[a, k], device_id=to, device_id_type=MESH)

        me = index(x, y, c)
        own = [pltpu.make_async_copy(ins[a], outs[a].at[me], local_sems.at[a]) for a in range(n)]
        first = []
        for a in range(n):
            first.append(copy(a, 0, me, sibling, src=ins[a]))
            for j, chip in enumerate(chips):
                first.append(copy(a, 1 + j, me, (*chip, c), src=ins[a]))
        return n, (x, y, c), sibling, chips, index, copy, own, first

    def start(self, ins, outs, sems):
        _, _, _, _, _, _, own, first = self._plan(ins, outs, sems)
        for cp in own + first:
            cp.start()

    def middle(self, ins, outs, sems):
        n, me, sibling, chips, index, copy, _, _ = self._plan(ins, outs, sems)
        for j, chip in enumerate(chips):
            for a in range(n):
                copy(a, 1 + j, index(*chip, me[2]), me).wait_recv()
                copy(a, 4 + j, index(*chip, me[2]), sibling).start()

    def finish(self, ins, outs, sems):
        n, me, sibling, chips, index, copy, own, first = self._plan(ins, outs, sems)
        c = me[2]
        passed = [copy(a, 4 + j, index(*chip, c), sibling) for j, chip in enumerate(chips) for a in range(n)]
        for a in range(n):
            copy(a, 0, index(me[0], me[1], 1 - c), me).wait_recv()
        for j, chip in enumerate(chips):
            for a in range(n):
                copy(a, 4 + j, index(*chip, 1 - c), me).wait_recv()
        for cp in first + passed:
            cp.wait_send()
        for cp in own:
            cp.wait()


class _Exchange:
    def __init__(self, arrs, chips_only=False):
        self.inputs = list(arrs)
        self.chips_only = chips_only
        self.out_shape = [jax.ShapeDtypeStruct(a.shape, a.dtype) for a in arrs]
        self.scratch = _comm_scratch(len(arrs))

    def _plan(self, ins, outs, sems):
        send_sems, recv_sems, local_sems = sems
        n = len(ins)
        x, y, c = _position()
        scale = 1 if self.chips_only else 2
        me = 2 * x + y if self.chips_only else 4 * x + 2 * y + c
        own = [pltpu.make_async_copy(ins[a].at[me], outs[a].at[me], local_sems.at[a]) for a in range(n)]
        sends, recvs = [], []
        for r in range(1, 4 * scale):
            px = 1 - x if r & (2 * scale) else x
            py = 1 - y if r & scale else y
            pc = 1 - c if (r & 1 and not self.chips_only) else c
            peer = 2 * px + py if self.chips_only else 4 * px + 2 * py + pc
            for a in range(n):
                sends.append(pltpu.make_async_remote_copy(
                    src_ref=ins[a].at[peer], dst_ref=outs[a].at[me],
                    send_sem=send_sems.at[a, r - 1], recv_sem=recv_sems.at[a, r - 1],
                    device_id=(px, py, pc), device_id_type=MESH))
                recvs.append(pltpu.make_async_remote_copy(
                    src_ref=ins[a].at[peer], dst_ref=outs[a].at[peer],
                    send_sem=send_sems.at[a, r - 1], recv_sem=recv_sems.at[a, r - 1],
                    device_id=(px, py, pc), device_id_type=MESH))
        return own, sends, recvs

    def start(self, ins, outs, sems):
        own, sends, _ = self._plan(ins, outs, sems)
        for cp in own + sends:
            cp.start()

    def middle(self, ins, outs, sems):
        pass

    def finish(self, ins, outs, sems):
        own, sends, recvs = self._plan(ins, outs, sems)
        for cp in recvs:
            cp.wait_recv()
        for cp in sends:
            cp.wait_send()
        for cp in own:
            cp.wait()


CHIPS = N_DEV // 2


def _sibling_scratch(slab_shape, stage_slots=CHIPS):
    return [pltpu.VMEM((stage_slots,) + tuple(slab_shape), BF16), pltpu.VMEM((CHIPS,) + tuple(slab_shape), BF16),
            pltpu.SemaphoreType.DMA((CHIPS,)), pltpu.SemaphoreType.DMA((CHIPS,))]


def _to_sibling(stage_s, land_s, send_sems, recv_sems, slot, q):
    x, y, c = _position()
    return pltpu.make_async_remote_copy(
        src_ref=stage_s.at[slot], dst_ref=land_s.at[q], send_sem=send_sems.at[q], recv_sem=recv_sems.at[q],
        device_id=(x, y, 1 - c), device_id_type=MESH)


def _sum_with_sibling(slab_of, q_ref, sibling_scratch):
    stage_s, land_s, send_sems, recv_sems = sibling_scratch
    c = lax.axis_index("c")
    for q in range(CHIPS):
        stage_s[q] = slab_of(2 * q + 1 - c).astype(BF16)
        _to_sibling(stage_s, land_s, send_sems, recv_sems, q, q).start()
    for q in range(CHIPS):
        _to_sibling(stage_s, land_s, send_sems, recv_sems, q, q).wait_recv()
        q_ref[q] = (slab_of(2 * q + c) + land_s[q].astype(F32)).astype(BF16)
    for q in range(CHIPS):
        _to_sibling(stage_s, land_s, send_sems, recv_sems, q, q).wait_send()


class _Together:
    def __init__(self, comms):
        self.comms = list(comms)
        self.inputs = [a for c in self.comms for a in c.inputs]
        self.out_shape = [s for c in self.comms for s in c.out_shape]
        self.scratch = [s for c in self.comms for s in c.scratch]

    def _each(self, ins, outs, sems):
        at = 0
        for k, c in enumerate(self.comms):
            n = len(c.inputs)
            yield c, ins[at:at + n], outs[at:at + n], sems[3 * k:3 * k + 3]
            at += n

    def start(self, ins, outs, sems):
        for c, i, o, s in self._each(ins, outs, sems):
            c.start(i, o, s)

    def middle(self, ins, outs, sems):
        for c, i, o, s in self._each(ins, outs, sems):
            c.middle(i, o, s)

    def finish(self, ins, outs, sems):
        for c, i, o, s in self._each(ins, outs, sems):
            c.finish(i, o, s)


def _comm_only(comm, name):
    n = len(comm.inputs)

    def body(*refs):
        ins, outs, sems = refs[:n], refs[n:2 * n], refs[2 * n:]
        comm.start(ins, outs, sems)
        comm.middle(ins, outs, sems)
        comm.finish(ins, outs, sems)

    any_spec = pl.BlockSpec(memory_space=pl.ANY)
    return pl.pallas_call(
        body, name=name, out_shape=comm.out_shape, in_specs=[any_spec] * n, out_specs=[any_spec] * n,
        scratch_shapes=comm.scratch, compiler_params=pltpu.CompilerParams(has_side_effects=True),
    )(*comm.inputs)


def _pcall(body, name, grid, in_specs, out_specs, out_shape, args, scratch_shapes=(), comm=None):
    in_specs, out_specs, out_shape, scratch_shapes = list(in_specs), list(out_specs), list(out_shape), list(scratch_shapes)
    if comm is None:
        outs = pl.pallas_call(body, name=name, grid=grid, in_specs=in_specs, out_specs=out_specs, out_shape=out_shape,
                              scratch_shapes=scratch_shapes, compiler_params=_params())(*args)
        return list(outs), []
    n_in, n_out, n_scr, n_c = len(in_specs), len(out_specs), len(scratch_shapes), len(comm.inputs)

    def carrying(*refs):
        ins, refs = refs[:n_in], refs[n_in:]
        c_ins, refs = refs[:n_c], refs[n_c:]
        outs, refs = refs[:n_out], refs[n_out:]
        c_outs, refs = refs[:n_c], refs[n_c:]
        scr, sems = refs[:n_scr], refs[n_scr:]
        step, steps = 0, 1
        for d, size in enumerate(grid):
            step = step * size + pl.program_id(d)
            steps *= size

        @pl.when(step == 0)
        def _():
            comm.start(c_ins, c_outs, sems)

        body(*ins, *outs, *scr)

        @pl.when(step == max(steps - 2, 0))
        def _():
            comm.middle(c_ins, c_outs, sems)

        @pl.when(step == steps - 1)
        def _():
            comm.finish(c_ins, c_outs, sems)

    any_spec = pl.BlockSpec(memory_space=pl.ANY)
    outs = pl.pallas_call(
        carrying, name=name, grid=grid, in_specs=in_specs + [any_spec] * n_c, out_specs=out_specs + [any_spec] * n_c,
        out_shape=out_shape + comm.out_shape, scratch_shapes=scratch_shapes + comm.scratch,
        compiler_params=_params(has_side_effects=True),
    )(*args, *comm.inputs)
    return list(outs[:n_out]), list(outs[n_out:])


def _in_proj_fwd(x, g_row, w_parts, name, comm=None):
    T, D = x.shape
    NS = w_parts[0].shape[-1]
    DP = w_parts[0].shape[-2]
    n_parts = len(w_parts)
    assert DP * n_parts == D
    TM = min(T, 512)

    def body(x_ref, g_ref, *refs):
        w_refs, (proj_ref, h_ref) = refs[:n_parts], refs[n_parts:]
        _, xh = _rms_stats(x_ref[...])
        h = (xh * g_ref[...]).astype(BF16)
        h_ref[...] = h
        for k in range(N_DEV):
            acc = _dot(h[:, 0:DP], w_refs[0][k])
            for p in range(1, n_parts):
                acc = acc + _dot(h[:, p * DP:(p + 1) * DP], w_refs[p][k])
            proj_ref[:, k * NS:(k + 1) * NS] = acc.astype(BF16)

    return _pcall(
        body, name, (T // TM,),
        in_specs=[pl.BlockSpec((TM, D), lambda i: (i, 0)), _pick(g_row)] + [_resident((N_DEV, DP, NS))] * n_parts,
        out_specs=[pl.BlockSpec((TM, N_DEV * NS), lambda i: (i, 0)),
                   pl.BlockSpec((TM, D), lambda i: (i, 0))],
        out_shape=[jax.ShapeDtypeStruct((T, N_DEV * NS), BF16), jax.ShapeDtypeStruct((T, D), BF16)],
        args=(x, g_row[0], *w_parts), comm=comm)


def _a_mix_fwd(proj, ln_g, ln_b, w_s, b_s, name, comm=None):
    T, E3 = proj.shape
    E = E3 // 3
    G, P = A_GROUPS, GMLP_BLOCK
    GD = E // G
    TB = min(T, 512)

    def body(p_ref, lg_ref, lb_ref, ws_ref, bs_ref, y_ref, xh_ref, dgl_ref, rstd_ref, v_s, us_s):
        def norm_chunk(ci, carry):
            rows = pl.ds(pl.multiple_of(ci * ROW_CHUNK, ROW_CHUNK), ROW_CHUNK)
            vg, dgl = _gelu_and_grad(p_ref[rows, E:2 * E].astype(F32))
            dgl_ref[rows, :] = dgl.astype(BF16)
            xc = vg - jnp.mean(vg, axis=-1, keepdims=True)
            rstd = lax.rsqrt(jnp.mean(xc * xc, axis=-1, keepdims=True) + EPS)
            rstd_ref[rows, :] = rstd
            xh = xc * rstd
            xh_ref[rows, :] = xh.astype(BF16)
            v_s[rows, :] = (xh * lg_ref[...] + lb_ref[...]).astype(BF16)
            return carry

        def gate_chunk(ci, carry):
            rows = pl.ds(pl.multiple_of(ci * ROW_CHUNK, ROW_CHUNK), ROW_CHUNK)
            z = p_ref[rows, 2 * E:3 * E].astype(F32)
            us_s[rows, :] = _gelu(p_ref[rows, 0:E].astype(F32)) * (z * _sigmoid(z))
            return carry

        lax.fori_loop(0, TB // ROW_CHUNK, norm_chunk, 0, unroll=2)
        lax.fori_loop(0, TB // ROW_CHUNK, gate_chunk, 0, unroll=2)
        mask = _spatial_mask()
        for g in range(G):
            wm = jnp.where(mask, ws_ref[g], 0.0).astype(BF16)
            cols = slice(g * GD, (g + 1) * GD)
            for b in range(TB // P):
                rows = slice(b * P, (b + 1) * P)
                mixed = _dot(wm, v_s[rows, cols]) + bs_ref[g]
                y_ref[rows, cols] = (us_s[rows, cols] * mixed).astype(BF16)

    return _pcall(
        body, name, (T // TB,),
        in_specs=[pl.BlockSpec((TB, E3), lambda i: (i, 0)),
                  _pick(ln_g), _pick(ln_b), _pick(w_s), _pick(b_s)],
        out_specs=[pl.BlockSpec((TB, E), lambda i: (i, 0)), pl.BlockSpec((TB, E), lambda i: (i, 0)),
                   pl.BlockSpec((TB, E), lambda i: (i, 0)), pl.BlockSpec((TB, 1), lambda i: (i, 0))],
        out_shape=[jax.ShapeDtypeStruct((T, E), BF16), jax.ShapeDtypeStruct((T, E), BF16),
                   jax.ShapeDtypeStruct((T, E), BF16), jax.ShapeDtypeStruct((T, 1), F32)],
        scratch_shapes=[pltpu.VMEM((TB, E), BF16), pltpu.VMEM((TB, E), F32)],
        args=(proj, ln_g[0], ln_b[0], w_s[0], b_s[0]), comm=comm)


def _window_sum_back(ext, win):
    s, k = ext, 1
    while k < win:
        s = s + pltpu.roll(s, k, axis=0)
        k *= 2
    return s


def _window_sum_ahead(ext, win):
    n = ext.shape[0]
    s, k = ext, 1
    while k < win:
        s = s + pltpu.roll(s, n - k, axis=0)
        k *= 2
    return s


def _inv_count(t0, rows, win):
    t1 = t0 + 1 + lax.broadcasted_iota(jnp.int32, (rows, 1), 0)
    return 1.0 / jnp.minimum(t1, win).astype(F32)


def _b_mix_fwd(proj, scale, wg_all, name, comm=None):
    T, E2 = proj.shape
    E = E2 // 2
    NG = len(POOL_WINDOWS)
    GB = E // NG
    TB = min(T, 256)
    RS = wg_all.shape[-2]

    def body(p_ref, sc_ref, wg_ref, y_ref, o_ref, carry_s):
        i = pl.program_id(0)

        @pl.when(i == 0)
        def _():
            carry_s[...] = jnp.zeros_like(carry_s)

        for g, win in enumerate(POOL_WINDOWS):
            cols = slice(g * GB, (g + 1) * GB)
            xg = p_ref[:, cols].astype(F32)
            ext = jnp.concatenate([carry_s[:, cols], xg], axis=0)
            pooled = _window_sum_back(ext, win)[HALO:, :] * _inv_count(i * TB, TB, win) - xg
            carry_s[:, cols] = xg[TB - HALO:, :]
            o = _dot(pooled.astype(BF16), wg_ref[:, g].reshape(GB, GB))
            o_ref[:, cols] = o.astype(BF16)
            z = p_ref[:, E + g * GB:E + (g + 1) * GB].astype(F32)
            y_ref[:, cols] = ((o * sc_ref[:, cols]) * (z * _sigmoid(z))).astype(BF16)

    return _pcall(
        body, name, (T // TB,),
        in_specs=[pl.BlockSpec((TB, E2), lambda i: (i, 0)),
                  pl.BlockSpec((1, E), lambda i: (0, 0)),
                  pl.BlockSpec((N_DEV, NG, RS, GB), lambda i: (0, 0, 0, 0))],
        out_specs=[pl.BlockSpec((TB, E), lambda i: (i, 0)), pl.BlockSpec((TB, E), lambda i: (i, 0))],
        out_shape=[jax.ShapeDtypeStruct((T, E), BF16), jax.ShapeDtypeStruct((T, E), BF16)],
        scratch_shapes=[pltpu.VMEM((HALO, E), F32)],
        args=(proj, scale, wg_all), comm=comm)


def _out_proj_fwd(y, w_all, x, g_row, name, comm=None):
    T, E = y.shape
    D = x.shape[1]
    ES = w_all.shape[-2]
    TM = min(T, 512)

    def body(y_ref, w_ref, x_ref, g_ref, xn_ref, out_ref):
        o = _dot(y_ref[...], w_ref[...].reshape(E, D))
        out_ref[...] = o
        _, oh = _rms_stats(o)
        xn_ref[...] = x_ref[...] + oh * g_ref[...]

    return _pcall(
        body, name, (T // TM,),
        in_specs=[pl.BlockSpec((TM, E), lambda i: (i, 0)),
                  pl.BlockSpec((N_DEV, ES, D), lambda i: (0, 0, 0)),
                  pl.BlockSpec((TM, D), lambda i: (i, 0)),
                  _pick(g_row)],
        out_specs=[pl.BlockSpec((TM, D), lambda i: (i, 0)), pl.BlockSpec((TM, D), lambda i: (i, 0))],
        out_shape=[jax.ShapeDtypeStruct((T, D), F32), jax.ShapeDtypeStruct((T, D), F32)],
        args=(y, w_all, x, g_row[0]), comm=comm)


def _loss_head(x, target, name):
    T, D = x.shape
    TM = min(T, 512)
    nT = T // TM

    def body(x_ref, t_ref, dx_ref, loss_ref, acc_s):
        i = pl.program_id(0)

        @pl.when(i == 0)
        def _():
            acc_s[...] = jnp.zeros_like(acc_s)

        e = x_ref[...] - t_ref[...]
        dx_ref[...] = e * (1.0 / D)
        acc_s[...] += jnp.sum(e * e, axis=0, keepdims=True)

        @pl.when(i == nT - 1)
        def _():
            total = jnp.sum(acc_s[...], axis=1, keepdims=True) * (0.5 / D)
            loss_ref[...] = jnp.broadcast_to(total, loss_ref.shape)

    return _pcall(
        body, name, (nT,),
        in_specs=[pl.BlockSpec((TM, D), lambda i: (i, 0)), pl.BlockSpec((TM, D), lambda i: (i, 0))],
        out_specs=[pl.BlockSpec((TM, D), lambda i: (i, 0)), pl.BlockSpec((1, 128), lambda i: (0, 0))],
        out_shape=[jax.ShapeDtypeStruct((T, D), F32), jax.ShapeDtypeStruct((1, 128), F32)],
        scratch_shapes=[pltpu.VMEM((1, D), F32)],
        args=(x, target))[0]


def _out_proj_bwd(dxn, out, g_row, w_all, y, name, comm=None):
    T, D = dxn.shape
    E = y.shape[1]
    ES = w_all.shape[-2]
    TM = min(T, 512)
    nT = T // TM

    def body(dxn_ref, out_ref, g_ref, w_ref, y_ref, dy_ref, dw_ref, dg_ref, acc_s, *sibling_scratch):
        i = pl.program_id(0)

        @pl.when(i == 0)
        def _():
            acc_s[...] = jnp.zeros_like(acc_s)
            dg_ref[...] = jnp.zeros_like(dg_ref)

        dxn_v = dxn_ref[...]
        r, oh = _rms_stats(out_ref[...])
        dg_ref[...] += jnp.sum(dxn_v * oh, axis=0, keepdims=True)
        dout = _rms_bwd(dxn_v, g_ref[...], r, oh).astype(BF16)
        dy_ref[...] = _dot_nt(dout, w_ref[...].reshape(E, D)).astype(BF16)
        acc_s[...] += _dot_tn(y_ref[...], dout)

        @pl.when(i == nT - 1)
        def _():
            def slab_of(k):
                return acc_s[pl.ds(pl.multiple_of(k * ES, ES), ES), :]

            _sum_with_sibling(slab_of, dw_ref, sibling_scratch)

    return _pcall(
        body, name, (nT,),
        in_specs=[pl.BlockSpec((TM, D), lambda i: (i, 0)),
                  pl.BlockSpec((TM, D), lambda i: (i, 0)),
                  _pick(g_row),
                  pl.BlockSpec((N_DEV, ES, D), lambda i: (0, 0, 0)),
                  pl.BlockSpec((TM, E), lambda i: (i, 0))],
        out_specs=[pl.BlockSpec((TM, E), lambda i: (i, 0)),
                   pl.BlockSpec((CHIPS, ES, D), lambda i: (0, 0, 0)),
                   pl.BlockSpec((1, D), lambda i: (0, 0))],
        out_shape=[jax.ShapeDtypeStruct((T, E), BF16), jax.ShapeDtypeStruct((CHIPS, ES, D), BF16),
                   jax.ShapeDtypeStruct((1, D), F32)],
        scratch_shapes=[pltpu.VMEM((E, D), F32)] + _sibling_scratch((ES, D)),
        args=(dxn, out, g_row[0], w_all, y), comm=comm)


def _a_mix_bwd(proj, dy, xh, dgl, rstd, ln_g, ln_b, w_s, b_s, name, comm=None):
    T, E3 = proj.shape
    E = E3 // 3
    G, P = A_GROUPS, GMLP_BLOCK
    GD = E // G
    TB = min(T, 256)

    def body(up_ref, zp_ref, dy_ref, xh_ref, dgl_ref, rstd_ref, lg_ref, lb_ref, ws_ref, bs_ref,
             dp_ref, dws_ref, dbs_ref, dlg_ref, dlb_ref, v_s, a_s, bz_s, c_s, dv_s):
        @pl.when(pl.program_id(0) == 0)
        def _():
            dws_ref[...] = jnp.zeros_like(dws_ref)
            dbs_ref[...] = jnp.zeros_like(dbs_ref)
            dlg_ref[...] = jnp.zeros_like(dlg_ref)
            dlb_ref[...] = jnp.zeros_like(dlb_ref)

        def recompute(ci, carry):
            rows = pl.ds(pl.multiple_of(ci * ROW_CHUNK, ROW_CHUNK), ROW_CHUNK)
            v_s[rows, :] = (xh_ref[rows, :].astype(F32) * lg_ref[...] + lb_ref[...]).astype(BF16)
            u, du = _gelu_and_grad(up_ref[rows, :].astype(F32))
            z = zp_ref[rows, :].astype(F32)
            sg = _sigmoid(z)
            s = z * sg
            ds = sg * (1.0 + z * (1.0 - sg))
            dyv = dy_ref[rows, :].astype(F32)
            a_s[rows, :] = dyv * s * du
            bz_s[rows, :] = dyv * u * ds
            c_s[rows, :] = (dyv * u * s).astype(BF16)
            return carry

        lax.fori_loop(0, TB // ROW_CHUNK, recompute, 0, unroll=2)

        mask = _spatial_mask()
        mask_t = _spatial_mask(transposed=True)
        for g in range(G):
            w_g = ws_ref[g]
            wm = jnp.where(mask, w_g, 0.0).astype(BF16)
            wm_t = jnp.where(mask_t, w_g.T, 0.0).astype(BF16)
            cols = slice(g * GD, (g + 1) * GD)
            dws_g = jnp.zeros((P, P), F32)
            dbs_g = jnp.zeros((P, 1), F32)
            for b in range(TB // P):
                rows = slice(b * P, (b + 1) * P)
                vb = v_s[rows, cols]
                cb = c_s[rows, cols]
                mixed = _dot(wm, vb) + bs_ref[g]
                dp_ref[rows, g * GD:(g + 1) * GD] = (a_s[rows, cols] * mixed).astype(BF16)
                dp_ref[rows, 2 * E + g * GD:2 * E + (g + 1) * GD] = (bz_s[rows, cols] * mixed).astype(BF16)
                dv_s[rows, cols] = _dot(wm_t, cb)
                dws_g = dws_g + _dot_nt(cb, vb)
                dbs_g = dbs_g + jnp.sum(cb.astype(F32), axis=1, keepdims=True)
            dws_ref[g] += jnp.where(mask, dws_g, 0.0)
            dbs_ref[g] += dbs_g

        def ln_bwd(ci, carry):
            rows = pl.ds(pl.multiple_of(ci * ROW_CHUNK, ROW_CHUNK), ROW_CHUNK)
            dv = dv_s[rows, :]
            xh = xh_ref[rows, :].astype(F32)
            dlg_ref[...] += jnp.sum(dv * xh, axis=0, keepdims=True)
            dlb_ref[...] += jnp.sum(dv, axis=0, keepdims=True)
            dxh = dv * lg_ref[...]
            dvg = rstd_ref[rows, :] * (dxh - jnp.mean(dxh, axis=-1, keepdims=True)
                                       - xh * jnp.mean(dxh * xh, axis=-1, keepdims=True))
            dp_ref[rows, E:2 * E] = (dvg * dgl_ref[rows, :].astype(F32)).astype(BF16)
            return carry

        lax.fori_loop(0, TB // ROW_CHUNK, ln_bwd, 0, unroll=2)

    return _pcall(
        body, name, (T // TB,),
        in_specs=[pl.BlockSpec((TB, E), lambda i: (i, 0)),
                  pl.BlockSpec((TB, E), lambda i: (i, 2)),
                  pl.BlockSpec((TB, E), lambda i: (i, 0)),
                  pl.BlockSpec((TB, E), lambda i: (i, 0)),
                  pl.BlockSpec((TB, E), lambda i: (i, 0)),
                  pl.BlockSpec((TB, 1), lambda i: (i, 0)),
                  _pick(ln_g), _pick(ln_b), _pick(w_s), _pick(b_s)],
        out_specs=[pl.BlockSpec((TB, E3), lambda i: (i, 0)),
                   pl.BlockSpec((G, P, P), lambda i: (0, 0, 0)),
                   pl.BlockSpec((G, P, 1), lambda i: (0, 0, 0)),
                   pl.BlockSpec((1, E), lambda i: (0, 0)),
                   pl.BlockSpec((1, E), lambda i: (0, 0))],
        out_shape=[jax.ShapeDtypeStruct((T, E3), BF16), jax.ShapeDtypeStruct((G, P, P), F32),
                   jax.ShapeDtypeStruct((G, P, 1), F32), jax.ShapeDtypeStruct((1, E), F32),
                   jax.ShapeDtypeStruct((1, E), F32)],
        scratch_shapes=[pltpu.VMEM((TB, E), BF16), pltpu.VMEM((TB, E), F32), pltpu.VMEM((TB, E), F32),
                        pltpu.VMEM((TB, E), BF16), pltpu.VMEM((TB, E), F32)],
        args=(proj, proj, dy, xh, dgl, rstd, ln_g[0], ln_b[0], w_s[0], b_s[0]), comm=comm)


def _b_mix_bwd(proj, dy, o, scale, wg_all, name, comm=None):
    T, E2 = proj.shape
    E = E2 // 2
    NG = len(POOL_WINDOWS)
    GB = E // NG
    TB = min(T, 256)
    nT = T // TB
    RS = wg_all.shape[-2]
    halo_per_tile = TB // HALO

    def body(p_ref, halo_ref, dy_ref, o_ref, sc_ref, wg_ref, dp_ref, dsc_ref, dwg_ref, acc_s, carry_s,
             *sibling_scratch):
        i = pl.program_id(0)
        tile = nT - 1 - i

        @pl.when(i == 0)
        def _():
            acc_s[...] = jnp.zeros_like(acc_s)
            carry_s[...] = jnp.zeros_like(carry_s)
            dsc_ref[...] = jnp.zeros_like(dsc_ref)

        has_history = (tile > 0).astype(F32)
        for g, win in enumerate(POOL_WINDOWS):
            cols = slice(g * GB, (g + 1) * GB)
            inv = _inv_count(tile * TB, TB, win)
            xg = p_ref[:, cols].astype(F32)
            ext = jnp.concatenate([halo_ref[:, cols].astype(F32) * has_history, xg], axis=0)
            pooled = _window_sum_back(ext, win)[HALO:, :] * inv - xg
            z = p_ref[:, E + g * GB:E + (g + 1) * GB].astype(F32)
            sg = _sigmoid(z)
            dyv = dy_ref[:, cols].astype(F32)
            ov = o_ref[:, cols].astype(F32)
            sc = sc_ref[:, cols]
            dmixed = dyv * (z * sg)
            dsc_ref[:, cols] += jnp.sum(dmixed * ov, axis=0, keepdims=True)
            dz = dyv * (ov * sc) * (sg * (1.0 + z * (1.0 - sg)))
            do = (dmixed * sc).astype(BF16)
            acc_s[:, g] += _dot_tn(pooled.astype(BF16), do).reshape(N_DEV, RS, GB)
            dpool = _dot_nt(do, wg_ref[:, g].reshape(GB, GB))
            q = dpool * inv
            ext_q = jnp.concatenate([q, carry_s[:, cols]], axis=0)
            dxb = _window_sum_ahead(ext_q, win)[:TB, :] - dpool
            carry_s[:, cols] = q[:HALO, :]
            dp_ref[:, cols] = dxb.astype(BF16)
            dp_ref[:, E + g * GB:E + (g + 1) * GB] = dz.astype(BF16)

        @pl.when(i == nT - 1)
        def _():
            _sum_with_sibling(lambda k: acc_s[k], dwg_ref, sibling_scratch)

    return _pcall(
        body, name, (nT,),
        in_specs=[pl.BlockSpec((TB, E2), lambda i: (nT - 1 - i, 0)),
                  pl.BlockSpec((HALO, E), lambda i: (jnp.maximum((nT - 1 - i) * halo_per_tile - 1, 0), 0)),
                  pl.BlockSpec((TB, E), lambda i: (nT - 1 - i, 0)),
                  pl.BlockSpec((TB, E), lambda i: (nT - 1 - i, 0)),
                  pl.BlockSpec((1, E), lambda i: (0, 0)),
                  pl.BlockSpec((N_DEV, NG, RS, GB), lambda i: (0, 0, 0, 0))],
        out_specs=[pl.BlockSpec((TB, E2), lambda i: (nT - 1 - i, 0)),
                   pl.BlockSpec((1, E), lambda i: (0, 0)),
                   pl.BlockSpec((CHIPS, NG, RS, GB), lambda i: (0, 0, 0, 0))],
        out_shape=[jax.ShapeDtypeStruct((T, E2), BF16), jax.ShapeDtypeStruct((1, E), F32),
                   jax.ShapeDtypeStruct((CHIPS, NG, RS, GB), BF16)],
        scratch_shapes=[pltpu.VMEM((N_DEV, NG, RS, GB), F32), pltpu.VMEM((HALO, E), F32)]
        + _sibling_scratch((NG, RS, GB)),
        args=(proj, proj, dy, o, scale, wg_all), comm=comm)


def _in_proj_bwd_dx(dproj, w_parts, x, g_row, dxn, name, comm=None):
    T, D = x.shape
    NS = w_parts[0].shape[-1]
    DP = w_parts[0].shape[-2]
    n_parts = len(w_parts)
    TM = min(T, 512)

    def body(dp_ref, *refs):
        w_refs, (x_ref, g_ref, dxn_ref, dx_ref, dg_ref) = refs[:n_parts], refs[n_parts:]

        @pl.when(pl.program_id(0) == 0)
        def _():
            dg_ref[...] = jnp.zeros_like(dg_ref)

        pieces = []
        for w_ref in w_refs:
            piece = _dot_nt(dp_ref[:, 0:NS], w_ref[0])
            for k in range(1, N_DEV):
                piece = piece + _dot_nt(dp_ref[:, k * NS:(k + 1) * NS], w_ref[k])
            pieces.append(piece)
        dh = pieces[0] if n_parts == 1 else jnp.concatenate(pieces, axis=1)
        r, xh = _rms_stats(x_ref[...])
        dg_ref[...] += jnp.sum(dh * xh, axis=0, keepdims=True)
        dx_ref[...] = dxn_ref[...] + _rms_bwd(dh, g_ref[...], r, xh)

    return _pcall(
        body, name, (T // TM,),
        in_specs=[pl.BlockSpec((TM, N_DEV * NS), lambda i: (i, 0))] + [_resident((N_DEV, DP, NS))] * n_parts
        + [pl.BlockSpec((TM, D), lambda i: (i, 0)),
           _pick(g_row),
           pl.BlockSpec((TM, D), lambda i: (i, 0))],
        out_specs=[pl.BlockSpec((TM, D), lambda i: (i, 0)), pl.BlockSpec((1, D), lambda i: (0, 0))],
        out_shape=[jax.ShapeDtypeStruct((T, D), F32), jax.ShapeDtypeStruct((1, D), F32)],
        args=(dproj, *w_parts, x, g_row[0], dxn), comm=comm)


STAGE_SLOTS = 2


def _dw_in(h, dproj, name, comm=None):
    T, D = h.shape
    NS = dproj.shape[1] // N_DEV
    TK = min(T, 2048)
    nK = T // TK

    def body(h_ref, dp_ref, q_ref, acc_s, stage_s, land_s, send_sems, recv_sems):
        k, t = pl.program_id(0), pl.program_id(1)
        c = lax.axis_index("c")

        def to_sibling(q):
            return _to_sibling(stage_s, land_s, send_sems, recv_sems, q % STAGE_SLOTS, q)

        @pl.when(t == 0)
        def _():
            acc_s[...] = jnp.zeros_like(acc_s)

        acc_s[...] += _dot_tn(h_ref[...], dp_ref[...])

        @pl.when(t == nK - 1)
        def _():
            q = k // 2

            @pl.when(k % 2 == c)
            def _():
                q_ref[q] = acc_s[...].astype(BF16)

            @pl.when(k % 2 != c)
            def _():
                @pl.when(q >= STAGE_SLOTS)
                def _():
                    to_sibling(q - STAGE_SLOTS).wait_send()

                stage_s[q % STAGE_SLOTS] = acc_s[...].astype(BF16)
                to_sibling(q).start()

        @pl.when((k == N_DEV - 1) & (t == nK - 1))
        def _():
            for q in range(CHIPS - STAGE_SLOTS, CHIPS):
                to_sibling(q).wait_send()
            for q in range(CHIPS):
                to_sibling(q).wait_recv()
                q_ref[q] = (q_ref[q].astype(F32) + land_s[q].astype(F32)).astype(BF16)

    return _pcall(
        body, name, (N_DEV, nK),
        in_specs=[pl.BlockSpec((TK, D), lambda k, t: (t, 0)), pl.BlockSpec((TK, NS), lambda k, t: (t, k))],
        out_specs=[pl.BlockSpec((CHIPS, D, NS), lambda k, t: (0, 0, 0))],
        out_shape=[jax.ShapeDtypeStruct((CHIPS, D, NS), BF16)],
        scratch_shapes=[pltpu.VMEM((D, NS), F32)] + _sibling_scratch((D, NS), STAGE_SLOTS),
        args=(h, dproj), comm=comm)


def _reduce_adam(recvs, w, m, v, name):
    L, R, C = w.shape
    assert len(recvs) == L
    senders = recvs[0].shape[0]
    TR = R
    for cand in (256, 128, 64, 32, 16):
        if R % cand == 0 and R > cand:
            TR = cand
            break
    nR = R // TR
    c1 = 1.0 - ADAM_B1 ** ADAM_STEP
    c2 = 1.0 - ADAM_B2 ** ADAM_STEP

    def body(*refs):
        recv_refs = refs[:L]
        w_ref, m_ref, v_ref, g_ref, d_ref, nm_ref, nv_ref, g_s = refs[L:]
        layer = pl.program_id(0)
        for l in range(L):
            @pl.when(layer == l)
            def _(l=l):
                acc = recv_refs[l][0].astype(F32)
                for j in range(1, senders):
                    acc = acc + recv_refs[l][j].astype(F32)
                g_s[...] = acc

        g = g_s[...]
        g_ref[...] = g
        nm = ADAM_B1 * m_ref[...] + (1.0 - ADAM_B1) * g
        nv = ADAM_B2 * v_ref[...] + (1.0 - ADAM_B2) * (g * g)
        nm_ref[...] = nm
        nv_ref[...] = nv
        d_ref[...] = -ADAM_LR * ((nm / c1) / (jnp.sqrt(nv / c2) + ADAM_EPS) + ADAM_WD * w_ref[...])

    def recv_spec(l):
        def index(layer, t):
            before = jnp.where(layer < l, 0, nR - 1)
            return (0, jnp.where(layer == l, t, before), 0)
        return pl.BlockSpec((senders, TR, C), index)

    wspec = pl.BlockSpec((None, TR, C), lambda layer, t: (layer, t, 0))
    out = jax.ShapeDtypeStruct((L, R, C), F32)
    return _pcall(
        body, name, (L, nR),
        in_specs=[recv_spec(l) for l in range(L)] + [wspec] * 3,
        out_specs=[wspec] * 4, out_shape=[out] * 4,
        scratch_shapes=[pltpu.VMEM((TR, C), F32)],
        args=(*recvs, w, m, v))[0]


def _adam_replicated(gathered, w, m, v, name):
    n_params = len(w)
    layers = [len(g) for g in gathered]
    flat = [g for per_param in gathered for g in per_param]
    c1 = 1.0 - ADAM_B1 ** ADAM_STEP
    c2 = 1.0 - ADAM_B2 ** ADAM_STEP

    def body(*refs):
        g_refs, refs = refs[:len(flat)], refs[len(flat):]
        w_refs, m_refs, v_refs = refs[:n_params], refs[n_params:2 * n_params], refs[2 * n_params:3 * n_params]
        out_refs = refs[3 * n_params:]
        at = 0
        for n in range(n_params):
            g_out, d_out, nm_out, nv_out = out_refs[4 * n:4 * n + 4]
            for l in range(layers[n]):
                g_ref = g_refs[at]
                at += 1
                g = g_ref[0]
                for s in range(1, N_DEV):
                    g = g + g_ref[s]
                nm = ADAM_B1 * m_refs[n][l] + (1.0 - ADAM_B1) * g
                nv = ADAM_B2 * v_refs[n][l] + (1.0 - ADAM_B2) * (g * g)
                g_out[l] = g
                nm_out[l] = nm
                nv_out[l] = nv
                d_out[l] = -ADAM_LR * ((nm / c1) / (jnp.sqrt(nv / c2) + ADAM_EPS) + ADAM_WD * w_refs[n][l])

    vmem = pl.BlockSpec(memory_space=pltpu.VMEM)
    args = [*flat, *w, *m, *v]
    outs = pl.pallas_call(
        body, name=name, in_specs=[vmem] * len(args), out_specs=[vmem] * (4 * n_params),
        out_shape=[jax.ShapeDtypeStruct(a.shape, F32) for a in w for _ in range(4)],
        compiler_params=_params(),
    )(*args)
    return [outs[4 * n:4 * n + 4] for n in range(n_params)]


A_W_IN_PARTS = 4
PACK_LANES = 128
PACK_ROWS_MULTIPLE = 256


def _pack(arrays):
    flat = jnp.concatenate([a.reshape(-1) for a in arrays])
    tile = PACK_LANES * PACK_ROWS_MULTIPLE
    padded = -(-flat.shape[0] // tile) * tile
    return jnp.pad(flat, (0, padded - flat.shape[0])).reshape(1, padded // PACK_LANES, PACK_LANES)


def _unpack(packed, like):
    flat = packed.reshape(-1)
    out, at = [], 0
    for a in like:
        out.append(flat[at:at + a.size].reshape(a.shape))
        at += a.size
    return out


def kernel(x, norm_pre, norm_post, a_w_in, a_ln_g, a_ln_b, a_w_s, a_b_s, a_w_out, b_w_in, b_w_grp, b_scale, b_w_out, loss_target, m_norm_pre, m_norm_post, m_a_w_in, m_a_ln_g, m_a_ln_b, m_a_w_s, m_a_b_s, m_a_w_out, m_b_w_in, m_b_w_grp, m_b_scale, m_b_w_out, v_norm_pre, v_norm_post, v_a_w_in, v_a_ln_g, v_a_ln_b, v_a_w_s, v_a_b_s, v_a_w_out, v_b_w_in, v_b_w_grp, v_b_scale, v_b_w_out):
    weights = dict(norm_pre=norm_pre, norm_post=norm_post, a_w_in=a_w_in, a_ln_g=a_ln_g, a_ln_b=a_ln_b, a_w_s=a_w_s,
                   a_b_s=a_b_s, a_w_out=a_w_out, b_w_in=b_w_in, b_w_grp=b_w_grp, b_scale=b_scale, b_w_out=b_w_out)
    mom_m = dict(norm_pre=m_norm_pre, norm_post=m_norm_post, a_w_in=m_a_w_in, a_ln_g=m_a_ln_g, a_ln_b=m_a_ln_b,
                 a_w_s=m_a_w_s, a_b_s=m_a_b_s, a_w_out=m_a_w_out, b_w_in=m_b_w_in, b_w_grp=m_b_w_grp,
                 b_scale=m_b_scale, b_w_out=m_b_w_out)
    mom_v = dict(norm_pre=v_norm_pre, norm_post=v_norm_post, a_w_in=v_a_w_in, a_ln_g=v_a_ln_g, a_ln_b=v_a_ln_b,
                 a_w_s=v_a_w_s, a_b_s=v_a_b_s, a_w_out=v_a_w_out, b_w_in=v_b_w_in, b_w_grp=v_b_w_grp,
                 b_scale=v_b_scale, b_w_out=v_b_w_out)
    names = list(weights)

    depth = norm_pre.shape[0]
    x0 = x[0]
    target = loss_target[0]
    T, D = x0.shape
    E = a_ln_g.shape[1]
    G, P = A_GROUPS, GMLP_BLOCK
    pre3, post3 = norm_pre.reshape(depth, 1, D), norm_post.reshape(depth, 1, D)
    ln_g3, ln_b3 = a_ln_g.reshape(-1, 1, E), a_ln_b.reshape(-1, 1, E)
    b_s4 = a_b_s.reshape(-1, G, P, 1)

    def shards_of(i):
        j = i // 2
        if i % 2 == 0:
            w = a_w_in[j].astype(BF16)
            rows = w.shape[0] // A_W_IN_PARTS
            parts = {f"w_in_{p}": w[p * rows:(p + 1) * rows] for p in range(A_W_IN_PARTS)}
            return dict(**parts, w_out=a_w_out[j].astype(BF16))
        return dict(w_in=b_w_in[j].astype(BF16), w_out=b_w_out[j].astype(BF16), grp=b_w_grp[j].astype(BF16))

    shard = [shards_of(i) for i in range(depth)]
    full = [dict() for _ in range(depth)]

    def gather_into(keys, got):
        for (i, key), arr in zip(keys, got):
            full[i][key] = arr

    def w_in_of(i):
        return [(i, k) for k in shard[i] if k.startswith("w_in")]

    def rest_of(i):
        return [(i, k) for k in shard[i] if not k.startswith("w_in")]

    def gather_of(keys):
        return _Gather([shard[a][k] for a, k in keys]) if keys else None

    first = _comm_only(_Gather([shard[0][k] for _, k in w_in_of(0)] + [b_scale]), "gather_first")
    gather_into(w_in_of(0), first)
    scale_full = jnp.transpose(first[-1], (1, 0, 2)).reshape(b_scale.shape[0], 1, E)

    saved = []
    xi = x0
    for i in range(depth):
        j = i // 2
        g_pre, g_post = (pre3, i), (post3, i)
        keys_in, keys_mix, keys_out = [], [], []
        if i % 2 == 0:
            keys_mix = [(0, "w_out")] if i == 0 else []
            if i + 1 < depth:
                keys_in = w_in_of(i + 1)
                keys_mix = keys_mix + rest_of(i + 1)
            if i + 2 < depth:
                keys_mix = keys_mix + [(i + 2, "w_in_0")]
                keys_out = [(i + 2, "w_in_1")]
        elif i + 1 < depth:
            keys_in = w_in_of(i + 1)[2:]
            keys_mix = rest_of(i + 1)
        kind = "a" if i % 2 == 0 else "b"
        (proj, h), got = _in_proj_fwd(xi, g_pre, [full[i][k] for _, k in w_in_of(i)], f"{kind}_in_fwd_{i}",
                                      gather_of(keys_in))
        gather_into(keys_in, got)
        if i % 2 == 0:
            (y, *o), got = _a_mix_fwd(proj, (ln_g3, j), (ln_b3, j), (a_w_s, j), (b_s4, j),
                                      f"a_mix_fwd_{i}", gather_of(keys_mix))
        else:
            (y, o), got = _b_mix_fwd(proj, scale_full[j], full[i]["grp"], f"b_mix_fwd_{i}", gather_of(keys_mix))
        gather_into(keys_mix, got)
        (x_next, out), got = _out_proj_fwd(y, full[i]["w_out"], xi, g_post, f"{kind}_out_fwd_{i}",
                                           gather_of(keys_out))
        gather_into(keys_out, got)
        saved.append((xi, h, proj, y, out, o))
        xi = x_next

    dx, loss_row = _loss_head(xi, target, "loss_head")

    n_a, n_b = a_ln_g.shape[0], b_scale.shape[0]
    d_pre, d_post = [None] * depth, [None] * depth
    recv = {"a_w_in": [None] * n_a, "a_w_out": [None] * n_a, "b_w_in": [None] * n_b,
            "b_w_grp": [None] * n_b, "b_w_out": [None] * n_b, "b_scale": [None] * n_b}
    small_a = [[None] * 4 for _ in range(n_a)]
    kinds = ("chips", "devices", "gather")

    def carried(items):
        of = {kind: [it[3] for it in items if it[0] == kind] for kind in kinds}
        comms = (([_Exchange(of["chips"], chips_only=True)] if of["chips"] else [])
                 + ([_Exchange(of["devices"])] if of["devices"] else [])
                 + ([_Gather(of["gather"])] if of["gather"] else []))
        return None if not comms else comms[0] if len(comms) == 1 else _Together(comms)

    def received(items, got):
        ordered = [it for kind in kinds for it in items if it[0] == kind]
        for it, arr in zip(ordered, got):
            if it[0] == "gather":
                small_a[it[2][0]][it[2][1]] = arr
            else:
                recv[it[1]][it[2]] = arr

    pending = []
    small_pending = []
    for i in reversed(range(depth)):
        j = i // 2
        xi, h, proj, y, out, o = saved[i]
        g_pre, g_post = (pre3, i), (post3, i)
        if i % 2 == 0:
            (dy, dw_out, d_post[i]), _ = _out_proj_bwd(dx, out, g_post, full[i]["w_out"], y, f"a_out_bwd_{i}")
            items, pending = pending, []
            (dproj, d_w_s, dbs, d_ln_g, d_ln_b), got = _a_mix_bwd(
                proj, dy, *o, (ln_g3, j), (ln_b3, j), (a_w_s, j), (b_s4, j),
                f"a_mix_bwd_{i}", carried(items))
            received(items, got)
            small_pending += [("gather", "small", (j, n), part) for n, part in enumerate((d_w_s, dbs, d_ln_g, d_ln_b))]
            items = []
            if i == 0:
                items, small_pending = [("chips", "a_w_out", j, dw_out)] + small_pending, []
            (dw_in,), got = _dw_in(h, dproj, f"a_dw_in_{i}", carried(items))
            received(items, got)
            items = [("chips", "a_w_in", j, dw_in)] if i == 0 else [("chips", "a_w_out", j, dw_out)]
            (dx, d_pre[i]), got = _in_proj_bwd_dx(dproj, [full[i][k] for _, k in w_in_of(i)], xi, g_pre, dx,
                                                  f"a_in_bwd_{i}", carried(items))
            received(items, got)
            if i > 0:
                pending.append(("chips", "a_w_in", j, dw_in))
        else:
            items, small_pending = small_pending, []
            (dy, dw_out, d_post[i]), got = _out_proj_bwd(dx, out, g_post, full[i]["w_out"], y, f"b_out_bwd_{i}",
                                                        carried(items))
            received(items, got)
            items, pending = pending, []
            (dproj, dsc, dw_grp), got = _b_mix_bwd(proj, dy, o, scale_full[j], full[i]["grp"], f"b_mix_bwd_{i}",
                                                  carried(items))
            received(items, got)
            (dx, d_pre[i]), _ = _in_proj_bwd_dx(dproj, [full[i][k] for _, k in w_in_of(i)], xi, g_pre, dx,
                                                f"b_in_bwd_{i}")
            (dw_in,), _ = _dw_in(h, dproj, f"b_dw_in_{i}")
            pending += [("chips", "b_w_out", j, dw_out), ("chips", "b_w_grp", j, dw_grp), ("chips", "b_w_in", j, dw_in),
                        ("devices", "b_scale", j, dsc.reshape(N_DEV, 1, E // N_DEV))]
    assert not pending and not small_pending

    gathered = _comm_only(_Gather([_pack([*d_pre, *d_post, loss_row[:, :1]])[0]]), "gather_norm_grads")
    results = {k: [None] * 4 for k in names}
    no_state = jnp.zeros((1, 1), F32)
    norm_like = [norm_pre, norm_post, no_state]
    outs = _reduce_adam([gathered[-1]], _pack(norm_like), _pack([m_norm_pre, m_norm_post, no_state]),
                        _pack([v_norm_pre, v_norm_post, no_state]), "adam_norms")
    for q, packed in enumerate(outs):
        results["norm_pre"][q], results["norm_post"][q], summed = _unpack(packed, norm_like)
        if q == 0:
            loss = summed[0, 0]
    a_small = ("a_w_s", "a_b_s", "a_ln_g", "a_ln_b")
    forms = (a_w_s.shape, b_s4.shape, ln_g3.shape, ln_b3.shape)
    outs = _adam_replicated([[small_a[j][n] for j in range(n_a)] for n in range(len(a_small))],
                            [weights[k].reshape(f) for k, f in zip(a_small, forms)],
                            [mom_m[k].reshape(f) for k, f in zip(a_small, forms)],
                            [mom_v[k].reshape(f) for k, f in zip(a_small, forms)], "adam_small")
    for k, four in zip(a_small, outs):
        results[k] = [o_.reshape(weights[k].shape) for o_ in four]

    def shard_view(a):
        return a.reshape(a.shape[0], -1, a.shape[-1])

    for k in ("a_w_in", "a_w_out", "b_w_in", "b_w_grp", "b_w_out"):
        w3 = shard_view(weights[k])
        recvs = [r.reshape(r.shape[0], w3.shape[1], w3.shape[2]) for r in recv[k]]
        outs = _reduce_adam(recvs, w3, shard_view(mom_m[k]), shard_view(mom_v[k]), f"adam_{k}")
        results[k] = [o_.reshape(weights[k].shape) for o_ in outs]
    sc_recv = jnp.concatenate(recv["b_scale"], axis=1)
    outs = _reduce_adam([sc_recv], b_scale[None], m_b_scale[None], v_b_scale[None], "adam_b_scale")
    results["b_scale"] = [o_[0] for o_ in outs]

    grad_x = dx[None]
    return (loss, grad_x, *[results[k][0] for k in names], *[results[k][1] for k in names],
            *[results[k][2] for k in names], *[results[k][3] for k in names])
```

```python
import jax
import jax.numpy as jnp
from jax import lax
from jax.experimental import pallas as pl
from jax.experimental.pallas import tpu as pltpu

F32 = jnp.float32
BF16 = jnp.bfloat16
MESH = pl.DeviceIdType.MESH

N_DEV = 8
EPS = 1e-6
CHUNK = 64
GMLP_BLOCK = 128
A_GROUPS = 8
POOL_WINDOWS = (2, 4, 8, 16)
HALO = 16
SUBLANES = 8
ADAM_LR = 0.001
ADAM_B1 = 0.9
ADAM_B2 = 0.999
ADAM_EPS = 1e-08
ADAM_WD = 0.01
ADAM_STEP = 10
GELU_C = 0.7978845608028654
GELU_A = 0.044715
ROW_CHUNK = 16
VMEM_LIMIT_BYTES = 56 * 1024 * 1024


def _params(**kw):
    return pltpu.CompilerParams(vmem_limit_bytes=VMEM_LIMIT_BYTES, **kw)


def _gelu(x):
    return 0.5 * x * (1.0 + jnp.tanh(GELU_C * (x + GELU_A * (x * x * x))))


def _gelu_and_grad(x):
    x2 = x * x
    t = jnp.tanh(GELU_C * (x + GELU_A * (x2 * x)))
    val = 0.5 * x * (1.0 + t)
    grad = 0.5 * (1.0 + t) + 0.5 * x * (1.0 - t * t) * (GELU_C * (1.0 + 3.0 * GELU_A * x2))
    return val, grad


def _sigmoid(z):
    return 0.5 * jnp.tanh(0.5 * z) + 0.5


def _dot(a, b):
    return jnp.dot(a, b, preferred_element_type=F32)


def _dot_nt(a, b):
    return lax.dot_general(a, b, (((1,), (1,)), ((), ())), preferred_element_type=F32)


def _dot_tn(a, b):
    return lax.dot_general(a, b, (((0,), (0,)), ((), ())), preferred_element_type=F32)


def _rms_stats(xf):
    r = lax.rsqrt(jnp.mean(xf * xf, axis=-1, keepdims=True) + EPS)
    return r, xf * r


def _rms_bwd(dy, g, r, xh):
    dxh = dy * g
    return r * (dxh - xh * jnp.mean(dxh * xh, axis=-1, keepdims=True))


def _resident(shape):
    return pl.BlockSpec(shape, lambda *_: (0,) * len(shape), pipeline_mode=pl.Buffered(1))


def _pick(stacked):
    arr, index = stacked
    return pl.BlockSpec((None,) + arr.shape[1:], lambda *_: (index,) + (0,) * (arr.ndim - 1))


def _spatial_mask(transposed=False):
    p = lax.broadcasted_iota(jnp.int32, (GMLP_BLOCK, GMLP_BLOCK), 0)
    q = lax.broadcasted_iota(jnp.int32, (GMLP_BLOCK, GMLP_BLOCK), 1)
    if transposed:
        p, q = q, p
    return (q // CHUNK) <= (p // CHUNK)


def _position():
    x, y, c = lax.axis_index("x"), lax.axis_index("y"), lax.axis_index("c")
    return x, y, c


def _comm_scratch(n):
    return [pltpu.SemaphoreType.DMA((n, 7)), pltpu.SemaphoreType.DMA((n, 7)), pltpu.SemaphoreType.DMA((n,))]


class _Gather:
    def __init__(self, arrs):
        self.inputs = list(arrs)
        self.out_shape = [jax.ShapeDtypeStruct((N_DEV,) + a.shape, a.dtype) for a in arrs]
        self.scratch = _comm_scratch(len(arrs))

    def _plan(self, ins, outs, sems):
        send_sems, recv_sems, local_sems = sems
        n = len(ins)
        x, y, c = _position()
        sibling = (x, y, 1 - c)
        chips = [(1 - x, y), (x, 1 - y), (1 - x, 1 - y)]

        def index(px, py, pc):
            return 4 * px + 2 * py + pc

        def copy(a, k, block, to, src=None):
            return pltpu.make_async_remote_copy(
                src_ref=outs[a].at[block] if src is None else src, dst_ref=outs[a].at[block],
                send_sem=send_sems.at[a, k], recv_sem=recv_sems.at[a, k], device_id=to, device_id_type=MESH)

        me = index(x, y, c)
        own = [pltpu.make_async_copy(ins[a], outs[a].at[me], local_sems.at[a]) for a in range(n)]
        first = []
        for a in range(n):
            first.append(copy(a, 0, me, sibling, src=ins[a]))
            for j, chip in enumerate(chips):
                first.append(copy(a, 1 + j, me, (*chip, c), src=ins[a]))
        return n, (x, y, c), sibling, chips, index, copy, own, first

    def start(self, ins, outs, sems):
        _, _, _, _, _, _, own, first = self._plan(ins, outs, sems)
        for cp in own + first:
            cp.start()

    def middle(self, ins, outs, sems):
        n, me, sibling, chips, index, copy, _, _ = self._plan(ins, outs, sems)
        for j, chip in enumerate(chips):
            for a in range(n):
                copy(a, 1 + j, index(*chip, me[2]), me).wait_recv()
                copy(a, 4 + j, index(*chip, me[2]), sibling).start()

    def finish(self, ins, outs, sems):
        n, me, sibling, chips, index, copy, own, first = self._plan(ins, outs, sems)
        c = me[2]
        passed = [copy(a, 4 + j, index(*chip, c), sibling) for j, chip in enumerate(chips) for a in range(n)]
        for a in range(n):
            copy(a, 0, index(me[0], me[1], 1 - c), me).wait_recv()
        for j, chip in enumerate(chips):
            for a in range(n):
                copy(a, 4 + j, index(*chip, 1 - c), me).wait_recv()
        for cp in first + passed:
            cp.wait_send()
        for cp in own:
            cp.wait()


class _Exchange:
    def __init__(self, arrs, chips_only=False):
        self.inputs = list(arrs)
        self.chips_only = chips_only
        self.out_shape = [jax.ShapeDtypeStruct(a.shape, a.dtype) for a in arrs]
        self.scratch = _comm_scratch(len(arrs))

    def _plan(self, ins, outs, sems):
        send_sems, recv_sems, local_sems = sems
        n = len(ins)
        x, y, c = _position()
        scale = 1 if self.chips_only else 2
        me = 2 * x + y if self.chips_only else 4 * x + 2 * y + c
        own = [pltpu.make_async_copy(ins[a].at[me], outs[a].at[me], local_sems.at[a]) for a in range(n)]
        sends, recvs = [], []
        for r in range(1, 4 * scale):
            px = 1 - x if r & (2 * scale) else x
            py = 1 - y if r & scale else y
            pc = 1 - c if (r & 1 and not self.chips_only) else c
            peer = 2 * px + py if self.chips_only else 4 * px + 2 * py + pc
            for a in range(n):
                sends.append(pltpu.make_async_remote_copy(
                    src_ref=ins[a].at[peer], dst_ref=outs[a].at[me],
                    send_sem=send_sems.at[a, r - 1], recv_sem=recv_sems.at[a, r - 1],
                    device_id=(px, py, pc), device_id_type=MESH))
                recvs.append(pltpu.make_async_remote_copy(
                    src_ref=ins[a].at[peer], dst_ref=outs[a].at[peer],
                    send_sem=send_sems.at[a, r - 1], recv_sem=recv_sems.at[a, r - 1],
                    device_id=(px, py, pc), device_id_type=MESH))
        return own, sends, recvs

    def start(self, ins, outs, sems):
        own, sends, _ = self._plan(ins, outs, sems)
        for cp in own + sends:
            cp.start()

    def middle(self, ins, outs, sems):
        pass

    def finish(self, ins, outs, sems):
        own, sends, recvs = self._plan(ins, outs, sems)
        for cp in recvs:
            cp.wait_recv()
        for cp in sends:
            cp.wait_send()
        for cp in own:
            cp.wait()


CHIPS = N_DEV // 2


def _sibling_scratch(slab_shape, stage_slots=CHIPS):
    return [pltpu.VMEM((stage_slots,) + tuple(slab_shape), BF16), pltpu.VMEM((CHIPS,) + tuple(slab_shape), BF16),
            pltpu.SemaphoreType.DMA((CHIPS,)), pltpu.SemaphoreType.DMA((CHIPS,))]


def _to_sibling(stage_s, land_s, send_sems, recv_sems, slot, q):
    x, y, c = _position()
    return pltpu.make_async_remote_copy(
        src_ref=stage_s.at[slot], dst_ref=land_s.at[q], send_sem=send_sems.at[q], recv_sem=recv_sems.at[q],
        device_id=(x, y, 1 - c), device_id_type=MESH)


def _sum_with_sibling(slab_of, q_ref, sibling_scratch):
    stage_s, land_s, send_sems, recv_sems = sibling_scratch
    c = lax.axis_index("c")
    for q in range(CHIPS):
        stage_s[q] = slab_of(2 * q + 1 - c).astype(BF16)
        _to_sibling(stage_s, land_s, send_sems, recv_sems, q, q).start()
    for q in range(CHIPS):
        _to_sibling(stage_s, land_s, send_sems, recv_sems, q, q).wait_recv()
        q_ref[q] = (slab_of(2 * q + c) + land_s[q].astype(F32)).astype(BF16)
    for q in range(CHIPS):
        _to_sibling(stage_s, land_s, send_sems, recv_sems, q, q).wait_send()


class _Together:
    def __init__(self, comms):
        self.comms = list(comms)
        self.inputs = [a for c in self.comms for a in c.inputs]
        self.out_shape = [s for c in self.comms for s in c.out_shape]
        self.scratch = [s for c in self.comms for s in c.scratch]

    def _each(self, ins, outs, sems):
        at = 0
        for k, c in enumerate(self.comms):
            n = len(c.inputs)
            yield c, ins[at:at + n], outs[at:at + n], sems[3 * k:3 * k + 3]
            at += n

    def start(self, ins, outs, sems):
        for c, i, o, s in self._each(ins, outs, sems):
            c.start(i, o, s)

    def middle(self, ins, outs, sems):
        for c, i, o, s in self._each(ins, outs, sems):
            c.middle(i, o, s)

    def finish(self, ins, outs, sems):
        for c, i, o, s in self._each(ins, outs, sems):
            c.finish(i, o, s)


def _comm_only(comm, name):
    n = len(comm.inputs)

    def body(*refs):
        ins, outs, sems = refs[:n], refs[n:2 * n], refs[2 * n:]
        comm.start(ins, outs, sems)
        comm.middle(ins, outs, sems)
        comm.finish(ins, outs, sems)

    any_spec = pl.BlockSpec(memory_space=pl.ANY)
    return pl.pallas_call(
        body, name=name, out_shape=comm.out_shape, in_specs=[any_spec] * n, out_specs=[any_spec] * n,
        scratch_shapes=comm.scratch, compiler_params=pltpu.CompilerParams(has_side_effects=True),
    )(*comm.inputs)


def _pcall(body, name, grid, in_specs, out_specs, out_shape, args, scratch_shapes=(), comm=None):
    in_specs, out_specs, out_shape, scratch_shapes = list(in_specs), list(out_specs), list(out_shape), list(scratch_shapes)
    if comm is None:
        outs = pl.pallas_call(body, name=name, grid=grid, in_specs=in_specs, out_specs=out_specs, out_shape=out_shape,
                              scratch_shapes=scratch_shapes, compiler_params=_params())(*args)
        return list(outs), []
    n_in, n_out, n_scr, n_c = len(in_specs), len(out_specs), len(scratch_shapes), len(comm.inputs)

    def carrying(*refs):
        ins, refs = refs[:n_in], refs[n_in:]
        c_ins, refs = refs[:n_c], refs[n_c:]
        outs, refs = refs[:n_out], refs[n_out:]
        c_outs, refs = refs[:n_c], refs[n_c:]
        scr, sems = refs[:n_scr], refs[n_scr:]
        step, steps = 0, 1
        for d, size in enumerate(grid):
            step = step * size + pl.program_id(d)
            steps *= size

        @pl.when(step == 0)
        def _():
            comm.start(c_ins, c_outs, sems)

        body(*ins, *outs, *scr)

        @pl.when(step == max(steps - 2, 0))
        def _():
            comm.middle(c_ins, c_outs, sems)

        @pl.when(step == steps - 1)
        def _():
            comm.finish(c_ins, c_outs, sems)

    any_spec = pl.BlockSpec(memory_space=pl.ANY)
    outs = pl.pallas_call(
        carrying, name=name, grid=grid, in_specs=in_specs + [any_spec] * n_c, out_specs=out_specs + [any_spec] * n_c,
        out_shape=out_shape + comm.out_shape, scratch_shapes=scratch_shapes + comm.scratch,
        compiler_params=_params(has_side_effects=True),
    )(*args, *comm.inputs)
    return list(outs[:n_out]), list(outs[n_out:])


def _in_proj_fwd(x, g_row, w_parts, name, comm=None):
    T, D = x.shape
    NS = w_parts[0].shape[-1]
    DP = w_parts[0].shape[-2]
    n_parts = len(w_parts)
    assert DP * n_parts == D
    TM = min(T, 512)

    def body(x_ref, g_ref, *refs):
        w_refs, (proj_ref, h_ref) = refs[:n_parts], refs[n_parts:]
        _, xh = _rms_stats(x_ref[...])
        h = (xh * g_ref[...]).astype(BF16)
        h_ref[...] = h
        for k in range(N_DEV):
            acc = _dot(h[:, 0:DP], w_refs[0][k])
            for p in range(1, n_parts):
                acc = acc + _dot(h[:, p * DP:(p + 1) * DP], w_refs[p][k])
            proj_ref[:, k * NS:(k + 1) * NS] = acc.astype(BF16)

    return _pcall(
        body, name, (T // TM,),
        in_specs=[pl.BlockSpec((TM, D), lambda i: (i, 0)), _pick(g_row)] + [_resident((N_DEV, DP, NS))] * n_parts,
        out_specs=[pl.BlockSpec((TM, N_DEV * NS), lambda i: (i, 0)),
                   pl.BlockSpec((TM, D), lambda i: (i, 0))],
        out_shape=[jax.ShapeDtypeStruct((T, N_DEV * NS), BF16), jax.ShapeDtypeStruct((T, D), BF16)],
        args=(x, g_row[0], *w_parts), comm=comm)


def _a_mix_fwd(proj, ln_g, ln_b, w_s, b_s, name, comm=None):
    T, E3 = proj.shape
    E = E3 // 3
    G, P = A_GROUPS, GMLP_BLOCK
    GD = E // G
    TB = min(T, 512)

    def body(p_ref, lg_ref, lb_ref, ws_ref, bs_ref, y_ref, xh_ref, dgl_ref, rstd_ref, v_s, us_s):
        def norm_chunk(ci, carry):
            rows = pl.ds(pl.multiple_of(ci * ROW_CHUNK, ROW_CHUNK), ROW_CHUNK)
            vg, dgl = _gelu_and_grad(p_ref[rows, E:2 * E].astype(F32))
            dgl_ref[rows, :] = dgl.astype(BF16)
            xc = vg - jnp.mean(vg, axis=-1, keepdims=True)
            rstd = lax.rsqrt(jnp.mean(xc * xc, axis=-1, keepdims=True) + EPS)
            rstd_ref[rows, :] = rstd
            xh = xc * rstd
            xh_ref[rows, :] = xh.astype(BF16)
            v_s[rows, :] = (xh * lg_ref[...] + lb_ref[...]).astype(BF16)
            return carry

        def gate_chunk(ci, carry):
            rows = pl.ds(pl.multiple_of(ci * ROW_CHUNK, ROW_CHUNK), ROW_CHUNK)
            z = p_ref[rows, 2 * E:3 * E].astype(F32)
            us_s[rows, :] = _gelu(p_ref[rows, 0:E].astype(F32)) * (z * _sigmoid(z))
            return carry

        lax.fori_loop(0, TB // ROW_CHUNK, norm_chunk, 0, unroll=2)
        lax.fori_loop(0, TB // ROW_CHUNK, gate_chunk, 0, unroll=2)
        mask = _spatial_mask()
        for g in range(G):
            wm = jnp.where(mask, ws_ref[g], 0.0).astype(BF16)
            cols = slice(g * GD, (g + 1) * GD)
            for b in range(TB // P):
                rows = slice(b * P, (b + 1) * P)
                mixed = _dot(wm, v_s[rows, cols]) + bs_ref[g]
                y_ref[rows, cols] = (us_s[rows, cols] * mixed).astype(BF16)

    return _pcall(
        body, name, (T // TB,),
        in_specs=[pl.BlockSpec((TB, E3), lambda i: (i, 0)),
                  _pick(ln_g), _pick(ln_b), _pick(w_s), _pick(b_s)],
        out_specs=[pl.BlockSpec((TB, E), lambda i: (i, 0)), pl.BlockSpec((TB, E), lambda i: (i, 0)),
                   pl.BlockSpec((TB, E), lambda i: (i, 0)), pl.BlockSpec((TB, 1), lambda i: (i, 0))],
        out_shape=[jax.ShapeDtypeStruct((T, E), BF16), jax.ShapeDtypeStruct((T, E), BF16),
                   jax.ShapeDtypeStruct((T, E), BF16), jax.ShapeDtypeStruct((T, 1), F32)],
        scratch_shapes=[pltpu.VMEM((TB, E), BF16), pltpu.VMEM((TB, E), F32)],
        args=(proj, ln_g[0], ln_b[0], w_s[0], b_s[0]), comm=comm)


def _window_sum_back(ext, win):
    s, k = ext, 1
    while k < win:
        s = s + pltpu.roll(s, k, axis=0)
        k *= 2
    return s


def _window_sum_ahead(ext, win):
    n = ext.shape[0]
    s, k = ext, 1
    while k < win:
        s = s + pltpu.roll(s, n - k, axis=0)
        k *= 2
    return s


def _inv_count(t0, rows, win):
    t1 = t0 + 1 + lax.broadcasted_iota(jnp.int32, (rows, 1), 0)
    return 1.0 / jnp.minimum(t1, win).astype(F32)


def _b_mix_fwd(proj, scale, wg_all, name, comm=None):
    T, E2 = proj.shape
    E = E2 // 2
    NG = len(POOL_WINDOWS)
    GB = E // NG
    TB = min(T, 256)
    RS = wg_all.shape[-2]

    def body(p_ref, sc_ref, wg_ref, y_ref, o_ref, carry_s):
        i = pl.program_id(0)

        @pl.when(i == 0)
        def _():
            carry_s[...] = jnp.zeros_like(carry_s)

        for g, win in enumerate(POOL_WINDOWS):
            cols = slice(g * GB, (g + 1) * GB)
            xg = p_ref[:, cols].astype(F32)
            ext = jnp.concatenate([carry_s[:, cols], xg], axis=0)
            pooled = _window_sum_back(ext, win)[HALO:, :] * _inv_count(i * TB, TB, win) - xg
            carry_s[:, cols] = xg[TB - HALO:, :]
            o = _dot(pooled.astype(BF16), wg_ref[:, g].reshape(GB, GB))
            o_ref[:, cols] = o.astype(BF16)
            z = p_ref[:, E + g * GB:E + (g + 1) * GB].astype(F32)
            y_ref[:, cols] = ((o * sc_ref[:, cols]) * (z * _sigmoid(z))).astype(BF16)

    return _pcall(
        body, name, (T // TB,),
        in_specs=[pl.BlockSpec((TB, E2), lambda i: (i, 0)),
                  pl.BlockSpec((1, E), lambda i: (0, 0)),
                  pl.BlockSpec((N_DEV, NG, RS, GB), lambda i: (0, 0, 0, 0))],
        out_specs=[pl.BlockSpec((TB, E), lambda i: (i, 0)), pl.BlockSpec((TB, E), lambda i: (i, 0))],
        out_shape=[jax.ShapeDtypeStruct((T, E), BF16), jax.ShapeDtypeStruct((T, E), BF16)],
        scratch_shapes=[pltpu.VMEM((HALO, E), F32)],
        args=(proj, scale, wg_all), comm=comm)


def _out_proj_fwd(y, w_all, x, g_row, name, comm=None):
    T, E = y.shape
    D = x.shape[1]
    ES = w_all.shape[-2]
    TM = min(T, 512)

    def body(y_ref, w_ref, x_ref, g_ref, xn_ref, out_ref):
        o = _dot(y_ref[...], w_ref[...].reshape(E, D))
        out_ref[...] = o
        _, oh = _rms_stats(o)
        xn_ref[...] = x_ref[...] + oh * g_ref[...]

    return _pcall(
        body, name, (T // TM,),
        in_specs=[pl.BlockSpec((TM, E), lambda i: (i, 0)),
                  pl.BlockSpec((N_DEV, ES, D), lambda i: (0, 0, 0)),
                  pl.BlockSpec((TM, D), lambda i: (i, 0)),
                  _pick(g_row)],
        out_specs=[pl.BlockSpec((TM, D), lambda i: (i, 0)), pl.BlockSpec((TM, D), lambda i: (i, 0))],
        out_shape=[jax.ShapeDtypeStruct((T, D), F32), jax.ShapeDtypeStruct((T, D), F32)],
        args=(y, w_all, x, g_row[0]), comm=comm)


def _loss_head(x, target, name):
    T, D = x.shape
    TM = min(T, 512)
    nT = T // TM

    def body(x_ref, t_ref, dx_ref, loss_ref, acc_s):
        i = pl.program_id(0)

        @pl.when(i == 0)
        def _():
            acc_s[...] = jnp.zeros_like(acc_s)

        e = x_ref[...] - t_ref[...]
        dx_ref[...] = e * (1.0 / D)
        acc_s[...] += jnp.sum(e * e, axis=0, keepdims=True)

        @pl.when(i == nT - 1)
        def _():
            total = jnp.sum(acc_s[...], axis=1, keepdims=True) * (0.5 / D)
            loss_ref[...] = jnp.broadcast_to(total, loss_ref.shape)

    return _pcall(
        body, name, (nT,),
        in_specs=[pl.BlockSpec((TM, D), lambda i: (i, 0)), pl.BlockSpec((TM, D), lambda i: (i, 0))],
        out_specs=[pl.BlockSpec((TM, D), lambda i: (i, 0)), pl.BlockSpec((1, 128), lambda i: (0, 0))],
        out_shape=[jax.ShapeDtypeStruct((T, D), F32), jax.ShapeDtypeStruct((1, 128), F32)],
        scratch_shapes=[pltpu.VMEM((1, D), F32)],
        args=(x, target))[0]


def _out_proj_bwd(dxn, out, g_row, w_all, y, name, comm=None):
    T, D = dxn.shape
    E = y.shape[1]
    ES = w_all.shape[-2]
    TM = min(T, 512)
    nT = T // TM

    def body(dxn_ref, out_ref, g_ref, w_ref, y_ref, dy_ref, dw_ref, dg_ref, acc_s, *sibling_scratch):
        i = pl.program_id(0)

        @pl.when(i == 0)
        def _():
            acc_s[...] = jnp.zeros_like(acc_s)
            dg_ref[...] = jnp.zeros_like(dg_ref)

        dxn_v = dxn_ref[...]
        r, oh = _rms_stats(out_ref[...])
        dg_ref[...] += jnp.sum(dxn_v * oh, axis=0, keepdims=True)
        dout = _rms_bwd(dxn_v, g_ref[...], r, oh).astype(BF16)
        dy_ref[...] = _dot_nt(dout, w_ref[...].reshape(E, D)).astype(BF16)
        acc_s[...] += _dot_tn(y_ref[...], dout)

        @pl.when(i == nT - 1)
        def _():
            def slab_of(k):
                return acc_s[pl.ds(pl.multiple_of(k * ES, ES), ES), :]

            _sum_with_sibling(slab_of, dw_ref, sibling_scratch)

    return _pcall(
        body, name, (nT,),
        in_specs=[pl.BlockSpec((TM, D), lambda i: (i, 0)),
                  pl.BlockSpec((TM, D), lambda i: (i, 0)),
                  _pick(g_row),
                  pl.BlockSpec((N_DEV, ES, D), lambda i: (0, 0, 0)),
                  pl.BlockSpec((TM, E), lambda i: (i, 0))],
        out_specs=[pl.BlockSpec((TM, E), lambda i: (i, 0)),
                   pl.BlockSpec((CHIPS, ES, D), lambda i: (0, 0, 0)),
                   pl.BlockSpec((1, D), lambda i: (0, 0))],
        out_shape=[jax.ShapeDtypeStruct((T, E), BF16), jax.ShapeDtypeStruct((CHIPS, ES, D), BF16),
                   jax.ShapeDtypeStruct((1, D), F32)],
        scratch_shapes=[pltpu.VMEM((E, D), F32)] + _sibling_scratch((ES, D)),
        args=(dxn, out, g_row[0], w_all, y), comm=comm)


def _a_mix_bwd(proj, dy, xh, dgl, rstd, ln_g, ln_b, w_s, b_s, name, comm=None):
    T, E3 = proj.shape
    E = E3 // 3
    G, P = A_GROUPS, GMLP_BLOCK
    GD = E // G
    TB = min(T, 256)

    def body(up_ref, zp_ref, dy_ref, xh_ref, dgl_ref, rstd_ref, lg_ref, lb_ref, ws_ref, bs_ref,
             dp_ref, dws_ref, dbs_ref, dlg_ref, dlb_ref, v_s, a_s, bz_s, c_s, dv_s):
        @pl.when(pl.program_id(0) == 0)
        def _():
            dws_ref[...] = jnp.zeros_like(dws_ref)
            dbs_ref[...] = jnp.zeros_like(dbs_ref)
            dlg_ref[...] = jnp.zeros_like(dlg_ref)
            dlb_ref[...] = jnp.zeros_like(dlb_ref)

        def recompute(ci, carry):
            rows = pl.ds(pl.multiple_of(ci * ROW_CHUNK, ROW_CHUNK), ROW_CHUNK)
            v_s[rows, :] = (xh_ref[rows, :].astype(F32) * lg_ref[...] + lb_ref[...]).astype(BF16)
            u, du = _gelu_and_grad(up_ref[rows, :].astype(F32))
            z = zp_ref[rows, :].astype(F32)
            sg = _sigmoid(z)
            s = z * sg
            ds = sg * (1.0 + z * (1.0 - sg))
            dyv = dy_ref[rows, :].astype(F32)
            a_s[rows, :] = dyv * s * du
            bz_s[rows, :] = dyv * u * ds
            c_s[rows, :] = (dyv * u * s).astype(BF16)
            return carry

        lax.fori_loop(0, TB // ROW_CHUNK, recompute, 0, unroll=2)

        mask = _spatial_mask()
        mask_t = _spatial_mask(transposed=True)
        for g in range(G):
            w_g = ws_ref[g]
            wm = jnp.where(mask, w_g, 0.0).astype(BF16)
            wm_t = jnp.where(mask_t, w_g.T, 0.0).astype(BF16)
            cols = slice(g * GD, (g + 1) * GD)
            dws_g = jnp.zeros((P, P), F32)
            dbs_g = jnp.zeros((SUBLANES, P), F32)
            for b in range(TB // P):
                rows = slice(b * P, (b + 1) * P)
                vb = v_s[rows, cols]
                cb = c_s[rows, cols]
                mixed = _dot(wm, vb) + bs_ref[g]
                dp_ref[rows, g * GD:(g + 1) * GD] = (a_s[rows, cols] * mixed).astype(BF16)
                dp_ref[rows, 2 * E + g * GD:2 * E + (g + 1) * GD] = (bz_s[rows, cols] * mixed).astype(BF16)
                dv_s[rows, cols] = _dot(wm_t, cb)
                dws_g = dws_g + _dot_nt(cb, vb)
                dbs_g = dbs_g + _dot_nt(jnp.ones((SUBLANES, GD), BF16), cb)
            dws_ref[g] += jnp.where(mask, dws_g, 0.0)
            dbs_ref[g:g + 1, :] += dbs_g[0:1, :]

        def ln_bwd(ci, carry):
            rows = pl.ds(pl.multiple_of(ci * ROW_CHUNK, ROW_CHUNK), ROW_CHUNK)
            dv = dv_s[rows, :]
            xh = xh_ref[rows, :].astype(F32)
            dlg_ref[...] += jnp.sum(dv * xh, axis=0, keepdims=True)
            dlb_ref[...] += jnp.sum(dv, axis=0, keepdims=True)
            dxh = dv * lg_ref[...]
            dvg = rstd_ref[rows, :] * (dxh - jnp.mean(dxh, axis=-1, keepdims=True)
                                       - xh * jnp.mean(dxh * xh, axis=-1, keepdims=True))
            dp_ref[rows, E:2 * E] = (dvg * dgl_ref[rows, :].astype(F32)).astype(BF16)
            return carry

        lax.fori_loop(0, TB // ROW_CHUNK, ln_bwd, 0, unroll=2)

    return _pcall(
        body, name, (T // TB,),
        in_specs=[pl.BlockSpec((TB, E), lambda i: (i, 0)),
                  pl.BlockSpec((TB, E), lambda i: (i, 2)),
                  pl.BlockSpec((TB, E), lambda i: (i, 0)),
                  pl.BlockSpec((TB, E), lambda i: (i, 0)),
                  pl.BlockSpec((TB, E), lambda i: (i, 0)),
                  pl.BlockSpec((TB, 1), lambda i: (i, 0)),
                  _pick(ln_g), _pick(ln_b), _pick(w_s), _pick(b_s)],
        out_specs=[pl.BlockSpec((TB, E3), lambda i: (i, 0)),
                   pl.BlockSpec((G, P, P), lambda i: (0, 0, 0)),
                   pl.BlockSpec((G, P), lambda i: (0, 0)),
                   pl.BlockSpec((1, E), lambda i: (0, 0)),
                   pl.BlockSpec((1, E), lambda i: (0, 0))],
        out_shape=[jax.ShapeDtypeStruct((T, E3), BF16), jax.ShapeDtypeStruct((G, P, P), F32),
                   jax.ShapeDtypeStruct((G, P), F32), jax.ShapeDtypeStruct((1, E), F32),
                   jax.ShapeDtypeStruct((1, E), F32)],
        scratch_shapes=[pltpu.VMEM((TB, E), BF16), pltpu.VMEM((TB, E), F32), pltpu.VMEM((TB, E), F32),
                        pltpu.VMEM((TB, E), BF16), pltpu.VMEM((TB, E), F32)],
        args=(proj, proj, dy, xh, dgl, rstd, ln_g[0], ln_b[0], w_s[0], b_s[0]), comm=comm)


def _b_mix_bwd(proj, dy, o, scale, wg_all, name, comm=None):
    T, E2 = proj.shape
    E = E2 // 2
    NG = len(POOL_WINDOWS)
    GB = E // NG
    TB = min(T, 256)
    nT = T // TB
    RS = wg_all.shape[-2]
    halo_per_tile = TB // HALO

    def body(p_ref, halo_ref, dy_ref, o_ref, sc_ref, wg_ref, dp_ref, dsc_ref, dwg_ref, acc_s, carry_s,
             *sibling_scratch):
        i = pl.program_id(0)
        tile = nT - 1 - i

        @pl.when(i == 0)
        def _():
            acc_s[...] = jnp.zeros_like(acc_s)
            carry_s[...] = jnp.zeros_like(carry_s)
            dsc_ref[...] = jnp.zeros_like(dsc_ref)

        has_history = (tile > 0).astype(F32)
        for g, win in enumerate(POOL_WINDOWS):
            cols = slice(g * GB, (g + 1) * GB)
            inv = _inv_count(tile * TB, TB, win)
            xg = p_ref[:, cols].astype(F32)
            ext = jnp.concatenate([halo_ref[:, cols].astype(F32) * has_history, xg], axis=0)
            pooled = _window_sum_back(ext, win)[HALO:, :] * inv - xg
            z = p_ref[:, E + g * GB:E + (g + 1) * GB].astype(F32)
            sg = _sigmoid(z)
            dyv = dy_ref[:, cols].astype(F32)
            ov = o_ref[:, cols].astype(F32)
            sc = sc_ref[:, cols]
            dmixed = dyv * (z * sg)
            dsc_ref[:, cols] += jnp.sum(dmixed * ov, axis=0, keepdims=True)
            dz = dyv * (ov * sc) * (sg * (1.0 + z * (1.0 - sg)))
            do = (dmixed * sc).astype(BF16)
            acc_s[:, g] += _dot_tn(pooled.astype(BF16), do).reshape(N_DEV, RS, GB)
            dpool = _dot_nt(do, wg_ref[:, g].reshape(GB, GB))
            q = dpool * inv
            ext_q = jnp.concatenate([q, carry_s[:, cols]], axis=0)
            dxb = _window_sum_ahead(ext_q, win)[:TB, :] - dpool
            carry_s[:, cols] = q[:HALO, :]
            dp_ref[:, cols] = dxb.astype(BF16)
            dp_ref[:, E + g * GB:E + (g + 1) * GB] = dz.astype(BF16)

        @pl.when(i == nT - 1)
        def _():
            _sum_with_sibling(lambda k: acc_s[k], dwg_ref, sibling_scratch)

    return _pcall(
        body, name, (nT,),
        in_specs=[pl.BlockSpec((TB, E2), lambda i: (nT - 1 - i, 0)),
                  pl.BlockSpec((HALO, E), lambda i: (jnp.maximum((nT - 1 - i) * halo_per_tile - 1, 0), 0)),
                  pl.BlockSpec((TB, E), lambda i: (nT - 1 - i, 0)),
                  pl.BlockSpec((TB, E), lambda i: (nT - 1 - i, 0)),
                  pl.BlockSpec((1, E), lambda i: (0, 0)),
                  pl.BlockSpec((N_DEV, NG, RS, GB), lambda i: (0, 0, 0, 0))],
        out_specs=[pl.BlockSpec((TB, E2), lambda i: (nT - 1 - i, 0)),
                   pl.BlockSpec((1, E), lambda i: (0, 0)),
                   pl.BlockSpec((CHIPS, NG, RS, GB), lambda i: (0, 0, 0, 0))],
        out_shape=[jax.ShapeDtypeStruct((T, E2), BF16), jax.ShapeDtypeStruct((1, E), F32),
                   jax.ShapeDtypeStruct((CHIPS, NG, RS, GB), BF16)],
        scratch_shapes=[pltpu.VMEM((N_DEV, NG, RS, GB), F32), pltpu.VMEM((HALO, E), F32)]
        + _sibling_scratch((NG, RS, GB)),
        args=(proj, proj, dy, o, scale, wg_all), comm=comm)


def _in_proj_bwd_dx(dproj, w_parts, x, g_row, dxn, name, comm=None):
    T, D = x.shape
    NS = w_parts[0].shape[-1]
    DP = w_parts[0].shape[-2]
    n_parts = len(w_parts)
    TM = min(T, 512)

    def body(dp_ref, *refs):
        w_refs, (x_ref, g_ref, dxn_ref, dx_ref, dg_ref) = refs[:n_parts], refs[n_parts:]

        @pl.when(pl.program_id(0) == 0)
        def _():
            dg_ref[...] = jnp.zeros_like(dg_ref)

        pieces = []
        for w_ref in w_refs:
            piece = _dot_nt(dp_ref[:, 0:NS], w_ref[0])
            for k in range(1, N_DEV):
                piece = piece + _dot_nt(dp_ref[:, k * NS:(k + 1) * NS], w_ref[k])
            pieces.append(piece)
        dh = pieces[0] if n_parts == 1 else jnp.concatenate(pieces, axis=1)
        r, xh = _rms_stats(x_ref[...])
        dg_ref[...] += jnp.sum(dh * xh, axis=0, keepdims=True)
        dx_ref[...] = dxn_ref[...] + _rms_bwd(dh, g_ref[...], r, xh)

    return _pcall(
        body, name, (T // TM,),
        in_specs=[pl.BlockSpec((TM, N_DEV * NS), lambda i: (i, 0))] + [_resident((N_DEV, DP, NS))] * n_parts
        + [pl.BlockSpec((TM, D), lambda i: (i, 0)),
           _pick(g_row),
           pl.BlockSpec((TM, D), lambda i: (i, 0))],
        out_specs=[pl.BlockSpec((TM, D), lambda i: (i, 0)), pl.BlockSpec((1, D), lambda i: (0, 0))],
        out_shape=[jax.ShapeDtypeStruct((T, D), F32), jax.ShapeDtypeStruct((1, D), F32)],
        args=(dproj, *w_parts, x, g_row[0], dxn), comm=comm)


STAGE_SLOTS = 2


def _dw_in(h, dproj, name, comm=None):
    T, D = h.shape
    NS = dproj.shape[1] // N_DEV
    TK = min(T, 2048)
    nK = T // TK

    def body(h_ref, dp_ref, q_ref, acc_s, stage_s, land_s, send_sems, recv_sems):
        k, t = pl.program_id(0), pl.program_id(1)
        c = lax.axis_index("c")

        def to_sibling(q):
            return _to_sibling(stage_s, land_s, send_sems, recv_sems, q % STAGE_SLOTS, q)

        @pl.when(t == 0)
        def _():
            acc_s[...] = jnp.zeros_like(acc_s)

        acc_s[...] += _dot_tn(h_ref[...], dp_ref[...])

        @pl.when(t == nK - 1)
        def _():
            q = k // 2

            @pl.when(k % 2 == c)
            def _():
                q_ref[q] = acc_s[...].astype(BF16)

            @pl.when(k % 2 != c)
            def _():
                @pl.when(q >= STAGE_SLOTS)
                def _():
                    to_sibling(q - STAGE_SLOTS).wait_send()

                stage_s[q % STAGE_SLOTS] = acc_s[...].astype(BF16)
                to_sibling(q).start()

        @pl.when((k == N_DEV - 1) & (t == nK - 1))
        def _():
            for q in range(CHIPS - STAGE_SLOTS, CHIPS):
                to_sibling(q).wait_send()
            for q in range(CHIPS):
                to_sibling(q).wait_recv()
                q_ref[q] = (q_ref[q].astype(F32) + land_s[q].astype(F32)).astype(BF16)

    return _pcall(
        body, name, (N_DEV, nK),
        in_specs=[pl.BlockSpec((TK, D), lambda k, t: (t, 0)), pl.BlockSpec((TK, NS), lambda k, t: (t, k))],
        out_specs=[pl.BlockSpec((CHIPS, D, NS), lambda k, t: (0, 0, 0))],
        out_shape=[jax.ShapeDtypeStruct((CHIPS, D, NS), BF16)],
        scratch_shapes=[pltpu.VMEM((D, NS), F32)] + _sibling_scratch((D, NS), STAGE_SLOTS),
        args=(h, dproj), comm=comm)


def _reduce_adam(recvs, w, m, v, name):
    L, R, C = w.shape
    assert len(recvs) == L
    senders = recvs[0].shape[0]
    TR = R
    for cand in (256, 128, 64, 32, 16):
        if R % cand == 0 and R > cand:
            TR = cand
            break
    nR = R // TR
    c1 = 1.0 - ADAM_B1 ** ADAM_STEP
    c2 = 1.0 - ADAM_B2 ** ADAM_STEP

    def body(*refs):
        recv_refs = refs[:L]
        w_ref, m_ref, v_ref, g_ref, d_ref, nm_ref, nv_ref, g_s = refs[L:]
        layer = pl.program_id(0)
        for l in range(L):
            @pl.when(layer == l)
            def _(l=l):
                acc = recv_refs[l][0].astype(F32)
                for j in range(1, senders):
                    acc = acc + recv_refs[l][j].astype(F32)
                g_s[...] = acc

        g = g_s[...]
        g_ref[...] = g
        nm = ADAM_B1 * m_ref[...] + (1.0 - ADAM_B1) * g
        nv = ADAM_B2 * v_ref[...] + (1.0 - ADAM_B2) * (g * g)
        nm_ref[...] = nm
        nv_ref[...] = nv
        d_ref[...] = -ADAM_LR * ((nm / c1) / (jnp.sqrt(nv / c2) + ADAM_EPS) + ADAM_WD * w_ref[...])

    def recv_spec(l):
        def index(layer, t):
            before = jnp.where(layer < l, 0, nR - 1)
            return (0, jnp.where(layer == l, t, before), 0)
        return pl.BlockSpec((senders, TR, C), index)

    wspec = pl.BlockSpec((None, TR, C), lambda layer, t: (layer, t, 0))
    out = jax.ShapeDtypeStruct((L, R, C), F32)
    return _pcall(
        body, name, (L, nR),
        in_specs=[recv_spec(l) for l in range(L)] + [wspec] * 3,
        out_specs=[wspec] * 4, out_shape=[out] * 4,
        scratch_shapes=[pltpu.VMEM((TR, C), F32)],
        args=(*recvs, w, m, v))[0]


def _adam_replicated(gathered, w, m, v, name):
    n_params = len(w)
    layers = [len(g) for g in gathered]
    flat = [g for per_param in gathered for g in per_param]
    c1 = 1.0 - ADAM_B1 ** ADAM_STEP
    c2 = 1.0 - ADAM_B2 ** ADAM_STEP

    def body(*refs):
        g_refs, refs = refs[:len(flat)], refs[len(flat):]
        w_refs, m_refs, v_refs = refs[:n_params], refs[n_params:2 * n_params], refs[2 * n_params:3 * n_params]
        out_refs = refs[3 * n_params:]
        at = 0
        for n in range(n_params):
            g_out, d_out, nm_out, nv_out = out_refs[4 * n:4 * n + 4]
            for l in range(layers[n]):
                g_ref = g_refs[at]
                at += 1
                g = g_ref[0]
                for s in range(1, N_DEV):
                    g = g + g_ref[s]
                nm = ADAM_B1 * m_refs[n][l] + (1.0 - ADAM_B1) * g
                nv = ADAM_B2 * v_refs[n][l] + (1.0 - ADAM_B2) * (g * g)
                g_out[l] = g
                nm_out[l] = nm
                nv_out[l] = nv
                d_out[l] = -ADAM_LR * ((nm / c1) / (jnp.sqrt(nv / c2) + ADAM_EPS) + ADAM_WD * w_refs[n][l])

    vmem = pl.BlockSpec(memory_space=pltpu.VMEM)
    args = [*flat, *w, *m, *v]
    outs = pl.pallas_call(
        body, name=name, in_specs=[vmem] * len(args), out_specs=[vmem] * (4 * n_params),
        out_shape=[jax.ShapeDtypeStruct(a.shape, F32) for a in w for _ in range(4)],
        compiler_params=_params(),
    )(*args)
    return [outs[4 * n:4 * n + 4] for n in range(n_params)]


A_W_IN_PARTS = 4
PACK_LANES = 128
PACK_ROWS_MULTIPLE = 256


def _pack(arrays):
    flat = jnp.concatenate([a.reshape(-1) for a in arrays])
    tile = PACK_LANES * PACK_ROWS_MULTIPLE
    padded = -(-flat.shape[0] // tile) * tile
    return jnp.pad(flat, (0, padded - flat.shape[0])).reshape(1, padded // PACK_LANES, PACK_LANES)


def _unpack(packed, like):
    flat = packed.reshape(-1)
    out, at = [], 0
    for a in like:
        out.append(flat[at:at + a.size].reshape(a.shape))
        at += a.size
    return out


def kernel(x, norm_pre, norm_post, a_w_in, a_ln_g, a_ln_b, a_w_s, a_b_s, a_w_out, b_w_in, b_w_grp, b_scale, b_w_out, loss_target, m_norm_pre, m_norm_post, m_a_w_in, m_a_ln_g, m_a_ln_b, m_a_w_s, m_a_b_s, m_a_w_out, m_b_w_in, m_b_w_grp, m_b_scale, m_b_w_out, v_norm_pre, v_norm_post, v_a_w_in, v_a_ln_g, v_a_ln_b, v_a_w_s, v_a_b_s, v_a_w_out, v_b_w_in, v_b_w_grp, v_b_scale, v_b_w_out):
    weights = dict(norm_pre=norm_pre, norm_post=norm_post, a_w_in=a_w_in, a_ln_g=a_ln_g, a_ln_b=a_ln_b, a_w_s=a_w_s,
                   a_b_s=a_b_s, a_w_out=a_w_out, b_w_in=b_w_in, b_w_grp=b_w_grp, b_scale=b_scale, b_w_out=b_w_out)
    mom_m = dict(norm_pre=m_norm_pre, norm_post=m_norm_post, a_w_in=m_a_w_in, a_ln_g=m_a_ln_g, a_ln_b=m_a_ln_b,
                 a_w_s=m_a_w_s, a_b_s=m_a_b_s, a_w_out=m_a_w_out, b_w_in=m_b_w_in, b_w_grp=m_b_w_grp,
                 b_scale=m_b_scale, b_w_out=m_b_w_out)
    mom_v = dict(norm_pre=v_norm_pre, norm_post=v_norm_post, a_w_in=v_a_w_in, a_ln_g=v_a_ln_g, a_ln_b=v_a_ln_b,
                 a_w_s=v_a_w_s, a_b_s=v_a_b_s, a_w_out=v_a_w_out, b_w_in=v_b_w_in, b_w_grp=v_b_w_grp,
                 b_scale=v_b_scale, b_w_out=v_b_w_out)
    names = list(weights)

    depth = norm_pre.shape[0]
    x0 = x[0]
    target = loss_target[0]
    T, D = x0.shape
    E = a_ln_g.shape[1]
    G, P = A_GROUPS, GMLP_BLOCK
    pre3, post3 = norm_pre.reshape(depth, 1, D), norm_post.reshape(depth, 1, D)
    ln_g3, ln_b3 = a_ln_g.reshape(-1, 1, E), a_ln_b.reshape(-1, 1, E)
    b_s4 = a_b_s.reshape(-1, G, P, 1)

    def shards_of(i):
        j = i // 2
        if i % 2 == 0:
            w = a_w_in[j].astype(BF16)
            rows = w.shape[0] // A_W_IN_PARTS
            parts = {f"w_in_{p}": w[p * rows:(p + 1) * rows] for p in range(A_W_IN_PARTS)}
            return dict(**parts, w_out=a_w_out[j].astype(BF16))
        return dict(w_in=b_w_in[j].astype(BF16), w_out=b_w_out[j].astype(BF16), grp=b_w_grp[j].astype(BF16))

    shard = [shards_of(i) for i in range(depth)]
    full = [dict() for _ in range(depth)]

    def gather_into(keys, got):
        for (i, key), arr in zip(keys, got):
            full[i][key] = arr

    def w_in_of(i):
        return [(i, k) for k in shard[i] if k.startswith("w_in")]

    def rest_of(i):
        return [(i, k) for k in shard[i] if not k.startswith("w_in")]

    def gather_of(keys):
        return _Gather([shard[a][k] for a, k in keys]) if keys else None

    first = _comm_only(_Gather([shard[0][k] for _, k in w_in_of(0)] + [b_scale]), "gather_first")
    gather_into(w_in_of(0), first)
    scale_full = jnp.transpose(first[-1], (1, 0, 2)).reshape(b_scale.shape[0], 1, E)

    saved = []
    xi = x0
    for i in range(depth):
        j = i // 2
        g_pre, g_post = (pre3, i), (post3, i)
        keys_in, keys_mix, keys_out = [], [], []
        if i % 2 == 0:
            keys_mix = [(0, "w_out")] if i == 0 else []
            if i + 1 < depth:
                keys_in = w_in_of(i + 1)
                keys_mix = keys_mix + rest_of(i + 1)
            if i + 2 < depth:
                keys_mix = keys_mix + [(i + 2, "w_in_0")]
                keys_out = [(i + 2, "w_in_1")]
        elif i + 1 < depth:
            keys_in = w_in_of(i + 1)[2:]
            keys_mix = rest_of(i + 1)
        kind = "a" if i % 2 == 0 else "b"
        (proj, h), got = _in_proj_fwd(xi, g_pre, [full[i][k] for _, k in w_in_of(i)], f"{kind}_in_fwd_{i}",
                                      gather_of(keys_in))
        gather_into(keys_in, got)
        if i % 2 == 0:
            (y, *o), got = _a_mix_fwd(proj, (ln_g3, j), (ln_b3, j), (a_w_s, j), (b_s4, j),
                                      f"a_mix_fwd_{i}", gather_of(keys_mix))
        else:
            (y, o), got = _b_mix_fwd(proj, scale_full[j], full[i]["grp"], f"b_mix_fwd_{i}", gather_of(keys_mix))
        gather_into(keys_mix, got)
        (x_next, out), got = _out_proj_fwd(y, full[i]["w_out"], xi, g_post, f"{kind}_out_fwd_{i}",
                                           gather_of(keys_out))
        gather_into(keys_out, got)
        saved.append((xi, h, proj, y, out, o))
        xi = x_next

    dx, loss_row = _loss_head(xi, target, "loss_head")

    n_a, n_b = a_ln_g.shape[0], b_scale.shape[0]
    d_pre, d_post = [None] * depth, [None] * depth
    recv = {"a_w_in": [None] * n_a, "a_w_out": [None] * n_a, "b_w_in": [None] * n_b,
            "b_w_grp": [None] * n_b, "b_w_out": [None] * n_b, "b_scale": [None] * n_b}
    small_a = [[None] * 4 for _ in range(n_a)]
    kinds = ("chips", "devices", "gather")

    def carried(items):
        of = {kind: [it[3] for it in items if it[0] == kind] for kind in kinds}
        comms = (([_Exchange(of["chips"], chips_only=True)] if of["chips"] else [])
                 + ([_Exchange(of["devices"])] if of["devices"] else [])
                 + ([_Gather(of["gather"])] if of["gather"] else []))
        return None if not comms else comms[0] if len(comms) == 1 else _Together(comms)

    def received(items, got):
        ordered = [it for kind in kinds for it in items if it[0] == kind]
        for it, arr in zip(ordered, got):
            if it[0] == "gather":
                small_a[it[2][0]][it[2][1]] = arr
            else:
                recv[it[1]][it[2]] = arr

    pending = []
    small_pending = []
    for i in reversed(range(depth)):
        j = i // 2
        xi, h, proj, y, out, o = saved[i]
        g_pre, g_post = (pre3, i), (post3, i)
        if i % 2 == 0:
            (dy, dw_out, d_post[i]), _ = _out_proj_bwd(dx, out, g_post, full[i]["w_out"], y, f"a_out_bwd_{i}")
            items, pending = pending, []
            (dproj, d_w_s, dbs, d_ln_g, d_ln_b), got = _a_mix_bwd(
                proj, dy, *o, (ln_g3, j), (ln_b3, j), (a_w_s, j), (b_s4, j),
                f"a_mix_bwd_{i}", carried(items))
            received(items, got)
            small_pending += [("gather", "small", (j, n), part) for n, part in enumerate((d_w_s, dbs, d_ln_g, d_ln_b))]
            items = []
            if i == 0:
                items, small_pending = [("chips", "a_w_out", j, dw_out)] + small_pending, []
            (dw_in,), got = _dw_in(h, dproj, f"a_dw_in_{i}", carried(items))
            received(items, got)
            items = [("chips", "a_w_in", j, dw_in)] if i == 0 else [("chips", "a_w_out", j, dw_out)]
            (dx, d_pre[i]), got = _in_proj_bwd_dx(dproj, [full[i][k] for _, k in w_in_of(i)], xi, g_pre, dx,
                                                  f"a_in_bwd_{i}", carried(items))
            received(items, got)
            if i > 0:
                pending.append(("chips", "a_w_in", j, dw_in))
        else:
            items, small_pending = small_pending, []
            (dy, dw_out, d_post[i]), got = _out_proj_bwd(dx, out, g_post, full[i]["w_out"], y, f"b_out_bwd_{i}",
                                                        carried(items))
            received(items, got)
            items, pending = pending, []
            (dproj, dsc, dw_grp), got = _b_mix_bwd(proj, dy, o, scale_full[j], full[i]["grp"], f"b_mix_bwd_{i}",
                                                  carried(items))
            received(items, got)
            (dx, d_pre[i]), _ = _in_proj_bwd_dx(dproj, [full[i][k] for _, k in w_in_of(i)], xi, g_pre, dx,
                                                f"b_in_bwd_{i}")
            (dw_in,), _ = _dw_in(h, dproj, f"b_dw_in_{i}")
            pending += [("chips", "b_w_out", j, dw_out), ("chips", "b_w_grp", j, dw_grp), ("chips", "b_w_in", j, dw_in),
                        ("devices", "b_scale", j, dsc.reshape(N_DEV, 1, E // N_DEV))]
    assert not pending and not small_pending

    gathered = _comm_only(_Gather([_pack([*d_pre, *d_post, loss_row[:, :1]])[0]]), "gather_norm_grads")
    results = {k: [None] * 4 for k in names}
    no_state = jnp.zeros((1, 1), F32)
    norm_like = [norm_pre, norm_post, no_state]
    outs = _reduce_adam([gathered[-1]], _pack(norm_like), _pack([m_norm_pre, m_norm_post, no_state]),
                        _pack([v_norm_pre, v_norm_post, no_state]), "adam_norms")
    for q, packed in enumerate(outs):
        results["norm_pre"][q], results["norm_post"][q], summed = _unpack(packed, norm_like)
        if q == 0:
            loss = summed[0, 0]
    a_small = ("a_w_s", "a_b_s", "a_ln_g", "a_ln_b")
    forms = (a_w_s.shape, a_b_s.shape, ln_g3.shape, ln_b3.shape)
    outs = _adam_replicated([[small_a[j][n] for j in range(n_a)] for n in range(len(a_small))],
                            [weights[k].reshape(f) for k, f in zip(a_small, forms)],
                            [mom_m[k].reshape(f) for k, f in zip(a_small, forms)],
                            [mom_v[k].reshape(f) for k, f in zip(a_small, forms)], "adam_small")
    for k, four in zip(a_small, outs):
        results[k] = [o_.reshape(weights[k].shape) for o_ in four]

    def shard_view(a):
        return a.reshape(a.shape[0], -1, a.shape[-1])

    for k in ("a_w_in", "a_w_out", "b_w_in", "b_w_grp", "b_w_out"):
        w3 = shard_view(weights[k])
        recvs = [r.reshape(r.shape[0], w3.shape[1], w3.shape[2]) for r in recv[k]]
        outs = _reduce_adam(recvs, w3, shard_view(mom_m[k]), shard_view(mom_v[k]), f"adam_{k}")
        results[k] = [o_.reshape(weights[k].shape) for o_ in outs]
    sc_recv = jnp.concatenate(recv["b_scale"], axis=1)
    outs = _reduce_adam([sc_recv], b_scale[None], m_b_scale[None], v_b_scale[None], "adam_b_scale")
    results["b_scale"] = [o_[0] for o_ in outs]

    grad_x = dx[None]
    return (loss, grad_x, *[results[k][0] for k in names], *[results[k][1] for k in names],
            *[results[k][2] for k in names], *[results[k][3] for k in names])
```

```python
import jax
import jax.numpy as jnp
from jax import lax
from jax.experimental import pallas as pl
from jax.experimental.pallas import tpu as pltpu

F32 = jnp.float32
BF16 = jnp.bfloat16
MESH = pl.DeviceIdType.MESH

N_DEV = 8
EPS = 1e-6
CHUNK = 64
GMLP_BLOCK = 128
A_GROUPS = 8
POOL_WINDOWS = (2, 4, 8, 16)
HALO = 16
SUBLANES = 8
ADAM_LR = 0.001
ADAM_B1 = 0.9
ADAM_B2 = 0.999
ADAM_EPS = 1e-08
ADAM_WD = 0.01
ADAM_STEP = 10
GELU_C = 0.7978845608028654
GELU_A = 0.044715
ROW_CHUNK = 16
VMEM_LIMIT_BYTES = 56 * 1024 * 1024


def _params(**kw):
    return pltpu.CompilerParams(vmem_limit_bytes=VMEM_LIMIT_BYTES, **kw)


def _gelu(x):
    return 0.5 * x * (1.0 + jnp.tanh(GELU_C * (x + GELU_A * (x * x * x))))


def _gelu_and_grad(x):
    x2 = x * x
    t = jnp.tanh(GELU_C * (x + GELU_A * (x2 * x)))
    val = 0.5 * x * (1.0 + t)
    grad = 0.5 * (1.0 + t) + 0.5 * x * (1.0 - t * t) * (GELU_C * (1.0 + 3.0 * GELU_A * x2))
    return val, grad


def _sigmoid(z):
    return 0.5 * jnp.tanh(0.5 * z) + 0.5


def _dot(a, b):
    return jnp.dot(a, b, preferred_element_type=F32)


def _dot_nt(a, b):
    return lax.dot_general(a, b, (((1,), (1,)), ((), ())), preferred_element_type=F32)


def _dot_tn(a, b):
    return lax.dot_general(a, b, (((0,), (0,)), ((), ())), preferred_element_type=F32)


def _rms_stats(xf):
    r = lax.rsqrt(jnp.mean(xf * xf, axis=-1, keepdims=True) + EPS)
    return r, xf * r


def _rms_bwd(dy, g, r, xh):
    dxh = dy * g
    return r * (dxh - xh * jnp.mean(dxh * xh, axis=-1, keepdims=True))


def _resident(shape):
    return pl.BlockSpec(shape, lambda *_: (0,) * len(shape), pipeline_mode=pl.Buffered(1))


def _pick(stacked):
    arr, index = stacked
    return pl.BlockSpec((None,) + arr.shape[1:], lambda *_: (index,) + (0,) * (arr.ndim - 1))


def _spatial_mask(transposed=False):
    p = lax.broadcasted_iota(jnp.int32, (GMLP_BLOCK, GMLP_BLOCK), 0)
    q = lax.broadcasted_iota(jnp.int32, (GMLP_BLOCK, GMLP_BLOCK), 1)
    if transposed:
        p, q = q, p
    return (q // CHUNK) <= (p // CHUNK)


def _position():
    x, y, c = lax.axis_index("x"), lax.axis_index("y"), lax.axis_index("c")
    return x, y, c


def _comm_scratch(n):
    return [pltpu.SemaphoreType.DMA((n, 7)), pltpu.SemaphoreType.DMA((n, 7)), pltpu.SemaphoreType.DMA((n,))]


class _Gather:
    def __init__(self, arrs):
        self.inputs = list(arrs)
        self.out_shape = [jax.ShapeDtypeStruct((N_DEV,) + a.shape, a.dtype) for a in arrs]
        self.scratch = _comm_scratch(len(arrs))

    def _plan(self, ins, outs, sems):
        send_sems, recv_sems, local_sems = sems
        n = len(ins)
        x, y, c = _position()
        sibling = (x, y, 1 - c)
        chips = [(1 - x, y), (x, 1 - y), (1 - x, 1 - y)]

        def index(px, py, pc):
            return 4 * px + 2 * py + pc

        def copy(a, k, block, to, src=None):
            return pltpu.make_async_remote_copy(
                src_ref=outs[a].at[block] if src is None else src, dst_ref=outs[a].at[block],
                send_sem=send_sems.at[a, k], recv_sem=recv_sems.at[a, k], device_id=to, device_id_type=MESH)

        me = index(x, y, c)
        own = [pltpu.make_async_copy(ins[a], outs[a].at[me], local_sems.at[a]) for a in range(n)]
        first = []
        for a in range(n):
            first.append(copy(a, 0, me, sibling, src=ins[a]))
            for j, chip in enumerate(chips):
                first.append(copy(a, 1 + j, me, (*chip, c), src=ins[a]))
        return n, (x, y, c), sibling, chips, index, copy, own, first

    def start(self, ins, outs, sems):
        _, _, _, _, _, _, own, first = self._plan(ins, outs, sems)
        for cp in own + first:
            cp.start()

    def middle(self, ins, outs, sems):
        n, me, sibling, chips, index, copy, _, _ = self._plan(ins, outs, sems)
        for j, chip in enumerate(chips):
            for a in range(n):
                copy(a, 1 + j, index(*chip, me[2]), me).wait_recv()
                copy(a, 4 + j, index(*chip, me[2]), sibling).start()

    def finish(self, ins, outs, sems):
        n, me, sibling, chips, index, copy, own, first = self._plan(ins, outs, sems)
        c = me[2]
        passed = [copy(a, 4 + j, index(*chip, c), sibling) for j, chip in enumerate(chips) for a in range(n)]
        for a in range(n):
            copy(a, 0, index(me[0], me[1], 1 - c), me).wait_recv()
        for j, chip in enumerate(chips):
            for a in range(n):
                copy(a, 4 + j, index(*chip, 1 - c), me).wait_recv()
        for cp in first + passed:
            cp.wait_send()
        for cp in own:
            cp.wait()


class _Exchange:
    def __init__(self, arrs, chips_only=False):
        self.inputs = list(arrs)
        self.chips_only = chips_only
        self.out_shape = [jax.ShapeDtypeStruct(a.shape, a.dtype) for a in arrs]
        self.scratch = _comm_scratch(len(arrs))

    def _plan(self, ins, outs, sems):
        send_sems, recv_sems, local_sems = sems
        n = len(ins)
        x, y, c = _position()
        scale = 1 if self.chips_only else 2
        me = 2 * x + y if self.chips_only else 4 * x + 2 * y + c
        own = [pltpu.make_async_copy(ins[a].at[me], outs[a].at[me], local_sems.at[a]) for a in range(n)]
        sends, recvs = [], []
        for r in range(1, 4 * scale):
            px = 1 - x if r & (2 * scale) else x
            py = 1 - y if r & scale else y
            pc = 1 - c if (r & 1 and not self.chips_only) else c
            peer = 2 * px + py if self.chips_only else 4 * px + 2 * py + pc
            for a in range(n):
                sends.append(pltpu.make_async_remote_copy(
                    src_ref=ins[a].at[peer], dst_ref=outs[a].at[me],
                    send_sem=send_sems.at[a, r - 1], recv_sem=recv_sems.at[a, r - 1],
                    device_id=(px, py, pc), device_id_type=MESH))
                recvs.append(pltpu.make_async_remote_copy(
                    src_ref=ins[a].at[peer], dst_ref=outs[a].at[peer],
                    send_sem=send_sems.at[a, r - 1], recv_sem=recv_sems.at[a, r - 1],
                    device_id=(px, py, pc), device_id_type=MESH))
        return own, sends, recvs

    def start(self, ins, outs, sems):
        own, sends, _ = self._plan(ins, outs, sems)
        for cp in own + sends:
            cp.start()

    def middle(self, ins, outs, sems):
        pass

    def finish(self, ins, outs, sems):
        own, sends, recvs = self._plan(ins, outs, sems)
        for cp in recvs:
            cp.wait_recv()
        for cp in sends:
            cp.wait_send()
        for cp in own:
            cp.wait()


CHIPS = N_DEV // 2


def _sibling_scratch(slab_shape, stage_slots=CHIPS):
    return [pltpu.VMEM((stage_slots,) + tuple(slab_shape), BF16), pltpu.VMEM((CHIPS,) + tuple(slab_shape), BF16),
            pltpu.SemaphoreType.DMA((CHIPS,)), pltpu.SemaphoreType.DMA((CHIPS,))]


def _to_sibling(stage_s, land_s, send_sems, recv_sems, slot, q):
    x, y, c = _position()
    return pltpu.make_async_remote_copy(
        src_ref=stage_s.at[slot], dst_ref=land_s.at[q], send_sem=send_sems.at[q], recv_sem=recv_sems.at[q],
        device_id=(x, y, 1 - c), device_id_type=MESH)


def _sum_with_sibling(slab_of, q_ref, sibling_scratch):
    stage_s, land_s, send_sems, recv_sems = sibling_scratch
    c = lax.axis_index("c")
    for q in range(CHIPS):
        stage_s[q] = slab_of(2 * q + 1 - c).astype(BF16)
        _to_sibling(stage_s, land_s, send_sems, recv_sems, q, q).start()
    for q in range(CHIPS):
        _to_sibling(stage_s, land_s, send_sems, recv_sems, q, q).wait_recv()
        q_ref[q] = (slab_of(2 * q + c) + land_s[q].astype(F32)).astype(BF16)
    for q in range(CHIPS):
        _to_sibling(stage_s, land_s, send_sems, recv_sems, q, q).wait_send()


class _Together:
    def __init__(self, comms):
        self.comms = list(comms)
        self.inputs = [a for c in self.comms for a in c.inputs]
        self.out_shape = [s for c in self.comms for s in c.out_shape]
        self.scratch = [s for c in self.comms for s in c.scratch]

    def _each(self, ins, outs, sems):
        at = 0
        for k, c in enumerate(self.comms):
            n = len(c.inputs)
            yield c, ins[at:at + n], outs[at:at + n], sems[3 * k:3 * k + 3]
            at += n

    def start(self, ins, outs, sems):
        for c, i, o, s in self._each(ins, outs, sems):
            c.start(i, o, s)

    def middle(self, ins, outs, sems):
        for c, i, o, s in self._each(ins, outs, sems):
            c.middle(i, o, s)

    def finish(self, ins, outs, sems):
        for c, i, o, s in self._each(ins, outs, sems):
            c.finish(i, o, s)


def _comm_only(comm, name):
    n = len(comm.inputs)

    def body(*refs):
        ins, outs, sems = refs[:n], refs[n:2 * n], refs[2 * n:]
        comm.start(ins, outs, sems)
        comm.middle(ins, outs, sems)
        comm.finish(ins, outs, sems)

    any_spec = pl.BlockSpec(memory_space=pl.ANY)
    return pl.pallas_call(
        body, name=name, out_shape=comm.out_shape, in_specs=[any_spec] * n, out_specs=[any_spec] * n,
        scratch_shapes=comm.scratch, compiler_params=pltpu.CompilerParams(has_side_effects=True),
    )(*comm.inputs)


def _pcall(body, name, grid, in_specs, out_specs, out_shape, args, scratch_shapes=(), comm=None):
    in_specs, out_specs, out_shape, scratch_shapes = list(in_specs), list(out_specs), list(out_shape), list(scratch_shapes)
    if comm is None:
        outs = pl.pallas_call(body, name=name, grid=grid, in_specs=in_specs, out_specs=out_specs, out_shape=out_shape,
                              scratch_shapes=scratch_shapes, compiler_params=_params())(*args)
        return list(outs), []
    n_in, n_out, n_scr, n_c = len(in_specs), len(out_specs), len(scratch_shapes), len(comm.inputs)

    def carrying(*refs):
        ins, refs = refs[:n_in], refs[n_in:]
        c_ins, refs = refs[:n_c], refs[n_c:]
        outs, refs = refs[:n_out], refs[n_out:]
        c_outs, refs = refs[:n_c], refs[n_c:]
        scr, sems = refs[:n_scr], refs[n_scr:]
        step, steps = 0, 1
        for d, size in enumerate(grid):
            step = step * size + pl.program_id(d)
            steps *= size

        @pl.when(step == 0)
        def _():
            comm.start(c_ins, c_outs, sems)

        body(*ins, *outs, *scr)

        @pl.when(step == max(steps - 2, 0))
        def _():
            comm.middle(c_ins, c_outs, sems)

        @pl.when(step == steps - 1)
        def _():
            comm.finish(c_ins, c_outs, sems)

    any_spec = pl.BlockSpec(memory_space=pl.ANY)
    outs = pl.pallas_call(
        carrying, name=name, grid=grid, in_specs=in_specs + [any_spec] * n_c, out_specs=out_specs + [any_spec] * n_c,
        out_shape=out_shape + comm.out_shape, scratch_shapes=scratch_shapes + comm.scratch,
        compiler_params=_params(has_side_effects=True),
    )(*args, *comm.inputs)
    return list(outs[:n_out]), list(outs[n_out:])


def _in_proj_fwd(x, g_row, w_parts, name, comm=None):
    T, D = x.shape
    NS = w_parts[0].shape[-1]
    DP = w_parts[0].shape[-2]
    n_parts = len(w_parts)
    assert DP * n_parts == D
    TM = min(T, 512)

    def body(x_ref, g_ref, *refs):
        w_refs, (proj_ref, h_ref) = refs[:n_parts], refs[n_parts:]
        _, xh = _rms_stats(x_ref[...])
        h = (xh * g_ref[...]).astype(BF16)
        h_ref[...] = h
        for k in range(N_DEV):
            acc = _dot(h[:, 0:DP], w_refs[0][k])
            for p in range(1, n_parts):
                acc = acc + _dot(h[:, p * DP:(p + 1) * DP], w_refs[p][k])
            proj_ref[:, k * NS:(k + 1) * NS] = acc.astype(BF16)

    return _pcall(
        body, name, (T // TM,),
        in_specs=[pl.BlockSpec((TM, D), lambda i: (i, 0)), _pick(g_row)] + [_resident((N_DEV, DP, NS))] * n_parts,
        out_specs=[pl.BlockSpec((TM, N_DEV * NS), lambda i: (i, 0)),
                   pl.BlockSpec((TM, D), lambda i: (i, 0))],
        out_shape=[jax.ShapeDtypeStruct((T, N_DEV * NS), BF16), jax.ShapeDtypeStruct((T, D), BF16)],
        args=(x, g_row[0], *w_parts), comm=comm)


def _a_mix_fwd(proj, ln_g, ln_b, w_s, b_s, name, comm=None):
    T, E3 = proj.shape
    E = E3 // 3
    G, P = A_GROUPS, GMLP_BLOCK
    GD = E // G
    TB = min(T, 512)

    def body(p_ref, lg_ref, lb_ref, ws_ref, bs_ref, y_ref, xh_ref, dgl_ref, rstd_ref, v_s, us_s):
        def norm_chunk(ci, carry):
            rows = pl.ds(pl.multiple_of(ci * ROW_CHUNK, ROW_CHUNK), ROW_CHUNK)
            vg, dgl = _gelu_and_grad(p_ref[rows, E:2 * E].astype(F32))
            dgl_ref[rows, :] = dgl.astype(BF16)
            xc = vg - jnp.mean(vg, axis=-1, keepdims=True)
            rstd = lax.rsqrt(jnp.mean(xc * xc, axis=-1, keepdims=True) + EPS)
            rstd_ref[rows, :] = rstd
            xh = xc * rstd
            xh_ref[rows, :] = xh.astype(BF16)
            v_s[rows, :] = (xh * lg_ref[...] + lb_ref[...]).astype(BF16)
            return carry

        def gate_chunk(ci, carry):
            rows = pl.ds(pl.multiple_of(ci * ROW_CHUNK, ROW_CHUNK), ROW_CHUNK)
            z = p_ref[rows, 2 * E:3 * E].astype(F32)
            us_s[rows, :] = _gelu(p_ref[rows, 0:E].astype(F32)) * (z * _sigmoid(z))
            return carry

        lax.fori_loop(0, TB // ROW_CHUNK, norm_chunk, 0, unroll=2)
        lax.fori_loop(0, TB // ROW_CHUNK, gate_chunk, 0, unroll=2)
        mask = _spatial_mask()
        for g in range(G):
            wm = jnp.where(mask, ws_ref[g], 0.0).astype(BF16)
            cols = slice(g * GD, (g + 1) * GD)
            for b in range(TB // P):
                rows = slice(b * P, (b + 1) * P)
                mixed = _dot(wm, v_s[rows, cols]) + bs_ref[g]
                y_ref[rows, cols] = (us_s[rows, cols] * mixed).astype(BF16)

    return _pcall(
        body, name, (T // TB,),
        in_specs=[pl.BlockSpec((TB, E3), lambda i: (i, 0)),
                  _pick(ln_g), _pick(ln_b), _pick(w_s), _pick(b_s)],
        out_specs=[pl.BlockSpec((TB, E), lambda i: (i, 0)), pl.BlockSpec((TB, E), lambda i: (i, 0)),
                   pl.BlockSpec((TB, E), lambda i: (i, 0)), pl.BlockSpec((TB, 1), lambda i: (i, 0))],
        out_shape=[jax.ShapeDtypeStruct((T, E), BF16), jax.ShapeDtypeStruct((T, E), BF16),
                   jax.ShapeDtypeStruct((T, E), BF16), jax.ShapeDtypeStruct((T, 1), F32)],
        scratch_shapes=[pltpu.VMEM((TB, E), BF16), pltpu.VMEM((TB, E), F32)],
        args=(proj, ln_g[0], ln_b[0], w_s[0], b_s[0]), comm=comm)


def _window_sum_back(ext, win):
    s, k = ext, 1
    while k < win:
        s = s + pltpu.roll(s, k, axis=0)
        k *= 2
    return s


def _window_sum_ahead(ext, win):
    n = ext.shape[0]
    s, k = ext, 1
    while k < win:
        s = s + pltpu.roll(s, n - k, axis=0)
        k *= 2
    return s


def _inv_count(t0, rows, win):
    t1 = t0 + 1 + lax.broadcasted_iota(jnp.int32, (rows, 1), 0)
    return 1.0 / jnp.minimum(t1, win).astype(F32)


def _b_mix_fwd(proj, scale, wg_all, name, comm=None):
    T, E2 = proj.shape
    E = E2 // 2
    NG = len(POOL_WINDOWS)
    GB = E // NG
    TB = min(T, 256)
    RS = wg_all.shape[-2]

    def body(p_ref, sc_ref, wg_ref, y_ref, o_ref, carry_s):
        i = pl.program_id(0)

        @pl.when(i == 0)
        def _():
            carry_s[...] = jnp.zeros_like(carry_s)

        for g, win in enumerate(POOL_WINDOWS):
            cols = slice(g * GB, (g + 1) * GB)
            xg = p_ref[:, cols].astype(F32)
            ext = jnp.concatenate([carry_s[:, cols], xg], axis=0)
            pooled = _window_sum_back(ext, win)[HALO:, :] * _inv_count(i * TB, TB, win) - xg
            carry_s[:, cols] = xg[TB - HALO:, :]
            o = _dot(pooled.astype(BF16), wg_ref[:, g].reshape(GB, GB))
            o_ref[:, cols] = o.astype(BF16)
            z = p_ref[:, E + g * GB:E + (g + 1) * GB].astype(F32)
            y_ref[:, cols] = ((o * sc_ref[:, cols]) * (z * _sigmoid(z))).astype(BF16)

    return _pcall(
        body, name, (T // TB,),
        in_specs=[pl.BlockSpec((TB, E2), lambda i: (i, 0)),
                  pl.BlockSpec((1, E), lambda i: (0, 0)),
                  pl.BlockSpec((N_DEV, NG, RS, GB), lambda i: (0, 0, 0, 0))],
        out_specs=[pl.BlockSpec((TB, E), lambda i: (i, 0)), pl.BlockSpec((TB, E), lambda i: (i, 0))],
        out_shape=[jax.ShapeDtypeStruct((T, E), BF16), jax.ShapeDtypeStruct((T, E), BF16)],
        scratch_shapes=[pltpu.VMEM((HALO, E), F32)],
        args=(proj, scale, wg_all), comm=comm)


def _out_proj_fwd(y, w_all, x, g_row, name, comm=None):
    T, E = y.shape
    D = x.shape[1]
    ES = w_all.shape[-2]
    TM = min(T, 512)

    def body(y_ref, w_ref, x_ref, g_ref, xn_ref, out_ref):
        o = _dot(y_ref[...], w_ref[...].reshape(E, D))
        out_ref[...] = o
        _, oh = _rms_stats(o)
        xn_ref[...] = x_ref[...] + oh * g_ref[...]

    return _pcall(
        body, name, (T // TM,),
        in_specs=[pl.BlockSpec((TM, E), lambda i: (i, 0)),
                  pl.BlockSpec((N_DEV, ES, D), lambda i: (0, 0, 0)),
                  pl.BlockSpec((TM, D), lambda i: (i, 0)),
                  _pick(g_row)],
        out_specs=[pl.BlockSpec((TM, D), lambda i: (i, 0)), pl.BlockSpec((TM, D), lambda i: (i, 0))],
        out_shape=[jax.ShapeDtypeStruct((T, D), F32), jax.ShapeDtypeStruct((T, D), F32)],
        args=(y, w_all, x, g_row[0]), comm=comm)


def _loss_head(x, target, name):
    T, D = x.shape
    TM = min(T, 512)
    nT = T // TM

    def body(x_ref, t_ref, dx_ref, loss_ref, acc_s):
        i = pl.program_id(0)

        @pl.when(i == 0)
        def _():
            acc_s[...] = jnp.zeros_like(acc_s)

        e = x_ref[...] - t_ref[...]
        dx_ref[...] = e * (1.0 / D)
        acc_s[...] += jnp.sum(e * e, axis=0, keepdims=True)

        @pl.when(i == nT - 1)
        def _():
            total = jnp.sum(acc_s[...], axis=1, keepdims=True) * (0.5 / D)
            loss_ref[...] = jnp.broadcast_to(total, loss_ref.shape)

    return _pcall(
        body, name, (nT,),
        in_specs=[pl.BlockSpec((TM, D), lambda i: (i, 0)), pl.BlockSpec((TM, D), lambda i: (i, 0))],
        out_specs=[pl.BlockSpec((TM, D), lambda i: (i, 0)), pl.BlockSpec((1, 128), lambda i: (0, 0))],
        out_shape=[jax.ShapeDtypeStruct((T, D), F32), jax.ShapeDtypeStruct((1, 128), F32)],
        scratch_shapes=[pltpu.VMEM((1, D), F32)],
        args=(x, target))[0]


def _out_proj_bwd(dxn, out, g_row, w_all, y, name, comm=None):
    T, D = dxn.shape
    E = y.shape[1]
    ES = w_all.shape[-2]
    TM = min(T, 512)
    nT = T // TM

    def body(dxn_ref, out_ref, g_ref, w_ref, y_ref, dy_ref, dw_ref, dg_ref, acc_s, *sibling_scratch):
        i = pl.program_id(0)

        @pl.when(i == 0)
        def _():
            acc_s[...] = jnp.zeros_like(acc_s)
            dg_ref[...] = jnp.zeros_like(dg_ref)

        dxn_v = dxn_ref[...]
        r, oh = _rms_stats(out_ref[...])
        dg_ref[...] += jnp.sum(dxn_v * oh, axis=0, keepdims=True)
        dout = _rms_bwd(dxn_v, g_ref[...], r, oh).astype(BF16)
        dy_ref[...] = _dot_nt(dout, w_ref[...].reshape(E, D)).astype(BF16)
        acc_s[...] += _dot_tn(y_ref[...], dout)

        @pl.when(i == nT - 1)
        def _():
            def slab_of(k):
                return acc_s[pl.ds(pl.multiple_of(k * ES, ES), ES), :]

            _sum_with_sibling(slab_of, dw_ref, sibling_scratch)

    return _pcall(
        body, name, (nT,),
        in_specs=[pl.BlockSpec((TM, D), lambda i: (i, 0)),
                  pl.BlockSpec((TM, D), lambda i: (i, 0)),
                  _pick(g_row),
                  pl.BlockSpec((N_DEV, ES, D), lambda i: (0, 0, 0)),
                  pl.BlockSpec((TM, E), lambda i: (i, 0))],
        out_specs=[pl.BlockSpec((TM, E), lambda i: (i, 0)),
                   pl.BlockSpec((CHIPS, ES, D), lambda i: (0, 0, 0)),
                   pl.BlockSpec((1, D), lambda i: (0, 0))],
        out_shape=[jax.ShapeDtypeStruct((T, E), BF16), jax.ShapeDtypeStruct((CHIPS, ES, D), BF16),
                   jax.ShapeDtypeStruct((1, D), F32)],
        scratch_shapes=[pltpu.VMEM((E, D), F32)] + _sibling_scratch((ES, D)),
        args=(dxn, out, g_row[0], w_all, y), comm=comm)


def _a_mix_bwd(proj, dy, xh, dgl, rstd, ln_g, ln_b, w_s, b_s, name, comm=None):
    T, E3 = proj.shape
    E = E3 // 3
    G, P = A_GROUPS, GMLP_BLOCK
    GD = E // G
    TB = min(T, 256)

    def body(up_ref, zp_ref, dy_ref, xh_ref, dgl_ref, rstd_ref, lg_ref, lb_ref, ws_ref, bs_ref,
             dp_ref, dws_ref, dbs_ref, dlg_ref, dlb_ref, v_s, a_s, bz_s, c_s, dv_s):
        @pl.when(pl.program_id(0) == 0)
        def _():
            dws_ref[...] = jnp.zeros_like(dws_ref)
            dbs_ref[...] = jnp.zeros_like(dbs_ref)
            dlg_ref[...] = jnp.zeros_like(dlg_ref)
            dlb_ref[...] = jnp.zeros_like(dlb_ref)

        def recompute(ci, carry):
            rows = pl.ds(pl.multiple_of(ci * ROW_CHUNK, ROW_CHUNK), ROW_CHUNK)
            v_s[rows, :] = (xh_ref[rows, :].astype(F32) * lg_ref[...] + lb_ref[...]).astype(BF16)
            u, du = _gelu_and_grad(up_ref[rows, :].astype(F32))
            z = zp_ref[rows, :].astype(F32)
            sg = _sigmoid(z)
            s = z * sg
            ds = sg * (1.0 + z * (1.0 - sg))
            dyv = dy_ref[rows, :].astype(F32)
            a_s[rows, :] = dyv * s * du
            bz_s[rows, :] = dyv * u * ds
            c_s[rows, :] = (dyv * u * s).astype(BF16)
            return carry

        lax.fori_loop(0, TB // ROW_CHUNK, recompute, 0, unroll=2)

        mask = _spatial_mask()
        mask_t = _spatial_mask(transposed=True)
        for g in range(G):
            w_g = ws_ref[g]
            wm = jnp.where(mask, w_g, 0.0).astype(BF16)
            wm_t = jnp.where(mask_t, w_g.T, 0.0).astype(BF16)
            cols = slice(g * GD, (g + 1) * GD)
            dws_g = jnp.zeros((P, P), F32)
            dbs_g = jnp.zeros((SUBLANES, P), F32)
            for b in range(TB // P):
                rows = slice(b * P, (b + 1) * P)
                vb = v_s[rows, cols]
                cb = c_s[rows, cols]
                mixed = _dot(wm, vb) + bs_ref[g]
                dp_ref[rows, g * GD:(g + 1) * GD] = (a_s[rows, cols] * mixed).astype(BF16)
                dp_ref[rows, 2 * E + g * GD:2 * E + (g + 1) * GD] = (bz_s[rows, cols] * mixed).astype(BF16)
                dv_s[rows, cols] = _dot(wm_t, cb)
                dws_g = dws_g + _dot_nt(cb, vb)
                dbs_g = dbs_g + _dot_nt(jnp.ones((SUBLANES, GD), BF16), cb)
            dws_ref[g] += jnp.where(mask, dws_g, 0.0)
            dbs_ref[g:g + 1, :] += dbs_g[0:1, :]

        def ln_bwd(ci, carry):
            rows = pl.ds(pl.multiple_of(ci * ROW_CHUNK, ROW_CHUNK), ROW_CHUNK)
            dv = dv_s[rows, :]
            xh = xh_ref[rows, :].astype(F32)
            dlg_ref[...] += jnp.sum(dv * xh, axis=0, keepdims=True)
            dlb_ref[...] += jnp.sum(dv, axis=0, keepdims=True)
            dxh = dv * lg_ref[...]
            dvg = rstd_ref[rows, :] * (dxh - jnp.mean(dxh, axis=-1, keepdims=True)
                                       - xh * jnp.mean(dxh * xh, axis=-1, keepdims=True))
            dp_ref[rows, E:2 * E] = (dvg * dgl_ref[rows, :].astype(F32)).astype(BF16)
            return carry

        lax.fori_loop(0, TB // ROW_CHUNK, ln_bwd, 0, unroll=2)

    return _pcall(
        body, name, (T // TB,),
        in_specs=[pl.BlockSpec((TB, E), lambda i: (i, 0)),
                  pl.BlockSpec((TB, E), lambda i: (i, 2)),
                  pl.BlockSpec((TB, E), lambda i: (i, 0)),
                  pl.BlockSpec((TB, E), lambda i: (i, 0)),
                  pl.BlockSpec((TB, E), lambda i: (i, 0)),
                  pl.BlockSpec((TB, 1), lambda i: (i, 0)),
                  _pick(ln_g), _pick(ln_b), _pick(w_s), _pick(b_s)],
        out_specs=[pl.BlockSpec((TB, E3), lambda i: (i, 0)),
                   pl.BlockSpec((G, P, P), lambda i: (0, 0, 0)),
                   pl.BlockSpec((G, P), lambda i: (0, 0)),
                   pl.BlockSpec((1, E), lambda i: (0, 0)),
                   pl.BlockSpec((1, E), lambda i: (0, 0))],
        out_shape=[jax.ShapeDtypeStruct((T, E3), BF16), jax.ShapeDtypeStruct((G, P, P), F32),
                   jax.ShapeDtypeStruct((G, P), F32), jax.ShapeDtypeStruct((1, E), F32),
                   jax.ShapeDtypeStruct((1, E), F32)],
        scratch_shapes=[pltpu.VMEM((TB, E), BF16), pltpu.VMEM((TB, E), F32), pltpu.VMEM((TB, E), F32),
                        pltpu.VMEM((TB, E), BF16), pltpu.VMEM((TB, E), F32)],
        args=(proj, proj, dy, xh, dgl, rstd, ln_g[0], ln_b[0], w_s[0], b_s[0]), comm=comm)


def _b_mix_bwd(proj, dy, o, scale, wg_all, name, comm=None):
    T, E2 = proj.shape
    E = E2 // 2
    NG = len(POOL_WINDOWS)
    GB = E // NG
    TB = min(T, 256)
    nT = T // TB
    RS = wg_all.shape[-2]
    halo_per_tile = TB // HALO

    def body(p_ref, halo_ref, dy_ref, o_ref, sc_ref, wg_ref, dp_ref, dsc_ref, dwg_ref, acc_s, carry_s,
             *sibling_scratch):
        i = pl.program_id(0)
        tile = nT - 1 - i

        @pl.when(i == 0)
        def _():
            acc_s[...] = jnp.zeros_like(acc_s)
            carry_s[...] = jnp.zeros_like(carry_s)
            dsc_ref[...] = jnp.zeros_like(dsc_ref)

        has_history = (tile > 0).astype(F32)
        for g, win in enumerate(POOL_WINDOWS):
            cols = slice(g * GB, (g + 1) * GB)
            inv = _inv_count(tile * TB, TB, win)
            xg = p_ref[:, cols].astype(F32)
            ext = jnp.concatenate([halo_ref[:, cols].astype(F32) * has_history, xg], axis=0)
            pooled = _window_sum_back(ext, win)[HALO:, :] * inv - xg
            z = p_ref[:, E + g * GB:E + (g + 1) * GB].astype(F32)
            sg = _sigmoid(z)
            dyv = dy_ref[:, cols].astype(F32)
            ov = o_ref[:, cols].astype(F32)
            sc = sc_ref[:, cols]
            dmixed = dyv * (z * sg)
            dsc_ref[:, cols] += jnp.sum(dmixed * ov, axis=0, keepdims=True)
            dz = dyv * (ov * sc) * (sg * (1.0 + z * (1.0 - sg)))
            do = (dmixed * sc).astype(BF16)
            acc_s[:, g] += _dot_tn(pooled.astype(BF16), do).reshape(N_DEV, RS, GB)
            dpool = _dot_nt(do, wg_ref[:, g].reshape(GB, GB))
            q = dpool * inv
            ext_q = jnp.concatenate([q, carry_s[:, cols]], axis=0)
            dxb = _window_sum_ahead(ext_q, win)[:TB, :] - dpool
            carry_s[:, cols] = q[:HALO, :]
            dp_ref[:, cols] = dxb.astype(BF16)
            dp_ref[:, E + g * GB:E + (g + 1) * GB] = dz.astype(BF16)

        @pl.when(i == nT - 1)
        def _():
            _sum_with_sibling(lambda k: acc_s[k], dwg_ref, sibling_scratch)

    return _pcall(
        body, name, (nT,),
        in_specs=[pl.BlockSpec((TB, E2), lambda i: (nT - 1 - i, 0)),
                  pl.BlockSpec((HALO, E), lambda i: (jnp.maximum((nT - 1 - i) * halo_per_tile - 1, 0), 0)),
                  pl.BlockSpec((TB, E), lambda i: (nT - 1 - i, 0)),
                  pl.BlockSpec((TB, E), lambda i: (nT - 1 - i, 0)),
                  pl.BlockSpec((1, E), lambda i: (0, 0)),
                  pl.BlockSpec((N_DEV, NG, RS, GB), lambda i: (0, 0, 0, 0))],
        out_specs=[pl.BlockSpec((TB, E2), lambda i: (nT - 1 - i, 0)),
                   pl.BlockSpec((1, E), lambda i: (0, 0)),
                   pl.BlockSpec((CHIPS, NG, RS, GB), lambda i: (0, 0, 0, 0))],
        out_shape=[jax.ShapeDtypeStruct((T, E2), BF16), jax.ShapeDtypeStruct((1, E), F32),
                   jax.ShapeDtypeStruct((CHIPS, NG, RS, GB), BF16)],
        scratch_shapes=[pltpu.VMEM((N_DEV, NG, RS, GB), F32), pltpu.VMEM((HALO, E), F32)]
        + _sibling_scratch((NG, RS, GB)),
        args=(proj, proj, dy, o, scale, wg_all), comm=comm)


def _in_proj_bwd_dx(dproj, w_parts, x, g_row, dxn, name, comm=None):
    T, D = x.shape
    NS = w_parts[0].shape[-1]
    DP = w_parts[0].shape[-2]
    n_parts = len(w_parts)
    TM = min(T, 512)

    def body(dp_ref, *refs):
        w_refs, (x_ref, g_ref, dxn_ref, dx_ref, dg_ref) = refs[:n_parts], refs[n_parts:]

        @pl.when(pl.program_id(0) == 0)
        def _():
            dg_ref[...] = jnp.zeros_like(dg_ref)

        pieces = []
        for w_ref in w_refs:
            piece = _dot_nt(dp_ref[:, 0:NS], w_ref[0])
            for k in range(1, N_DEV):
                piece = piece + _dot_nt(dp_ref[:, k * NS:(k + 1) * NS], w_ref[k])
            pieces.append(piece)
        dh = pieces[0] if n_parts == 1 else jnp.concatenate(pieces, axis=1)
        r, xh = _rms_stats(x_ref[...])
        dg_ref[...] += jnp.sum(dh * xh, axis=0, keepdims=True)
        dx_ref[...] = dxn_ref[...] + _rms_bwd(dh, g_ref[...], r, xh)

    return _pcall(
        body, name, (T // TM,),
        in_specs=[pl.BlockSpec((TM, N_DEV * NS), lambda i: (i, 0))] + [_resident((N_DEV, DP, NS))] * n_parts
        + [pl.BlockSpec((TM, D), lambda i: (i, 0)),
           _pick(g_row),
           pl.BlockSpec((TM, D), lambda i: (i, 0))],
        out_specs=[pl.BlockSpec((TM, D), lambda i: (i, 0)), pl.BlockSpec((1, D), lambda i: (0, 0))],
        out_shape=[jax.ShapeDtypeStruct((T, D), F32), jax.ShapeDtypeStruct((1, D), F32)],
        args=(dproj, *w_parts, x, g_row[0], dxn), comm=comm)


STAGE_SLOTS = 2


def _dw_in(h, dproj, name, comm=None):
    T, D = h.shape
    NS = dproj.shape[1] // N_DEV
    TK = min(T, 2048)
    nK = T // TK

    def body(h_ref, dp_ref, q_ref, acc_s, stage_s, land_s, send_sems, recv_sems):
        k, t = pl.program_id(0), pl.program_id(1)
        c = lax.axis_index("c")

        def to_sibling(q):
            return _to_sibling(stage_s, land_s, send_sems, recv_sems, q % STAGE_SLOTS, q)

        @pl.when(t == 0)
        def _():
            acc_s[...] = jnp.zeros_like(acc_s)

        acc_s[...] += _dot_tn(h_ref[...], dp_ref[...])

        @pl.when(t == nK - 1)
        def _():
            q = k // 2

            @pl.when(k % 2 == c)
            def _():
                q_ref[q] = acc_s[...].astype(BF16)

            @pl.when(k % 2 != c)
            def _():
                @pl.when(q >= STAGE_SLOTS)
                def _():
                    to_sibling(q - STAGE_SLOTS).wait_send()

                stage_s[q % STAGE_SLOTS] = acc_s[...].astype(BF16)
                to_sibling(q).start()

        @pl.when((k == N_DEV - 1) & (t == nK - 1))
        def _():
            for q in range(CHIPS - STAGE_SLOTS, CHIPS):
                to_sibling(q).wait_send()
            for q in range(CHIPS):
                to_sibling(q).wait_recv()
                q_ref[q] = (q_ref[q].astype(F32) + land_s[q].astype(F32)).astype(BF16)

    return _pcall(
        body, name, (N_DEV, nK),
        in_specs=[pl.BlockSpec((TK, D), lambda k, t: (t, 0)), pl.BlockSpec((TK, NS), lambda k, t: (t, k))],
        out_specs=[pl.BlockSpec((CHIPS, D, NS), lambda k, t: (0, 0, 0))],
        out_shape=[jax.ShapeDtypeStruct((CHIPS, D, NS), BF16)],
        scratch_shapes=[pltpu.VMEM((D, NS), F32)] + _sibling_scratch((D, NS), STAGE_SLOTS),
        args=(h, dproj), comm=comm)


def _reduce_adam(recvs, w, m, v, name):
    L, R, C = w.shape
    assert len(recvs) == L
    senders = recvs[0].shape[0]
    TR = R
    for cand in (256, 128, 64, 32, 16):
        if R % cand == 0 and R > cand:
            TR = cand
            break
    nR = R // TR
    c1 = 1.0 - ADAM_B1 ** ADAM_STEP
    c2 = 1.0 - ADAM_B2 ** ADAM_STEP

    def body(*refs):
        recv_refs = refs[:L]
        w_ref, m_ref, v_ref, g_ref, d_ref, nm_ref, nv_ref, g_s = refs[L:]
        layer = pl.program_id(0)
        for l in range(L):
            @pl.when(layer == l)
            def _(l=l):
                acc = recv_refs[l][0].astype(F32)
                for j in range(1, senders):
                    acc = acc + recv_refs[l][j].astype(F32)
                g_s[...] = acc

        g = g_s[...]
        g_ref[...] = g
        nm = ADAM_B1 * m_ref[...] + (1.0 - ADAM_B1) * g
        nv = ADAM_B2 * v_ref[...] + (1.0 - ADAM_B2) * (g * g)
        nm_ref[...] = nm
        nv_ref[...] = nv
        d_ref[...] = -ADAM_LR * ((nm / c1) / (jnp.sqrt(nv / c2) + ADAM_EPS) + ADAM_WD * w_ref[...])

    def recv_spec(l):
        def index(layer, t):
            before = jnp.where(layer < l, 0, nR - 1)
            return (0, jnp.where(layer == l, t, before), 0)
        return pl.BlockSpec((senders, TR, C), index)

    wspec = pl.BlockSpec((None, TR, C), lambda layer, t: (layer, t, 0))
    out = jax.ShapeDtypeStruct((L, R, C), F32)
    return _pcall(
        body, name, (L, nR),
        in_specs=[recv_spec(l) for l in range(L)] + [wspec] * 3,
        out_specs=[wspec] * 4, out_shape=[out] * 4,
        scratch_shapes=[pltpu.VMEM((TR, C), F32)],
        args=(*recvs, w, m, v))[0]


def _adam_replicated(gathered, w, m, v, name):
    n_params = len(w)
    layers = [len(g) for g in gathered]
    flat = [g for per_param in gathered for g in per_param]
    c1 = 1.0 - ADAM_B1 ** ADAM_STEP
    c2 = 1.0 - ADAM_B2 ** ADAM_STEP

    def body(*refs):
        g_refs, refs = refs[:len(flat)], refs[len(flat):]
        w_refs, m_refs, v_refs = refs[:n_params], refs[n_params:2 * n_params], refs[2 * n_params:3 * n_params]
        out_refs = refs[3 * n_params:]
        at = 0
        for n in range(n_params):
            g_out, d_out, nm_out, nv_out = out_refs[4 * n:4 * n + 4]
            for l in range(layers[n]):
                g_ref = g_refs[at]
                at += 1
                at_l = l if w_refs[n].ndim > 2 else slice(l, l + 1)
                g = g_ref[0]
                for s in range(1, N_DEV):
                    g = g + g_ref[s]
                nm = ADAM_B1 * m_refs[n][at_l] + (1.0 - ADAM_B1) * g
                nv = ADAM_B2 * v_refs[n][at_l] + (1.0 - ADAM_B2) * (g * g)
                g_out[at_l] = g
                nm_out[at_l] = nm
                nv_out[at_l] = nv
                d_out[at_l] = -ADAM_LR * ((nm / c1) / (jnp.sqrt(nv / c2) + ADAM_EPS) + ADAM_WD * w_refs[n][at_l])

    vmem = pl.BlockSpec(memory_space=pltpu.VMEM)
    args = [*flat, *w, *m, *v]
    outs = pl.pallas_call(
        body, name=name, in_specs=[vmem] * len(args), out_specs=[vmem] * (4 * n_params),
        out_shape=[jax.ShapeDtypeStruct(a.shape, F32) for a in w for _ in range(4)],
        compiler_params=_params(),
    )(*args)
    return [outs[4 * n:4 * n + 4] for n in range(n_params)]


A_W_IN_PARTS = 4
PACK_LANES = 128
PACK_ROWS_MULTIPLE = 256


def _pack(arrays):
    flat = jnp.concatenate([a.reshape(-1) for a in arrays])
    tile = PACK_LANES * PACK_ROWS_MULTIPLE
    padded = -(-flat.shape[0] // tile) * tile
    return jnp.pad(flat, (0, padded - flat.shape[0])).reshape(1, padded // PACK_LANES, PACK_LANES)


def _unpack(packed, like):
    flat = packed.reshape(-1)
    out, at = [], 0
    for a in like:
        out.append(flat[at:at + a.size].reshape(a.shape))
        at += a.size
    return out


def kernel(x, norm_pre, norm_post, a_w_in, a_ln_g, a_ln_b, a_w_s, a_b_s, a_w_out, b_w_in, b_w_grp, b_scale, b_w_out, loss_target, m_norm_pre, m_norm_post, m_a_w_in, m_a_ln_g, m_a_ln_b, m_a_w_s, m_a_b_s, m_a_w_out, m_b_w_in, m_b_w_grp, m_b_scale, m_b_w_out, v_norm_pre, v_norm_post, v_a_w_in, v_a_ln_g, v_a_ln_b, v_a_w_s, v_a_b_s, v_a_w_out, v_b_w_in, v_b_w_grp, v_b_scale, v_b_w_out):
    weights = dict(norm_pre=norm_pre, norm_post=norm_post, a_w_in=a_w_in, a_ln_g=a_ln_g, a_ln_b=a_ln_b, a_w_s=a_w_s,
                   a_b_s=a_b_s, a_w_out=a_w_out, b_w_in=b_w_in, b_w_grp=b_w_grp, b_scale=b_scale, b_w_out=b_w_out)
    mom_m = dict(norm_pre=m_norm_pre, norm_post=m_norm_post, a_w_in=m_a_w_in, a_ln_g=m_a_ln_g, a_ln_b=m_a_ln_b,
                 a_w_s=m_a_w_s, a_b_s=m_a_b_s, a_w_out=m_a_w_out, b_w_in=m_b_w_in, b_w_grp=m_b_w_grp,
                 b_scale=m_b_scale, b_w_out=m_b_w_out)
    mom_v = dict(norm_pre=v_norm_pre, norm_post=v_norm_post, a_w_in=v_a_w_in, a_ln_g=v_a_ln_g, a_ln_b=v_a_ln_b,
                 a_w_s=v_a_w_s, a_b_s=v_a_b_s, a_w_out=v_a_w_out, b_w_in=v_b_w_in, b_w_grp=v_b_w_grp,
                 b_scale=v_b_scale, b_w_out=v_b_w_out)
    names = list(weights)

    depth = norm_pre.shape[0]
    x0 = x[0]
    target = loss_target[0]
    T, D = x0.shape
    E = a_ln_g.shape[1]
    G, P = A_GROUPS, GMLP_BLOCK
    pre3, post3 = norm_pre.reshape(depth, 1, D), norm_post.reshape(depth, 1, D)
    ln_g3, ln_b3 = a_ln_g.reshape(-1, 1, E), a_ln_b.reshape(-1, 1, E)
    b_s4 = a_b_s.reshape(-1, G, P, 1)

    def shards_of(i):
        j = i // 2
        if i % 2 == 0:
            w = a_w_in[j].astype(BF16)
            rows = w.shape[0] // A_W_IN_PARTS
            parts = {f"w_in_{p}": w[p * rows:(p + 1) * rows] for p in range(A_W_IN_PARTS)}
            return dict(**parts, w_out=a_w_out[j].astype(BF16))
        return dict(w_in=b_w_in[j].astype(BF16), w_out=b_w_out[j].astype(BF16), grp=b_w_grp[j].astype(BF16))

    shard = [shards_of(i) for i in range(depth)]
    full = [dict() for _ in range(depth)]

    def gather_into(keys, got):
        for (i, key), arr in zip(keys, got):
            full[i][key] = arr

    def w_in_of(i):
        return [(i, k) for k in shard[i] if k.startswith("w_in")]

    def rest_of(i):
        return [(i, k) for k in shard[i] if not k.startswith("w_in")]

    def gather_of(keys):
        return _Gather([shard[a][k] for a, k in keys]) if keys else None

    first = _comm_only(_Gather([shard[0][k] for _, k in w_in_of(0)] + [b_scale]), "gather_first")
    gather_into(w_in_of(0), first)
    scale_full = jnp.transpose(first[-1], (1, 0, 2)).reshape(b_scale.shape[0], 1, E)

    saved = []
    xi = x0
    for i in range(depth):
        j = i // 2
        g_pre, g_post = (pre3, i), (post3, i)
        keys_in, keys_mix, keys_out = [], [], []
        if i % 2 == 0:
            keys_mix = [(0, "w_out")] if i == 0 else []
            if i + 1 < depth:
                keys_in = w_in_of(i + 1)
                keys_mix = keys_mix + rest_of(i + 1)
            if i + 2 < depth:
                keys_out = w_in_of(i + 2)[:1]
        elif i + 1 < depth:
            keys_in = w_in_of(i + 1)[1:3]
            keys_mix = rest_of(i + 1)
            keys_out = w_in_of(i + 1)[3:]
        kind = "a" if i % 2 == 0 else "b"
        (proj, h), got = _in_proj_fwd(xi, g_pre, [full[i][k] for _, k in w_in_of(i)], f"{kind}_in_fwd_{i}",
                                      gather_of(keys_in))
        gather_into(keys_in, got)
        if i % 2 == 0:
            (y, *o), got = _a_mix_fwd(proj, (ln_g3, j), (ln_b3, j), (a_w_s, j), (b_s4, j),
                                      f"a_mix_fwd_{i}", gather_of(keys_mix))
        else:
            (y, o), got = _b_mix_fwd(proj, scale_full[j], full[i]["grp"], f"b_mix_fwd_{i}", gather_of(keys_mix))
        gather_into(keys_mix, got)
        (x_next, out), got = _out_proj_fwd(y, full[i]["w_out"], xi, g_post, f"{kind}_out_fwd_{i}",
                                           gather_of(keys_out))
        gather_into(keys_out, got)
        saved.append((xi, h, proj, y, out, o))
        xi = x_next

    dx, loss_row = _loss_head(xi, target, "loss_head")

    n_a, n_b = a_ln_g.shape[0], b_scale.shape[0]
    d_pre, d_post = [None] * depth, [None] * depth
    recv = {"a_w_in": [None] * n_a, "a_w_out": [None] * n_a, "b_w_in": [None] * n_b,
            "b_w_grp": [None] * n_b, "b_w_out": [None] * n_b, "b_scale": [None] * n_b}
    small_a = [[None] * 4 for _ in range(n_a)]
    kinds = ("chips", "devices", "gather")

    def carried(items):
        of = {kind: [it[3] for it in items if it[0] == kind] for kind in kinds}
        comms = (([_Exchange(of["chips"], chips_only=True)] if of["chips"] else [])
                 + ([_Exchange(of["devices"])] if of["devices"] else [])
                 + ([_Gather(of["gather"])] if of["gather"] else []))
        return None if not comms else comms[0] if len(comms) == 1 else _Together(comms)

    def received(items, got):
        ordered = [it for kind in kinds for it in items if it[0] == kind]
        for it, arr in zip(ordered, got):
            if it[0] == "gather":
                small_a[it[2][0]][it[2][1]] = arr
            else:
                recv[it[1]][it[2]] = arr

    pending = []
    small_pending = []
    for i in reversed(range(depth)):
        j = i // 2
        xi, h, proj, y, out, o = saved[i]
        g_pre, g_post = (pre3, i), (post3, i)
        if i % 2 == 0:
            (dy, dw_out, d_post[i]), _ = _out_proj_bwd(dx, out, g_post, full[i]["w_out"], y, f"a_out_bwd_{i}")
            items, pending = pending, []
            (dproj, d_w_s, dbs, d_ln_g, d_ln_b), got = _a_mix_bwd(
                proj, dy, *o, (ln_g3, j), (ln_b3, j), (a_w_s, j), (b_s4, j),
                f"a_mix_bwd_{i}", carried(items))
            received(items, got)
            small_pending += [("gather", "small", (j, n), part) for n, part in enumerate((d_w_s, dbs, d_ln_g, d_ln_b))]
            items = []
            if i == 0:
                items, small_pending = [("chips", "a_w_out", j, dw_out)] + small_pending, []
            (dw_in,), got = _dw_in(h, dproj, f"a_dw_in_{i}", carried(items))
            received(items, got)
            items = [("chips", "a_w_in", j, dw_in)] if i == 0 else [("chips", "a_w_out", j, dw_out)]
            (dx, d_pre[i]), got = _in_proj_bwd_dx(dproj, [full[i][k] for _, k in w_in_of(i)], xi, g_pre, dx,
                                                  f"a_in_bwd_{i}", carried(items))
            received(items, got)
            if i > 0:
                pending.append(("chips", "a_w_in", j, dw_in))
        else:
            items, small_pending = small_pending, []
            (dy, dw_out, d_post[i]), got = _out_proj_bwd(dx, out, g_post, full[i]["w_out"], y, f"b_out_bwd_{i}",
                                                        carried(items))
            received(items, got)
            items, pending = pending, []
            (dproj, dsc, dw_grp), got = _b_mix_bwd(proj, dy, o, scale_full[j], full[i]["grp"], f"b_mix_bwd_{i}",
                                                  carried(items))
            received(items, got)
            (dx, d_pre[i]), _ = _in_proj_bwd_dx(dproj, [full[i][k] for _, k in w_in_of(i)], xi, g_pre, dx,
                                                f"b_in_bwd_{i}")
            (dw_in,), _ = _dw_in(h, dproj, f"b_dw_in_{i}")
            pending += [("chips", "b_w_out", j, dw_out), ("chips", "b_w_grp", j, dw_grp), ("chips", "b_w_in", j, dw_in),
                        ("devices", "b_scale", j, dsc.reshape(N_DEV, 1, E // N_DEV))]
    assert not pending and not small_pending

    gathered = _comm_only(_Gather([_pack([*d_pre, *d_post, loss_row[:, :1]])[0]]), "gather_norm_grads")
    results = {k: [None] * 4 for k in names}
    no_state = jnp.zeros((1, 1), F32)
    norm_like = [norm_pre, norm_post, no_state]
    outs = _reduce_adam([gathered[-1]], _pack(norm_like), _pack([m_norm_pre, m_norm_post, no_state]),
                        _pack([v_norm_pre, v_norm_post, no_state]), "adam_norms")
    for q, packed in enumerate(outs):
        results["norm_pre"][q], results["norm_post"][q], summed = _unpack(packed, norm_like)
        if q == 0:
            loss = summed[0, 0]
    a_small = ("a_w_s", "a_b_s", "a_ln_g", "a_ln_b")
    outs = _adam_replicated([[small_a[j][n] for j in range(n_a)] for n in range(len(a_small))],
                            [weights[k] for k in a_small], [mom_m[k] for k in a_small],
                            [mom_v[k] for k in a_small], "adam_small")
    for k, four in zip(a_small, outs):
        results[k] = list(four)

    def shard_view(a):
        return a.reshape(a.shape[0], -1, a.shape[-1])

    for k in ("a_w_in", "a_w_out", "b_w_in", "b_w_grp", "b_w_out"):
        w3 = shard_view(weights[k])
        recvs = [r.reshape(r.shape[0], w3.shape[1], w3.shape[2]) for r in recv[k]]
        outs = _reduce_adam(recvs, w3, shard_view(mom_m[k]), shard_view(mom_v[k]), f"adam_{k}")
        results[k] = [o_.reshape(weights[k].shape) for o_ in outs]
    sc_recv = jnp.concatenate(recv["b_scale"], axis=1)
    outs = _reduce_adam([sc_recv], b_scale[None], m_b_scale[None], v_b_scale[None], "adam_b_scale")
    results["b_scale"] = [o_[0] for o_ in outs]

    grad_x = dx[None]
    return (loss, grad_x, *[results[k][0] for k in names], *[results[k][1] for k in names],
            *[results[k][2] for k in names], *[results[k][3] for k in names])
```

```python
import jax
import jax.numpy as jnp
from jax import lax
from jax.experimental import pallas as pl
from jax.experimental.pallas import tpu as pltpu

F32 = jnp.float32
BF16 = jnp.bfloat16
MESH = pl.DeviceIdType.MESH

N_DEV = 8
EPS = 1e-6
CHUNK = 64
GMLP_BLOCK = 128
A_GROUPS = 8
POOL_WINDOWS = (2, 4, 8, 16)
HALO = 16
SUBLANES = 8
ADAM_LR = 0.001
ADAM_B1 = 0.9
ADAM_B2 = 0.999
ADAM_EPS = 1e-08
ADAM_WD = 0.01
ADAM_STEP = 10
GELU_C = 0.7978845608028654
GELU_A = 0.044715
ROW_CHUNK = 16
VMEM_LIMIT_BYTES = 56 * 1024 * 1024


def _params(**kw):
    return pltpu.CompilerParams(vmem_limit_bytes=VMEM_LIMIT_BYTES, **kw)


def _gelu(x):
    return 0.5 * x * (1.0 + jnp.tanh(GELU_C * (x + GELU_A * (x * x * x))))


def _gelu_and_grad(x):
    x2 = x * x
    t = jnp.tanh(GELU_C * (x + GELU_A * (x2 * x)))
    val = 0.5 * x * (1.0 + t)
    grad = 0.5 * (1.0 + t) + 0.5 * x * (1.0 - t * t) * (GELU_C * (1.0 + 3.0 * GELU_A * x2))
    return val, grad


def _sigmoid(z):
    return 0.5 * jnp.tanh(0.5 * z) + 0.5


def _dot(a, b):
    return jnp.dot(a, b, preferred_element_type=F32)


def _dot_nt(a, b):
    return lax.dot_general(a, b, (((1,), (1,)), ((), ())), preferred_element_type=F32)


def _dot_tn(a, b):
    return lax.dot_general(a, b, (((0,), (0,)), ((), ())), preferred_element_type=F32)


def _rms_stats(xf):
    r = lax.rsqrt(jnp.mean(xf * xf, axis=-1, keepdims=True) + EPS)
    return r, xf * r


def _rms_bwd(dy, g, r, xh):
    dxh = dy * g
    return r * (dxh - xh * jnp.mean(dxh * xh, axis=-1, keepdims=True))


def _resident(shape):
    return pl.BlockSpec(shape, lambda *_: (0,) * len(shape), pipeline_mode=pl.Buffered(1))


def _pick(stacked):
    arr, index = stacked
    return pl.BlockSpec((None,) + arr.shape[1:], lambda *_: (index,) + (0,) * (arr.ndim - 1))


def _spatial_mask(transposed=False):
    p = lax.broadcasted_iota(jnp.int32, (GMLP_BLOCK, GMLP_BLOCK), 0)
    q = lax.broadcasted_iota(jnp.int32, (GMLP_BLOCK, GMLP_BLOCK), 1)
    if transposed:
        p, q = q, p
    return (q // CHUNK) <= (p // CHUNK)


def _position():
    x, y, c = lax.axis_index("x"), lax.axis_index("y"), lax.axis_index("c")
    return x, y, c


def _comm_scratch(n):
    return [pltpu.SemaphoreType.DMA((n, 7)), pltpu.SemaphoreType.DMA((n, 7)), pltpu.SemaphoreType.DMA((n,))]


class _Gather:
    def __init__(self, arrs):
        self.inputs = list(arrs)
        self.out_shape = [jax.ShapeDtypeStruct((N_DEV,) + a.shape, a.dtype) for a in arrs]
        self.scratch = _comm_scratch(len(arrs))

    def _plan(self, ins, outs, sems):
        send_sems, recv_sems, local_sems = sems
        n = len(ins)
        x, y, c = _position()
        sibling = (x, y, 1 - c)
        chips = [(1 - x, y), (x, 1 - y), (1 - x, 1 - y)]

        def index(px, py, pc):
            return 4 * px + 2 * py + pc

        def copy(a, k, block, to, src=None):
            return pltpu.make_async_remote_copy(
                src_ref=outs[a].at[block] if src is None else src, dst_ref=outs[a].at[block],
                send_sem=send_sems.at[a, k], recv_sem=recv_sems.at[a, k], device_id=to, device_id_type=MESH)

        me = index(x, y, c)
        own = [pltpu.make_async_copy(ins[a], outs[a].at[me], local_sems.at[a]) for a in range(n)]
        first = []
        for a in range(n):
            first.append(copy(a, 0, me, sibling, src=ins[a]))
            for j, chip in enumerate(chips):
                first.append(copy(a, 1 + j, me, (*chip, c), src=ins[a]))
        return n, (x, y, c), sibling, chips, index, copy, own, first

    def start(self, ins, outs, sems):
        _, _, _, _, _, _, own, first = self._plan(ins, outs, sems)
        for cp in own + first:
            cp.start()

    def middle(self, ins, outs, sems):
        n, me, sibling, chips, index, copy, _, _ = self._plan(ins, outs, sems)
        for j, chip in enumerate(chips):
            for a in range(n):
                copy(a, 1 + j, index(*chip, me[2]), me).wait_recv()
                copy(a, 4 + j, index(*chip, me[2]), sibling).start()

    def finish(self, ins, outs, sems):
        n, me, sibling, chips, index, copy, own, first = self._plan(ins, outs, sems)
        c = me[2]
        passed = [copy(a, 4 + j, index(*chip, c), sibling) for j, chip in enumerate(chips) for a in range(n)]
        for a in range(n):
            copy(a, 0, index(me[0], me[1], 1 - c), me).wait_recv()
        for j, chip in enumerate(chips):
            for a in range(n):
                copy(a, 4 + j, index(*chip, 1 - c), me).wait_recv()
        for cp in first + passed:
            cp.wait_send()
        for cp in own:
            cp.wait()


class _Exchange:
    def __init__(self, arrs, chips_only=False):
        self.inputs = list(arrs)
        self.chips_only = chips_only
        self.out_shape = [jax.ShapeDtypeStruct(a.shape, a.dtype) for a in arrs]
        self.scratch = _comm_scratch(len(arrs))

    def _plan(self, ins, outs, sems):
        send_sems, recv_sems, local_sems = sems
        n = len(ins)
        x, y, c = _position()
        scale = 1 if self.chips_only else 2
        me = 2 * x + y if self.chips_only else 4 * x + 2 * y + c
        own = [pltpu.make_async_copy(ins[a].at[me], outs[a].at[me], local_sems.at[a]) for a in range(n)]
        sends, recvs = [], []
        for r in range(1, 4 * scale):
            px = 1 - x if r & (2 * scale) else x
            py = 1 - y if r & scale else y
            pc = 1 - c if (r & 1 and not self.chips_only) else c
            peer = 2 * px + py if self.chips_only else 4 * px + 2 * py + pc
            for a in range(n):
                sends.append(pltpu.make_async_remote_copy(
                    src_ref=ins[a].at[peer], dst_ref=outs[a].at[me],
                    send_sem=send_sems.at[a, r - 1], recv_sem=recv_sems.at[a, r - 1],
                    device_id=(px, py, pc), device_id_type=MESH))
                recvs.append(pltpu.make_async_remote_copy(
                    src_ref=ins[a].at[peer], dst_ref=outs[a].at[peer],
                    send_sem=send_sems.at[a, r - 1], recv_sem=recv_sems.at[a, r - 1],
                    device_id=(px, py, pc), device_id_type=MESH))
        return own, sends, recvs

    def start(self, ins, outs, sems):
        own, sends, _ = self._plan(ins, outs, sems)
        for cp in own + sends:
            cp.start()

    def middle(self, ins, outs, sems):
        pass

    def finish(self, ins, outs, sems):
        own, sends, recvs = self._plan(ins, outs, sems)
        for cp in recvs:
            cp.wait_recv()
        for cp in sends:
            cp.wait_send()
        for cp in own:
            cp.wait()


CHIPS = N_DEV // 2


def _sibling_scratch(slab_shape, stage_slots=CHIPS):
    return [pltpu.VMEM((stage_slots,) + tuple(slab_shape), BF16), pltpu.VMEM((CHIPS,) + tuple(slab_shape), BF16),
            pltpu.SemaphoreType.DMA((CHIPS,)), pltpu.SemaphoreType.DMA((CHIPS,))]


def _to_sibling(stage_s, land_s, send_sems, recv_sems, slot, q):
    x, y, c = _position()
    return pltpu.make_async_remote_copy(
        src_ref=stage_s.at[slot], dst_ref=land_s.at[q], send_sem=send_sems.at[q], recv_sem=recv_sems.at[q],
        device_id=(x, y, 1 - c), device_id_type=MESH)


def _sum_with_sibling(slab_of, q_ref, sibling_scratch):
    stage_s, land_s, send_sems, recv_sems = sibling_scratch
    c = lax.axis_index("c")
    for q in range(CHIPS):
        stage_s[q] = slab_of(2 * q + 1 - c).astype(BF16)
        _to_sibling(stage_s, land_s, send_sems, recv_sems, q, q).start()
    for q in range(CHIPS):
        _to_sibling(stage_s, land_s, send_sems, recv_sems, q, q).wait_recv()
        q_ref[q] = (slab_of(2 * q + c) + land_s[q].astype(F32)).astype(BF16)
    for q in range(CHIPS):
        _to_sibling(stage_s, land_s, send_sems, recv_sems, q, q).wait_send()


class _Together:
    def __init__(self, comms):
        self.comms = list(comms)
        self.inputs = [a for c in self.comms for a in c.inputs]
        self.out_shape = [s for c in self.comms for s in c.out_shape]
        self.scratch = [s for c in self.comms for s in c.scratch]

    def _each(self, ins, outs, sems):
        at = 0
        for k, c in enumerate(self.comms):
            n = len(c.inputs)
            yield c, ins[at:at + n], outs[at:at + n], sems[3 * k:3 * k + 3]
            at += n

    def start(self, ins, outs, sems):
        for c, i, o, s in self._each(ins, outs, sems):
            c.start(i, o, s)

    def middle(self, ins, outs, sems):
        for c, i, o, s in self._each(ins, outs, sems):
            c.middle(i, o, s)

    def finish(self, ins, outs, sems):
        for c, i, o, s in self._each(ins, outs, sems):
            c.finish(i, o, s)


def _comm_only(comm, name):
    n = len(comm.inputs)

    def body(*refs):
        ins, outs, sems = refs[:n], refs[n:2 * n], refs[2 * n:]
        comm.start(ins, outs, sems)
        comm.middle(ins, outs, sems)
        comm.finish(ins, outs, sems)

    any_spec = pl.BlockSpec(memory_space=pl.ANY)
    return pl.pallas_call(
        body, name=name, out_shape=comm.out_shape, in_specs=[any_spec] * n, out_specs=[any_spec] * n,
        scratch_shapes=comm.scratch, compiler_params=pltpu.CompilerParams(has_side_effects=True),
    )(*comm.inputs)


def _pcall(body, name, grid, in_specs, out_specs, out_shape, args, scratch_shapes=(), comm=None):
    in_specs, out_specs, out_shape, scratch_shapes = list(in_specs), list(out_specs), list(out_shape), list(scratch_shapes)
    if comm is None:
        outs = pl.pallas_call(body, name=name, grid=grid, in_specs=in_specs, out_specs=out_specs, out_shape=out_shape,
                              scratch_shapes=scratch_shapes, compiler_params=_params())(*args)
        return list(outs), []
    n_in, n_out, n_scr, n_c = len(in_specs), len(out_specs), len(scratch_shapes), len(comm.inputs)

    def carrying(*refs):
        ins, refs = refs[:n_in], refs[n_in:]
        c_ins, refs = refs[:n_c], refs[n_c:]
        outs, refs = refs[:n_out], refs[n_out:]
        c_outs, refs = refs[:n_c], refs[n_c:]
        scr, sems = refs[:n_scr], refs[n_scr:]
        step, steps = 0, 1
        for d, size in enumerate(grid):
            step = step * size + pl.program_id(d)
            steps *= size

        @pl.when(step == 0)
        def _():
            comm.start(c_ins, c_outs, sems)

        body(*ins, *outs, *scr)

        @pl.when(step == max(steps - 2, 0))
        def _():
            comm.middle(c_ins, c_outs, sems)

        @pl.when(step == steps - 1)
        def _():
            comm.finish(c_ins, c_outs, sems)

    any_spec = pl.BlockSpec(memory_space=pl.ANY)
    outs = pl.pallas_call(
        carrying, name=name, grid=grid, in_specs=in_specs + [any_spec] * n_c, out_specs=out_specs + [any_spec] * n_c,
        out_shape=out_shape + comm.out_shape, scratch_shapes=scratch_shapes + comm.scratch,
        compiler_params=_params(has_side_effects=True),
    )(*args, *comm.inputs)
    return list(outs[:n_out]), list(outs[n_out:])


def _in_proj_fwd(x, g_row, w_parts, name, comm=None):
    T, D = x.shape
    NS = w_parts[0].shape[-1]
    DP = w_parts[0].shape[-2]
    n_parts = len(w_parts)
    assert DP * n_parts == D
    TM = min(T, 512)

    def body(x_ref, g_ref, *refs):
        w_refs, (proj_ref, h_ref) = refs[:n_parts], refs[n_parts:]
        _, xh = _rms_stats(x_ref[...])
        h = (xh * g_ref[...]).astype(BF16)
        h_ref[...] = h
        for k in range(N_DEV):
            acc = _dot(h[:, 0:DP], w_refs[0][k])
            for p in range(1, n_parts):
                acc = acc + _dot(h[:, p * DP:(p + 1) * DP], w_refs[p][k])
            proj_ref[:, k * NS:(k + 1) * NS] = acc.astype(BF16)

    return _pcall(
        body, name, (T // TM,),
        in_specs=[pl.BlockSpec((TM, D), lambda i: (i, 0)), _pick(g_row)] + [_resident((N_DEV, DP, NS))] * n_parts,
        out_specs=[pl.BlockSpec((TM, N_DEV * NS), lambda i: (i, 0)),
                   pl.BlockSpec((TM, D), lambda i: (i, 0))],
        out_shape=[jax.ShapeDtypeStruct((T, N_DEV * NS), BF16), jax.ShapeDtypeStruct((T, D), BF16)],
        args=(x, g_row[0], *w_parts), comm=comm)


def _a_mix_fwd(proj, ln_g, ln_b, w_s, b_s, name, comm=None):
    T, E3 = proj.shape
    E = E3 // 3
    G, P = A_GROUPS, GMLP_BLOCK
    GD = E // G
    TB = min(T, 512)

    def body(p_ref, lg_ref, lb_ref, ws_ref, bs_ref, y_ref, xh_ref, dgl_ref, rstd_ref, v_s, us_s):
        def norm_chunk(ci, carry):
            rows = pl.ds(pl.multiple_of(ci * ROW_CHUNK, ROW_CHUNK), ROW_CHUNK)
            vg, dgl = _gelu_and_grad(p_ref[rows, E:2 * E].astype(F32))
            dgl_ref[rows, :] = dgl.astype(BF16)
            xc = vg - jnp.mean(vg, axis=-1, keepdims=True)
            rstd = lax.rsqrt(jnp.mean(xc * xc, axis=-1, keepdims=True) + EPS)
            rstd_ref[rows, :] = rstd
            xh = xc * rstd
            xh_ref[rows, :] = xh.astype(BF16)
            v_s[rows, :] = (xh * lg_ref[...] + lb_ref[...]).astype(BF16)
            return carry

        def gate_chunk(ci, carry):
            rows = pl.ds(pl.multiple_of(ci * ROW_CHUNK, ROW_CHUNK), ROW_CHUNK)
            z = p_ref[rows, 2 * E:3 * E].astype(F32)
            us_s[rows, :] = _gelu(p_ref[rows, 0:E].astype(F32)) * (z * _sigmoid(z))
            return carry

        lax.fori_loop(0, TB // ROW_CHUNK, norm_chunk, 0, unroll=2)
        lax.fori_loop(0, TB // ROW_CHUNK, gate_chunk, 0, unroll=2)
        mask = _spatial_mask()
        for g in range(G):
            wm = jnp.where(mask, ws_ref[g], 0.0).astype(BF16)
            cols = slice(g * GD, (g + 1) * GD)
            for b in range(TB // P):
                rows = slice(b * P, (b + 1) * P)
                mixed = _dot(wm, v_s[rows, cols]) + bs_ref[g]
                y_ref[rows, cols] = (us_s[rows, cols] * mixed).astype(BF16)

    return _pcall(
        body, name, (T // TB,),
        in_specs=[pl.BlockSpec((TB, E3), lambda i: (i, 0)),
                  _pick(ln_g), _pick(ln_b), _pick(w_s), _pick(b_s)],
        out_specs=[pl.BlockSpec((TB, E), lambda i: (i, 0)), pl.BlockSpec((TB, E), lambda i: (i, 0)),
                   pl.BlockSpec((TB, E), lambda i: (i, 0)), pl.BlockSpec((TB, 1), lambda i: (i, 0))],
        out_shape=[jax.ShapeDtypeStruct((T, E), BF16), jax.ShapeDtypeStruct((T, E), BF16),
                   jax.ShapeDtypeStruct((T, E), BF16), jax.ShapeDtypeStruct((T, 1), F32)],
        scratch_shapes=[pltpu.VMEM((TB, E), BF16), pltpu.VMEM((TB, E), F32)],
        args=(proj, ln_g[0], ln_b[0], w_s[0], b_s[0]), comm=comm)


def _window_sum_back(ext, win):
    s, k = ext, 1
    while k < win:
        s = s + pltpu.roll(s, k, axis=0)
        k *= 2
    return s


def _window_sum_ahead(ext, win):
    n = ext.shape[0]
    s, k = ext, 1
    while k < win:
        s = s + pltpu.roll(s, n - k, axis=0)
        k *= 2
    return s


def _inv_count(t0, rows, win):
    t1 = t0 + 1 + lax.broadcasted_iota(jnp.int32, (rows, 1), 0)
    return 1.0 / jnp.minimum(t1, win).astype(F32)


def _b_mix_fwd(proj, scale, wg_all, name, comm=None):
    T, E2 = proj.shape
    E = E2 // 2
    NG = len(POOL_WINDOWS)
    GB = E // NG
    TB = min(T, 256)
    RS = wg_all.shape[-2]

    def body(p_ref, sc_ref, wg_ref, y_ref, o_ref, carry_s):
        i = pl.program_id(0)

        @pl.when(i == 0)
        def _():
            carry_s[...] = jnp.zeros_like(carry_s)

        for g, win in enumerate(POOL_WINDOWS):
            cols = slice(g * GB, (g + 1) * GB)
            xg = p_ref[:, cols].astype(F32)
            ext = jnp.concatenate([carry_s[:, cols], xg], axis=0)
            pooled = _window_sum_back(ext, win)[HALO:, :] * _inv_count(i * TB, TB, win) - xg
            carry_s[:, cols] = xg[TB - HALO:, :]
            o = _dot(pooled.astype(BF16), wg_ref[:, g].reshape(GB, GB))
            o_ref[:, cols] = o.astype(BF16)
            z = p_ref[:, E + g * GB:E + (g + 1) * GB].astype(F32)
            y_ref[:, cols] = ((o * sc_ref[:, cols]) * (z * _sigmoid(z))).astype(BF16)

    return _pcall(
        body, name, (T // TB,),
        in_specs=[pl.BlockSpec((TB, E2), lambda i: (i, 0)),
                  pl.BlockSpec((1, E), lambda i: (0, 0)),
                  pl.BlockSpec((N_DEV, NG, RS, GB), lambda i: (0, 0, 0, 0))],
        out_specs=[pl.BlockSpec((TB, E), lambda i: (i, 0)), pl.BlockSpec((TB, E), lambda i: (i, 0))],
        out_shape=[jax.ShapeDtypeStruct((T, E), BF16), jax.ShapeDtypeStruct((T, E), BF16)],
        scratch_shapes=[pltpu.VMEM((HALO, E), F32)],
        args=(proj, scale, wg_all), comm=comm)


def _out_proj_fwd(y, w_all, x, g_row, name, comm=None):
    T, E = y.shape
    D = x.shape[1]
    ES = w_all.shape[-2]
    TM = min(T, 512)

    def body(y_ref, w_ref, x_ref, g_ref, xn_ref, out_ref):
        o = _dot(y_ref[...], w_ref[...].reshape(E, D))
        out_ref[...] = o
        _, oh = _rms_stats(o)
        xn_ref[...] = x_ref[...] + oh * g_ref[...]

    return _pcall(
        body, name, (T // TM,),
        in_specs=[pl.BlockSpec((TM, E), lambda i: (i, 0)),
                  pl.BlockSpec((N_DEV, ES, D), lambda i: (0, 0, 0)),
                  pl.BlockSpec((TM, D), lambda i: (i, 0)),
                  _pick(g_row)],
        out_specs=[pl.BlockSpec((TM, D), lambda i: (i, 0)), pl.BlockSpec((TM, D), lambda i: (i, 0))],
        out_shape=[jax.ShapeDtypeStruct((T, D), F32), jax.ShapeDtypeStruct((T, D), F32)],
        args=(y, w_all, x, g_row[0]), comm=comm)


def _out_proj_loss_fwd(y, w_all, x, g_row, target, name):
    T, E = y.shape
    D = x.shape[1]
    ES = w_all.shape[-2]
    TM = min(T, 512)
    nT = T // TM

    def body(y_ref, w_ref, x_ref, g_ref, t_ref, dx_ref, out_ref, loss_ref, acc_s):
        i = pl.program_id(0)

        @pl.when(i == 0)
        def _():
            acc_s[...] = jnp.zeros_like(acc_s)

        o = _dot(y_ref[...], w_ref[...].reshape(E, D))
        out_ref[...] = o
        _, oh = _rms_stats(o)
        e = (x_ref[...] + oh * g_ref[...]) - t_ref[...]
        dx_ref[...] = e * (1.0 / D)
        acc_s[...] += jnp.sum(e * e, axis=0, keepdims=True)

        @pl.when(i == nT - 1)
        def _():
            total = jnp.sum(acc_s[...], axis=1, keepdims=True) * (0.5 / D)
            loss_ref[...] = jnp.broadcast_to(total, loss_ref.shape)

    return _pcall(
        body, name, (nT,),
        in_specs=[pl.BlockSpec((TM, E), lambda i: (i, 0)),
                  pl.BlockSpec((N_DEV, ES, D), lambda i: (0, 0, 0)),
                  pl.BlockSpec((TM, D), lambda i: (i, 0)),
                  _pick(g_row),
                  pl.BlockSpec((TM, D), lambda i: (i, 0))],
        out_specs=[pl.BlockSpec((TM, D), lambda i: (i, 0)), pl.BlockSpec((TM, D), lambda i: (i, 0)),
                   pl.BlockSpec((1, 128), lambda i: (0, 0))],
        out_shape=[jax.ShapeDtypeStruct((T, D), F32), jax.ShapeDtypeStruct((T, D), F32),
                   jax.ShapeDtypeStruct((1, 128), F32)],
        scratch_shapes=[pltpu.VMEM((1, D), F32)],
        args=(y, w_all, x, g_row[0], target))[0]


def _out_proj_bwd(dxn, out, g_row, w_all, y, name, comm=None):
    T, D = dxn.shape
    E = y.shape[1]
    ES = w_all.shape[-2]
    TM = min(T, 512)
    nT = T // TM

    def body(dxn_ref, out_ref, g_ref, w_ref, y_ref, dy_ref, dw_ref, dg_ref, acc_s, *sibling_scratch):
        i = pl.program_id(0)

        @pl.when(i == 0)
        def _():
            acc_s[...] = jnp.zeros_like(acc_s)
            dg_ref[...] = jnp.zeros_like(dg_ref)

        dxn_v = dxn_ref[...]
        r, oh = _rms_stats(out_ref[...])
        dg_ref[...] += jnp.sum(dxn_v * oh, axis=0, keepdims=True)
        dout = _rms_bwd(dxn_v, g_ref[...], r, oh).astype(BF16)
        dy_ref[...] = _dot_nt(dout, w_ref[...].reshape(E, D)).astype(BF16)
        acc_s[...] += _dot_tn(y_ref[...], dout)

        @pl.when(i == nT - 1)
        def _():
            def slab_of(k):
                return acc_s[pl.ds(pl.multiple_of(k * ES, ES), ES), :]

            _sum_with_sibling(slab_of, dw_ref, sibling_scratch)

    return _pcall(
        body, name, (nT,),
        in_specs=[pl.BlockSpec((TM, D), lambda i: (i, 0)),
                  pl.BlockSpec((TM, D), lambda i: (i, 0)),
                  _pick(g_row),
                  pl.BlockSpec((N_DEV, ES, D), lambda i: (0, 0, 0)),
                  pl.BlockSpec((TM, E), lambda i: (i, 0))],
        out_specs=[pl.BlockSpec((TM, E), lambda i: (i, 0)),
                   pl.BlockSpec((CHIPS, ES, D), lambda i: (0, 0, 0)),
                   pl.BlockSpec((1, D), lambda i: (0, 0))],
        out_shape=[jax.ShapeDtypeStruct((T, E), BF16), jax.ShapeDtypeStruct((CHIPS, ES, D), BF16),
                   jax.ShapeDtypeStruct((1, D), F32)],
        scratch_shapes=[pltpu.VMEM((E, D), F32)] + _sibling_scratch((ES, D)),
        args=(dxn, out, g_row[0], w_all, y), comm=comm)


def _a_mix_bwd(proj, dy, xh, dgl, rstd, ln_g, ln_b, w_s, b_s, name, comm=None):
    T, E3 = proj.shape
    E = E3 // 3
    G, P = A_GROUPS, GMLP_BLOCK
    GD = E // G
    TB = min(T, 256)

    def body(up_ref, zp_ref, dy_ref, xh_ref, dgl_ref, rstd_ref, lg_ref, lb_ref, ws_ref, bs_ref,
             dp_ref, dws_ref, dbs_ref, dlg_ref, dlb_ref, v_s, a_s, bz_s, c_s, dv_s):
        @pl.when(pl.program_id(0) == 0)
        def _():
            dws_ref[...] = jnp.zeros_like(dws_ref)
            dbs_ref[...] = jnp.zeros_like(dbs_ref)
            dlg_ref[...] = jnp.zeros_like(dlg_ref)
            dlb_ref[...] = jnp.zeros_like(dlb_ref)

        def recompute(ci, carry):
            rows = pl.ds(pl.multiple_of(ci * ROW_CHUNK, ROW_CHUNK), ROW_CHUNK)
            v_s[rows, :] = (xh_ref[rows, :].astype(F32) * lg_ref[...] + lb_ref[...]).astype(BF16)
            u, du = _gelu_and_grad(up_ref[rows, :].astype(F32))
            z = zp_ref[rows, :].astype(F32)
            sg = _sigmoid(z)
            s = z * sg
            ds = sg * (1.0 + z * (1.0 - sg))
            dyv = dy_ref[rows, :].astype(F32)
            a_s[rows, :] = dyv * s * du
            bz_s[rows, :] = dyv * u * ds
            c_s[rows, :] = (dyv * u * s).astype(BF16)
            return carry

        lax.fori_loop(0, TB // ROW_CHUNK, recompute, 0, unroll=2)

        mask = _spatial_mask()
        mask_t = _spatial_mask(transposed=True)
        for g in range(G):
            w_g = ws_ref[g]
            wm = jnp.where(mask, w_g, 0.0).astype(BF16)
            wm_t = jnp.where(mask_t, w_g.T, 0.0).astype(BF16)
            cols = slice(g * GD, (g + 1) * GD)
            dws_g = jnp.zeros((P, P), F32)
            dbs_g = jnp.zeros((SUBLANES, P), F32)
            for b in range(TB // P):
                rows = slice(b * P, (b + 1) * P)
                vb = v_s[rows, cols]
                cb = c_s[rows, cols]
                mixed = _dot(wm, vb) + bs_ref[g]
                dp_ref[rows, g * GD:(g + 1) * GD] = (a_s[rows, cols] * mixed).astype(BF16)
                dp_ref[rows, 2 * E + g * GD:2 * E + (g + 1) * GD] = (bz_s[rows, cols] * mixed).astype(BF16)
                dv_s[rows, cols] = _dot(wm_t, cb)
                dws_g = dws_g + _dot_nt(cb, vb)
                dbs_g = dbs_g + _dot_nt(jnp.ones((SUBLANES, GD), BF16), cb)
            dws_ref[g] += jnp.where(mask, dws_g, 0.0)
            dbs_ref[g:g + 1, :] += dbs_g[0:1, :]

        def ln_bwd(ci, carry):
            rows = pl.ds(pl.multiple_of(ci * ROW_CHUNK, ROW_CHUNK), ROW_CHUNK)
            dv = dv_s[rows, :]
            xh = xh_ref[rows, :].astype(F32)
            dlg_ref[...] += jnp.sum(dv * xh, axis=0, keepdims=True)
            dlb_ref[...] += jnp.sum(dv, axis=0, keepdims=True)
            dxh = dv * lg_ref[...]
            dvg = rstd_ref[rows, :] * (dxh - jnp.mean(dxh, axis=-1, keepdims=True)
                                       - xh * jnp.mean(dxh * xh, axis=-1, keepdims=True))
            dp_ref[rows, E:2 * E] = (dvg * dgl_ref[rows, :].astype(F32)).astype(BF16)
            return carry

        lax.fori_loop(0, TB // ROW_CHUNK, ln_bwd, 0, unroll=2)

    return _pcall(
        body, name, (T // TB,),
        in_specs=[pl.BlockSpec((TB, E), lambda i: (i, 0)),
                  pl.BlockSpec((TB, E), lambda i: (i, 2)),
                  pl.BlockSpec((TB, E), lambda i: (i, 0)),
                  pl.BlockSpec((TB, E), lambda i: (i, 0)),
                  pl.BlockSpec((TB, E), lambda i: (i, 0)),
                  pl.BlockSpec((TB, 1), lambda i: (i, 0)),
                  _pick(ln_g), _pick(ln_b), _pick(w_s), _pick(b_s)],
        out_specs=[pl.BlockSpec((TB, E3), lambda i: (i, 0)),
                   pl.BlockSpec((G, P, P), lambda i: (0, 0, 0)),
                   pl.BlockSpec((G, P), lambda i: (0, 0)),
                   pl.BlockSpec((1, E), lambda i: (0, 0)),
                   pl.BlockSpec((1, E), lambda i: (0, 0))],
        out_shape=[jax.ShapeDtypeStruct((T, E3), BF16), jax.ShapeDtypeStruct((G, P, P), F32),
                   jax.ShapeDtypeStruct((G, P), F32), jax.ShapeDtypeStruct((1, E), F32),
                   jax.ShapeDtypeStruct((1, E), F32)],
        scratch_shapes=[pltpu.VMEM((TB, E), BF16), pltpu.VMEM((TB, E), F32), pltpu.VMEM((TB, E), F32),
                        pltpu.VMEM((TB, E), BF16), pltpu.VMEM((TB, E), F32)],
        args=(proj, proj, dy, xh, dgl, rstd, ln_g[0], ln_b[0], w_s[0], b_s[0]), comm=comm)


def _b_mix_bwd(proj, dy, o, scale, wg_all, name, comm=None):
    T, E2 = proj.shape
    E = E2 // 2
    NG = len(POOL_WINDOWS)
    GB = E // NG
    TB = min(T, 256)
    nT = T // TB
    RS = wg_all.shape[-2]
    halo_per_tile = TB // HALO

    def body(p_ref, halo_ref, dy_ref, o_ref, sc_ref, wg_ref, dp_ref, dsc_ref, dwg_ref, acc_s, carry_s,
             *sibling_scratch):
        i = pl.program_id(0)
        tile = nT - 1 - i

        @pl.when(i == 0)
        def _():
            acc_s[...] = jnp.zeros_like(acc_s)
            carry_s[...] = jnp.zeros_like(carry_s)
            dsc_ref[...] = jnp.zeros_like(dsc_ref)

        has_history = (tile > 0).astype(F32)
        for g, win in enumerate(POOL_WINDOWS):
            cols = slice(g * GB, (g + 1) * GB)
            inv = _inv_count(tile * TB, TB, win)
            xg = p_ref[:, cols].astype(F32)
            ext = jnp.concatenate([halo_ref[:, cols].astype(F32) * has_history, xg], axis=0)
            pooled = _window_sum_back(ext, win)[HALO:, :] * inv - xg
            z = p_ref[:, E + g * GB:E + (g + 1) * GB].astype(F32)
            sg = _sigmoid(z)
            dyv = dy_ref[:, cols].astype(F32)
            ov = o_ref[:, cols].astype(F32)
            sc = sc_ref[:, cols]
            dmixed = dyv * (z * sg)
            dsc_ref[:, cols] += jnp.sum(dmixed * ov, axis=0, keepdims=True)
            dz = dyv * (ov * sc) * (sg * (1.0 + z * (1.0 - sg)))
            do = (dmixed * sc).astype(BF16)
            acc_s[:, g] += _dot_tn(pooled.astype(BF16), do).reshape(N_DEV, RS, GB)
            dpool = _dot_nt(do, wg_ref[:, g].reshape(GB, GB))
            q = dpool * inv
            ext_q = jnp.concatenate([q, carry_s[:, cols]], axis=0)
            dxb = _window_sum_ahead(ext_q, win)[:TB, :] - dpool
            carry_s[:, cols] = q[:HALO, :]
            dp_ref[:, cols] = dxb.astype(BF16)
            dp_ref[:, E + g * GB:E + (g + 1) * GB] = dz.astype(BF16)

        @pl.when(i == nT - 1)
        def _():
            _sum_with_sibling(lambda k: acc_s[k], dwg_ref, sibling_scratch)

    return _pcall(
        body, name, (nT,),
        in_specs=[pl.BlockSpec((TB, E2), lambda i: (nT - 1 - i, 0)),
                  pl.BlockSpec((HALO, E), lambda i: (jnp.maximum((nT - 1 - i) * halo_per_tile - 1, 0), 0)),
                  pl.BlockSpec((TB, E), lambda i: (nT - 1 - i, 0)),
                  pl.BlockSpec((TB, E), lambda i: (nT - 1 - i, 0)),
                  pl.BlockSpec((1, E), lambda i: (0, 0)),
                  pl.BlockSpec((N_DEV, NG, RS, GB), lambda i: (0, 0, 0, 0))],
        out_specs=[pl.BlockSpec((TB, E2), lambda i: (nT - 1 - i, 0)),
                   pl.BlockSpec((1, E), lambda i: (0, 0)),
                   pl.BlockSpec((CHIPS, NG, RS, GB), lambda i: (0, 0, 0, 0))],
        out_shape=[jax.ShapeDtypeStruct((T, E2), BF16), jax.ShapeDtypeStruct((1, E), F32),
                   jax.ShapeDtypeStruct((CHIPS, NG, RS, GB), BF16)],
        scratch_shapes=[pltpu.VMEM((N_DEV, NG, RS, GB), F32), pltpu.VMEM((HALO, E), F32)]
        + _sibling_scratch((NG, RS, GB)),
        args=(proj, proj, dy, o, scale, wg_all), comm=comm)


def _in_proj_bwd_dx(dproj, w_parts, x, g_row, dxn, name, comm=None):
    T, D = x.shape
    NS = w_parts[0].shape[-1]
    DP = w_parts[0].shape[-2]
    n_parts = len(w_parts)
    TM = min(T, 512)

    def body(dp_ref, *refs):
        w_refs, (x_ref, g_ref, dxn_ref, dx_ref, dg_ref) = refs[:n_parts], refs[n_parts:]

        @pl.when(pl.program_id(0) == 0)
        def _():
            dg_ref[...] = jnp.zeros_like(dg_ref)

        pieces = []
        for w_ref in w_refs:
            piece = _dot_nt(dp_ref[:, 0:NS], w_ref[0])
            for k in range(1, N_DEV):
                piece = piece + _dot_nt(dp_ref[:, k * NS:(k + 1) * NS], w_ref[k])
            pieces.append(piece)
        dh = pieces[0] if n_parts == 1 else jnp.concatenate(pieces, axis=1)
        r, xh = _rms_stats(x_ref[...])
        dg_ref[...] += jnp.sum(dh * xh, axis=0, keepdims=True)
        dx_ref[...] = dxn_ref[...] + _rms_bwd(dh, g_ref[...], r, xh)

    return _pcall(
        body, name, (T // TM,),
        in_specs=[pl.BlockSpec((TM, N_DEV * NS), lambda i: (i, 0))] + [_resident((N_DEV, DP, NS))] * n_parts
        + [pl.BlockSpec((TM, D), lambda i: (i, 0)),
           _pick(g_row),
           pl.BlockSpec((TM, D), lambda i: (i, 0))],
        out_specs=[pl.BlockSpec((TM, D), lambda i: (i, 0)), pl.BlockSpec((1, D), lambda i: (0, 0))],
        out_shape=[jax.ShapeDtypeStruct((T, D), F32), jax.ShapeDtypeStruct((1, D), F32)],
        args=(dproj, *w_parts, x, g_row[0], dxn), comm=comm)


STAGE_SLOTS = 2


def _dw_in(h, dproj, name, comm=None):
    T, D = h.shape
    NS = dproj.shape[1] // N_DEV
    TK = min(T, 2048)
    nK = T // TK

    def body(h_ref, dp_ref, q_ref, acc_s, stage_s, land_s, send_sems, recv_sems):
        k, t = pl.program_id(0), pl.program_id(1)
        c = lax.axis_index("c")

        def to_sibling(q):
            return _to_sibling(stage_s, land_s, send_sems, recv_sems, q % STAGE_SLOTS, q)

        @pl.when(t == 0)
        def _():
            acc_s[...] = jnp.zeros_like(acc_s)

        acc_s[...] += _dot_tn(h_ref[...], dp_ref[...])

        @pl.when(t == nK - 1)
        def _():
            q = k // 2

            @pl.when(k % 2 == c)
            def _():
                q_ref[q] = acc_s[...].astype(BF16)

            @pl.when(k % 2 != c)
            def _():
                @pl.when(q >= STAGE_SLOTS)
                def _():
                    to_sibling(q - STAGE_SLOTS).wait_send()

                stage_s[q % STAGE_SLOTS] = acc_s[...].astype(BF16)
                to_sibling(q).start()

        @pl.when((k == N_DEV - 1) & (t == nK - 1))
        def _():
            for q in range(CHIPS - STAGE_SLOTS, CHIPS):
                to_sibling(q).wait_send()
            for q in range(CHIPS):
                to_sibling(q).wait_recv()
                q_ref[q] = (q_ref[q].astype(F32) + land_s[q].astype(F32)).astype(BF16)

    return _pcall(
        body, name, (N_DEV, nK),
        in_specs=[pl.BlockSpec((TK, D), lambda k, t: (t, 0)), pl.BlockSpec((TK, NS), lambda k, t: (t, k))],
        out_specs=[pl.BlockSpec((CHIPS, D, NS), lambda k, t: (0, 0, 0))],
        out_shape=[jax.ShapeDtypeStruct((CHIPS, D, NS), BF16)],
        scratch_shapes=[pltpu.VMEM((D, NS), F32)] + _sibling_scratch((D, NS), STAGE_SLOTS),
        args=(h, dproj), comm=comm)


def _reduce_adam(recvs, w, m, v, name):
    L, R, C = w.shape
    assert len(recvs) == L
    senders = recvs[0].shape[0]
    TR = R
    for cand in (256, 128, 64, 32, 16):
        if R % cand == 0 and R > cand:
            TR = cand
            break
    nR = R // TR
    c1 = 1.0 - ADAM_B1 ** ADAM_STEP
    c2 = 1.0 - ADAM_B2 ** ADAM_STEP

    def body(*refs):
        recv_refs = refs[:L]
        w_ref, m_ref, v_ref, g_ref, d_ref, nm_ref, nv_ref, g_s = refs[L:]
        layer = pl.program_id(0)
        for l in range(L):
            @pl.when(layer == l)
            def _(l=l):
                acc = recv_refs[l][0].astype(F32)
                for j in range(1, senders):
                    acc = acc + recv_refs[l][j].astype(F32)
                g_s[...] = acc

        g = g_s[...]
        g_ref[...] = g
        nm = ADAM_B1 * m_ref[...] + (1.0 - ADAM_B1) * g
        nv = ADAM_B2 * v_ref[...] + (1.0 - ADAM_B2) * (g * g)
        nm_ref[...] = nm
        nv_ref[...] = nv
        d_ref[...] = -ADAM_LR * ((nm / c1) / (jnp.sqrt(nv / c2) + ADAM_EPS) + ADAM_WD * w_ref[...])

    def recv_spec(l):
        def index(layer, t):
            before = jnp.where(layer < l, 0, nR - 1)
            return (0, jnp.where(layer == l, t, before), 0)
        return pl.BlockSpec((senders, TR, C), index)

    wspec = pl.BlockSpec((None, TR, C), lambda layer, t: (layer, t, 0))
    out = jax.ShapeDtypeStruct((L, R, C), F32)
    return _pcall(
        body, name, (L, nR),
        in_specs=[recv_spec(l) for l in range(L)] + [wspec] * 3,
        out_specs=[wspec] * 4, out_shape=[out] * 4,
        scratch_shapes=[pltpu.VMEM((TR, C), F32)],
        args=(*recvs, w, m, v))[0]


def _adam_replicated(gathered, w, m, v, name):
    n_params = len(w)
    layers = [len(g) for g in gathered]
    flat = [g for per_param in gathered for g in per_param]
    c1 = 1.0 - ADAM_B1 ** ADAM_STEP
    c2 = 1.0 - ADAM_B2 ** ADAM_STEP

    def body(*refs):
        g_refs, refs = refs[:len(flat)], refs[len(flat):]
        w_refs, m_refs, v_refs = refs[:n_params], refs[n_params:2 * n_params], refs[2 * n_params:3 * n_params]
        out_refs = refs[3 * n_params:]
        at = 0
        for n in range(n_params):
            g_out, d_out, nm_out, nv_out = out_refs[4 * n:4 * n + 4]
            for l in range(layers[n]):
                g_ref = g_refs[at]
                at += 1
                at_l = l if w_refs[n].ndim > 2 else slice(l, l + 1)
                g = g_ref[0]
                for s in range(1, N_DEV):
                    g = g + g_ref[s]
                nm = ADAM_B1 * m_refs[n][at_l] + (1.0 - ADAM_B1) * g
                nv = ADAM_B2 * v_refs[n][at_l] + (1.0 - ADAM_B2) * (g * g)
                g_out[at_l] = g
                nm_out[at_l] = nm
                nv_out[at_l] = nv
                d_out[at_l] = -ADAM_LR * ((nm / c1) / (jnp.sqrt(nv / c2) + ADAM_EPS) + ADAM_WD * w_refs[n][at_l])

    vmem = pl.BlockSpec(memory_space=pltpu.VMEM)
    args = [*flat, *w, *m, *v]
    outs = pl.pallas_call(
        body, name=name, in_specs=[vmem] * len(args), out_specs=[vmem] * (4 * n_params),
        out_shape=[jax.ShapeDtypeStruct(a.shape, F32) for a in w for _ in range(4)],
        compiler_params=_params(),
    )(*args)
    return [outs[4 * n:4 * n + 4] for n in range(n_params)]


A_W_IN_PARTS = 4
PACK_LANES = 128
PACK_ROWS_MULTIPLE = 256


def _pack(arrays):
    flat = jnp.concatenate([a.reshape(-1) for a in arrays])
    tile = PACK_LANES * PACK_ROWS_MULTIPLE
    padded = -(-flat.shape[0] // tile) * tile
    return jnp.pad(flat, (0, padded - flat.shape[0])).reshape(1, padded // PACK_LANES, PACK_LANES)


def _unpack(packed, like):
    flat = packed.reshape(-1)
    out, at = [], 0
    for a in like:
        out.append(flat[at:at + a.size].reshape(a.shape))
        at += a.size
    return out


def kernel(x, norm_pre, norm_post, a_w_in, a_ln_g, a_ln_b, a_w_s, a_b_s, a_w_out, b_w_in, b_w_grp, b_scale, b_w_out, loss_target, m_norm_pre, m_norm_post, m_a_w_in, m_a_ln_g, m_a_ln_b, m_a_w_s, m_a_b_s, m_a_w_out, m_b_w_in, m_b_w_grp, m_b_scale, m_b_w_out, v_norm_pre, v_norm_post, v_a_w_in, v_a_ln_g, v_a_ln_b, v_a_w_s, v_a_b_s, v_a_w_out, v_b_w_in, v_b_w_grp, v_b_scale, v_b_w_out):
    weights = dict(norm_pre=norm_pre, norm_post=norm_post, a_w_in=a_w_in, a_ln_g=a_ln_g, a_ln_b=a_ln_b, a_w_s=a_w_s,
                   a_b_s=a_b_s, a_w_out=a_w_out, b_w_in=b_w_in, b_w_grp=b_w_grp, b_scale=b_scale, b_w_out=b_w_out)
    mom_m = dict(norm_pre=m_norm_pre, norm_post=m_norm_post, a_w_in=m_a_w_in, a_ln_g=m_a_ln_g, a_ln_b=m_a_ln_b,
                 a_w_s=m_a_w_s, a_b_s=m_a_b_s, a_w_out=m_a_w_out, b_w_in=m_b_w_in, b_w_grp=m_b_w_grp,
                 b_scale=m_b_scale, b_w_out=m_b_w_out)
    mom_v = dict(norm_pre=v_norm_pre, norm_post=v_norm_post, a_w_in=v_a_w_in, a_ln_g=v_a_ln_g, a_ln_b=v_a_ln_b,
                 a_w_s=v_a_w_s, a_b_s=v_a_b_s, a_w_out=v_a_w_out, b_w_in=v_b_w_in, b_w_grp=v_b_w_grp,
                 b_scale=v_b_scale, b_w_out=v_b_w_out)
    names = list(weights)

    depth = norm_pre.shape[0]
    x0 = x[0]
    target = loss_target[0]
    T, D = x0.shape
    E = a_ln_g.shape[1]
    G, P = A_GROUPS, GMLP_BLOCK
    pre3, post3 = norm_pre.reshape(depth, 1, D), norm_post.reshape(depth, 1, D)
    ln_g3, ln_b3 = a_ln_g.reshape(-1, 1, E), a_ln_b.reshape(-1, 1, E)
    b_s4 = a_b_s.reshape(-1, G, P, 1)

    def shards_of(i):
        j = i // 2
        if i % 2 == 0:
            w = a_w_in[j].astype(BF16)
            rows = w.shape[0] // A_W_IN_PARTS
            parts = {f"w_in_{p}": w[p * rows:(p + 1) * rows] for p in range(A_W_IN_PARTS)}
            return dict(**parts, w_out=a_w_out[j].astype(BF16))
        return dict(w_in=b_w_in[j].astype(BF16), w_out=b_w_out[j].astype(BF16), grp=b_w_grp[j].astype(BF16))

    shard = [shards_of(i) for i in range(depth)]
    full = [dict() for _ in range(depth)]

    def gather_into(keys, got):
        for (i, key), arr in zip(keys, got):
            full[i][key] = arr

    def w_in_of(i):
        return [(i, k) for k in shard[i] if k.startswith("w_in")]

    def rest_of(i):
        return [(i, k) for k in shard[i] if not k.startswith("w_in")]

    def gather_of(keys):
        return _Gather([shard[a][k] for a, k in keys]) if keys else None

    first = _comm_only(_Gather([shard[0][k] for _, k in w_in_of(0)] + [b_scale]), "gather_first")
    gather_into(w_in_of(0), first)
    scale_full = jnp.transpose(first[-1], (1, 0, 2)).reshape(b_scale.shape[0], 1, E)

    saved = []
    xi = x0
    for i in range(depth):
        j = i // 2
        g_pre, g_post = (pre3, i), (post3, i)
        keys_in, keys_mix, keys_out = [], [], []
        if i % 2 == 0:
            keys_mix = [(0, "w_out")] if i == 0 else []
            if i + 1 < depth:
                keys_in = w_in_of(i + 1)
                keys_mix = keys_mix + rest_of(i + 1)
            if i + 2 < depth:
                keys_out = w_in_of(i + 2)[:1]
        elif i + 1 < depth:
            keys_in = w_in_of(i + 1)[1:3]
            keys_mix = rest_of(i + 1)
            keys_out = w_in_of(i + 1)[3:]
        kind = "a" if i % 2 == 0 else "b"
        (proj, h), got = _in_proj_fwd(xi, g_pre, [full[i][k] for _, k in w_in_of(i)], f"{kind}_in_fwd_{i}",
                                      gather_of(keys_in))
        gather_into(keys_in, got)
        if i % 2 == 0:
            (y, *o), got = _a_mix_fwd(proj, (ln_g3, j), (ln_b3, j), (a_w_s, j), (b_s4, j),
                                      f"a_mix_fwd_{i}", gather_of(keys_mix))
        else:
            (y, o), got = _b_mix_fwd(proj, scale_full[j], full[i]["grp"], f"b_mix_fwd_{i}", gather_of(keys_mix))
        gather_into(keys_mix, got)
        if i + 1 < depth:
            (x_next, out), got = _out_proj_fwd(y, full[i]["w_out"], xi, g_post, f"{kind}_out_fwd_{i}",
                                               gather_of(keys_out))
            gather_into(keys_out, got)
        else:
            dx, out, loss_row = _out_proj_loss_fwd(y, full[i]["w_out"], xi, g_post, target, f"{kind}_out_loss_fwd_{i}")
            x_next = None
        saved.append((xi, h, proj, y, out, o))
        xi = x_next

    n_a, n_b = a_ln_g.shape[0], b_scale.shape[0]
    d_pre, d_post = [None] * depth, [None] * depth
    recv = {"a_w_in": [None] * n_a, "a_w_out": [None] * n_a, "b_w_in": [None] * n_b,
            "b_w_grp": [None] * n_b, "b_w_out": [None] * n_b, "b_scale": [None] * n_b}
    small_a = [[None] * 4 for _ in range(n_a)]
    kinds = ("chips", "devices", "gather")

    def carried(items):
        of = {kind: [it[3] for it in items if it[0] == kind] for kind in kinds}
        comms = (([_Exchange(of["chips"], chips_only=True)] if of["chips"] else [])
                 + ([_Exchange(of["devices"])] if of["devices"] else [])
                 + ([_Gather(of["gather"])] if of["gather"] else []))
        return None if not comms else comms[0] if len(comms) == 1 else _Together(comms)

    def received(items, got):
        ordered = [it for kind in kinds for it in items if it[0] == kind]
        for it, arr in zip(ordered, got):
            if it[0] == "gather":
                small_a[it[2][0]][it[2][1]] = arr
            else:
                recv[it[1]][it[2]] = arr

    pending = []
    small_pending = []
    for i in reversed(range(depth)):
        j = i // 2
        xi, h, proj, y, out, o = saved[i]
        g_pre, g_post = (pre3, i), (post3, i)
        if i % 2 == 0:
            (dy, dw_out, d_post[i]), _ = _out_proj_bwd(dx, out, g_post, full[i]["w_out"], y, f"a_out_bwd_{i}")
            items, pending = pending, []
            (dproj, d_w_s, dbs, d_ln_g, d_ln_b), got = _a_mix_bwd(
                proj, dy, *o, (ln_g3, j), (ln_b3, j), (a_w_s, j), (b_s4, j),
                f"a_mix_bwd_{i}", carried(items))
            received(items, got)
            small_pending += [("gather", "small", (j, n), part) for n, part in enumerate((d_w_s, dbs, d_ln_g, d_ln_b))]
            items = []
            if i == 0:
                items, small_pending = [("chips", "a_w_out", j, dw_out)] + small_pending, []
            (dw_in,), got = _dw_in(h, dproj, f"a_dw_in_{i}", carried(items))
            received(items, got)
            items = [("chips", "a_w_in", j, dw_in)] if i == 0 else [("chips", "a_w_out", j, dw_out)]
            (dx, d_pre[i]), got = _in_proj_bwd_dx(dproj, [full[i][k] for _, k in w_in_of(i)], xi, g_pre, dx,
                                                  f"a_in_bwd_{i}", carried(items))
            received(items, got)
            if i > 0:
                pending.append(("chips", "a_w_in", j, dw_in))
        else:
            items, small_pending = small_pending, []
            (dy, dw_out, d_post[i]), got = _out_proj_bwd(dx, out, g_post, full[i]["w_out"], y, f"b_out_bwd_{i}",
                                                        carried(items))
            received(items, got)
            items, pending = pending, []
            (dproj, dsc, dw_grp), got = _b_mix_bwd(proj, dy, o, scale_full[j], full[i]["grp"], f"b_mix_bwd_{i}",
                                                  carried(items))
            received(items, got)
            (dx, d_pre[i]), _ = _in_proj_bwd_dx(dproj, [full[i][k] for _, k in w_in_of(i)], xi, g_pre, dx,
                                                f"b_in_bwd_{i}")
            (dw_in,), _ = _dw_in(h, dproj, f"b_dw_in_{i}")
            pending += [("chips", "b_w_out", j, dw_out), ("chips", "b_w_grp", j, dw_grp), ("chips", "b_w_in", j, dw_in),
                        ("devices", "b_scale", j, dsc.reshape(N_DEV, 1, E // N_DEV))]
    assert not pending and not small_pending

    gathered = _comm_only(_Gather([_pack([*d_pre, *d_post, loss_row[:, :1]])[0]]), "gather_norm_grads")
    results = {k: [None] * 4 for k in names}
    no_state = jnp.zeros((1, 1), F32)
    norm_like = [norm_pre, norm_post, no_state]
    outs = _reduce_adam([gathered[-1]], _pack(norm_like), _pack([m_norm_pre, m_norm_post, no_state]),
                        _pack([v_norm_pre, v_norm_post, no_state]), "adam_norms")
    for q, packed in enumerate(outs):
        results["norm_pre"][q], results["norm_post"][q], summed = _unpack(packed, norm_like)
        if q == 0:
            loss = summed[0, 0]
    a_small = ("a_w_s", "a_b_s", "a_ln_g", "a_ln_b")
    outs = _adam_replicated([[small_a[j][n] for j in range(n_a)] for n in range(len(a_small))],
                            [weights[k] for k in a_small], [mom_m[k] for k in a_small],
                            [mom_v[k] for k in a_small], "adam_small")
    for k, four in zip(a_small, outs):
        results[k] = list(four)

    def shard_view(a):
        return a.reshape(a.shape[0], -1, a.shape[-1])

    for k in ("a_w_in", "a_w_out", "b_w_in", "b_w_grp", "b_w_out"):
        w3 = shard_view(weights[k])
        recvs = [r.reshape(r.shape[0], w3.shape[1], w3.shape[2]) for r in recv[k]]
        outs = _reduce_adam(recvs, w3, shard_view(mom_m[k]), shard_view(mom_v[k]), f"adam_{k}")
        results[k] = [o_.reshape(weights[k].shape) for o_ in outs]
    sc_recv = jnp.concatenate(recv["b_scale"], axis=1)
    outs = _reduce_adam([sc_recv], b_scale[None], m_b_scale[None], v_b_scale[None], "adam_b_scale")
    results["b_scale"] = [o_[0] for o_ in outs]

    grad_x = dx[None]
    return (loss, grad_x, *[results[k][0] for k in names], *[results[k][1] for k in names],
            *[results[k][2] for k in names], *[results[k][3] for k in names])
```

```python
import jax
import jax.numpy as jnp
from jax import lax
from jax.experimental import pallas as pl
from jax.experimental.pallas import tpu as pltpu

F32 = jnp.float32
BF16 = jnp.bfloat16
MESH = pl.DeviceIdType.MESH

N_DEV = 8
EPS = 1e-6
CHUNK = 64
GMLP_BLOCK = 128
A_GROUPS = 8
POOL_WINDOWS = (2, 4, 8, 16)
HALO = 16
SUBLANES = 8
ADAM_LR = 0.001
ADAM_B1 = 0.9
ADAM_B2 = 0.999
ADAM_EPS = 1e-08
ADAM_WD = 0.01
ADAM_STEP = 10
GELU_C = 0.7978845608028654
GELU_A = 0.044715
ROW_CHUNK = 16
VMEM_LIMIT_BYTES = 56 * 1024 * 1024


def _params(**kw):
    return pltpu.CompilerParams(vmem_limit_bytes=VMEM_LIMIT_BYTES, **kw)


def _gelu(x):
    return 0.5 * x * (1.0 + jnp.tanh(GELU_C * (x + GELU_A * (x * x * x))))


def _gelu_and_grad(x):
    x2 = x * x
    t = jnp.tanh(GELU_C * (x + GELU_A * (x2 * x)))
    val = 0.5 * x * (1.0 + t)
    grad = 0.5 * (1.0 + t) + 0.5 * x * (1.0 - t * t) * (GELU_C * (1.0 + 3.0 * GELU_A * x2))
    return val, grad


def _sigmoid(z):
    return 0.5 * jnp.tanh(0.5 * z) + 0.5


def _dot(a, b):
    return jnp.dot(a, b, preferred_element_type=F32)


def _dot_nt(a, b):
    return lax.dot_general(a, b, (((1,), (1,)), ((), ())), preferred_element_type=F32)


def _dot_tn(a, b):
    return lax.dot_general(a, b, (((0,), (0,)), ((), ())), preferred_element_type=F32)


def _rms_stats(xf):
    r = lax.rsqrt(jnp.mean(xf * xf, axis=-1, keepdims=True) + EPS)
    return r, xf * r


def _rms_bwd(dy, g, r, xh):
    dxh = dy * g
    return r * (dxh - xh * jnp.mean(dxh * xh, axis=-1, keepdims=True))


def _resident(shape):
    return pl.BlockSpec(shape, lambda *_: (0,) * len(shape), pipeline_mode=pl.Buffered(1))


def _pick(stacked):
    arr, index = stacked
    return pl.BlockSpec((None,) + arr.shape[1:], lambda *_: (index,) + (0,) * (arr.ndim - 1))


def _spatial_mask(transposed=False):
    p = lax.broadcasted_iota(jnp.int32, (GMLP_BLOCK, GMLP_BLOCK), 0)
    q = lax.broadcasted_iota(jnp.int32, (GMLP_BLOCK, GMLP_BLOCK), 1)
    if transposed:
        p, q = q, p
    return (q // CHUNK) <= (p // CHUNK)


def _position():
    x, y, c = lax.axis_index("x"), lax.axis_index("y"), lax.axis_index("c")
    return x, y, c


def _comm_scratch(n):
    return [pltpu.SemaphoreType.DMA((n, 7)), pltpu.SemaphoreType.DMA((n, 7)), pltpu.SemaphoreType.DMA((n,))]


class _Gather:
    def __init__(self, arrs):
        self.inputs = list(arrs)
        self.out_shape = [jax.ShapeDtypeStruct((N_DEV,) + a.shape, a.dtype) for a in arrs]
        self.scratch = _comm_scratch(len(arrs))

    def _plan(self, ins, outs, sems):
        send_sems, recv_sems, local_sems = sems
        n = len(ins)
        x, y, c = _position()
        sibling = (x, y, 1 - c)
        chips = [(1 - x, y), (x, 1 - y), (1 - x, 1 - y)]

        def index(px, py, pc):
            return 4 * px + 2 * py + pc

        def copy(a, k, block, to, src=None):
            return pltpu.make_async_remote_copy(
                src_ref=outs[a].at[block] if src is None else src, dst_ref=outs[a].at[block],
                send_sem=send_sems.at[a, k], recv_sem=recv_sems.at[a, k], device_id=to, device_id_type=MESH)

        me = index(x, y, c)
        own = [pltpu.make_async_copy(ins[a], outs[a].at[me], local_sems.at[a]) for a in range(n)]
        first = []
        for a in range(n):
            first.append(copy(a, 0, me, sibling, src=ins[a]))
            for j, chip in enumerate(chips):
                first.append(copy(a, 1 + j, me, (*chip, c), src=ins[a]))
        return n, (x, y, c), sibling, chips, index, copy, own, first

    def start(self, ins, outs, sems):
        _, _, _, _, _, _, own, first = self._plan(ins, outs, sems)
        for cp in own + first:
            cp.start()

    def middle(self, ins, outs, sems):
        n, me, sibling, chips, index, copy, _, _ = self._plan(ins, outs, sems)
        for j, chip in enumerate(chips):
            for a in range(n):
                copy(a, 1 + j, index(*chip, me[2]), me).wait_recv()
                copy(a, 4 + j, index(*chip, me[2]), sibling).start()

    def finish(self, ins, outs, sems):
        n, me, sibling, chips, index, copy, own, first = self._plan(ins, outs, sems)
        c = me[2]
        passed = [copy(a, 4 + j, index(*chip, c), sibling) for j, chip in enumerate(chips) for a in range(n)]
        for a in range(n):
            copy(a, 0, index(me[0], me[1], 1 - c), me).wait_recv()
        for j, chip in enumerate(chips):
            for a in range(n):
                copy(a, 4 + j, index(*chip, 1 - c), me).wait_recv()
        for cp in first + passed:
            cp.wait_send()
        for cp in own:
            cp.wait()


class _GatherTwoWays:
    def __init__(self, arrs):
        self.inputs = list(arrs)
        self.out_shape = [jax.ShapeDtypeStruct((N_DEV,) + a.shape, a.dtype) for a in arrs]
        n = len(arrs)
        self.scratch = [pltpu.SemaphoreType.DMA((n, 8)), pltpu.SemaphoreType.DMA((n, 8)),
                        pltpu.SemaphoreType.DMA((n,))]

    def run(self, ins, outs, sems):
        send_sems, recv_sems, local_sems = sems
        n = len(ins)
        x, y, c = _position()
        me_id, sibling = (x, y, c), (x, y, 1 - c)
        chip_x, chip_y, chip_d = (1 - x, y), (x, 1 - y), (1 - x, 1 - y)

        def index(chip, pc):
            return 4 * chip[0] + 2 * chip[1] + pc

        def copy(a, k, block, to, rows=None, src=None):
            half = ins[a].shape[0] // 2
            where = (block,) if rows is None else (block, pl.ds(rows * half, half))
            return pltpu.make_async_remote_copy(
                src_ref=outs[a].at[where] if src is None else src, dst_ref=outs[a].at[where],
                send_sem=send_sems.at[a, k], recv_sem=recv_sems.at[a, k], device_id=to, device_id_type=MESH)

        me = index((x, y), c)
        own = [pltpu.make_async_copy(ins[a], outs[a].at[me], local_sems.at[a]) for a in range(n)]
        sent = []

        def start(cp):
            cp.start()
            sent.append(cp)

        for cp in own:
            cp.start()
        for a in range(n):
            start(copy(a, 0, me, sibling, src=ins[a]))
            start(copy(a, 1, me, (*chip_x, c), src=ins[a]))
            start(copy(a, 2, me, (*chip_y, c), src=ins[a]))
        for a in range(n):
            copy(a, 1, index(chip_x, c), me_id).wait_recv()
            start(copy(a, 3, index(chip_x, c), (*chip_y, c), rows=0))
            start(copy(a, 5, index(chip_x, c), sibling))
        for a in range(n):
            copy(a, 2, index(chip_y, c), me_id).wait_recv()
            start(copy(a, 4, index(chip_y, c), (*chip_x, c), rows=1))
            start(copy(a, 6, index(chip_y, c), sibling))
        for a in range(n):
            copy(a, 3, index(chip_d, c), me_id, rows=0).wait_recv()
            copy(a, 4, index(chip_d, c), me_id, rows=1).wait_recv()
            start(copy(a, 7, index(chip_d, c), sibling))
        for a in range(n):
            copy(a, 0, index((x, y), 1 - c), me_id).wait_recv()
            for k, chip in ((5, chip_x), (6, chip_y), (7, chip_d)):
                copy(a, k, index(chip, 1 - c), me_id).wait_recv()
        for cp in sent:
            cp.wait_send()
        for cp in own:
            cp.wait()

    def start(self, ins, outs, sems):
        self.run(ins, outs, sems)

    def middle(self, ins, outs, sems):
        pass

    def finish(self, ins, outs, sems):
        pass


class _Exchange:
    def __init__(self, arrs, chips_only=False):
        self.inputs = list(arrs)
        self.chips_only = chips_only
        self.out_shape = [jax.ShapeDtypeStruct(a.shape, a.dtype) for a in arrs]
        self.scratch = _comm_scratch(len(arrs))

    def _plan(self, ins, outs, sems):
        send_sems, recv_sems, local_sems = sems
        n = len(ins)
        x, y, c = _position()
        scale = 1 if self.chips_only else 2
        me = 2 * x + y if self.chips_only else 4 * x + 2 * y + c
        own = [pltpu.make_async_copy(ins[a].at[me], outs[a].at[me], local_sems.at[a]) for a in range(n)]
        sends, recvs = [], []
        for r in range(1, 4 * scale):
            px = 1 - x if r & (2 * scale) else x
            py = 1 - y if r & scale else y
            pc = 1 - c if (r & 1 and not self.chips_only) else c
            peer = 2 * px + py if self.chips_only else 4 * px + 2 * py + pc
            for a in range(n):
                sends.append(pltpu.make_async_remote_copy(
                    src_ref=ins[a].at[peer], dst_ref=outs[a].at[me],
                    send_sem=send_sems.at[a, r - 1], recv_sem=recv_sems.at[a, r - 1],
                    device_id=(px, py, pc), device_id_type=MESH))
                recvs.append(pltpu.make_async_remote_copy(
                    src_ref=ins[a].at[peer], dst_ref=outs[a].at[peer],
                    send_sem=send_sems.at[a, r - 1], recv_sem=recv_sems.at[a, r - 1],
                    device_id=(px, py, pc), device_id_type=MESH))
        return own, sends, recvs

    def start(self, ins, outs, sems):
        own, sends, _ = self._plan(ins, outs, sems)
        for cp in own + sends:
            cp.start()

    def middle(self, ins, outs, sems):
        pass

    def finish(self, ins, outs, sems):
        own, sends, recvs = self._plan(ins, outs, sems)
        for cp in recvs:
            cp.wait_recv()
        for cp in sends:
            cp.wait_send()
        for cp in own:
            cp.wait()


CHIPS = N_DEV // 2


def _sibling_scratch(slab_shape, stage_slots=CHIPS):
    return [pltpu.VMEM((stage_slots,) + tuple(slab_shape), BF16), pltpu.VMEM((CHIPS,) + tuple(slab_shape), BF16),
            pltpu.SemaphoreType.DMA((CHIPS,)), pltpu.SemaphoreType.DMA((CHIPS,))]


def _to_sibling(stage_s, land_s, send_sems, recv_sems, slot, q):
    x, y, c = _position()
    return pltpu.make_async_remote_copy(
        src_ref=stage_s.at[slot], dst_ref=land_s.at[q], send_sem=send_sems.at[q], recv_sem=recv_sems.at[q],
        device_id=(x, y, 1 - c), device_id_type=MESH)


def _sum_with_sibling(slab_of, q_ref, sibling_scratch):
    stage_s, land_s, send_sems, recv_sems = sibling_scratch
    c = lax.axis_index("c")
    for q in range(CHIPS):
        stage_s[q] = slab_of(2 * q + 1 - c).astype(BF16)
        _to_sibling(stage_s, land_s, send_sems, recv_sems, q, q).start()
    for q in range(CHIPS):
        _to_sibling(stage_s, land_s, send_sems, recv_sems, q, q).wait_recv()
        q_ref[q] = (slab_of(2 * q + c) + land_s[q].astype(F32)).astype(BF16)
    for q in range(CHIPS):
        _to_sibling(stage_s, land_s, send_sems, recv_sems, q, q).wait_send()


class _Together:
    def __init__(self, comms):
        self.comms = list(comms)
        self.inputs = [a for c in self.comms for a in c.inputs]
        self.out_shape = [s for c in self.comms for s in c.out_shape]
        self.scratch = [s for c in self.comms for s in c.scratch]

    def _each(self, ins, outs, sems):
        at = 0
        for k, c in enumerate(self.comms):
            n = len(c.inputs)
            yield c, ins[at:at + n], outs[at:at + n], sems[3 * k:3 * k + 3]
            at += n

    def start(self, ins, outs, sems):
        for c, i, o, s in self._each(ins, outs, sems):
            c.start(i, o, s)

    def middle(self, ins, outs, sems):
        for c, i, o, s in self._each(ins, outs, sems):
            c.middle(i, o, s)

    def finish(self, ins, outs, sems):
        for c, i, o, s in self._each(ins, outs, sems):
            c.finish(i, o, s)


def _comm_only(comm, name):
    n = len(comm.inputs)

    def body(*refs):
        ins, outs, sems = refs[:n], refs[n:2 * n], refs[2 * n:]
        comm.start(ins, outs, sems)
        comm.middle(ins, outs, sems)
        comm.finish(ins, outs, sems)

    any_spec = pl.BlockSpec(memory_space=pl.ANY)
    return pl.pallas_call(
        body, name=name, out_shape=comm.out_shape, in_specs=[any_spec] * n, out_specs=[any_spec] * n,
        scratch_shapes=comm.scratch, compiler_params=pltpu.CompilerParams(has_side_effects=True),
    )(*comm.inputs)


def _pcall(body, name, grid, in_specs, out_specs, out_shape, args, scratch_shapes=(), comm=None):
    in_specs, out_specs, out_shape, scratch_shapes = list(in_specs), list(out_specs), list(out_shape), list(scratch_shapes)
    if comm is None:
        outs = pl.pallas_call(body, name=name, grid=grid, in_specs=in_specs, out_specs=out_specs, out_shape=out_shape,
                              scratch_shapes=scratch_shapes, compiler_params=_params())(*args)
        return list(outs), []
    n_in, n_out, n_scr, n_c = len(in_specs), len(out_specs), len(scratch_shapes), len(comm.inputs)

    def carrying(*refs):
        ins, refs = refs[:n_in], refs[n_in:]
        c_ins, refs = refs[:n_c], refs[n_c:]
        outs, refs = refs[:n_out], refs[n_out:]
        c_outs, refs = refs[:n_c], refs[n_c:]
        scr, sems = refs[:n_scr], refs[n_scr:]
        step, steps = 0, 1
        for d, size in enumerate(grid):
            step = step * size + pl.program_id(d)
            steps *= size

        @pl.when(step == 0)
        def _():
            comm.start(c_ins, c_outs, sems)

        body(*ins, *outs, *scr)

        @pl.when(step == max(steps - 2, 0))
        def _():
            comm.middle(c_ins, c_outs, sems)

        @pl.when(step == steps - 1)
        def _():
            comm.finish(c_ins, c_outs, sems)

    any_spec = pl.BlockSpec(memory_space=pl.ANY)
    outs = pl.pallas_call(
        carrying, name=name, grid=grid, in_specs=in_specs + [any_spec] * n_c, out_specs=out_specs + [any_spec] * n_c,
        out_shape=out_shape + comm.out_shape, scratch_shapes=scratch_shapes + comm.scratch,
        compiler_params=_params(has_side_effects=True),
    )(*args, *comm.inputs)
    return list(outs[:n_out]), list(outs[n_out:])


def _in_proj_fwd(x, g_row, w_parts, name, comm=None):
    T, D = x.shape
    NS = w_parts[0].shape[-1]
    DP = w_parts[0].shape[-2]
    n_parts = len(w_parts)
    assert DP * n_parts == D
    TM = min(T, 512)

    def body(x_ref, g_ref, *refs):
        w_refs, (proj_ref, h_ref) = refs[:n_parts], refs[n_parts:]
        _, xh = _rms_stats(x_ref[...])
        h = (xh * g_ref[...]).astype(BF16)
        h_ref[...] = h
        for k in range(N_DEV):
            acc = _dot(h[:, 0:DP], w_refs[0][k])
            for p in range(1, n_parts):
                acc = acc + _dot(h[:, p * DP:(p + 1) * DP], w_refs[p][k])
            proj_ref[:, k * NS:(k + 1) * NS] = acc.astype(BF16)

    return _pcall(
        body, name, (T // TM,),
        in_specs=[pl.BlockSpec((TM, D), lambda i: (i, 0)), _pick(g_row)] + [_resident((N_DEV, DP, NS))] * n_parts,
        out_specs=[pl.BlockSpec((TM, N_DEV * NS), lambda i: (i, 0)),
                   pl.BlockSpec((TM, D), lambda i: (i, 0))],
        out_shape=[jax.ShapeDtypeStruct((T, N_DEV * NS), BF16), jax.ShapeDtypeStruct((T, D), BF16)],
        args=(x, g_row[0], *w_parts), comm=comm)


def _a_mix_fwd(proj, ln_g, ln_b, w_s, b_s, name, comm=None):
    T, E3 = proj.shape
    E = E3 // 3
    G, P = A_GROUPS, GMLP_BLOCK
    GD = E // G
    TB = min(T, 512)

    def body(p_ref, lg_ref, lb_ref, ws_ref, bs_ref, y_ref, xh_ref, dgl_ref, rstd_ref, v_s, us_s):
        def norm_chunk(ci, carry):
            rows = pl.ds(pl.multiple_of(ci * ROW_CHUNK, ROW_CHUNK), ROW_CHUNK)
            vg, dgl = _gelu_and_grad(p_ref[rows, E:2 * E].astype(F32))
            dgl_ref[rows, :] = dgl.astype(BF16)
            xc = vg - jnp.mean(vg, axis=-1, keepdims=True)
            rstd = lax.rsqrt(jnp.mean(xc * xc, axis=-1, keepdims=True) + EPS)
            rstd_ref[rows, :] = rstd
            xh = xc * rstd
            xh_ref[rows, :] = xh.astype(BF16)
            v_s[rows, :] = (xh * lg_ref[...] + lb_ref[...]).astype(BF16)
            return carry

        def gate_chunk(ci, carry):
            rows = pl.ds(pl.multiple_of(ci * ROW_CHUNK, ROW_CHUNK), ROW_CHUNK)
            z = p_ref[rows, 2 * E:3 * E].astype(F32)
            us_s[rows, :] = _gelu(p_ref[rows, 0:E].astype(F32)) * (z * _sigmoid(z))
            return carry

        lax.fori_loop(0, TB // ROW_CHUNK, norm_chunk, 0, unroll=2)
        lax.fori_loop(0, TB // ROW_CHUNK, gate_chunk, 0, unroll=2)
        mask = _spatial_mask()
        for g in range(G):
            wm = jnp.where(mask, ws_ref[g], 0.0).astype(BF16)
            cols = slice(g * GD, (g + 1) * GD)
            for b in range(TB // P):
                rows = slice(b * P, (b + 1) * P)
                mixed = _dot(wm, v_s[rows, cols]) + bs_ref[g]
                y_ref[rows, cols] = (us_s[rows, cols] * mixed).astype(BF16)

    return _pcall(
        body, name, (T // TB,),
        in_specs=[pl.BlockSpec((TB, E3), lambda i: (i, 0)),
                  _pick(ln_g), _pick(ln_b), _pick(w_s), _pick(b_s)],
        out_specs=[pl.BlockSpec((TB, E), lambda i: (i, 0)), pl.BlockSpec((TB, E), lambda i: (i, 0)),
                   pl.BlockSpec((TB, E), lambda i: (i, 0)), pl.BlockSpec((TB, 1), lambda i: (i, 0))],
        out_shape=[jax.ShapeDtypeStruct((T, E), BF16), jax.ShapeDtypeStruct((T, E), BF16),
                   jax.ShapeDtypeStruct((T, E), BF16), jax.ShapeDtypeStruct((T, 1), F32)],
        scratch_shapes=[pltpu.VMEM((TB, E), BF16), pltpu.VMEM((TB, E), F32)],
        args=(proj, ln_g[0], ln_b[0], w_s[0], b_s[0]), comm=comm)


def _window_sum_back(ext, win):
    s, k = ext, 1
    while k < win:
        s = s + pltpu.roll(s, k, axis=0)
        k *= 2
    return s


def _window_sum_ahead(ext, win):
    n = ext.shape[0]
    s, k = ext, 1
    while k < win:
        s = s + pltpu.roll(s, n - k, axis=0)
        k *= 2
    return s


def _inv_count(t0, rows, win):
    t1 = t0 + 1 + lax.broadcasted_iota(jnp.int32, (rows, 1), 0)
    return 1.0 / jnp.minimum(t1, win).astype(F32)


def _b_mix_fwd(proj, scale, wg_all, name, comm=None):
    T, E2 = proj.shape
    E = E2 // 2
    NG = len(POOL_WINDOWS)
    GB = E // NG
    TB = min(T, 256)
    RS = wg_all.shape[-2]

    def body(p_ref, sc_ref, wg_ref, y_ref, o_ref, carry_s):
        i = pl.program_id(0)

        @pl.when(i == 0)
        def _():
            carry_s[...] = jnp.zeros_like(carry_s)

        for g, win in enumerate(POOL_WINDOWS):
            cols = slice(g * GB, (g + 1) * GB)
            xg = p_ref[:, cols].astype(F32)
            ext = jnp.concatenate([carry_s[:, cols], xg], axis=0)
            pooled = _window_sum_back(ext, win)[HALO:, :] * _inv_count(i * TB, TB, win) - xg
            carry_s[:, cols] = xg[TB - HALO:, :]
            o = _dot(pooled.astype(BF16), wg_ref[:, g].reshape(GB, GB))
            o_ref[:, cols] = o.astype(BF16)
            z = p_ref[:, E + g * GB:E + (g + 1) * GB].astype(F32)
            y_ref[:, cols] = ((o * sc_ref[:, cols]) * (z * _sigmoid(z))).astype(BF16)

    return _pcall(
        body, name, (T // TB,),
        in_specs=[pl.BlockSpec((TB, E2), lambda i: (i, 0)),
                  pl.BlockSpec((1, E), lambda i: (0, 0)),
                  pl.BlockSpec((N_DEV, NG, RS, GB), lambda i: (0, 0, 0, 0))],
        out_specs=[pl.BlockSpec((TB, E), lambda i: (i, 0)), pl.BlockSpec((TB, E), lambda i: (i, 0))],
        out_shape=[jax.ShapeDtypeStruct((T, E), BF16), jax.ShapeDtypeStruct((T, E), BF16)],
        scratch_shapes=[pltpu.VMEM((HALO, E), F32)],
        args=(proj, scale, wg_all), comm=comm)


def _out_proj_fwd(y, w_all, x, g_row, name, comm=None):
    T, E = y.shape
    D = x.shape[1]
    ES = w_all.shape[-2]
    TM = min(T, 512)

    def body(y_ref, w_ref, x_ref, g_ref, xn_ref, out_ref):
        o = _dot(y_ref[...], w_ref[...].reshape(E, D))
        out_ref[...] = o
        _, oh = _rms_stats(o)
        xn_ref[...] = x_ref[...] + oh * g_ref[...]

    return _pcall(
        body, name, (T // TM,),
        in_specs=[pl.BlockSpec((TM, E), lambda i: (i, 0)),
                  pl.BlockSpec((N_DEV, ES, D), lambda i: (0, 0, 0)),
                  pl.BlockSpec((TM, D), lambda i: (i, 0)),
                  _pick(g_row)],
        out_specs=[pl.BlockSpec((TM, D), lambda i: (i, 0)), pl.BlockSpec((TM, D), lambda i: (i, 0))],
        out_shape=[jax.ShapeDtypeStruct((T, D), F32), jax.ShapeDtypeStruct((T, D), F32)],
        args=(y, w_all, x, g_row[0]), comm=comm)


def _out_proj_loss_fwd(y, w_all, x, g_row, target, name):
    T, E = y.shape
    D = x.shape[1]
    ES = w_all.shape[-2]
    TM = min(T, 512)
    nT = T // TM

    def body(y_ref, w_ref, x_ref, g_ref, t_ref, dx_ref, out_ref, loss_ref, acc_s):
        i = pl.program_id(0)

        @pl.when(i == 0)
        def _():
            acc_s[...] = jnp.zeros_like(acc_s)

        o = _dot(y_ref[...], w_ref[...].reshape(E, D))
        out_ref[...] = o
        _, oh = _rms_stats(o)
        e = (x_ref[...] + oh * g_ref[...]) - t_ref[...]
        dx_ref[...] = e * (1.0 / D)
        acc_s[...] += jnp.sum(e * e, axis=0, keepdims=True)

        @pl.when(i == nT - 1)
        def _():
            total = jnp.sum(acc_s[...], axis=1, keepdims=True) * (0.5 / D)
            loss_ref[...] = jnp.broadcast_to(total, loss_ref.shape)

    return _pcall(
        body, name, (nT,),
        in_specs=[pl.BlockSpec((TM, E), lambda i: (i, 0)),
                  pl.BlockSpec((N_DEV, ES, D), lambda i: (0, 0, 0)),
                  pl.BlockSpec((TM, D), lambda i: (i, 0)),
                  _pick(g_row),
                  pl.BlockSpec((TM, D), lambda i: (i, 0))],
        out_specs=[pl.BlockSpec((TM, D), lambda i: (i, 0)), pl.BlockSpec((TM, D), lambda i: (i, 0)),
                   pl.BlockSpec((1, 128), lambda i: (0, 0))],
        out_shape=[jax.ShapeDtypeStruct((T, D), F32), jax.ShapeDtypeStruct((T, D), F32),
                   jax.ShapeDtypeStruct((1, 128), F32)],
        scratch_shapes=[pltpu.VMEM((1, D), F32)],
        args=(y, w_all, x, g_row[0], target))[0]


def _out_proj_bwd(dxn, out, g_row, w_all, y, name, comm=None):
    T, D = dxn.shape
    E = y.shape[1]
    ES = w_all.shape[-2]
    TM = min(T, 512)
    nT = T // TM

    def body(dxn_ref, out_ref, g_ref, w_ref, y_ref, dy_ref, dw_ref, dg_ref, acc_s, *sibling_scratch):
        i = pl.program_id(0)

        @pl.when(i == 0)
        def _():
            acc_s[...] = jnp.zeros_like(acc_s)
            dg_ref[...] = jnp.zeros_like(dg_ref)

        dxn_v = dxn_ref[...]
        r, oh = _rms_stats(out_ref[...])
        dg_ref[...] += jnp.sum(dxn_v * oh, axis=0, keepdims=True)
        dout = _rms_bwd(dxn_v, g_ref[...], r, oh).astype(BF16)
        dy_ref[...] = _dot_nt(dout, w_ref[...].reshape(E, D)).astype(BF16)
        acc_s[...] += _dot_tn(y_ref[...], dout)

        @pl.when(i == nT - 1)
        def _():
            def slab_of(k):
                return acc_s[pl.ds(pl.multiple_of(k * ES, ES), ES), :]

            _sum_with_sibling(slab_of, dw_ref, sibling_scratch)

    return _pcall(
        body, name, (nT,),
        in_specs=[pl.BlockSpec((TM, D), lambda i: (i, 0)),
                  pl.BlockSpec((TM, D), lambda i: (i, 0)),
                  _pick(g_row),
                  pl.BlockSpec((N_DEV, ES, D), lambda i: (0, 0, 0)),
                  pl.BlockSpec((TM, E), lambda i: (i, 0))],
        out_specs=[pl.BlockSpec((TM, E), lambda i: (i, 0)),
                   pl.BlockSpec((CHIPS, ES, D), lambda i: (0, 0, 0)),
                   pl.BlockSpec((1, D), lambda i: (0, 0))],
        out_shape=[jax.ShapeDtypeStruct((T, E), BF16), jax.ShapeDtypeStruct((CHIPS, ES, D), BF16),
                   jax.ShapeDtypeStruct((1, D), F32)],
        scratch_shapes=[pltpu.VMEM((E, D), F32)] + _sibling_scratch((ES, D)),
        args=(dxn, out, g_row[0], w_all, y), comm=comm)


def _a_mix_bwd(proj, dy, xh, dgl, rstd, ln_g, ln_b, w_s, b_s, name, comm=None):
    T, E3 = proj.shape
    E = E3 // 3
    G, P = A_GROUPS, GMLP_BLOCK
    GD = E // G
    TB = min(T, 256)

    def body(up_ref, zp_ref, dy_ref, xh_ref, dgl_ref, rstd_ref, lg_ref, lb_ref, ws_ref, bs_ref,
             dp_ref, dws_ref, dbs_ref, dlg_ref, dlb_ref, v_s, a_s, bz_s, c_s, dv_s):
        @pl.when(pl.program_id(0) == 0)
        def _():
            dws_ref[...] = jnp.zeros_like(dws_ref)
            dbs_ref[...] = jnp.zeros_like(dbs_ref)
            dlg_ref[...] = jnp.zeros_like(dlg_ref)
            dlb_ref[...] = jnp.zeros_like(dlb_ref)

        def recompute(ci, carry):
            rows = pl.ds(pl.multiple_of(ci * ROW_CHUNK, ROW_CHUNK), ROW_CHUNK)
            v_s[rows, :] = (xh_ref[rows, :].astype(F32) * lg_ref[...] + lb_ref[...]).astype(BF16)
            u, du = _gelu_and_grad(up_ref[rows, :].astype(F32))
            z = zp_ref[rows, :].astype(F32)
            sg = _sigmoid(z)
            s = z * sg
            ds = sg * (1.0 + z * (1.0 - sg))
            dyv = dy_ref[rows, :].astype(F32)
            a_s[rows, :] = dyv * s * du
            bz_s[rows, :] = dyv * u * ds
            c_s[rows, :] = (dyv * u * s).astype(BF16)
            return carry

        lax.fori_loop(0, TB // ROW_CHUNK, recompute, 0, unroll=2)

        mask = _spatial_mask()
        mask_t = _spatial_mask(transposed=True)
        for g in range(G):
            w_g = ws_ref[g]
            wm = jnp.where(mask, w_g, 0.0).astype(BF16)
            wm_t = jnp.where(mask_t, w_g.T, 0.0).astype(BF16)
            cols = slice(g * GD, (g + 1) * GD)
            dws_g = jnp.zeros((P, P), F32)
            dbs_g = jnp.zeros((SUBLANES, P), F32)
            for b in range(TB // P):
                rows = slice(b * P, (b + 1) * P)
                vb = v_s[rows, cols]
                cb = c_s[rows, cols]
                mixed = _dot(wm, vb) + bs_ref[g]
                dp_ref[rows, g * GD:(g + 1) * GD] = (a_s[rows, cols] * mixed).astype(BF16)
                dp_ref[rows, 2 * E + g * GD:2 * E + (g + 1) * GD] = (bz_s[rows, cols] * mixed).astype(BF16)
                dv_s[rows, cols] = _dot(wm_t, cb)
                dws_g = dws_g + _dot_nt(cb, vb)
                dbs_g = dbs_g + _dot_nt(jnp.ones((SUBLANES, GD), BF16), cb)
            dws_ref[g] += jnp.where(mask, dws_g, 0.0)
            dbs_ref[g:g + 1, :] += dbs_g[0:1, :]

        def ln_bwd(ci, carry):
            rows = pl.ds(pl.multiple_of(ci * ROW_CHUNK, ROW_CHUNK), ROW_CHUNK)
            dv = dv_s[rows, :]
            xh = xh_ref[rows, :].astype(F32)
            dlg_ref[...] += jnp.sum(dv * xh, axis=0, keepdims=True)
            dlb_ref[...] += jnp.sum(dv, axis=0, keepdims=True)
            dxh = dv * lg_ref[...]
            dvg = rstd_ref[rows, :] * (dxh - jnp.mean(dxh, axis=-1, keepdims=True)
                                       - xh * jnp.mean(dxh * xh, axis=-1, keepdims=True))
            dp_ref[rows, E:2 * E] = (dvg * dgl_ref[rows, :].astype(F32)).astype(BF16)
            return carry

        lax.fori_loop(0, TB // ROW_CHUNK, ln_bwd, 0, unroll=2)

    return _pcall(
        body, name, (T // TB,),
        in_specs=[pl.BlockSpec((TB, E), lambda i: (i, 0)),
                  pl.BlockSpec((TB, E), lambda i: (i, 2)),
                  pl.BlockSpec((TB, E), lambda i: (i, 0)),
                  pl.BlockSpec((TB, E), lambda i: (i, 0)),
                  pl.BlockSpec((TB, E), lambda i: (i, 0)),
                  pl.BlockSpec((TB, 1), lambda i: (i, 0)),
                  _pick(ln_g), _pick(ln_b), _pick(w_s), _pick(b_s)],
        out_specs=[pl.BlockSpec((TB, E3), lambda i: (i, 0)),
                   pl.BlockSpec((G, P, P), lambda i: (0, 0, 0)),
                   pl.BlockSpec((G, P), lambda i: (0, 0)),
                   pl.BlockSpec((1, E), lambda i: (0, 0)),
                   pl.BlockSpec((1, E), lambda i: (0, 0))],
        out_shape=[jax.ShapeDtypeStruct((T, E3), BF16), jax.ShapeDtypeStruct((G, P, P), F32),
                   jax.ShapeDtypeStruct((G, P), F32), jax.ShapeDtypeStruct((1, E), F32),
                   jax.ShapeDtypeStruct((1, E), F32)],
        scratch_shapes=[pltpu.VMEM((TB, E), BF16), pltpu.VMEM((TB, E), F32), pltpu.VMEM((TB, E), F32),
                        pltpu.VMEM((TB, E), BF16), pltpu.VMEM((TB, E), F32)],
        args=(proj, proj, dy, xh, dgl, rstd, ln_g[0], ln_b[0], w_s[0], b_s[0]), comm=comm)


def _b_mix_bwd(proj, dy, o, scale, wg_all, name, comm=None):
    T, E2 = proj.shape
    E = E2 // 2
    NG = len(POOL_WINDOWS)
    GB = E // NG
    TB = min(T, 256)
    nT = T // TB
    RS = wg_all.shape[-2]
    halo_per_tile = TB // HALO

    def body(p_ref, halo_ref, dy_ref, o_ref, sc_ref, wg_ref, dp_ref, dsc_ref, dwg_ref, acc_s, carry_s,
             *sibling_scratch):
        i = pl.program_id(0)
        tile = nT - 1 - i

        @pl.when(i == 0)
        def _():
            acc_s[...] = jnp.zeros_like(acc_s)
            carry_s[...] = jnp.zeros_like(carry_s)
            dsc_ref[...] = jnp.zeros_like(dsc_ref)

        has_history = (tile > 0).astype(F32)
        for g, win in enumerate(POOL_WINDOWS):
            cols = slice(g * GB, (g + 1) * GB)
            inv = _inv_count(tile * TB, TB, win)
            xg = p_ref[:, cols].astype(F32)
            ext = jnp.concatenate([halo_ref[:, cols].astype(F32) * has_history, xg], axis=0)
            pooled = _window_sum_back(ext, win)[HALO:, :] * inv - xg
            z = p_ref[:, E + g * GB:E + (g + 1) * GB].astype(F32)
            sg = _sigmoid(z)
            dyv = dy_ref[:, cols].astype(F32)
            ov = o_ref[:, cols].astype(F32)
            sc = sc_ref[:, cols]
            dmixed = dyv * (z * sg)
            dsc_ref[:, cols] += jnp.sum(dmixed * ov, axis=0, keepdims=True)
            dz = dyv * (ov * sc) * (sg * (1.0 + z * (1.0 - sg)))
            do = (dmixed * sc).astype(BF16)
            acc_s[:, g] += _dot_tn(pooled.astype(BF16), do).reshape(N_DEV, RS, GB)
            dpool = _dot_nt(do, wg_ref[:, g].reshape(GB, GB))
            q = dpool * inv
            ext_q = jnp.concatenate([q, carry_s[:, cols]], axis=0)
            dxb = _window_sum_ahead(ext_q, win)[:TB, :] - dpool
            carry_s[:, cols] = q[:HALO, :]
            dp_ref[:, cols] = dxb.astype(BF16)
            dp_ref[:, E + g * GB:E + (g + 1) * GB] = dz.astype(BF16)

        @pl.when(i == nT - 1)
        def _():
            _sum_with_sibling(lambda k: acc_s[k], dwg_ref, sibling_scratch)

    return _pcall(
        body, name, (nT,),
        in_specs=[pl.BlockSpec((TB, E2), lambda i: (nT - 1 - i, 0)),
                  pl.BlockSpec((HALO, E), lambda i: (jnp.maximum((nT - 1 - i) * halo_per_tile - 1, 0), 0)),
                  pl.BlockSpec((TB, E), lambda i: (nT - 1 - i, 0)),
                  pl.BlockSpec((TB, E), lambda i: (nT - 1 - i, 0)),
                  pl.BlockSpec((1, E), lambda i: (0, 0)),
                  pl.BlockSpec((N_DEV, NG, RS, GB), lambda i: (0, 0, 0, 0))],
        out_specs=[pl.BlockSpec((TB, E2), lambda i: (nT - 1 - i, 0)),
                   pl.BlockSpec((1, E), lambda i: (0, 0)),
                   pl.BlockSpec((CHIPS, NG, RS, GB), lambda i: (0, 0, 0, 0))],
        out_shape=[jax.ShapeDtypeStruct((T, E2), BF16), jax.ShapeDtypeStruct((1, E), F32),
                   jax.ShapeDtypeStruct((CHIPS, NG, RS, GB), BF16)],
        scratch_shapes=[pltpu.VMEM((N_DEV, NG, RS, GB), F32), pltpu.VMEM((HALO, E), F32)]
        + _sibling_scratch((NG, RS, GB)),
        args=(proj, proj, dy, o, scale, wg_all), comm=comm)


def _in_proj_bwd_dx(dproj, w_parts, x, g_row, dxn, name, comm=None):
    T, D = x.shape
    NS = w_parts[0].shape[-1]
    DP = w_parts[0].shape[-2]
    n_parts = len(w_parts)
    TM = min(T, 512)

    def body(dp_ref, *refs):
        w_refs, (x_ref, g_ref, dxn_ref, dx_ref, dg_ref) = refs[:n_parts], refs[n_parts:]

        @pl.when(pl.program_id(0) == 0)
        def _():
            dg_ref[...] = jnp.zeros_like(dg_ref)

        pieces = []
        for w_ref in w_refs:
            piece = _dot_nt(dp_ref[:, 0:NS], w_ref[0])
            for k in range(1, N_DEV):
                piece = piece + _dot_nt(dp_ref[:, k * NS:(k + 1) * NS], w_ref[k])
            pieces.append(piece)
        dh = pieces[0] if n_parts == 1 else jnp.concatenate(pieces, axis=1)
        r, xh = _rms_stats(x_ref[...])
        dg_ref[...] += jnp.sum(dh * xh, axis=0, keepdims=True)
        dx_ref[...] = dxn_ref[...] + _rms_bwd(dh, g_ref[...], r, xh)

    return _pcall(
        body, name, (T // TM,),
        in_specs=[pl.BlockSpec((TM, N_DEV * NS), lambda i: (i, 0))] + [_resident((N_DEV, DP, NS))] * n_parts
        + [pl.BlockSpec((TM, D), lambda i: (i, 0)),
           _pick(g_row),
           pl.BlockSpec((TM, D), lambda i: (i, 0))],
        out_specs=[pl.BlockSpec((TM, D), lambda i: (i, 0)), pl.BlockSpec((1, D), lambda i: (0, 0))],
        out_shape=[jax.ShapeDtypeStruct((T, D), F32), jax.ShapeDtypeStruct((1, D), F32)],
        args=(dproj, *w_parts, x, g_row[0], dxn), comm=comm)


STAGE_SLOTS = 2


def _dw_in(h, dproj, name, comm=None):
    T, D = h.shape
    NS = dproj.shape[1] // N_DEV
    TK = min(T, 2048)
    nK = T // TK

    def body(h_ref, dp_ref, q_ref, acc_s, stage_s, land_s, send_sems, recv_sems):
        k, t = pl.program_id(0), pl.program_id(1)
        c = lax.axis_index("c")

        def to_sibling(q):
            return _to_sibling(stage_s, land_s, send_sems, recv_sems, q % STAGE_SLOTS, q)

        @pl.when(t == 0)
        def _():
            acc_s[...] = jnp.zeros_like(acc_s)

        acc_s[...] += _dot_tn(h_ref[...], dp_ref[...])

        @pl.when(t == nK - 1)
        def _():
            q = k // 2

            @pl.when(k % 2 == c)
            def _():
                q_ref[q] = acc_s[...].astype(BF16)

            @pl.when(k % 2 != c)
            def _():
                @pl.when(q >= STAGE_SLOTS)
                def _():
                    to_sibling(q - STAGE_SLOTS).wait_send()

                stage_s[q % STAGE_SLOTS] = acc_s[...].astype(BF16)
                to_sibling(q).start()

        @pl.when((k == N_DEV - 1) & (t == nK - 1))
        def _():
            for q in range(CHIPS - STAGE_SLOTS, CHIPS):
                to_sibling(q).wait_send()
            for q in range(CHIPS):
                to_sibling(q).wait_recv()
                q_ref[q] = (q_ref[q].astype(F32) + land_s[q].astype(F32)).astype(BF16)

    return _pcall(
        body, name, (N_DEV, nK),
        in_specs=[pl.BlockSpec((TK, D), lambda k, t: (t, 0)), pl.BlockSpec((TK, NS), lambda k, t: (t, k))],
        out_specs=[pl.BlockSpec((CHIPS, D, NS), lambda k, t: (0, 0, 0))],
        out_shape=[jax.ShapeDtypeStruct((CHIPS, D, NS), BF16)],
        scratch_shapes=[pltpu.VMEM((D, NS), F32)] + _sibling_scratch((D, NS), STAGE_SLOTS),
        args=(h, dproj), comm=comm)


def _reduce_adam(recvs, w, m, v, name):
    L, R, C = w.shape
    assert len(recvs) == L
    senders = recvs[0].shape[0]
    TR = R
    for cand in (256, 128, 64, 32, 16):
        if R % cand == 0 and R > cand:
            TR = cand
            break
    nR = R // TR
    c1 = 1.0 - ADAM_B1 ** ADAM_STEP
    c2 = 1.0 - ADAM_B2 ** ADAM_STEP

    def body(*refs):
        recv_refs = refs[:L]
        w_ref, m_ref, v_ref, g_ref, d_ref, nm_ref, nv_ref, g_s = refs[L:]
        layer = pl.program_id(0)
        for l in range(L):
            @pl.when(layer == l)
            def _(l=l):
                acc = recv_refs[l][0].astype(F32)
                for j in range(1, senders):
                    acc = acc + recv_refs[l][j].astype(F32)
                g_s[...] = acc

        g = g_s[...]
        g_ref[...] = g
        nm = ADAM_B1 * m_ref[...] + (1.0 - ADAM_B1) * g
        nv = ADAM_B2 * v_ref[...] + (1.0 - ADAM_B2) * (g * g)
        nm_ref[...] = nm
        nv_ref[...] = nv
        d_ref[...] = -ADAM_LR * ((nm / c1) / (jnp.sqrt(nv / c2) + ADAM_EPS) + ADAM_WD * w_ref[...])

    def recv_spec(l):
        def index(layer, t):
            before = jnp.where(layer < l, 0, nR - 1)
            return (0, jnp.where(layer == l, t, before), 0)
        return pl.BlockSpec((senders, TR, C), index)

    wspec = pl.BlockSpec((None, TR, C), lambda layer, t: (layer, t, 0))
    out = jax.ShapeDtypeStruct((L, R, C), F32)
    return _pcall(
        body, name, (L, nR),
        in_specs=[recv_spec(l) for l in range(L)] + [wspec] * 3,
        out_specs=[wspec] * 4, out_shape=[out] * 4,
        scratch_shapes=[pltpu.VMEM((TR, C), F32)],
        args=(*recvs, w, m, v))[0]


def _adam_replicated(gathered, w, m, v, name):
    n_params = len(w)
    layers = [len(g) for g in gathered]
    flat = [g for per_param in gathered for g in per_param]
    c1 = 1.0 - ADAM_B1 ** ADAM_STEP
    c2 = 1.0 - ADAM_B2 ** ADAM_STEP

    def body(*refs):
        g_refs, refs = refs[:len(flat)], refs[len(flat):]
        w_refs, m_refs, v_refs = refs[:n_params], refs[n_params:2 * n_params], refs[2 * n_params:3 * n_params]
        out_refs = refs[3 * n_params:]
        at = 0
        for n in range(n_params):
            g_out, d_out, nm_out, nv_out = out_refs[4 * n:4 * n + 4]
            for l in range(layers[n]):
                g_ref = g_refs[at]
                at += 1
                at_l = l if w_refs[n].ndim > 2 else slice(l, l + 1)
                g = g_ref[0]
                for s in range(1, N_DEV):
                    g = g + g_ref[s]
                nm = ADAM_B1 * m_refs[n][at_l] + (1.0 - ADAM_B1) * g
                nv = ADAM_B2 * v_refs[n][at_l] + (1.0 - ADAM_B2) * (g * g)
                g_out[at_l] = g
                nm_out[at_l] = nm
                nv_out[at_l] = nv
                d_out[at_l] = -ADAM_LR * ((nm / c1) / (jnp.sqrt(nv / c2) + ADAM_EPS) + ADAM_WD * w_refs[n][at_l])

    vmem = pl.BlockSpec(memory_space=pltpu.VMEM)
    args = [*flat, *w, *m, *v]
    outs = pl.pallas_call(
        body, name=name, in_specs=[vmem] * len(args), out_specs=[vmem] * (4 * n_params),
        out_shape=[jax.ShapeDtypeStruct(a.shape, F32) for a in w for _ in range(4)],
        compiler_params=_params(),
    )(*args)
    return [outs[4 * n:4 * n + 4] for n in range(n_params)]


A_W_IN_PARTS = 4
PACK_LANES = 128
PACK_ROWS_MULTIPLE = 256


def _pack(arrays):
    flat = jnp.concatenate([a.reshape(-1) for a in arrays])
    tile = PACK_LANES * PACK_ROWS_MULTIPLE
    padded = -(-flat.shape[0] // tile) * tile
    return jnp.pad(flat, (0, padded - flat.shape[0])).reshape(1, padded // PACK_LANES, PACK_LANES)


def _unpack(packed, like):
    flat = packed.reshape(-1)
    out, at = [], 0
    for a in like:
        out.append(flat[at:at + a.size].reshape(a.shape))
        at += a.size
    return out


def kernel(x, norm_pre, norm_post, a_w_in, a_ln_g, a_ln_b, a_w_s, a_b_s, a_w_out, b_w_in, b_w_grp, b_scale, b_w_out, loss_target, m_norm_pre, m_norm_post, m_a_w_in, m_a_ln_g, m_a_ln_b, m_a_w_s, m_a_b_s, m_a_w_out, m_b_w_in, m_b_w_grp, m_b_scale, m_b_w_out, v_norm_pre, v_norm_post, v_a_w_in, v_a_ln_g, v_a_ln_b, v_a_w_s, v_a_b_s, v_a_w_out, v_b_w_in, v_b_w_grp, v_b_scale, v_b_w_out):
    weights = dict(norm_pre=norm_pre, norm_post=norm_post, a_w_in=a_w_in, a_ln_g=a_ln_g, a_ln_b=a_ln_b, a_w_s=a_w_s,
                   a_b_s=a_b_s, a_w_out=a_w_out, b_w_in=b_w_in, b_w_grp=b_w_grp, b_scale=b_scale, b_w_out=b_w_out)
    mom_m = dict(norm_pre=m_norm_pre, norm_post=m_norm_post, a_w_in=m_a_w_in, a_ln_g=m_a_ln_g, a_ln_b=m_a_ln_b,
                 a_w_s=m_a_w_s, a_b_s=m_a_b_s, a_w_out=m_a_w_out, b_w_in=m_b_w_in, b_w_grp=m_b_w_grp,
                 b_scale=m_b_scale, b_w_out=m_b_w_out)
    mom_v = dict(norm_pre=v_norm_pre, norm_post=v_norm_post, a_w_in=v_a_w_in, a_ln_g=v_a_ln_g, a_ln_b=v_a_ln_b,
                 a_w_s=v_a_w_s, a_b_s=v_a_b_s, a_w_out=v_a_w_out, b_w_in=v_b_w_in, b_w_grp=v_b_w_grp,
                 b_scale=v_b_scale, b_w_out=v_b_w_out)
    names = list(weights)

    depth = norm_pre.shape[0]
    x0 = x[0]
    target = loss_target[0]
    T, D = x0.shape
    E = a_ln_g.shape[1]
    G, P = A_GROUPS, GMLP_BLOCK
    pre3, post3 = norm_pre.reshape(depth, 1, D), norm_post.reshape(depth, 1, D)
    ln_g3, ln_b3 = a_ln_g.reshape(-1, 1, E), a_ln_b.reshape(-1, 1, E)
    b_s4 = a_b_s.reshape(-1, G, P, 1)

    def shards_of(i):
        j = i // 2
        if i % 2 == 0:
            w = a_w_in[j].astype(BF16)
            rows = w.shape[0] // A_W_IN_PARTS
            parts = {f"w_in_{p}": w[p * rows:(p + 1) * rows] for p in range(A_W_IN_PARTS)}
            return dict(**parts, w_out=a_w_out[j].astype(BF16))
        return dict(w_in=b_w_in[j].astype(BF16), w_out=b_w_out[j].astype(BF16), grp=b_w_grp[j].astype(BF16))

    shard = [shards_of(i) for i in range(depth)]
    full = [dict() for _ in range(depth)]

    def gather_into(keys, got):
        for (i, key), arr in zip(keys, got):
            full[i][key] = arr

    def w_in_of(i):
        return [(i, k) for k in shard[i] if k.startswith("w_in")]

    def rest_of(i):
        return [(i, k) for k in shard[i] if not k.startswith("w_in")]

    def gather_of(keys):
        return _Gather([shard[a][k] for a, k in keys]) if keys else None

    first = _comm_only(_Together([_Gather([b_scale]), _GatherTwoWays([shard[0][k] for _, k in w_in_of(0)])]),
                       "gather_first")
    gather_into(w_in_of(0), first[1:])
    scale_full = jnp.transpose(first[0], (1, 0, 2)).reshape(b_scale.shape[0], 1, E)

    saved = []
    xi = x0
    for i in range(depth):
        j = i // 2
        g_pre, g_post = (pre3, i), (post3, i)
        keys_in, keys_mix, keys_out = [], [], []
        if i % 2 == 0:
            keys_mix = [(0, "w_out")] if i == 0 else []
            if i + 1 < depth:
                keys_in = w_in_of(i + 1)
                keys_mix = keys_mix + rest_of(i + 1)
            if i + 2 < depth:
                keys_out = w_in_of(i + 2)[:1]
        elif i + 1 < depth:
            keys_in = w_in_of(i + 1)[1:3]
            keys_mix = rest_of(i + 1)
            keys_out = w_in_of(i + 1)[3:]
        kind = "a" if i % 2 == 0 else "b"
        (proj, h), got = _in_proj_fwd(xi, g_pre, [full[i][k] for _, k in w_in_of(i)], f"{kind}_in_fwd_{i}",
                                      gather_of(keys_in))
        gather_into(keys_in, got)
        if i % 2 == 0:
            (y, *o), got = _a_mix_fwd(proj, (ln_g3, j), (ln_b3, j), (a_w_s, j), (b_s4, j),
                                      f"a_mix_fwd_{i}", gather_of(keys_mix))
        else:
            (y, o), got = _b_mix_fwd(proj, scale_full[j], full[i]["grp"], f"b_mix_fwd_{i}", gather_of(keys_mix))
        gather_into(keys_mix, got)
        if i + 1 < depth:
            (x_next, out), got = _out_proj_fwd(y, full[i]["w_out"], xi, g_post, f"{kind}_out_fwd_{i}",
                                               gather_of(keys_out))
            gather_into(keys_out, got)
        else:
            dx, out, loss_row = _out_proj_loss_fwd(y, full[i]["w_out"], xi, g_post, target, f"{kind}_out_loss_fwd_{i}")
            x_next = None
        saved.append((xi, h, proj, y, out, o))
        xi = x_next

    n_a, n_b = a_ln_g.shape[0], b_scale.shape[0]
    d_pre, d_post = [None] * depth, [None] * depth
    recv = {"a_w_in": [None] * n_a, "a_w_out": [None] * n_a, "b_w_in": [None] * n_b,
            "b_w_grp": [None] * n_b, "b_w_out": [None] * n_b, "b_scale": [None] * n_b}
    small_a = [[None] * 4 for _ in range(n_a)]
    kinds = ("chips", "devices", "gather")

    def carried(items):
        of = {kind: [it[3] for it in items if it[0] == kind] for kind in kinds}
        comms = (([_Exchange(of["chips"], chips_only=True)] if of["chips"] else [])
                 + ([_Exchange(of["devices"])] if of["devices"] else [])
                 + ([_Gather(of["gather"])] if of["gather"] else []))
        return None if not comms else comms[0] if len(comms) == 1 else _Together(comms)

    def received(items, got):
        ordered = [it for kind in kinds for it in items if it[0] == kind]
        for it, arr in zip(ordered, got):
            if it[0] == "gather":
                small_a[it[2][0]][it[2][1]] = arr
            else:
                recv[it[1]][it[2]] = arr

    pending = []
    small_pending = []
    for i in reversed(range(depth)):
        j = i // 2
        xi, h, proj, y, out, o = saved[i]
        g_pre, g_post = (pre3, i), (post3, i)
        if i % 2 == 0:
            (dy, dw_out, d_post[i]), _ = _out_proj_bwd(dx, out, g_post, full[i]["w_out"], y, f"a_out_bwd_{i}")
            items, pending = pending, []
            (dproj, d_w_s, dbs, d_ln_g, d_ln_b), got = _a_mix_bwd(
                proj, dy, *o, (ln_g3, j), (ln_b3, j), (a_w_s, j), (b_s4, j),
                f"a_mix_bwd_{i}", carried(items))
            received(items, got)
            small_pending += [("gather", "small", (j, n), part) for n, part in enumerate((d_w_s, dbs, d_ln_g, d_ln_b))]
            items = []
            if i == 0:
                items, small_pending = [("chips", "a_w_out", j, dw_out)] + small_pending, []
            (dw_in,), got = _dw_in(h, dproj, f"a_dw_in_{i}", carried(items))
            received(items, got)
            items = [("chips", "a_w_in", j, dw_in)] if i == 0 else [("chips", "a_w_out", j, dw_out)]
            (dx, d_pre[i]), got = _in_proj_bwd_dx(dproj, [full[i][k] for _, k in w_in_of(i)], xi, g_pre, dx,
                                                  f"a_in_bwd_{i}", carried(items))
            received(items, got)
            if i > 0:
                pending.append(("chips", "a_w_in", j, dw_in))
        else:
            items, small_pending = small_pending, []
            (dy, dw_out, d_post[i]), got = _out_proj_bwd(dx, out, g_post, full[i]["w_out"], y, f"b_out_bwd_{i}",
                                                        carried(items))
            received(items, got)
            items, pending = pending, []
            (dproj, dsc, dw_grp), got = _b_mix_bwd(proj, dy, o, scale_full[j], full[i]["grp"], f"b_mix_bwd_{i}",
                                                  carried(items))
            received(items, got)
            (dx, d_pre[i]), _ = _in_proj_bwd_dx(dproj, [full[i][k] for _, k in w_in_of(i)], xi, g_pre, dx,
                                                f"b_in_bwd_{i}")
            (dw_in,), _ = _dw_in(h, dproj, f"b_dw_in_{i}")
            pending += [("chips", "b_w_out", j, dw_out), ("chips", "b_w_grp", j, dw_grp), ("chips", "b_w_in", j, dw_in),
                        ("devices", "b_scale", j, dsc.reshape(N_DEV, 1, E // N_DEV))]
    assert not pending and not small_pending

    gathered = _comm_only(_Gather([_pack([*d_pre, *d_post, loss_row[:, :1]])[0]]), "gather_norm_grads")
    results = {k: [None] * 4 for k in names}
    no_state = jnp.zeros((1, 1), F32)
    norm_like = [norm_pre, norm_post, no_state]
    outs = _reduce_adam([gathered[-1]], _pack(norm_like), _pack([m_norm_pre, m_norm_post, no_state]),
                        _pack([v_norm_pre, v_norm_post, no_state]), "adam_norms")
    for q, packed in enumerate(outs):
        results["norm_pre"][q], results["norm_post"][q], summed = _unpack(packed, norm_like)
        if q == 0:
            loss = summed[0, 0]
    a_small = ("a_w_s", "a_b_s", "a_ln_g", "a_ln_b")
    outs = _adam_replicated([[small_a[j][n] for j in range(n_a)] for n in range(len(a_small))],
                            [weights[k] for k in a_small], [mom_m[k] for k in a_small],
                            [mom_v[k] for k in a_small], "adam_small")
    for k, four in zip(a_small, outs):
        results[k] = list(four)

    def shard_view(a):
        return a.reshape(a.shape[0], -1, a.shape[-1])

    for k in ("a_w_in", "a_w_out", "b_w_in", "b_w_grp", "b_w_out"):
        w3 = shard_view(weights[k])
        recvs = [r.reshape(r.shape[0], w3.shape[1], w3.shape[2]) for r in recv[k]]
        outs = _reduce_adam(recvs, w3, shard_view(mom_m[k]), shard_view(mom_v[k]), f"adam_{k}")
        results[k] = [o_.reshape(weights[k].shape) for o_ in outs]
    sc_recv = jnp.concatenate(recv["b_scale"], axis=1)
    outs = _reduce_adam([sc_recv], b_scale[None], m_b_scale[None], v_b_scale[None], "adam_b_scale")
    results["b_scale"] = [o_[0] for o_ in outs]

    grad_x = dx[None]
    return (loss, grad_x, *[results[k][0] for k in names], *[results[k][1] for k in names],
            *[results[k][2] for k in names], *[results[k][3] for k in names])
```
